```python
import jax
import jax.numpy as jnp
from jax import lax
import numpy as np

D_MODEL = 2048
BATCH = 32
SEQ = 256
DEPTH = 4
DEC_BATCH = 8
DEC_SEQ = 2048
PAST_LEN = 256

GRID_W = 64
RMS_EPS = 1e-6
N_EVEN = (DEPTH + 1) // 2
N_ODD = DEPTH // 2
A_WIDTH = D_MODEL // 2
A_HEAD = 64
A_HEADS = A_WIDTH // A_HEAD
LORA = 64
GN_EPS = 64e-5
B_WIDTH = D_MODEL - A_WIDTH
B_GROUPS = 4
B_GROUP_CH = B_WIDTH // B_GROUPS
SHIFT_COLS = 3 * A_WIDTH + 4 * LORA
EVEN_IN = SHIFT_COLS + A_WIDTH + 2 * B_WIDTH
C_HEAD = 64
C_HEADS = D_MODEL // C_HEAD
C_KV_HEADS = C_HEADS // 4
C_GROUP = C_HEADS // C_KV_HEADS
WINDOW = 128
BLOCK = 128
ROPE_BASE = 10000.0
ODD_IN = (C_HEADS + 2 * C_KV_HEADS) * C_HEAD + D_MODEL
NEG_INF = -1e30

kernel_name = 'hybrid_rwkv7_fnet_swa_dit_step'


def _rmsnorm(x, g):
    x32 = x.astype(jnp.float32)
    y = x32 * lax.rsqrt(jnp.mean(x32 * x32, axis=-1, keepdims=True) + RMS_EPS)
    return (y * g.astype(jnp.float32)).astype(x.dtype)


def _modulation(cond, w, b):
    m = (jax.nn.silu(cond) @ w + b)[..., None, :]
    return jnp.split(m, 3, axis=-1)


def _centred_shift(p):
    zero = jnp.zeros_like(p[:, :1])
    prev = jnp.concatenate([zero, p[:, :-1]], axis=1)
    nxt = jnp.concatenate([p[:, 1:], zero], axis=1)
    return 0.5 * (prev + nxt)


def _heads(t):
    return t.reshape(t.shape[:-1] + (A_HEADS, A_HEAD))


def _rev(t):
    return t[:, ::-1]


def _dirs(t):
    return jnp.stack([t[0], _rev(t[1])])


def _both(t):
    return jnp.stack([t, _rev(t)])


def _wkv_scan(w, kk, kka, k, v, r, s0):
    def step(s, inp):
        w_t, kk_t, kka_t, k_t, v_t, r_t = inp
        s = (s * w_t[..., None, :]
             - jnp.einsum('dbhij,dbhj->dbhi', s, kk_t)[..., None] * kka_t[..., None, :]
             + v_t[..., :, None] * k_t[..., None, :])
        return s, jnp.einsum('dbhij,dbhj->dbhi', s, r_t)
    xs = tuple(jnp.moveaxis(a, 2, 0) for a in (w, kk, kka, k, v, r))
    s_fin, o = lax.scan(step, s0, xs)
    return jnp.moveaxis(o, 0, 2), s_fin


def _rwkv_fourier_mixer(h, s0, w_in, mu, w0, w_up, a0, a_up, k_k, k_a, r_k, gn_w, gn_b, w_out):
    bsz, t_len, _ = h.shape
    f32 = jnp.float32
    proj = h @ w_in
    sh = proj[..., :SHIFT_COLS]
    sh = sh + mu * (_centred_shift(sh) - sh)
    r = sh[..., :A_WIDTH].astype(f32)
    k = sh[..., A_WIDTH:2 * A_WIDTH].astype(f32)
    v = sh[..., 2 * A_WIDTH:3 * A_WIDTH].astype(f32)
    low = sh[..., 3 * A_WIDTH:].reshape(bsz, t_len, 2, 2, LORA)
    w_low, a_low = low[:, :, 0], low[:, :, 1]
    o0 = SHIFT_COLS
    gate_a = proj[..., o0:o0 + A_WIDTH].astype(f32)
    u = proj[..., o0 + A_WIDTH:o0 + A_WIDTH + B_WIDTH].astype(f32)
    gate_b = proj[..., o0 + A_WIDTH + B_WIDTH:].astype(f32)

    w_raw = (w0[:, None, None] + jnp.einsum('btdl,dla->dbta', jnp.tanh(w_low), w_up)).astype(f32)
    decay = jnp.exp(-jnp.exp(-jax.nn.softplus(-w_raw) - 0.5))
    a = jax.nn.sigmoid((a0[:, None, None] + jnp.einsum('btdl,dla->dbta', a_low, a_up)).astype(f32))
    k_dir = k[None] * (1.0 + (a - 1.0) * k_a.astype(f32))
    kk = _heads(k * k_k.astype(f32))
    kk = kk / jnp.maximum(jnp.linalg.norm(kk, axis=-1, keepdims=True), 1e-12)
    rh, vh = _heads(r), _heads(v)
    o, s_fin = _wkv_scan(_dirs(_heads(decay)), _both(kk), _dirs(kk[None] * _heads(a)),
                         _dirs(_heads(k_dir)), _both(vh), _both(rh), s0.astype(f32))
    o = o[0] + _rev(o[1])
    mean = jnp.mean(o, axis=-1, keepdims=True)
    var = jnp.mean(jnp.square(o - mean), axis=-1, keepdims=True)
    o = ((o - mean) * lax.rsqrt(var + GN_EPS)).reshape(bsz, t_len, A_WIDTH)
    o = o * gn_w.astype(f32) + gn_b.astype(f32)
    bonus = jnp.sum(rh * _heads(jnp.mean(k_dir, axis=0)) * r_k.astype(f32), axis=-1, keepdims=True) * vh
    y_a = (o + bonus.reshape(bsz, t_len, A_WIDTH)) * jax.nn.silu(gate_a)

    ug = u.reshape(bsz, t_len, B_GROUPS, B_GROUP_CH)
    y_b = jnp.fft.fft2(ug, axes=(1, 3), norm='ortho').real.reshape(bsz, t_len, B_WIDTH)
    y_b = y_b * jax.nn.silu(gate_b)
    y = jnp.concatenate([y_a, y_b], axis=-1).astype(h.dtype) @ w_out
    return y, s_fin


def _axial_rope(x):
    t_len = x.shape[1]
    rows = t_len // GRID_W
    row = jnp.repeat(jnp.arange(rows), GRID_W)
    col = jnp.tile(jnp.arange(GRID_W), rows)
    half = C_HEAD // 2
    nf = half // 2
    inv = 1.0 / (ROPE_BASE ** (jnp.arange(nf, dtype=jnp.float32) / nf))
    shape = (1, t_len) + (1,) * (x.ndim - 3) + (nf,)

    def rot(seg, pos):
        ang = pos.astype(jnp.float32)[:, None] * inv
        cos, sin = jnp.cos(ang).reshape(shape), jnp.sin(ang).reshape(shape)
        s1, s2 = seg[..., :nf], seg[..., nf:]
        return jnp.concatenate([s1 * cos - s2 * sin, s2 * cos + s1 * sin], axis=-1)

    x32 = x.astype(jnp.float32)
    return jnp.concatenate([rot(x32[..., :half], row), rot(x32[..., half:], col)], axis=-1).astype(x.dtype)


def _attend(q, k, v, valid, sink):
    s = jnp.einsum('bqhgd,bkhd->bhgqk', q, k).astype(jnp.float32) * (C_HEAD ** -0.5)
    if valid is not None:
        s = jnp.where(valid, s, NEG_INF)
    sink_col = jnp.broadcast_to(sink.astype(jnp.float32)[None, :, :, None, None], s.shape[:-1] + (1,))
    p = jax.nn.softmax(jnp.concatenate([s, sink_col], axis=-1), axis=-1)[..., :-1]
    return jnp.einsum('bhgqk,bkhd->bqhgd', p.astype(v.dtype), v)


def _attn_split(h, w_in):
    bsz, t_len, _ = h.shape
    proj = h @ w_in
    nq, nkv = C_HEADS * C_HEAD, C_KV_HEADS * C_HEAD
    q = proj[..., :nq].reshape(bsz, t_len, C_KV_HEADS, C_GROUP, C_HEAD)
    k = proj[..., nq:nq + nkv].reshape(bsz, t_len, C_KV_HEADS, C_HEAD)
    v = proj[..., nq + nkv:nq + 2 * nkv].reshape(bsz, t_len, C_KV_HEADS, C_HEAD)
    gate = proj[..., nq + 2 * nkv:]
    return q, k, v, gate


def _gated_out(o, gate, w_out):
    bsz, t_len = o.shape[:2]
    y = o.reshape(bsz, t_len, C_HEADS * C_HEAD).astype(jnp.float32) * jax.nn.silu(gate.astype(jnp.float32))
    return y.astype(gate.dtype) @ w_out


def _query_blocks(t):
    bsz, t_len = t.shape[:2]
    return jnp.moveaxis(t.reshape((bsz, t_len // BLOCK, BLOCK) + t.shape[2:]), 1, 0)


def _merge_blocks(o):
    nb, bsz = o.shape[:2]
    return jnp.moveaxis(o, 0, 1).reshape((bsz, nb * BLOCK) + o.shape[3:])


def _attention_context(h, w_in, sink, w_out):
    q, k, v, gate = _attn_split(h, w_in)
    sink = sink.reshape(C_KV_HEADS, C_GROUP)
    o = lax.map(lambda qb: _attend(qb, k, v, None, sink), _query_blocks(q))
    return _gated_out(_merge_blocks(o), gate, w_out), k, v


def _attention_latent(h, ctx_k, ctx_v, w_in, sink, w_out):
    q, k, v, gate = _attn_split(h, w_in)
    q, k = _axial_rope(q), _axial_rope(k)
    bsz, t_len = h.shape[:2]
    nb = t_len // BLOCK
    pad = ((0, 0), (BLOCK, BLOCK), (0, 0), (0, 0))
    kp = jnp.pad(k, pad).reshape(bsz, nb + 2, BLOCK, C_KV_HEADS, C_HEAD)
    vp = jnp.pad(v, pad).reshape(bsz, nb + 2, BLOCK, C_KV_HEADS, C_HEAD)

    def band(t):
        return jnp.moveaxis(jnp.concatenate([t[:, :-2], t[:, 1:-1], t[:, 2:]], axis=2), 1, 0)

    qpos = jnp.arange(nb)[:, None, None] * BLOCK + jnp.arange(BLOCK)[None, :, None]
    kpos = jnp.arange(nb)[:, None, None] * BLOCK - BLOCK + jnp.arange(3 * BLOCK)[None, None, :]
    valid = (jnp.abs(qpos - kpos) <= WINDOW) & (kpos >= 0) & (kpos < t_len)
    valid = jnp.concatenate([valid, jnp.ones((nb, BLOCK, ctx_k.shape[1]), dtype=bool)], axis=-1)
    sink = sink.reshape(C_KV_HEADS, C_GROUP)

    def blk(args):
        qi, ki, vi, mi = args
        return _attend(qi, jnp.concatenate([ki, ctx_k], axis=1), jnp.concatenate([vi, ctx_v], axis=1), mi, sink)

    o = lax.map(blk, (_query_blocks(q), band(kp), band(vp), valid))
    return _gated_out(_merge_blocks(o), gate, w_out)


def setup_inputs(seed: int = 0) -> dict:
    key = jax.random.key(seed)
    ks = iter(jax.random.split(key, 32))
    f32 = jnp.float32

    def nrm(shape, scale):
        return jax.random.normal(next(ks), shape, f32) * scale

    def unif(shape, lo, hi):
        return jax.random.uniform(next(ks), shape, f32, lo, hi)

    d = D_MODEL
    return {
        'x_prompt': nrm((BATCH, SEQ, d), 1.0),
        'x_sample': nrm((DEC_BATCH, DEC_SEQ, d), 1.0),
        'c': nrm((DEC_BATCH, d), 1.0),
        'state_wkv': nrm((DEC_BATCH, N_EVEN, 2, A_HEADS, A_HEAD, A_HEAD), 0.5),
        'cache_k': nrm((DEC_BATCH, N_ODD, PAST_LEN, C_KV_HEADS, C_HEAD), 1.0),
        'cache_v': nrm((DEC_BATCH, N_ODD, PAST_LEN, C_KV_HEADS, C_HEAD), 1.0),
        'c_ctx': nrm((d,), 1.0),
        'mod_w': nrm((DEPTH, d, 3 * d), 0.5 * d ** -0.5),
        'mod_b': nrm((DEPTH, 3 * d), 0.01),
        'norm_pre': 1.0 + nrm((DEPTH, d), 0.05),
        'norm_post': 1.0 + nrm((DEPTH, d), 0.05),
        'even_w_in': nrm((N_EVEN, d, EVEN_IN), d ** -0.5),
        'even_mu': unif((N_EVEN, SHIFT_COLS), 0.0, 1.0),
        'even_w0': unif((N_EVEN, 2, A_WIDTH), -4.0, 1.0),
        'even_w_up': nrm((N_EVEN, 2, LORA, A_WIDTH), 0.5 * LORA ** -0.5),
        'even_a0': nrm((N_EVEN, 2, A_WIDTH), 0.5),
        'even_a_up': nrm((N_EVEN, 2, LORA, A_WIDTH), 0.5 * LORA ** -0.5),
        'even_k_k': 0.85 + nrm((N_EVEN, A_WIDTH), 0.05),
        'even_k_a': 1.0 + nrm((N_EVEN, A_WIDTH), 0.05),
        'even_r_k': nrm((N_EVEN, A_HEADS, A_HEAD), 0.1),
        'even_gn_w': 1.0 + nrm((N_EVEN, A_WIDTH), 0.05),
        'even_gn_b': nrm((N_EVEN, A_WIDTH), 0.01),
        'even_w_out': nrm((N_EVEN, d, d), d ** -0.5),
        'odd_w_in': nrm((N_ODD, d, ODD_IN), d ** -0.5),
        'odd_sink': nrm((N_ODD, C_HEADS), 1.0),
        'odd_w_out': nrm((N_ODD, d, d), d ** -0.5),
    }


def reference(x_prompt, x_sample, c, state_wkv, cache_k, cache_v, c_ctx, mod_w, mod_b, norm_pre, norm_post,
              even_w_in, even_mu, even_w0, even_w_up, even_a0, even_a_up, even_k_k, even_k_a, even_r_k,
              even_gn_w, even_gn_b, even_w_out, odd_w_in, odd_sink, odd_w_out):
    xp, xs = x_prompt, x_sample
    new_wkv, new_k, new_v = [], [], []
    for layer in range(DEPTH):
        i = layer // 2
        sh_p, sc_p, g_p = _modulation(c_ctx, mod_w[layer], mod_b[layer])
        sh_s, sc_s, g_s = _modulation(c, mod_w[layer], mod_b[layer])
        hp = _rmsnorm(xp, norm_pre[layer]) * (1.0 + sc_p) + sh_p
        hs = _rmsnorm(xs, norm_pre[layer]) * (1.0 + sc_s) + sh_s
        if layer % 2 == 0:
            p = (even_w_in[i], even_mu[i], even_w0[i], even_w_up[i], even_a0[i], even_a_up[i],
                 even_k_k[i], even_k_a[i], even_r_k[i], even_gn_w[i], even_gn_b[i], even_w_out[i])
            s0 = jnp.zeros((2, xp.shape[0], A_HEADS, A_HEAD, A_HEAD), jnp.float32)
            yp, s_ctx = _rwkv_fourier_mixer(hp, s0, *p)
            ys, _ = _rwkv_fourier_mixer(hs, jnp.moveaxis(state_wkv[:, i], 1, 0), *p)
            new_wkv.append(jnp.moveaxis(s_ctx, 0, 1))
        else:
            yp, kc, vc = _attention_context(hp, odd_w_in[i], odd_sink[i], odd_w_out[i])
            ys = _attention_latent(hs, cache_k[:, i], cache_v[:, i], odd_w_in[i], odd_sink[i], odd_w_out[i])
            new_k.append(kc)
            new_v.append(vc)
        xp = xp + g_p * _rmsnorm(yp, norm_post[layer])
        xs = xs + g_s * _rmsnorm(ys, norm_post[layer])
    new_state_wkv = jnp.stack(new_wkv, axis=1)
    new_cache_k = jnp.stack(new_k, axis=1)
    new_cache_v = jnp.stack(new_v, axis=1)
    return (xp, xs, new_state_wkv, new_cache_k, new_cache_v)
```

```python
import functools

import numpy as np
import jax
import jax.numpy as jnp
from jax import lax
from jax.experimental import pallas as pl
from jax.experimental.pallas import tpu as pltpu

F32 = jnp.float32
BF16 = jnp.bfloat16

D_MODEL = 2048
BATCH = 32
SEQ = 256
DEPTH = 4
DEC_BATCH = 8
DEC_SEQ = 2048
PAST_LEN = 256
GRID_W = 64
RMS_EPS = 1e-6
A_WIDTH = 1024
A_HEAD = 64
A_HEADS = 16
LORA = 64
GN_EPS = 64e-5
B_WIDTH = 1024
B_GROUPS = 4
B_GROUP_CH = 256
SHIFT_COLS = 3 * A_WIDTH + 4 * LORA
EVEN_IN = SHIFT_COLS + A_WIDTH + 2 * B_WIDTH
C_HEAD = 64
C_HEADS = 32
C_KV_HEADS = 8
C_GROUP = 4
WINDOW = 128
BLOCK = 128
ROPE_BASE = 10000.0
ODD_IN = (C_HEADS + 2 * C_KV_HEADS) * C_HEAD + D_MODEL
NEG_INF = -1e30

M_PROMPT = BATCH * SEQ
M_SAMPLE = DEC_BATCH * DEC_SEQ
M_ALL = M_PROMPT + M_SAMPLE
N_COND = 16
LANES = 128
CHAINS = LANES
CW = 256
TM_IN = 512
TN_IN = 1280
TM_ROW = 256
WKV_TT = 32
VMEM_LIMIT = 48 * 1024 * 1024


def _cparams(sem):
    return pltpu.CompilerParams(dimension_semantics=sem, vmem_limit_bytes=VMEM_LIMIT)


def _cond_of_row(row0):
    return jnp.where(row0 < M_PROMPT, 0, 1 + (row0 - M_PROMPT) // DEC_SEQ)


def _sigmoid(x):
    return 1.0 / (1.0 + jnp.exp(-x))


def _silu(x):
    return x * _sigmoid(x)


def _split(a):
    hi = a.astype(BF16)
    lo = (a - hi.astype(F32)).astype(BF16)
    return hi, lo


def _dot(a, b):
    return jnp.dot(a, b, preferred_element_type=F32)


def _dot3(a, b):
    ah, al = _split(a)
    bh, bl = _split(b)
    return _dot(ah, bh) + (_dot(ah, bl) + _dot(al, bh))


def _gsum(x, gmat):
    xh, xl = _split(x)
    return _dot(xh, gmat) + _dot(xl, gmat)


def _mod_kernel(c_ref, w_ref, b_ref, o_ref):
    o_ref[...] = _dot3(_silu(c_ref[...]), w_ref[...]) + b_ref[...]


def _modulation(conds, mod_w, mod_b):
    tn = 512
    n = 3 * D_MODEL
    return pl.pallas_call(
        _mod_kernel,
        out_shape=jax.ShapeDtypeStruct((DEPTH, N_COND, n), F32),
        grid=(DEPTH, n // tn),
        in_specs=[
            pl.BlockSpec((N_COND, D_MODEL), lambda l, j: (0, 0)),
            pl.BlockSpec((None, D_MODEL, tn), lambda l, j: (l, 0, j)),
            pl.BlockSpec((None, 1, tn), lambda l, j: (l, 0, j)),
        ],
        out_specs=pl.BlockSpec((None, N_COND, tn), lambda l, j: (l, 0, j)),
        compiler_params=_cparams(("parallel", "parallel")),
        name="modulation",
    )(conds, mod_w, mod_b.reshape(DEPTH, 1, n))


def _in_kernel(x_ref, g_ref, sh_ref, sc_ref, w_ref, o_ref, h_ref):
    @pl.when(pl.program_id(1) == 0)
    def _():
        x = x_ref[...]
        ms = jnp.mean(x * x, axis=-1, keepdims=True)
        y = x * lax.rsqrt(ms + RMS_EPS) * g_ref[...]
        h_ref[...] = (y * (1.0 + sc_ref[...]) + sh_ref[...]).astype(BF16)

    o_ref[...] = _dot(h_ref[...], w_ref[...])


def _in_proj(x, norm_g, mods4, w_bf16, layer, li):
    n = w_bf16.shape[-1]
    cond = lambda i: _cond_of_row(i * TM_IN)
    return pl.pallas_call(
        _in_kernel,
        out_shape=jax.ShapeDtypeStruct((M_ALL, n), F32),
        grid=(M_ALL // TM_IN, n // TN_IN),
        in_specs=[
            pl.BlockSpec((TM_IN, D_MODEL), lambda i, j: (i, 0)),
            pl.BlockSpec((None, 1, D_MODEL), lambda i, j: (layer, 0, 0)),
            pl.BlockSpec((None, None, 1, D_MODEL), lambda i, j: (layer, cond(i), 0, 0)),
            pl.BlockSpec((None, None, 1, D_MODEL), lambda i, j: (layer, cond(i), 0, 1)),
            pl.BlockSpec((None, D_MODEL, TN_IN), lambda i, j: (li, 0, j)),
        ],
        out_specs=pl.BlockSpec((TM_IN, TN_IN), lambda i, j: (i, j)),
        scratch_shapes=[pltpu.VMEM((TM_IN, D_MODEL), BF16)],
        compiler_params=_cparams(("parallel", "arbitrary")),
        name="in_proj",
    )(x, norm_g, mods4, mods4, w_bf16)


def _token_shift(cur, prev8, next8, mu, first, last):
    tm = cur.shape[0]
    rows = lax.broadcasted_iota(jnp.int32, cur.shape, 0)
    prow = jnp.where(first, 0.0, prev8[7:8, :])
    nrow = jnp.where(last, 0.0, next8[0:1, :])
    up = jnp.where(rows == 0, prow, pltpu.roll(cur, 1, axis=0))
    dn = jnp.where(rows == tm - 1, nrow, pltpu.roll(cur, tm - 1, axis=0))
    return cur + mu * (0.5 * (up + dn) - cur)


def _prep_kernel(r_ref, k_ref, v_ref, lo_ref, rp_ref, kp_ref, vp_ref, lp_ref, rn_ref, kn_ref, vn_ref, ln_ref,
                 mur_ref, muk_ref, muv_ref, mul_ref, w0_ref, wup_ref, a0_ref, aup_ref, kkw_ref, ka_ref, rk_ref,
                 gm_ref,
                 r_o, kk_o, v_o, wf_o, wb_o, kkaf_o, kkab_o, kdf_o, kdb_o, bon_o):
    i = pl.program_id(0)
    tiles_per_seq = DEC_SEQ // TM_ROW
    j = i - M_PROMPT // TM_ROW
    first = jnp.logical_or(j < 0, lax.rem(j, tiles_per_seq) == 0)
    last = jnp.logical_or(j < 0, lax.rem(j, tiles_per_seq) == tiles_per_seq - 1)

    r = _token_shift(r_ref[...], rp_ref[...], rn_ref[...], mur_ref[...], first, last)
    k = _token_shift(k_ref[...], kp_ref[...], kn_ref[...], muk_ref[...], first, last)
    v = _token_shift(v_ref[...], vp_ref[...], vn_ref[...], muv_ref[...], first, last)
    low = _token_shift(lo_ref[...], lp_ref[...], ln_ref[...], mul_ref[...], first, last)
    low_t = jnp.tanh(low)
    gm = gm_ref[...]

    kk = k * kkw_ref[...]
    nrm = jnp.sqrt(_gsum(kk * kk, gm))
    kk = kk / jnp.maximum(nrm, 1e-12)
    ka = ka_ref[...]
    kd_sum = None
    outs = ((wf_o, kkaf_o, kdf_o), (wb_o, kkab_o, kdb_o))
    for d in range(2):
        w_raw = w0_ref[d:d + 1, :] + _dot3(low_t, wup_ref[d])
        z = -w_raw
        sp = jnp.maximum(z, 0.0) + jnp.log(1.0 + jnp.exp(-jnp.abs(z)))
        decay = jnp.exp(-jnp.exp(-sp - 0.5))
        a = _sigmoid(a0_ref[d:d + 1, :] + _dot3(low, aup_ref[d]))
        kd = k * (1.0 + (a - 1.0) * ka)
        w_o, kka_o, kd_o = outs[d]
        w_o[...] = decay
        kka_o[...] = kk * a
        kd_o[...] = kd
        kd_sum = kd if kd_sum is None else kd_sum + kd
    r_o[...] = r
    kk_o[...] = kk
    v_o[...] = v
    bon_o[...] = _gsum(r * (0.5 * kd_sum) * rk_ref[...], gm) * v


def _wkv_prep(proj, mu, w0, wup_pad, a0, aup_pad, k_k, k_a, r_k, gmat):
    tm = TM_ROW
    nb8 = M_ALL // 8
    ncb = A_WIDTH // CW
    low_cb = 3 * A_WIDTH // CW
    assert 4 * LORA == CW

    def main(off):
        return pl.BlockSpec((tm, CW), lambda i, c: (i, off + c))

    def prev(off):
        return pl.BlockSpec((8, CW), lambda i, c: (jnp.maximum(i * (tm // 8) - 1, 0), off + c))

    def nxt(off):
        return pl.BlockSpec((8, CW), lambda i, c: (jnp.minimum((i + 1) * (tm // 8), nb8 - 1), off + c))

    def fixed(spec_fn):
        return [spec_fn(0), spec_fn(ncb), spec_fn(2 * ncb)]

    low_main = pl.BlockSpec((tm, CW), lambda i, c: (i, low_cb))
    low_prev = pl.BlockSpec((8, CW), lambda i, c: (jnp.maximum(i * (tm // 8) - 1, 0), low_cb))
    low_next = pl.BlockSpec((8, CW), lambda i, c: (jnp.minimum((i + 1) * (tm // 8), nb8 - 1), low_cb))

    def vec(off):
        return pl.BlockSpec((1, CW), lambda i, c: (0, off + c))

    in_specs = (fixed(main) + [low_main] + fixed(prev) + [low_prev] + fixed(nxt) + [low_next]
                + [vec(0), vec(ncb), vec(2 * ncb), pl.BlockSpec((1, CW), lambda i, c: (0, low_cb))]
                + [pl.BlockSpec((2, CW), lambda i, c: (0, c)),
                   pl.BlockSpec((2, CW, CW), lambda i, c: (0, 0, c)),
                   pl.BlockSpec((2, CW), lambda i, c: (0, c)),
                   pl.BlockSpec((2, CW, CW), lambda i, c: (0, 0, c)),
                   vec(0), vec(0), vec(0),
                   pl.BlockSpec((CW, CW), lambda i, c: (0, 0))])
    out_spec = pl.BlockSpec((tm, CW), lambda i, c: (i, c))
    out_sds = jax.ShapeDtypeStruct((M_ALL, A_WIDTH), F32)
    return pl.pallas_call(
        _prep_kernel,
        out_shape=[out_sds] * 10,
        grid=(M_ALL // tm, ncb),
        in_specs=in_specs,
        out_specs=[out_spec] * 10,
        compiler_params=_cparams(("parallel", "parallel")),
        name="wkv_prep",
    )(*([proj] * 12), mu, mu, mu, mu, w0, wup_pad, a0, aup_pad, k_k, k_a, r_k, gmat)


def _wkv_kernel(r_ref, kk_ref, v_ref, w_ref, kka_ref, kd_ref, s0_ref, o_ref, sf_ref, s_ref, *, tt_steps):
    d = pl.program_id(0)
    c = pl.program_id(2)

    @pl.when(c == 0)
    def _():
        s_ref[...] = s0_ref[...]

    def time_step(tt, carry):
        t = d * (tt_steps - 1) + (1 - 2 * d) * tt

        def row_block(ib, carry2):
            i0 = pl.multiple_of(ib * 8, 8)
            v8 = v_ref[t, pl.ds(i0, 8), :]
            rows = []
            for q in range(8):
                s = s_ref[i0 + q]
                sa = jnp.sum(s * kk_ref[t], axis=0, keepdims=True)
                sn = s * w_ref[t] - sa * kka_ref[t] + v8[q:q + 1, :] * kd_ref[t]
                s_ref[i0 + q] = sn
                rows.append(jnp.sum(sn * r_ref[t], axis=0, keepdims=True))
            o_ref[t, pl.ds(i0, 8), :] = jnp.concatenate(rows, axis=0)
            return carry2

        return lax.fori_loop(0, A_HEAD // 8, row_block, carry)

    lax.fori_loop(0, tt_steps, time_step, 0)

    @pl.when(c == pl.num_programs(2) - 1)
    def _():
        sf_ref[...] = s_ref[...]


def _wkv(r, kk, v, w, kka, kd, s0):
    g_n, t_n = r.shape[0], r.shape[1]
    tt = WKV_TT
    nc = t_n // tt

    def tidx(d, c):
        return c + d * (nc - 1 - 2 * c)

    shared = pl.BlockSpec((None, tt, A_HEAD, CHAINS), lambda d, g, c: (g, tidx(d, c), 0, 0))
    perdir = pl.BlockSpec((None, None, tt, A_HEAD, CHAINS), lambda d, g, c: (d, g, tidx(d, c), 0, 0))
    state = pl.BlockSpec((None, None, A_HEAD, A_HEAD, CHAINS), lambda d, g, c: (d, g, 0, 0, 0))
    return pl.pallas_call(
        functools.partial(_wkv_kernel, tt_steps=tt),
        out_shape=[jax.ShapeDtypeStruct((2, g_n, t_n, A_HEAD, CHAINS), F32),
                   jax.ShapeDtypeStruct((2, g_n, A_HEAD, A_HEAD, CHAINS), F32)],
        grid=(2, g_n, nc),
        in_specs=[shared, shared, shared, perdir, perdir, perdir, state],
        out_specs=[perdir, state],
        scratch_shapes=[pltpu.VMEM((A_HEAD, A_HEAD, CHAINS), F32)],
        compiler_params=_cparams(("parallel", "parallel", "arbitrary")),
        name="wkv",
    )(r, kk, v, w, kka, kd, s0)


def _to_chains(a):
    p = a[:M_PROMPT].reshape(BATCH // 8, 8, SEQ, A_HEADS, A_HEAD)
    p = p.transpose(0, 2, 4, 1, 3).reshape(BATCH // 8, SEQ, A_HEAD, CHAINS)
    s = a[M_PROMPT:].reshape(DEC_BATCH, DEC_SEQ, A_HEADS, A_HEAD)
    s = s.transpose(1, 3, 0, 2).reshape(1, DEC_SEQ, A_HEAD, CHAINS)
    return p, s


def _from_chains(op, os):
    p = op.reshape(2, BATCH // 8, SEQ, A_HEAD, 8, A_HEADS).transpose(0, 1, 4, 2, 5, 3).reshape(2, M_PROMPT, A_WIDTH)
    s = os.reshape(2, DEC_SEQ, A_HEAD, DEC_BATCH, A_HEADS).transpose(0, 3, 1, 4, 2).reshape(2, M_SAMPLE, A_WIDTH)
    return jnp.concatenate([p, s], axis=1)


def _ya_kernel(of_ref, ob_ref, bon_ref, gate_ref, gw_ref, gb_ref, gm_ref, o_ref):
    gm = gm_ref[...]
    o = of_ref[...] + ob_ref[...]
    mean = _gsum(o, gm) * (1.0 / A_HEAD)
    dev = o - mean
    var = _gsum(dev * dev, gm) * (1.0 / A_HEAD)
    y = dev * lax.rsqrt(var + GN_EPS) * gw_ref[...] + gb_ref[...]
    o_ref[...] = ((y + bon_ref[...]) * _silu(gate_ref[...])).astype(BF16)


def _ya(o2, bonus, proj, gn_w, gn_b, gmat):
    tm = TM_ROW
    gate_cb = (SHIFT_COLS) // CW
    blk = pl.BlockSpec((tm, CW), lambda i, c: (i, c))
    vec = pl.BlockSpec((1, CW), lambda i, c: (0, c))
    return pl.pallas_call(
        _ya_kernel,
        out_shape=jax.ShapeDtypeStruct((M_ALL, A_WIDTH), BF16),
        grid=(M_ALL // tm, A_WIDTH // CW),
        in_specs=[pl.BlockSpec((None, tm, CW), lambda i, c: (0, i, c)),
                  pl.BlockSpec((None, tm, CW), lambda i, c: (1, i, c)),
                  blk,
                  pl.BlockSpec((tm, CW), lambda i, c: (i, gate_cb + c)),
                  vec, vec,
                  pl.BlockSpec((CW, CW), lambda i, c: (0, 0))],
        out_specs=blk,
        compiler_params=_cparams(("parallel", "parallel")),
        name="wkv_post",
    )(o2, o2, bonus, proj, gn_w, gn_b, gmat)


def _dft_ch_kernel(u_ref, cs_ref, zc_ref, zs_ref):
    z = _dot(u_ref[...].astype(BF16), cs_ref[...])
    zc_ref[...] = z[:, :B_GROUP_CH].astype(BF16)
    zs_ref[...] = z[:, B_GROUP_CH:].astype(BF16)


def _dft_channels(proj, cs_mat):
    tm = 512
    u_cb = (SHIFT_COLS + A_WIDTH) // B_GROUP_CH
    blk = pl.BlockSpec((tm, B_GROUP_CH), lambda i, g: (i, g))
    sds = jax.ShapeDtypeStruct((M_ALL, B_WIDTH), BF16)
    return pl.pallas_call(
        _dft_ch_kernel,
        out_shape=[sds, sds],
        grid=(M_ALL // tm, B_GROUPS),
        in_specs=[pl.BlockSpec((tm, B_GROUP_CH), lambda i, g: (i, u_cb + g)),
                  pl.BlockSpec((B_GROUP_CH, 2 * B_GROUP_CH), lambda i, g: (0, 0))],
        out_specs=[blk, blk],
        compiler_params=_cparams(("parallel", "parallel")),
        name="dft_channels",
    )(proj, cs_mat)


def _dft_time_kernel(c_ref, s_ref, zc_ref, zs_ref, gate_ref, o_ref, *, scale):
    acc = _dot(c_ref[...], zc_ref[...]) - _dot(s_ref[...], zs_ref[...])
    o_ref[...] = (acc * scale * _silu(gate_ref[...])).astype(BF16)


def _dft_time(zc, zs, proj, cmat, smat, t_len, n_seq, row0):
    tm = min(t_len, 512)
    mt = t_len // tm
    gate_cb = (SHIFT_COLS + A_WIDTH + B_WIDTH) // B_GROUP_CH
    seq0 = row0 // t_len
    tile0 = row0 // tm
    scale = 1.0 / float(np.sqrt(t_len * B_GROUP_CH))
    return pl.pallas_call(
        functools.partial(_dft_time_kernel, scale=scale),
        out_shape=jax.ShapeDtypeStruct((n_seq * t_len, B_WIDTH), BF16),
        grid=(n_seq, mt, B_GROUPS),
        in_specs=[pl.BlockSpec((tm, t_len), lambda b, m, g: (m, 0)),
                  pl.BlockSpec((tm, t_len), lambda b, m, g: (m, 0)),
                  pl.BlockSpec((t_len, B_GROUP_CH), lambda b, m, g: (seq0 + b, g)),
                  pl.BlockSpec((t_len, B_GROUP_CH), lambda b, m, g: (seq0 + b, g)),
                  pl.BlockSpec((tm, B_GROUP_CH), lambda b, m, g: (tile0 + b * mt + m, gate_cb + g))],
        out_specs=pl.BlockSpec((tm, B_GROUP_CH), lambda b, m, g: (b * mt + m, g)),
        compiler_params=_cparams(("parallel", "parallel", "parallel")),
        name="dft_time",
    )(cmat, smat, zc, zs, proj)


def _dft_mats(n):
    idx = jnp.arange(n, dtype=jnp.int32)
    prod = (idx[:, None] * idx[None, :]) % n
    ang = prod.astype(F32) * (2.0 * np.pi / n)
    return jnp.cos(ang), jnp.sin(ang)


def _out_kernel(a1_ref, a2_ref, w1_ref, w2_ref, x_ref, gate_ref, g_ref, o_ref):
    y = _dot(a1_ref[...], w1_ref[...]) + _dot(a2_ref[...], w2_ref[...])
    ms = jnp.mean(y * y, axis=-1, keepdims=True)
    yn = y * lax.rsqrt(ms + RMS_EPS) * g_ref[...]
    o_ref[...] = x_ref[...] + gate_ref[...] * yn


def _out_proj(a1, a1_cb, a2, a2_cb, w_bf16, li, x, mods4, norm_g, layer):
    tm = TM_ROW
    half = D_MODEL // 2
    cond = lambda i: _cond_of_row(i * tm)
    return pl.pallas_call(
        _out_kernel,
        out_shape=jax.ShapeDtypeStruct((M_ALL, D_MODEL), F32),
        grid=(M_ALL // tm,),
        in_specs=[pl.BlockSpec((tm, half), lambda i: (i, a1_cb)),
                  pl.BlockSpec((tm, half), lambda i: (i, a2_cb)),
                  pl.BlockSpec((None, half, D_MODEL), lambda i: (li, 0, 0)),
                  pl.BlockSpec((None, half, D_MODEL), lambda i: (li, 1, 0)),
                  pl.BlockSpec((tm, D_MODEL), lambda i: (i, 0)),
                  pl.BlockSpec((None, None, 1, D_MODEL), lambda i: (layer, cond(i), 0, 2)),
                  pl.BlockSpec((None, 1, D_MODEL), lambda i: (layer, 0, 0))],
        out_specs=pl.BlockSpec((tm, D_MODEL), lambda i: (i, 0)),
        compiler_params=_cparams(("parallel",)),
        name="out_proj",
    )(a1, a2, w_bf16, w_bf16, x, mods4, norm_g)


def _softmax_pv(scores, values, sink):
    m = sink
    for s in scores:
        m = jnp.maximum(m, jnp.max(s, axis=-1, keepdims=True))
    den = jnp.exp(sink - m)
    acc = None
    for s, v in zip(scores, values):
        p = jnp.exp(s - m)
        den = den + jnp.sum(p, axis=-1, keepdims=True)
        pv = _dot(p.astype(BF16), v)
        acc = pv if acc is None else acc + pv
    return acc / den


def _qk(q, k):
    return lax.dot_general(q, k, (((1,), (1,)), ((), ())), preferred_element_type=F32)


def _attn_ctx_kernel(sink_ref, q_ref, k_ref, v_ref, gate_ref, o_ref):
    kp = pl.program_id(1)
    q = q_ref[...] * (C_HEAD ** -0.5)
    k = k_ref[...]
    v = v_ref[...]
    outs = []
    for hh in range(2):
        kh = k[:, hh * C_HEAD:(hh + 1) * C_HEAD].astype(BF16)
        vh = v[:, hh * C_HEAD:(hh + 1) * C_HEAD].astype(BF16)
        for g in range(C_GROUP):
            c0 = (hh * C_GROUP + g) * C_HEAD
            qh = q[:, c0:c0 + C_HEAD].astype(BF16)
            sink = sink_ref[(kp * 2 + hh) * C_GROUP + g]
            outs.append(_softmax_pv([_qk(qh, kh)], [vh], sink))
    o = jnp.concatenate(outs, axis=-1)
    o_ref[...] = (o * _silu(gate_ref[...])).astype(BF16)


def _attn_context(proj, sink):
    qw = 2 * C_GROUP * C_HEAD
    k_cb = C_HEADS * C_HEAD // LANES
    v_cb = (C_HEADS + C_KV_HEADS) * C_HEAD // LANES
    gate_cb = (C_HEADS + 2 * C_KV_HEADS) * C_HEAD // qw
    return pl.pallas_call(
        _attn_ctx_kernel,
        out_shape=jax.ShapeDtypeStruct((M_PROMPT, D_MODEL), BF16),
        grid=(BATCH, C_KV_HEADS // 2),
        in_specs=[pl.BlockSpec(memory_space=pltpu.SMEM),
                  pl.BlockSpec((SEQ, qw), lambda b, kp: (b, kp)),
                  pl.BlockSpec((SEQ, LANES), lambda b, kp: (b, k_cb + kp)),
                  pl.BlockSpec((SEQ, LANES), lambda b, kp: (b, v_cb + kp)),
                  pl.BlockSpec((SEQ, qw), lambda b, kp: (b, gate_cb + kp))],
        out_specs=pl.BlockSpec((SEQ, qw), lambda b, kp: (b, kp)),
        compiler_params=_cparams(("parallel", "parallel")),
        name="attn_context",
    )(sink, proj, proj, proj, proj)


def _rope(x, cos, sin_signed):
    lane = lax.broadcasted_iota(jnp.int32, x.shape, 1)
    first = (lane & 31) < 16
    partner = jnp.where(first, pltpu.roll(x, LANES - 16, axis=1), pltpu.roll(x, 16, axis=1))
    return x * cos + partner * sin_signed


def _attn_lat_kernel(sink_ref, q_ref, kp_ref, ko_ref, kn_ref, vp_ref, vo_ref, vn_ref, ck_ref, cv_ref,
                     cq_ref, sq_ref, cp_ref, sp_ref, cn_ref, sn_ref, gate_ref, o_ref):
    kpair = pl.program_id(1)
    qb = pl.program_id(2)
    cq = cq_ref[...]
    sq = sq_ref[...]
    q = q_ref[...] * (C_HEAD ** -0.5)
    qr = [_rope(q[:, n * LANES:(n + 1) * LANES], cq, sq) for n in range(4)]
    kband = jnp.concatenate([_rope(kp_ref[...], cp_ref[...], sp_ref[...]),
                             _rope(ko_ref[...], cq, sq),
                             _rope(kn_ref[...], cn_ref[...], sn_ref[...])], axis=0)
    vband = jnp.concatenate([vp_ref[...], vo_ref[...], vn_ref[...]], axis=0)
    ck = ck_ref[...]
    cv = cv_ref[...]
    qpos = qb * BLOCK + lax.broadcasted_iota(jnp.int32, (BLOCK, 3 * BLOCK), 0)
    kpos = (qb - 1) * BLOCK + lax.broadcasted_iota(jnp.int32, (BLOCK, 3 * BLOCK), 1)
    valid = (jnp.abs(qpos - kpos) <= WINDOW) & (kpos >= 0) & (kpos < DEC_SEQ)
    outs = []
    for hh in range(2):
        sl = slice(hh * C_HEAD, (hh + 1) * C_HEAD)
        kb = kband[:, sl].astype(BF16)
        vb = vband[:, sl].astype(BF16)
        ckh = ck[:, sl].astype(BF16)
        cvh = cv[:, sl].astype(BF16)
        for g in range(C_GROUP):
            n = hh * C_GROUP + g
            qh = qr[n // 2][:, (n % 2) * C_HEAD:(n % 2 + 1) * C_HEAD].astype(BF16)
            sink = sink_ref[(kpair * 2 + hh) * C_GROUP + g]
            s_band = jnp.where(valid, _qk(qh, kb), NEG_INF)
            outs.append(_softmax_pv([s_band, _qk(qh, ckh)], [vb, cvh], sink))
    o = jnp.concatenate(outs, axis=-1)
    o_ref[...] = (o * _silu(gate_ref[...])).astype(BF16)


def _attn_latent(proj, cache_k4, cache_v4, li, sink, cos_t, sin_t):
    qw = 2 * C_GROUP * C_HEAD
    k_cb = C_HEADS * C_HEAD // LANES
    v_cb = (C_HEADS + C_KV_HEADS) * C_HEAD // LANES
    gate_cb = (C_HEADS + 2 * C_KV_HEADS) * C_HEAD // qw
    nqb = DEC_SEQ // BLOCK
    rb0 = M_PROMPT // BLOCK

    def rows(delta):
        def idx(b, kp, qb):
            return rb0 + b * nqb + jnp.clip(qb + delta, 0, nqb - 1)
        return idx

    def kv_spec(cb, delta):
        r = rows(delta)
        return pl.BlockSpec((BLOCK, LANES), lambda b, kp, qb: (r(b, kp, qb), cb + kp))

    def tab_spec(delta):
        return pl.BlockSpec((BLOCK, LANES), lambda b, kp, qb: (jnp.clip(qb + delta, 0, nqb - 1), 0))

    cache_spec = pl.BlockSpec((None, None, PAST_LEN, LANES), lambda b, kp, qb: (b, li, 0, kp))
    own = rows(0)
    return pl.pallas_call(
        _attn_lat_kernel,
        out_shape=jax.ShapeDtypeStruct((M_SAMPLE, D_MODEL), BF16),
        grid=(DEC_BATCH, C_KV_HEADS // 2, nqb),
        in_specs=[pl.BlockSpec(memory_space=pltpu.SMEM),
                  pl.BlockSpec((BLOCK, qw), lambda b, kp, qb: (own(b, kp, qb), kp)),
                  kv_spec(k_cb, -1), kv_spec(k_cb, 0), kv_spec(k_cb, 1),
                  kv_spec(v_cb, -1), kv_spec(v_cb, 0), kv_spec(v_cb, 1),
                  cache_spec, cache_spec,
                  tab_spec(0), tab_spec(0), tab_spec(-1), tab_spec(-1), tab_spec(1), tab_spec(1),
                  pl.BlockSpec((BLOCK, qw), lambda b, kp, qb: (own(b, kp, qb), gate_cb + kp))],
        out_specs=pl.BlockSpec((BLOCK, qw), lambda b, kp, qb: (b * nqb + qb, kp)),
        compiler_params=_cparams(("parallel", "parallel", "parallel")),
        name="attn_latent",
    )(sink, proj, proj, proj, proj, proj, proj, proj, cache_k4, cache_v4,
      cos_t, sin_t, cos_t, sin_t, cos_t, sin_t, proj)


def _rope_tables():
    t = jnp.arange(DEC_SEQ, dtype=jnp.int32)
    row = (t // GRID_W).astype(F32)
    col = (t % GRID_W).astype(F32)
    nf = C_HEAD // 4
    inv = 1.0 / (ROPE_BASE ** (jnp.arange(nf, dtype=F32) / nf))
    lane = np.arange(LANES)
    f_of_lane = lane % nf
    use_col = (lane % C_HEAD) >= C_HEAD // 2
    sign = np.where((lane % 32) < 16, -1.0, 1.0).astype(np.float32)
    pos = jnp.where(jnp.asarray(use_col)[None, :], col[:, None], row[:, None])
    ang = pos * inv[jnp.asarray(f_of_lane)][None, :]
    return jnp.cos(ang), jnp.sin(ang) * jnp.asarray(sign)[None, :]


def kernel(x_prompt, x_sample, c, state_wkv, cache_k, cache_v, c_ctx, mod_w, mod_b, norm_pre, norm_post,
           even_w_in, even_mu, even_w0, even_w_up, even_a0, even_a_up, even_k_k, even_k_a, even_r_k,
           even_gn_w, even_gn_b, even_w_out, odd_w_in, odd_sink, odd_w_out):
    n_even = even_w_in.shape[0]
    n_odd = odd_w_in.shape[0]
    x = jnp.concatenate([x_prompt.reshape(M_PROMPT, D_MODEL), x_sample.reshape(M_SAMPLE, D_MODEL)], axis=0)
    conds = jnp.concatenate([c_ctx[None, :], c, jnp.zeros((N_COND - 1 - DEC_BATCH, D_MODEL), F32)], axis=0)
    mods4 = _modulation(conds, mod_w, mod_b).reshape(DEPTH, N_COND, 1, 3 * D_MODEL)
    norm_pre3 = norm_pre.reshape(DEPTH, 1, D_MODEL)
    norm_post3 = norm_post.reshape(DEPTH, 1, D_MODEL)

    even_w_in_b = even_w_in.astype(BF16)
    even_w_out_b = even_w_out.astype(BF16)
    odd_w_in_b = odd_w_in.astype(BF16)
    odd_w_out_b = odd_w_out.astype(BF16)

    head_of_lane = np.arange(CW) // A_HEAD
    gmat = jnp.asarray((head_of_lane[:, None] == head_of_lane[None, :]).astype(np.float32)).astype(BF16)
    c_ch, s_ch = _dft_mats(B_GROUP_CH)
    cs_mat = jnp.concatenate([c_ch, s_ch], axis=1).astype(BF16)
    dft_p = tuple(m.astype(BF16) for m in _dft_mats(SEQ))
    dft_s = tuple(m.astype(BF16) for m in _dft_mats(DEC_SEQ))
    cos_t, sin_t = _rope_tables()
    cache_k4 = cache_k.reshape(DEC_BATCH, n_odd, PAST_LEN, C_KV_HEADS * C_HEAD)
    cache_v4 = cache_v.reshape(DEC_BATCH, n_odd, PAST_LEN, C_KV_HEADS * C_HEAD)

    new_wkv, new_k, new_v = [], [], []
    for layer in range(DEPTH):
        i = layer // 2
        if layer % 2 == 0:
            proj = _in_proj(x, norm_pre3, mods4, even_w_in_b, layer, i)
            zeros = jnp.zeros((LORA, A_WIDTH), F32)
            wup_pad = jnp.stack([jnp.concatenate([even_w_up[i, 0], zeros, zeros, zeros], axis=0),
                                 jnp.concatenate([zeros, even_w_up[i, 1], zeros, zeros], axis=0)])
            aup_pad = jnp.stack([jnp.concatenate([zeros, zeros, even_a_up[i, 0], zeros], axis=0),
                                 jnp.concatenate([zeros, zeros, zeros, even_a_up[i, 1]], axis=0)])
            r, kk, v, wf, wb, kkaf, kkab, kdf, kdb, bonus = _wkv_prep(
                proj, even_mu[i][None, :], even_w0[i], wup_pad, even_a0[i], aup_pad,
                even_k_k[i][None, :], even_k_a[i][None, :], even_r_k[i].reshape(1, A_WIDTH), gmat)
            rp, rs = _to_chains(r)
            kkp, kks = _to_chains(kk)
            vp, vs = _to_chains(v)
            pd = [_to_chains(a) for a in (wf, wb, kkaf, kkab, kdf, kdb)]
            wp, ws = jnp.stack([pd[0][0], pd[1][0]]), jnp.stack([pd[0][1], pd[1][1]])
            kkap, kkas = jnp.stack([pd[2][0], pd[3][0]]), jnp.stack([pd[2][1], pd[3][1]])
            kdp, kds = jnp.stack([pd[4][0], pd[5][0]]), jnp.stack([pd[4][1], pd[5][1]])
            s0p = jnp.zeros((2, BATCH // 8, A_HEAD, A_HEAD, CHAINS), F32)
            s0s = state_wkv[:, i].transpose(1, 3, 4, 0, 2).reshape(2, 1, A_HEAD, A_HEAD, CHAINS)
            op, sfp = _wkv(rp, kkp, vp, wp, kkap, kdp, s0p)
            os, _ = _wkv(rs, kks, vs, ws, kkas, kds, s0s)
            sfp = sfp.reshape(2, BATCH // 8, A_HEAD, A_HEAD, 8, A_HEADS).transpose(1, 4, 0, 5, 2, 3)
            new_wkv.append(sfp.reshape(BATCH, 2, A_HEADS, A_HEAD, A_HEAD))
            o2 = _from_chains(op, os)
            ya = _ya(o2, bonus, proj, even_gn_w[i][None, :], even_gn_b[i][None, :], gmat)
            zc, zs = _dft_channels(proj, cs_mat)
            yb_p = _dft_time(zc, zs, proj, dft_p[0], dft_p[1], SEQ, BATCH, 0)
            yb_s = _dft_time(zc, zs, proj, dft_s[0], dft_s[1], DEC_SEQ, DEC_BATCH, M_PROMPT)
            yb = jnp.concatenate([yb_p, yb_s], axis=0)
            x = _out_proj(ya, 0, yb, 0, even_w_out_b, i, x, mods4, norm_post3, layer)
        else:
            proj = _in_proj(x, norm_pre3, mods4, odd_w_in_b, layer, i)
            kv0 = C_HEADS * C_HEAD
            kvn = C_KV_HEADS * C_HEAD
            new_k.append(proj[:M_PROMPT, kv0:kv0 + kvn].reshape(BATCH, SEQ, C_KV_HEADS, C_HEAD))
            new_v.append(proj[:M_PROMPT, kv0 + kvn:kv0 + 2 * kvn].reshape(BATCH, SEQ, C_KV_HEADS, C_HEAD))
            o_p = _attn_context(proj, odd_sink[i])
            o_s = _attn_latent(proj, cache_k4, cache_v4, i, odd_sink[i], cos_t, sin_t)
            att = jnp.concatenate([o_p, o_s], axis=0)
            x = _out_proj(att, 0, att, 1, odd_w_out_b, i, x, mods4, norm_post3, layer)

    y_prompt = x[:M_PROMPT].reshape(BATCH, SEQ, D_MODEL)
    y_sample = x[M_PROMPT:].reshape(DEC_BATCH, DEC_SEQ, D_MODEL)
    return (y_prompt, y_sample, jnp.stack(new_wkv, axis=1), jnp.stack(new_k, axis=1), jnp.stack(new_v, axis=1))
```

```python
import collections
import functools

import numpy as np
import jax
import jax.numpy as jnp
from jax import lax
from jax.experimental import pallas as pl
from jax.experimental.pallas import tpu as pltpu

F32 = jnp.float32
BF16 = jnp.bfloat16

D_MODEL = 2048
BATCH = 32
SEQ = 256
DEPTH = 4
DEC_BATCH = 8
DEC_SEQ = 2048
PAST_LEN = 256
GRID_W = 64
RMS_EPS = 1e-6
A_WIDTH = 1024
A_HEAD = 64
A_HEADS = 16
LORA = 64
GN_EPS = 64e-5
B_WIDTH = 1024
B_GROUPS = 4
B_GROUP_CH = 256
SHIFT_COLS = 3 * A_WIDTH + 4 * LORA
EVEN_IN = SHIFT_COLS + A_WIDTH + 2 * B_WIDTH
C_HEAD = 64
C_HEADS = 32
C_KV_HEADS = 8
C_GROUP = 4
WINDOW = 128
BLOCK = 128
ROPE_BASE = 10000.0
ODD_IN = (C_HEADS + 2 * C_KV_HEADS) * C_HEAD + D_MODEL
NEG_INF = -1e30

N_COND = 16
SUBLANES = 8
LANES = 128
CHAINS = LANES
SEQ_PER_GROUP = CHAINS // A_HEADS
CW = 256
TM_IN = 512
TN_IN = 1280
TM_ROW = 256
WKV_TT = 32
WKV_ROWS = 16
WKV_ACCS = 2
VMEM_LIMIT = 48 * 1024 * 1024

Stream = collections.namedtuple("Stream", "rows t_len n_seq cond0 cond_per_seq")
PROMPT = Stream(BATCH * SEQ, SEQ, BATCH, 0, 0)
SAMPLE = Stream(DEC_BATCH * DEC_SEQ, DEC_SEQ, DEC_BATCH, 1, 1)


def _cparams(sem):
    return pltpu.CompilerParams(dimension_semantics=sem, vmem_limit_bytes=VMEM_LIMIT)


def _cond_of_row(st, row0):
    return st.cond0 + st.cond_per_seq * (row0 // st.t_len)


def _sigmoid(x):
    return 1.0 / (1.0 + jnp.exp(-x))


def _silu(x):
    return x * _sigmoid(x)


def _split(a):
    hi = a.astype(BF16)
    lo = (a - hi.astype(F32)).astype(BF16)
    return hi, lo


def _dot(a, b):
    return jnp.dot(a, b, preferred_element_type=F32)


def _dot3(a, b):
    ah, al = _split(a)
    bh, bl = _split(b)
    return _dot(ah, bh) + (_dot(ah, bl) + _dot(al, bh))


def _gsum(x, gmat):
    xh, xl = _split(x)
    return _dot(xh, gmat) + _dot(xl, gmat)


def _mod_kernel(c_ref, w_ref, b_ref, o_ref):
    o_ref[...] = _dot3(_silu(c_ref[...]), w_ref[...]) + b_ref[...]


def _modulation(conds, mod_w, mod_b):
    tn = 512
    n = 3 * D_MODEL
    return pl.pallas_call(
        _mod_kernel,
        out_shape=jax.ShapeDtypeStruct((DEPTH, N_COND, n), F32),
        grid=(DEPTH, n // tn),
        in_specs=[
            pl.BlockSpec((N_COND, D_MODEL), lambda l, j: (0, 0)),
            pl.BlockSpec((None, D_MODEL, tn), lambda l, j: (l, 0, j)),
            pl.BlockSpec((None, 1, tn), lambda l, j: (l, 0, j)),
        ],
        out_specs=pl.BlockSpec((None, N_COND, tn), lambda l, j: (l, 0, j)),
        compiler_params=_cparams(("parallel", "parallel")),
        name="modulation",
    )(conds, mod_w, mod_b.reshape(DEPTH, 1, n))


def _in_kernel(x_ref, g_ref, sh_ref, sc_ref, w_ref, o_ref, h_ref):
    @pl.when(pl.program_id(1) == 0)
    def _():
        x = x_ref[...]
        ms = jnp.mean(x * x, axis=-1, keepdims=True)
        y = x * lax.rsqrt(ms + RMS_EPS) * g_ref[...]
        h_ref[...] = (y * (1.0 + sc_ref[...]) + sh_ref[...]).astype(BF16)

    o_ref[...] = _dot(h_ref[...], w_ref[...])


def _in_proj(st, x, norm_g, mods4, w_bf16, layer, li):
    n = w_bf16.shape[-1]
    cond = lambda i: _cond_of_row(st, i * TM_IN)
    return pl.pallas_call(
        _in_kernel,
        out_shape=jax.ShapeDtypeStruct((st.rows, n), F32),
        grid=(st.rows // TM_IN, n // TN_IN),
        in_specs=[
            pl.BlockSpec((TM_IN, D_MODEL), lambda i, j: (i, 0)),
            pl.BlockSpec((None, 1, D_MODEL), lambda i, j: (layer, 0, 0)),
            pl.BlockSpec((None, None, 1, D_MODEL), lambda i, j: (layer, cond(i), 0, 0)),
            pl.BlockSpec((None, None, 1, D_MODEL), lambda i, j: (layer, cond(i), 0, 1)),
            pl.BlockSpec((None, D_MODEL, TN_IN), lambda i, j: (li, 0, j)),
        ],
        out_specs=pl.BlockSpec((TM_IN, TN_IN), lambda i, j: (i, j)),
        scratch_shapes=[pltpu.VMEM((TM_IN, D_MODEL), BF16)],
        compiler_params=_cparams(("parallel", "arbitrary")),
        name="in_proj",
    )(x, norm_g, mods4, mods4, w_bf16)


def _token_shift(cur, prev8, next8, mu, first, last):
    tm = cur.shape[0]
    rows = lax.broadcasted_iota(jnp.int32, cur.shape, 0)
    prow = jnp.where(first, 0.0, prev8[7:8, :])
    nrow = jnp.where(last, 0.0, next8[0:1, :])
    up = jnp.where(rows == 0, prow, pltpu.roll(cur, 1, axis=0))
    dn = jnp.where(rows == tm - 1, nrow, pltpu.roll(cur, tm - 1, axis=0))
    return cur + mu * (0.5 * (up + dn) - cur)


def _prep_kernel(r_ref, k_ref, v_ref, lo_ref, rp_ref, kp_ref, vp_ref, lp_ref, rn_ref, kn_ref, vn_ref, ln_ref,
                 mur_ref, muk_ref, muv_ref, mul_ref, w0_ref, wup_ref, a0_ref, aup_ref, kkw_ref, ka_ref, rk_ref,
                 gm_ref,
                 r_o, kk_o, v_o, w_o, kka_o, kd_o, bon_o, *, tiles_per_seq):
    i = pl.program_id(0)
    first = lax.rem(i, tiles_per_seq) == 0
    last = lax.rem(i, tiles_per_seq) == tiles_per_seq - 1

    r = _token_shift(r_ref[...], rp_ref[...], rn_ref[...], mur_ref[...], first, last)
    k = _token_shift(k_ref[...], kp_ref[...], kn_ref[...], muk_ref[...], first, last)
    v = _token_shift(v_ref[...], vp_ref[...], vn_ref[...], muv_ref[...], first, last)
    low = _token_shift(lo_ref[...], lp_ref[...], ln_ref[...], mul_ref[...], first, last)
    low_t = jnp.tanh(low)
    gm = gm_ref[...]

    kk = k * kkw_ref[...]
    nrm = jnp.sqrt(_gsum(kk * kk, gm))
    kk = kk / jnp.maximum(nrm, 1e-12)
    ka = ka_ref[...]
    kd_sum = None
    for d in range(2):
        w_raw = w0_ref[d:d + 1, :] + _dot3(low_t, wup_ref[d])
        z = -w_raw
        sp = jnp.maximum(z, 0.0) + jnp.log(1.0 + jnp.exp(-jnp.abs(z)))
        a = _sigmoid(a0_ref[d:d + 1, :] + _dot3(low, aup_ref[d]))
        kd = k * (1.0 + (a - 1.0) * ka)
        w_o[d] = jnp.exp(-jnp.exp(-sp - 0.5))
        kka_o[d] = kk * a
        kd_o[d] = kd
        kd_sum = kd if kd_sum is None else kd_sum + kd
    r_o[...] = r
    kk_o[...] = kk
    v_o[...] = v
    bon_o[...] = _gsum(r * (0.5 * kd_sum) * rk_ref[...], gm) * v


def _wkv_prep(st, proj, mu, w0, wup_pad, a0, aup_pad, k_k, k_a, r_k, gmat):
    tm = TM_ROW
    nb8 = st.rows // SUBLANES
    ncb = A_WIDTH // CW
    low_cb = 3 * A_WIDTH // CW
    assert 4 * LORA == CW and st.t_len % tm == 0

    def main(off):
        return pl.BlockSpec((tm, CW), lambda i, c: (i, off + c))

    def prev(off):
        return pl.BlockSpec((SUBLANES, CW), lambda i, c: (jnp.maximum(i * (tm // SUBLANES) - 1, 0), off + c))

    def nxt(off):
        return pl.BlockSpec((SUBLANES, CW),
                            lambda i, c: (jnp.minimum((i + 1) * (tm // SUBLANES), nb8 - 1), off + c))

    def fixed(spec_fn):
        return [spec_fn(0), spec_fn(ncb), spec_fn(2 * ncb)]

    low_main = pl.BlockSpec((tm, CW), lambda i, c: (i, low_cb))
    low_prev = pl.BlockSpec((SUBLANES, CW), lambda i, c: (jnp.maximum(i * (tm // SUBLANES) - 1, 0), low_cb))
    low_next = pl.BlockSpec((SUBLANES, CW),
                            lambda i, c: (jnp.minimum((i + 1) * (tm // SUBLANES), nb8 - 1), low_cb))

    def vec(off):
        return pl.BlockSpec((1, CW), lambda i, c: (0, off + c))

    in_specs = (fixed(main) + [low_main] + fixed(prev) + [low_prev] + fixed(nxt) + [low_next]
                + [vec(0), vec(ncb), vec(2 * ncb), pl.BlockSpec((1, CW), lambda i, c: (0, low_cb))]
                + [pl.BlockSpec((2, CW), lambda i, c: (0, c)),
                   pl.BlockSpec((2, CW, CW), lambda i, c: (0, 0, c)),
                   pl.BlockSpec((2, CW), lambda i, c: (0, c)),
                   pl.BlockSpec((2, CW, CW), lambda i, c: (0, 0, c)),
                   vec(0), vec(0), vec(0),
                   pl.BlockSpec((CW, CW), lambda i, c: (0, 0))])
    one = pl.BlockSpec((tm, CW), lambda i, c: (i, c))
    two = pl.BlockSpec((2, tm, CW), lambda i, c: (0, i, c))
    one_sds = jax.ShapeDtypeStruct((st.rows, A_WIDTH), F32)
    two_sds = jax.ShapeDtypeStruct((2, st.rows, A_WIDTH), F32)
    return pl.pallas_call(
        functools.partial(_prep_kernel, tiles_per_seq=st.t_len // tm),
        out_shape=[one_sds, one_sds, one_sds, two_sds, two_sds, two_sds, one_sds],
        grid=(st.rows // tm, ncb),
        in_specs=in_specs,
        out_specs=[one, one, one, two, two, two, one],
        compiler_params=_cparams(("parallel", "parallel")),
        name="wkv_prep",
    )(*([proj] * 12), mu, mu, mu, mu, w0, wup_pad, a0, aup_pad, k_k, k_a, r_k, gmat)


def _wkv_kernel(r_ref, kk_ref, v_ref, w_ref, kka_ref, kd_ref, s0_ref, o_ref, sf_ref, s_ref, *, tt_steps):
    d = pl.program_id(0)
    c = pl.program_id(2)
    nblk = WKV_ROWS // SUBLANES

    @pl.when(c == 0)
    def _():
        s_ref[...] = s0_ref[...]

    def bcast_row(ref, t, j):
        return ref[t, pl.ds(j, SUBLANES, stride=0), :]

    def time_step(tt, carry):
        t = d * (tt_steps - 1) + (1 - 2 * d) * tt

        for ib in range(A_HEAD // WKV_ROWS):
            i0 = ib * WKV_ROWS
            rows = [pl.ds(i0 + SUBLANES * b, SUBLANES) for b in range(nblk)]
            sa = [[None] * WKV_ACCS for _ in range(nblk)]
            for j in range(A_HEAD):
                kkj = bcast_row(kk_ref, t, j)
                for b in range(nblk):
                    p = s_ref[j, rows[b], :] * kkj
                    sa[b][j % WKV_ACCS] = p if sa[b][j % WKV_ACCS] is None else sa[b][j % WKV_ACCS] + p
            sa = [functools.reduce(lambda x, y: x + y, parts) for parts in sa]
            v8 = [v_ref[t, rows[b], :] for b in range(nblk)]
            out = [[None] * WKV_ACCS for _ in range(nblk)]
            for j in range(A_HEAD):
                wj = bcast_row(w_ref, t, j)
                kkaj = bcast_row(kka_ref, t, j)
                kdj = bcast_row(kd_ref, t, j)
                rj = bcast_row(r_ref, t, j)
                for b in range(nblk):
                    sn = s_ref[j, rows[b], :] * wj - sa[b] * kkaj + v8[b] * kdj
                    s_ref[j, rows[b], :] = sn
                    q = sn * rj
                    out[b][j % WKV_ACCS] = q if out[b][j % WKV_ACCS] is None else out[b][j % WKV_ACCS] + q
            for b in range(nblk):
                o_ref[t, rows[b], :] = functools.reduce(lambda x, y: x + y, out[b])
        return carry

    lax.fori_loop(0, tt_steps, time_step, 0)

    @pl.when(c == pl.num_programs(2) - 1)
    def _():
        sf_ref[...] = s_ref[...]


def _wkv(r, kk, v, w, kka, kd, s0):
    g_n, t_n = r.shape[0], r.shape[1]
    tt = WKV_TT
    nc = t_n // tt

    def tidx(d, c):
        return c + d * (nc - 1 - 2 * c)

    shared = pl.BlockSpec((None, tt, A_HEAD, CHAINS), lambda d, g, c: (g, tidx(d, c), 0, 0))
    perdir = pl.BlockSpec((None, None, tt, A_HEAD, CHAINS), lambda d, g, c: (d, g, tidx(d, c), 0, 0))
    state = pl.BlockSpec((None, None, A_HEAD, A_HEAD, CHAINS), lambda d, g, c: (d, g, 0, 0, 0))
    return pl.pallas_call(
        functools.partial(_wkv_kernel, tt_steps=tt),
        out_shape=[jax.ShapeDtypeStruct((2, g_n, t_n, A_HEAD, CHAINS), F32),
                   jax.ShapeDtypeStruct((2, g_n, A_HEAD, A_HEAD, CHAINS), F32)],
        grid=(2, g_n, nc),
        in_specs=[shared, shared, shared, perdir, perdir, perdir, state],
        out_specs=[perdir, state],
        scratch_shapes=[pltpu.VMEM((A_HEAD, A_HEAD, CHAINS), F32)],
        compiler_params=_cparams(("parallel", "parallel", "arbitrary")),
        name="wkv",
    )(r, kk, v, w, kka, kd, s0)


def _to_chains(st, a):
    lead = a.shape[:-2]
    n = len(lead)
    g_n = st.n_seq // SEQ_PER_GROUP
    a = a.reshape(lead + (g_n, SEQ_PER_GROUP, st.t_len, A_HEADS, A_HEAD))
    perm = tuple(range(n)) + (n, n + 2, n + 4, n + 1, n + 3)
    return a.transpose(perm).reshape(lead + (g_n, st.t_len, A_HEAD, CHAINS))


def _from_chains(st, o):
    g_n = st.n_seq // SEQ_PER_GROUP
    o = o.reshape(2, g_n, st.t_len, A_HEAD, SEQ_PER_GROUP, A_HEADS)
    return o.transpose(0, 1, 4, 2, 5, 3).reshape(2, st.rows, A_WIDTH)


def _ya_kernel(of_ref, ob_ref, bon_ref, gate_ref, gw_ref, gb_ref, gm_ref, o_ref):
    gm = gm_ref[...]
    o = of_ref[...] + ob_ref[...]
    mean = _gsum(o, gm) * (1.0 / A_HEAD)
    dev = o - mean
    var = _gsum(dev * dev, gm) * (1.0 / A_HEAD)
    y = dev * lax.rsqrt(var + GN_EPS) * gw_ref[...] + gb_ref[...]
    o_ref[...] = ((y + bon_ref[...]) * _silu(gate_ref[...])).astype(BF16)


def _ya(st, o2, bonus, proj, gn_w, gn_b, gmat):
    tm = TM_ROW
    gate_cb = (SHIFT_COLS) // CW
    blk = pl.BlockSpec((tm, CW), lambda i, c: (i, c))
    vec = pl.BlockSpec((1, CW), lambda i, c: (0, c))
    return pl.pallas_call(
        _ya_kernel,
        out_shape=jax.ShapeDtypeStruct((st.rows, A_WIDTH), BF16),
        grid=(st.rows // tm, A_WIDTH // CW),
        in_specs=[pl.BlockSpec((None, tm, CW), lambda i, c: (0, i, c)),
                  pl.BlockSpec((None, tm, CW), lambda i, c: (1, i, c)),
                  blk,
                  pl.BlockSpec((tm, CW), lambda i, c: (i, gate_cb + c)),
                  vec, vec,
                  pl.BlockSpec((CW, CW), lambda i, c: (0, 0))],
        out_specs=blk,
        compiler_params=_cparams(("parallel", "parallel")),
        name="wkv_post",
    )(o2, o2, bonus, proj, gn_w, gn_b, gmat)


def _dft_ch_kernel(u_ref, cs_ref, zc_ref, zs_ref):
    z = _dot(u_ref[...].astype(BF16), cs_ref[...])
    zc_ref[...] = z[:, :B_GROUP_CH].astype(BF16)
    zs_ref[...] = z[:, B_GROUP_CH:].astype(BF16)


def _dft_channels(st, proj, cs_mat):
    tm = 512
    u_cb = (SHIFT_COLS + A_WIDTH) // B_GROUP_CH
    blk = pl.BlockSpec((tm, B_GROUP_CH), lambda i, g: (i, g))
    sds = jax.ShapeDtypeStruct((st.rows, B_WIDTH), BF16)
    return pl.pallas_call(
        _dft_ch_kernel,
        out_shape=[sds, sds],
        grid=(st.rows // tm, B_GROUPS),
        in_specs=[pl.BlockSpec((tm, B_GROUP_CH), lambda i, g: (i, u_cb + g)),
                  pl.BlockSpec((B_GROUP_CH, 2 * B_GROUP_CH), lambda i, g: (0, 0))],
        out_specs=[blk, blk],
        compiler_params=_cparams(("parallel", "parallel")),
        name="dft_channels",
    )(proj, cs_mat)


def _dft_time_kernel(c_ref, s_ref, zc_ref, zs_ref, gate_ref, o_ref, *, scale):
    acc = _dot(c_ref[...], zc_ref[...]) - _dot(s_ref[...], zs_ref[...])
    o_ref[...] = (acc * scale * _silu(gate_ref[...])).astype(BF16)


def _dft_time(st, zc, zs, proj, cmat, smat):
    t_len = st.t_len
    tm = min(t_len, 512)
    mt = t_len // tm
    gate_cb = (SHIFT_COLS + A_WIDTH + B_WIDTH) // B_GROUP_CH
    scale = 1.0 / float(np.sqrt(t_len * B_GROUP_CH))
    return pl.pallas_call(
        functools.partial(_dft_time_kernel, scale=scale),
        out_shape=jax.ShapeDtypeStruct((st.rows, B_WIDTH), BF16),
        grid=(st.n_seq, mt, B_GROUPS),
        in_specs=[pl.BlockSpec((tm, t_len), lambda b, m, g: (m, 0)),
                  pl.BlockSpec((tm, t_len), lambda b, m, g: (m, 0)),
                  pl.BlockSpec((t_len, B_GROUP_CH), lambda b, m, g: (b, g)),
                  pl.BlockSpec((t_len, B_GROUP_CH), lambda b, m, g: (b, g)),
                  pl.BlockSpec((tm, B_GROUP_CH), lambda b, m, g: (b * mt + m, gate_cb + g))],
        out_specs=pl.BlockSpec((tm, B_GROUP_CH), lambda b, m, g: (b * mt + m, g)),
        compiler_params=_cparams(("parallel", "parallel", "parallel")),
        name="dft_time",
    )(cmat, smat, zc, zs, proj)


def _dft_mats(n):
    idx = jnp.arange(n, dtype=jnp.int32)
    prod = (idx[:, None] * idx[None, :]) % n
    ang = prod.astype(F32) * (2.0 * np.pi / n)
    return jnp.cos(ang), jnp.sin(ang)


def _out_kernel(a1_ref, a2_ref, w1_ref, w2_ref, x_ref, gate_ref, g_ref, o_ref):
    y = _dot(a1_ref[...], w1_ref[...]) + _dot(a2_ref[...], w2_ref[...])
    ms = jnp.mean(y * y, axis=-1, keepdims=True)
    yn = y * lax.rsqrt(ms + RMS_EPS) * g_ref[...]
    o_ref[...] = x_ref[...] + gate_ref[...] * yn


def _out_proj(st, a1, a1_cb, a2, a2_cb, w_bf16, li, x, mods4, norm_g, layer):
    tm = TM_ROW
    half = D_MODEL // 2
    cond = lambda i: _cond_of_row(st, i * tm)
    return pl.pallas_call(
        _out_kernel,
        out_shape=jax.ShapeDtypeStruct((st.rows, D_MODEL), F32),
        grid=(st.rows // tm,),
        in_specs=[pl.BlockSpec((tm, half), lambda i: (i, a1_cb)),
                  pl.BlockSpec((tm, half), lambda i: (i, a2_cb)),
                  pl.BlockSpec((None, half, D_MODEL), lambda i: (li, 0, 0)),
                  pl.BlockSpec((None, half, D_MODEL), lambda i: (li, 1, 0)),
                  pl.BlockSpec((tm, D_MODEL), lambda i: (i, 0)),
                  pl.BlockSpec((None, None, 1, D_MODEL), lambda i: (layer, cond(i), 0, 2)),
                  pl.BlockSpec((None, 1, D_MODEL), lambda i: (layer, 0, 0))],
        out_specs=pl.BlockSpec((tm, D_MODEL), lambda i: (i, 0)),
        compiler_params=_cparams(("parallel",)),
        name="out_proj",
    )(a1, a2, w_bf16, w_bf16, x, mods4, norm_g)


def _softmax_pv(scores, values, sink):
    m = sink
    for s in scores:
        m = jnp.maximum(m, jnp.max(s, axis=-1, keepdims=True))
    den = jnp.exp(sink - m)
    acc = None
    for s, v in zip(scores, values):
        p = jnp.exp(s - m)
        den = den + jnp.sum(p, axis=-1, keepdims=True)
        pv = _dot(p.astype(BF16), v)
        acc = pv if acc is None else acc + pv
    return acc / den


def _qk(q, k):
    return lax.dot_general(q, k, (((1,), (1,)), ((), ())), preferred_element_type=F32)


def _attn_ctx_kernel(sink_ref, q_ref, k_ref, v_ref, gate_ref, o_ref):
    kp = pl.program_id(1)
    q = q_ref[...] * (C_HEAD ** -0.5)
    k = k_ref[...]
    v = v_ref[...]
    outs = []
    for hh in range(2):
        kh = k[:, hh * C_HEAD:(hh + 1) * C_HEAD].astype(BF16)
        vh = v[:, hh * C_HEAD:(hh + 1) * C_HEAD].astype(BF16)
        for g in range(C_GROUP):
            c0 = (hh * C_GROUP + g) * C_HEAD
            qh = q[:, c0:c0 + C_HEAD].astype(BF16)
            sink = sink_ref[(kp * 2 + hh) * C_GROUP + g]
            outs.append(_softmax_pv([_qk(qh, kh)], [vh], sink))
    o = jnp.concatenate(outs, axis=-1)
    o_ref[...] = (o * _silu(gate_ref[...])).astype(BF16)


def _attn_context(proj, sink):
    qw = 2 * C_GROUP * C_HEAD
    k_cb = C_HEADS * C_HEAD // LANES
    v_cb = (C_HEADS + C_KV_HEADS) * C_HEAD // LANES
    gate_cb = (C_HEADS + 2 * C_KV_HEADS) * C_HEAD // qw
    return pl.pallas_call(
        _attn_ctx_kernel,
        out_shape=jax.ShapeDtypeStruct((PROMPT.rows, D_MODEL), BF16),
        grid=(BATCH, C_KV_HEADS // 2),
        in_specs=[pl.BlockSpec(memory_space=pltpu.SMEM),
                  pl.BlockSpec((SEQ, qw), lambda b, kp: (b, kp)),
                  pl.BlockSpec((SEQ, LANES), lambda b, kp: (b, k_cb + kp)),
                  pl.BlockSpec((SEQ, LANES), lambda b, kp: (b, v_cb + kp)),
                  pl.BlockSpec((SEQ, qw), lambda b, kp: (b, gate_cb + kp))],
        out_specs=pl.BlockSpec((SEQ, qw), lambda b, kp: (b, kp)),
        compiler_params=_cparams(("parallel", "parallel")),
        name="attn_context",
    )(sink, proj, proj, proj, proj)


def _rope(x, cos, sin_signed):
    lane = lax.broadcasted_iota(jnp.int32, x.shape, 1)
    first = (lane & 31) < 16
    partner = jnp.where(first, pltpu.roll(x, LANES - 16, axis=1), pltpu.roll(x, 16, axis=1))
    return x * cos + partner * sin_signed


def _attn_lat_kernel(sink_ref, q_ref, kp_ref, ko_ref, kn_ref, vp_ref, vo_ref, vn_ref, ck_ref, cv_ref,
                     cq_ref, sq_ref, cp_ref, sp_ref, cn_ref, sn_ref, gate_ref, o_ref):
    kpair = pl.program_id(1)
    qb = pl.program_id(2)
    cq = cq_ref[...]
    sq = sq_ref[...]
    q = q_ref[...] * (C_HEAD ** -0.5)
    qr = [_rope(q[:, n * LANES:(n + 1) * LANES], cq, sq) for n in range(4)]
    kband = jnp.concatenate([_rope(kp_ref[...], cp_ref[...], sp_ref[...]),
                             _rope(ko_ref[...], cq, sq),
                             _rope(kn_ref[...], cn_ref[...], sn_ref[...])], axis=0)
    vband = jnp.concatenate([vp_ref[...], vo_ref[...], vn_ref[...]], axis=0)
    ck = ck_ref[...]
    cv = cv_ref[...]
    qpos = qb * BLOCK + lax.broadcasted_iota(jnp.int32, (BLOCK, 3 * BLOCK), 0)
    kpos = (qb - 1) * BLOCK + lax.broadcasted_iota(jnp.int32, (BLOCK, 3 * BLOCK), 1)
    valid = (jnp.abs(qpos - kpos) <= WINDOW) & (kpos >= 0) & (kpos < DEC_SEQ)
    outs = []
    for hh in range(2):
        sl = slice(hh * C_HEAD, (hh + 1) * C_HEAD)
        kb = kband[:, sl].astype(BF16)
        vb = vband[:, sl].astype(BF16)
        ckh = ck[:, sl].astype(BF16)
        cvh = cv[:, sl].astype(BF16)
        for g in range(C_GROUP):
            n = hh * C_GROUP + g
            qh = qr[n // 2][:, (n % 2) * C_HEAD:(n % 2 + 1) * C_HEAD].astype(BF16)
            sink = sink_ref[(kpair * 2 + hh) * C_GROUP + g]
            s_band = jnp.where(valid, _qk(qh, kb), NEG_INF)
            outs.append(_softmax_pv([s_band, _qk(qh, ckh)], [vb, cvh], sink))
    o = jnp.concatenate(outs, axis=-1)
    o_ref[...] = (o * _silu(gate_ref[...])).astype(BF16)


def _attn_latent(proj, cache_k4, cache_v4, li, sink, cos_t, sin_t):
    qw = 2 * C_GROUP * C_HEAD
    k_cb = C_HEADS * C_HEAD // LANES
    v_cb = (C_HEADS + C_KV_HEADS) * C_HEAD // LANES
    gate_cb = (C_HEADS + 2 * C_KV_HEADS) * C_HEAD // qw
    nqb = DEC_SEQ // BLOCK

    def rows(delta):
        def idx(b, kp, qb):
            return b * nqb + jnp.clip(qb + delta, 0, nqb - 1)
        return idx

    def kv_spec(cb, delta):
        r = rows(delta)
        return pl.BlockSpec((BLOCK, LANES), lambda b, kp, qb: (r(b, kp, qb), cb + kp))

    def tab_spec(delta):
        return pl.BlockSpec((BLOCK, LANES), lambda b, kp, qb: (jnp.clip(qb + delta, 0, nqb - 1), 0))

    cache_spec = pl.BlockSpec((None, None, PAST_LEN, LANES), lambda b, kp, qb: (b, li, 0, kp))
    own = rows(0)
    return pl.pallas_call(
        _attn_lat_kernel,
        out_shape=jax.ShapeDtypeStruct((SAMPLE.rows, D_MODEL), BF16),
        grid=(DEC_BATCH, C_KV_HEADS // 2, nqb),
        in_specs=[pl.BlockSpec(memory_space=pltpu.SMEM),
                  pl.BlockSpec((BLOCK, qw), lambda b, kp, qb: (own(b, kp, qb), kp)),
                  kv_spec(k_cb, -1), kv_spec(k_cb, 0), kv_spec(k_cb, 1),
                  kv_spec(v_cb, -1), kv_spec(v_cb, 0), kv_spec(v_cb, 1),
                  cache_spec, cache_spec,
                  tab_spec(0), tab_spec(0), tab_spec(-1), tab_spec(-1), tab_spec(1), tab_spec(1),
                  pl.BlockSpec((BLOCK, qw), lambda b, kp, qb: (own(b, kp, qb), gate_cb + kp))],
        out_specs=pl.BlockSpec((BLOCK, qw), lambda b, kp, qb: (own(b, kp, qb), kp)),
        compiler_params=_cparams(("parallel", "parallel", "parallel")),
        name="attn_latent",
    )(sink, proj, proj, proj, proj, proj, proj, proj, cache_k4, cache_v4,
      cos_t, sin_t, cos_t, sin_t, cos_t, sin_t, proj)


def _rope_tables():
    t = jnp.arange(DEC_SEQ, dtype=jnp.int32)
    row = (t // GRID_W).astype(F32)
    col = (t % GRID_W).astype(F32)
    nf = C_HEAD // 4
    inv = 1.0 / (ROPE_BASE ** (jnp.arange(nf, dtype=F32) / nf))
    lane = np.arange(LANES)
    f_of_lane = lane % nf
    use_col = (lane % C_HEAD) >= C_HEAD // 2
    sign = np.where((lane % 32) < 16, -1.0, 1.0).astype(np.float32)
    pos = jnp.where(jnp.asarray(use_col)[None, :], col[:, None], row[:, None])
    ang = pos * inv[jnp.asarray(f_of_lane)][None, :]
    return jnp.cos(ang), jnp.sin(ang) * jnp.asarray(sign)[None, :]


def _even_mixer(st, proj, s0, p, gmat, cs_mat, dft):
    r, kk, v, w2, kka2, kd2, bonus = _wkv_prep(st, proj, p["mu"], p["w0"], p["wup"], p["a0"], p["aup"],
                                                p["k_k"], p["k_a"], p["r_k"], gmat)
    o, s_fin = _wkv(_to_chains(st, r), _to_chains(st, kk), _to_chains(st, v),
                    _to_chains(st, w2), _to_chains(st, kka2), _to_chains(st, kd2), s0)
    ya = _ya(st, _from_chains(st, o), bonus, proj, p["gn_w"], p["gn_b"], gmat)
    zc, zs = _dft_channels(st, proj, cs_mat)
    yb = _dft_time(st, zc, zs, proj, dft[0], dft[1])
    return ya, yb, s_fin


def kernel(x_prompt, x_sample, c, state_wkv, cache_k, cache_v, c_ctx, mod_w, mod_b, norm_pre, norm_post,
           even_w_in, even_mu, even_w0, even_w_up, even_a0, even_a_up, even_k_k, even_k_a, even_r_k,
           even_gn_w, even_gn_b, even_w_out, odd_w_in, odd_sink, odd_w_out):
    n_odd = odd_w_in.shape[0]
    xs = {PROMPT: x_prompt.reshape(PROMPT.rows, D_MODEL), SAMPLE: x_sample.reshape(SAMPLE.rows, D_MODEL)}
    conds = jnp.concatenate([c_ctx[None, :], c, jnp.zeros((N_COND - 1 - DEC_BATCH, D_MODEL), F32)], axis=0)
    mods4 = _modulation(conds, mod_w, mod_b).reshape(DEPTH, N_COND, 1, 3 * D_MODEL)
    norm_pre3 = norm_pre.reshape(DEPTH, 1, D_MODEL)
    norm_post3 = norm_post.reshape(DEPTH, 1, D_MODEL)

    even_w_in_b = even_w_in.astype(BF16)
    even_w_out_b = even_w_out.astype(BF16)
    odd_w_in_b = odd_w_in.astype(BF16)
    odd_w_out_b = odd_w_out.astype(BF16)

    head_of_lane = np.arange(CW) // A_HEAD
    gmat = jnp.asarray((head_of_lane[:, None] == head_of_lane[None, :]).astype(np.float32)).astype(BF16)
    c_ch, s_ch = _dft_mats(B_GROUP_CH)
    cs_mat = jnp.concatenate([c_ch, s_ch], axis=1).astype(BF16)
    dft = {PROMPT: tuple(m.astype(BF16) for m in _dft_mats(SEQ)),
           SAMPLE: tuple(m.astype(BF16) for m in _dft_mats(DEC_SEQ))}
    cos_t, sin_t = _rope_tables()
    cache_k4 = cache_k.reshape(DEC_BATCH, n_odd, PAST_LEN, C_KV_HEADS * C_HEAD)
    cache_v4 = cache_v.reshape(DEC_BATCH, n_odd, PAST_LEN, C_KV_HEADS * C_HEAD)

    new_wkv, new_k, new_v = [], [], []
    for layer in range(DEPTH):
        i = layer // 2
        if layer % 2 == 0:
            zeros = jnp.zeros((LORA, A_WIDTH), F32)
            p = {
                "mu": even_mu[i][None, :], "w0": even_w0[i], "a0": even_a0[i],
                "wup": jnp.stack([jnp.concatenate([even_w_up[i, 0], zeros, zeros, zeros], axis=0),
                                  jnp.concatenate([zeros, even_w_up[i, 1], zeros, zeros], axis=0)]),
                "aup": jnp.stack([jnp.concatenate([zeros, zeros, even_a_up[i, 0], zeros], axis=0),
                                  jnp.concatenate([zeros, zeros, zeros, even_a_up[i, 1]], axis=0)]),
                "k_k": even_k_k[i][None, :], "k_a": even_k_a[i][None, :], "r_k": even_r_k[i].reshape(1, A_WIDTH),
                "gn_w": even_gn_w[i][None, :], "gn_b": even_gn_b[i][None, :],
            }
            s0 = {PROMPT: jnp.zeros((2, BATCH // SEQ_PER_GROUP, A_HEAD, A_HEAD, CHAINS), F32),
                  SAMPLE: state_wkv[:, i].transpose(1, 4, 3, 0, 2).reshape(2, 1, A_HEAD, A_HEAD, CHAINS)}
            for st in (PROMPT, SAMPLE):
                proj = _in_proj(st, xs[st], norm_pre3, mods4, even_w_in_b, layer, i)
                ya, yb, s_fin = _even_mixer(st, proj, s0[st], p, gmat, cs_mat, dft[st])
                xs[st] = _out_proj(st, ya, 0, yb, 0, even_w_out_b, i, xs[st], mods4, norm_post3, layer)
                if st is PROMPT:
                    s_fin = s_fin.reshape(2, BATCH // SEQ_PER_GROUP, A_HEAD, A_HEAD, SEQ_PER_GROUP, A_HEADS)
                    new_wkv.append(s_fin.transpose(1, 4, 0, 5, 3, 2).reshape(BATCH, 2, A_HEADS, A_HEAD, A_HEAD))
        else:
            proj_p = _in_proj(PROMPT, xs[PROMPT], norm_pre3, mods4, odd_w_in_b, layer, i)
            proj_s = _in_proj(SAMPLE, xs[SAMPLE], norm_pre3, mods4, odd_w_in_b, layer, i)
            kv0 = C_HEADS * C_HEAD
            kvn = C_KV_HEADS * C_HEAD
            new_k.append(proj_p[:, kv0:kv0 + kvn].reshape(BATCH, SEQ, C_KV_HEADS, C_HEAD))
            new_v.append(proj_p[:, kv0 + kvn:kv0 + 2 * kvn].reshape(BATCH, SEQ, C_KV_HEADS, C_HEAD))
            att_p = _attn_context(proj_p, odd_sink[i])
            att_s = _attn_latent(proj_s, cache_k4, cache_v4, i, odd_sink[i], cos_t, sin_t)
            xs[PROMPT] = _out_proj(PROMPT, att_p, 0, att_p, 1, odd_w_out_b, i, xs[PROMPT], mods4, norm_post3, layer)
            xs[SAMPLE] = _out_proj(SAMPLE, att_s, 0, att_s, 1, odd_w_out_b, i, xs[SAMPLE], mods4, norm_post3, layer)

    y_prompt = xs[PROMPT].reshape(BATCH, SEQ, D_MODEL)
    y_sample = xs[SAMPLE].reshape(DEC_BATCH, DEC_SEQ, D_MODEL)
    return (y_prompt, y_sample, jnp.stack(new_wkv, axis=1), jnp.stack(new_k, axis=1), jnp.stack(new_v, axis=1))
```

```python
import collections
import functools

import numpy as np
import jax
import jax.numpy as jnp
from jax import lax
from jax.experimental import pallas as pl
from jax.experimental.pallas import tpu as pltpu

F32 = jnp.float32
BF16 = jnp.bfloat16

D_MODEL = 2048
BATCH = 32
SEQ = 256
DEPTH = 4
DEC_BATCH = 8
DEC_SEQ = 2048
PAST_LEN = 256
GRID_W = 64
RMS_EPS = 1e-6
A_WIDTH = 1024
A_HEAD = 64
A_HEADS = 16
LORA = 64
GN_EPS = 64e-5
B_WIDTH = 1024
B_GROUPS = 4
B_GROUP_CH = 256
SHIFT_COLS = 3 * A_WIDTH + 4 * LORA
EVEN_IN = SHIFT_COLS + A_WIDTH + 2 * B_WIDTH
C_HEAD = 64
C_HEADS = 32
C_KV_HEADS = 8
C_GROUP = 4
WINDOW = 128
BLOCK = 128
ROPE_BASE = 10000.0
ODD_IN = (C_HEADS + 2 * C_KV_HEADS) * C_HEAD + D_MODEL
NEG_INF = -1e30

N_COND = 16
SUBLANES = 8
LANES = 128
CHAINS = LANES
SEQ_PER_GROUP = CHAINS // A_HEADS
CW = 256
TM_IN = 512
TN_IN = 1280
TM_ROW = 256
WKV_TT = 32
WKV_ROWS = 16
WKV_ACCS = 2
VMEM_LIMIT = 48 * 1024 * 1024

Stream = collections.namedtuple("Stream", "rows t_len n_seq cond0 cond_per_seq")
PROMPT = Stream(BATCH * SEQ, SEQ, BATCH, 0, 0)
SAMPLE = Stream(DEC_BATCH * DEC_SEQ, DEC_SEQ, DEC_BATCH, 1, 1)


def _cparams(sem):
    return pltpu.CompilerParams(dimension_semantics=sem, vmem_limit_bytes=VMEM_LIMIT)


def _cond_of_row(st, row0):
    return st.cond0 + st.cond_per_seq * (row0 // st.t_len)


def _sigmoid(x):
    return 1.0 / (1.0 + jnp.exp(-x))


def _silu(x):
    return x * _sigmoid(x)


def _split(a):
    hi = a.astype(BF16)
    lo = (a - hi.astype(F32)).astype(BF16)
    return hi, lo


def _dot(a, b):
    return jnp.dot(a, b, preferred_element_type=F32)


def _dot3(a, b):
    ah, al = _split(a)
    bh, bl = _split(b)
    return _dot(ah, bh) + (_dot(ah, bl) + _dot(al, bh))


def _gsum(x, gmat):
    xh, xl = _split(x)
    return _dot(xh, gmat) + _dot(xl, gmat)


def _mod_kernel(c_ref, w_ref, b_ref, o_ref):
    o_ref[...] = _dot3(_silu(c_ref[...]), w_ref[...]) + b_ref[...]


def _modulation(conds, mod_w, mod_b):
    tn = 512
    n = 3 * D_MODEL
    return pl.pallas_call(
        _mod_kernel,
        out_shape=jax.ShapeDtypeStruct((DEPTH, N_COND, n), F32),
        grid=(DEPTH, n // tn),
        in_specs=[
            pl.BlockSpec((N_COND, D_MODEL), lambda l, j: (0, 0)),
            pl.BlockSpec((None, D_MODEL, tn), lambda l, j: (l, 0, j)),
            pl.BlockSpec((None, 1, tn), lambda l, j: (l, 0, j)),
        ],
        out_specs=pl.BlockSpec((None, N_COND, tn), lambda l, j: (l, 0, j)),
        compiler_params=_cparams(("parallel", "parallel")),
        name="modulation",
    )(conds, mod_w, mod_b.reshape(DEPTH, 1, n))


def _in_kernel(x_ref, g_ref, sh_ref, sc_ref, w_ref, o_ref, h_ref):
    @pl.when(pl.program_id(1) == 0)
    def _():
        x = x_ref[...]
        ms = jnp.mean(x * x, axis=-1, keepdims=True)
        y = x * lax.rsqrt(ms + RMS_EPS) * g_ref[...]
        h_ref[...] = (y * (1.0 + sc_ref[...]) + sh_ref[...]).astype(BF16)

    o_ref[...] = _dot(h_ref[...], w_ref[...])


def _in_proj(st, x, norm_g, mods4, w_bf16, layer, li):
    n = w_bf16.shape[-1]
    cond = lambda i: _cond_of_row(st, i * TM_IN)
    return pl.pallas_call(
        _in_kernel,
        out_shape=jax.ShapeDtypeStruct((st.rows, n), F32),
        grid=(st.rows // TM_IN, n // TN_IN),
        in_specs=[
            pl.BlockSpec((TM_IN, D_MODEL), lambda i, j: (i, 0)),
            pl.BlockSpec((None, 1, D_MODEL), lambda i, j: (layer, 0, 0)),
            pl.BlockSpec((None, None, 1, D_MODEL), lambda i, j: (layer, cond(i), 0, 0)),
            pl.BlockSpec((None, None, 1, D_MODEL), lambda i, j: (layer, cond(i), 0, 1)),
            pl.BlockSpec((None, D_MODEL, TN_IN), lambda i, j: (li, 0, j)),
        ],
        out_specs=pl.BlockSpec((TM_IN, TN_IN), lambda i, j: (i, j)),
        scratch_shapes=[pltpu.VMEM((TM_IN, D_MODEL), BF16)],
        compiler_params=_cparams(("parallel", "arbitrary")),
        name="in_proj",
    )(x, norm_g, mods4, mods4, w_bf16)


def _token_shift(cur, prev8, next8, mu, first, last):
    tm = cur.shape[0]
    rows = lax.broadcasted_iota(jnp.int32, cur.shape, 0)
    prow = jnp.where(first, 0.0, prev8[7:8, :])
    nrow = jnp.where(last, 0.0, next8[0:1, :])
    up = jnp.where(rows == 0, prow, pltpu.roll(cur, 1, axis=0))
    dn = jnp.where(rows == tm - 1, nrow, pltpu.roll(cur, tm - 1, axis=0))
    return cur + mu * (0.5 * (up + dn) - cur)


def _prep_kernel(r_ref, k_ref, v_ref, lo_ref, rp_ref, kp_ref, vp_ref, lp_ref, rn_ref, kn_ref, vn_ref, ln_ref,
                 mur_ref, muk_ref, muv_ref, mul_ref, w0_ref, wup_ref, a0_ref, aup_ref, kkw_ref, ka_ref, rk_ref,
                 gm_ref,
                 r_o, kk_o, v_o, w_o, kka_o, kd_o, bon_o, *, tiles_per_seq):
    i = pl.program_id(0)
    first = lax.rem(i, tiles_per_seq) == 0
    last = lax.rem(i, tiles_per_seq) == tiles_per_seq - 1

    r = _token_shift(r_ref[...], rp_ref[...], rn_ref[...], mur_ref[...], first, last)
    k = _token_shift(k_ref[...], kp_ref[...], kn_ref[...], muk_ref[...], first, last)
    v = _token_shift(v_ref[...], vp_ref[...], vn_ref[...], muv_ref[...], first, last)
    low = _token_shift(lo_ref[...], lp_ref[...], ln_ref[...], mul_ref[...], first, last)
    low_t = jnp.tanh(low)
    gm = gm_ref[...]

    kk = k * kkw_ref[...]
    nrm = jnp.sqrt(_gsum(kk * kk, gm))
    kk = kk / jnp.maximum(nrm, 1e-12)
    ka = ka_ref[...]
    kd_sum = None
    for d in range(2):
        w_raw = w0_ref[d:d + 1, :] + _dot3(low_t, wup_ref[d])
        z = -w_raw
        sp = jnp.maximum(z, 0.0) + jnp.log(1.0 + jnp.exp(-jnp.abs(z)))
        a = _sigmoid(a0_ref[d:d + 1, :] + _dot3(low, aup_ref[d]))
        kd = k * (1.0 + (a - 1.0) * ka)
        w_o[d] = jnp.exp(-jnp.exp(-sp - 0.5))
        kka_o[d] = kk * a
        kd_o[d] = kd
        kd_sum = kd if kd_sum is None else kd_sum + kd
    r_o[...] = r
    kk_o[...] = kk
    v_o[...] = v
    bon_o[...] = _gsum(r * (0.5 * kd_sum) * rk_ref[...], gm) * v


def _wkv_prep(st, proj, mu, w0, wup_pad, a0, aup_pad, k_k, k_a, r_k, gmat):
    tm = TM_ROW
    nb8 = st.rows // SUBLANES
    ncb = A_WIDTH // CW
    low_cb = 3 * A_WIDTH // CW
    assert 4 * LORA == CW and st.t_len % tm == 0

    def main(off):
        return pl.BlockSpec((tm, CW), lambda i, c: (i, off + c))

    def prev(off):
        return pl.BlockSpec((SUBLANES, CW), lambda i, c: (jnp.maximum(i * (tm // SUBLANES) - 1, 0), off + c))

    def nxt(off):
        return pl.BlockSpec((SUBLANES, CW),
                            lambda i, c: (jnp.minimum((i + 1) * (tm // SUBLANES), nb8 - 1), off + c))

    def fixed(spec_fn):
        return [spec_fn(0), spec_fn(ncb), spec_fn(2 * ncb)]

    low_main = pl.BlockSpec((tm, CW), lambda i, c: (i, low_cb))
    low_prev = pl.BlockSpec((SUBLANES, CW), lambda i, c: (jnp.maximum(i * (tm // SUBLANES) - 1, 0), low_cb))
    low_next = pl.BlockSpec((SUBLANES, CW),
                            lambda i, c: (jnp.minimum((i + 1) * (tm // SUBLANES), nb8 - 1), low_cb))

    def vec(off):
        return pl.BlockSpec((1, CW), lambda i, c: (0, off + c))

    in_specs = (fixed(main) + [low_main] + fixed(prev) + [low_prev] + fixed(nxt) + [low_next]
                + [vec(0), vec(ncb), vec(2 * ncb), pl.BlockSpec((1, CW), lambda i, c: (0, low_cb))]
                + [pl.BlockSpec((2, CW), lambda i, c: (0, c)),
                   pl.BlockSpec((2, CW, CW), lambda i, c: (0, 0, c)),
                   pl.BlockSpec((2, CW), lambda i, c: (0, c)),
                   pl.BlockSpec((2, CW, CW), lambda i, c: (0, 0, c)),
                   vec(0), vec(0), vec(0),
                   pl.BlockSpec((CW, CW), lambda i, c: (0, 0))])
    tps = st.t_len // tm
    g_n = st.n_seq // SEQ_PER_GROUP

    def til(i, c):
        s = i // tps
        return (s // SEQ_PER_GROUP, lax.rem(i, tps), lax.rem(s, SEQ_PER_GROUP) * ncb + c)

    one = pl.BlockSpec((tm, CW), lambda i, c: (i, c))
    til1 = pl.BlockSpec((None, tm, CW), lambda i, c: til(i, c))
    til2 = pl.BlockSpec((2, None, tm, CW), lambda i, c: (0,) + til(i, c))
    one_sds = jax.ShapeDtypeStruct((st.rows, A_WIDTH), F32)
    til1_sds = jax.ShapeDtypeStruct((g_n, st.t_len, SEQ_PER_GROUP * A_WIDTH), F32)
    til2_sds = jax.ShapeDtypeStruct((2, g_n, st.t_len, SEQ_PER_GROUP * A_WIDTH), F32)
    return pl.pallas_call(
        functools.partial(_prep_kernel, tiles_per_seq=tps),
        out_shape=[til1_sds, til1_sds, til1_sds, til2_sds, til2_sds, til2_sds, one_sds],
        grid=(st.rows // tm, ncb),
        in_specs=in_specs,
        out_specs=[til1, til1, til1, til2, til2, til2, one],
        compiler_params=_cparams(("parallel", "parallel")),
        name="wkv_prep",
    )(*([proj] * 12), mu, mu, mu, mu, w0, wup_pad, a0, aup_pad, k_k, k_a, r_k, gmat)


def _low_half():
    return lax.broadcasted_iota(jnp.int32, (A_HEAD, LANES), 1) < A_HEAD


def _pair_to_chain_tiles(x_ref, c_ref, tp):
    low = _low_half()
    t0 = tp * 2
    x2 = jnp.concatenate([x_ref[t0 + t2, :, hv * LANES:(hv + 1) * LANES]
                          for t2 in range(2) for hv in range(A_HEADS // 2)], axis=0)
    y = x2.T
    top, bot = y[:A_HEAD], y[A_HEAD:]
    c_ref[t0] = jnp.where(low, top, pltpu.roll(bot, A_HEAD, axis=1))
    c_ref[t0 + 1] = jnp.where(low, pltpu.roll(top, A_HEAD, axis=1), bot)


def _pair_from_chain_tiles(c_ref, o_ref, tp):
    low = _low_half()
    t0 = tp * 2
    d0 = c_ref[0]
    d1 = c_ref[1]
    top = jnp.where(low, d0, pltpu.roll(d1, A_HEAD, axis=1))
    bot = jnp.where(low, pltpu.roll(d0, A_HEAD, axis=1), d1)
    x2 = jnp.concatenate([top, bot], axis=0).T
    for t2 in range(2):
        for hv in range(A_HEADS // 2):
            r0 = (t2 * (A_HEADS // 2) + hv) * SUBLANES
            o_ref[t0 + t2, :, hv * LANES:(hv + 1) * LANES] = x2[r0:r0 + SUBLANES, :]


def _wkv_kernel(rx_ref, kkx_ref, vx_ref, wx_ref, kkax_ref, kdx_ref, s0_ref, ox_ref, sf_ref,
                s_ref, r2_ref, kk2_ref, v2_ref, w2_ref, kka2_ref, kd2_ref, o_ref, *, tt_steps):
    d = pl.program_id(0)
    s = pl.program_id(2)
    nblk = WKV_ROWS // SUBLANES
    n_pairs = tt_steps // 2
    fill = lax.rem(s, 2)
    use = 1 - fill
    srcs = (rx_ref, kkx_ref, vx_ref, wx_ref, kkax_ref, kdx_ref)
    tiles = (r2_ref, kk2_ref, v2_ref, w2_ref, kka2_ref, kd2_ref)
    r_ref, kk_ref, v_ref, w_ref, kka_ref, kd_ref = (ref.at[use] for ref in tiles)

    @pl.when(s == 0)
    def _():
        s_ref[...] = s0_ref[...]

        def first_chunk(tp, carry):
            for x_ref, c_ref in zip(srcs, tiles):
                _pair_to_chain_tiles(x_ref, c_ref.at[0], tp)
            return carry

        lax.fori_loop(0, n_pairs, first_chunk, 0)

    def bcast_row(ref, t, j):
        return ref[t, pl.ds(j, SUBLANES, stride=0), :]

    def time_step(t):
        for ib in range(A_HEAD // WKV_ROWS):
            i0 = ib * WKV_ROWS
            rows = [pl.ds(i0 + SUBLANES * b, SUBLANES) for b in range(nblk)]
            sa = [[None] * WKV_ACCS for _ in range(nblk)]
            for j in range(A_HEAD):
                kkj = bcast_row(kk_ref, t, j)
                for b in range(nblk):
                    p = s_ref[j, rows[b], :] * kkj
                    sa[b][j % WKV_ACCS] = p if sa[b][j % WKV_ACCS] is None else sa[b][j % WKV_ACCS] + p
            sa = [functools.reduce(lambda x, y: x + y, parts) for parts in sa]
            v8 = [v_ref[t, rows[b], :] for b in range(nblk)]
            out = [[None] * WKV_ACCS for _ in range(nblk)]
            for j in range(A_HEAD):
                wj = bcast_row(w_ref, t, j)
                kkaj = bcast_row(kka_ref, t, j)
                kdj = bcast_row(kd_ref, t, j)
                rj = bcast_row(r_ref, t, j)
                for b in range(nblk):
                    sn = s_ref[j, rows[b], :] * wj - sa[b] * kkaj + v8[b] * kdj
                    s_ref[j, rows[b], :] = sn
                    q = sn * rj
                    out[b][j % WKV_ACCS] = q if out[b][j % WKV_ACCS] is None else out[b][j % WKV_ACCS] + q
            for b in range(nblk):
                o_ref[t & 1, rows[b], :] = functools.reduce(lambda x, y: x + y, out[b])

    @pl.when(s > 0)
    def _():
        def pair(tp, carry):
            p = d * (n_pairs - 1) + (1 - 2 * d) * tp
            for t2 in range(2):
                time_step(2 * p + d + (1 - 2 * d) * t2)
            _pair_from_chain_tiles(o_ref, ox_ref, p)
            for x_ref, c_ref in zip(srcs, tiles):
                _pair_to_chain_tiles(x_ref, c_ref.at[fill], tp)
            return carry

        lax.fori_loop(0, n_pairs, pair, 0)

    @pl.when(s == pl.num_programs(2) - 1)
    def _():
        sf_ref[...] = s_ref[...]


def _wkv(r, kk, v, w, kka, kd, s0):
    g_n, t_n = r.shape[0], r.shape[1]
    tt = WKV_TT
    nc = t_n // tt
    split = lambda a: a.reshape(a.shape[:-1] + (SEQ_PER_GROUP, A_WIDTH))

    def tidx(d, c):
        return c + d * (nc - 1 - 2 * c)

    def t_in(d, s):
        return tidx(d, jnp.minimum(s, nc - 1))

    def t_out(d, s):
        return tidx(d, jnp.maximum(s - 1, 0))

    shared = pl.BlockSpec((None, tt, SEQ_PER_GROUP, A_WIDTH), lambda d, g, s: (g, t_in(d, s), 0, 0))
    perdir = pl.BlockSpec((None, None, tt, SEQ_PER_GROUP, A_WIDTH), lambda d, g, s: (d, g, t_in(d, s), 0, 0))
    out = pl.BlockSpec((None, None, tt, SEQ_PER_GROUP, A_WIDTH), lambda d, g, s: (d, g, t_out(d, s), 0, 0))
    state = pl.BlockSpec((None, None, A_HEAD, A_HEAD, CHAINS), lambda d, g, s: (d, g, 0, 0, 0))
    tiles = pltpu.VMEM((2, tt, A_HEAD, CHAINS), F32)
    o, s_fin = pl.pallas_call(
        functools.partial(_wkv_kernel, tt_steps=tt),
        out_shape=[jax.ShapeDtypeStruct((2, g_n, t_n, SEQ_PER_GROUP, A_WIDTH), F32),
                   jax.ShapeDtypeStruct((2, g_n, A_HEAD, A_HEAD, CHAINS), F32)],
        grid=(2, g_n, nc + 1),
        in_specs=[shared, shared, shared, perdir, perdir, perdir, state],
        out_specs=[out, state],
        scratch_shapes=([pltpu.VMEM((A_HEAD, A_HEAD, CHAINS), F32)] + [tiles] * 6
                        + [pltpu.VMEM((2, A_HEAD, CHAINS), F32)]),
        compiler_params=_cparams(("parallel", "parallel", "arbitrary")),
        name="wkv",
    )(split(r), split(kk), split(v), split(w), split(kka), split(kd), s0)
    return o.reshape(2, g_n, t_n, SEQ_PER_GROUP * A_WIDTH), s_fin


def _ya_kernel(of_ref, ob_ref, bon_ref, gate_ref, gw_ref, gb_ref, gm_ref, o_ref):
    gm = gm_ref[...]
    o = of_ref[...] + ob_ref[...]
    mean = _gsum(o, gm) * (1.0 / A_HEAD)
    dev = o - mean
    var = _gsum(dev * dev, gm) * (1.0 / A_HEAD)
    y = dev * lax.rsqrt(var + GN_EPS) * gw_ref[...] + gb_ref[...]
    o_ref[...] = ((y + bon_ref[...]) * _silu(gate_ref[...])).astype(BF16)


def _ya(st, o2, bonus, proj, gn_w, gn_b, gmat):
    tm = TM_ROW
    gate_cb = (SHIFT_COLS) // CW
    blk = pl.BlockSpec((tm, CW), lambda i, c: (i, c))
    vec = pl.BlockSpec((1, CW), lambda i, c: (0, c))
    tps = st.t_len // tm
    ncb = A_WIDTH // CW

    def til(d):
        def idx(i, c):
            s = i // tps
            return (d, s // SEQ_PER_GROUP, lax.rem(i, tps), lax.rem(s, SEQ_PER_GROUP) * ncb + c)
        return pl.BlockSpec((None, None, tm, CW), idx)

    return pl.pallas_call(
        _ya_kernel,
        out_shape=jax.ShapeDtypeStruct((st.rows, A_WIDTH), BF16),
        grid=(st.rows // tm, A_WIDTH // CW),
        in_specs=[til(0), til(1),
                  blk,
                  pl.BlockSpec((tm, CW), lambda i, c: (i, gate_cb + c)),
                  vec, vec,
                  pl.BlockSpec((CW, CW), lambda i, c: (0, 0))],
        out_specs=blk,
        compiler_params=_cparams(("parallel", "parallel")),
        name="wkv_post",
    )(o2, o2, bonus, proj, gn_w, gn_b, gmat)


def _dft_ch_kernel(u_ref, cs_ref, zc_ref, zs_ref):
    z = _dot(u_ref[...].astype(BF16), cs_ref[...])
    zc_ref[...] = z[:, :B_GROUP_CH].astype(BF16)
    zs_ref[...] = z[:, B_GROUP_CH:].astype(BF16)


def _dft_channels(st, proj, cs_mat):
    tm = 512
    u_cb = (SHIFT_COLS + A_WIDTH) // B_GROUP_CH
    blk = pl.BlockSpec((tm, B_GROUP_CH), lambda i, g: (i, g))
    sds = jax.ShapeDtypeStruct((st.rows, B_WIDTH), BF16)
    return pl.pallas_call(
        _dft_ch_kernel,
        out_shape=[sds, sds],
        grid=(st.rows // tm, B_GROUPS),
        in_specs=[pl.BlockSpec((tm, B_GROUP_CH), lambda i, g: (i, u_cb + g)),
                  pl.BlockSpec((B_GROUP_CH, 2 * B_GROUP_CH), lambda i, g: (0, 0))],
        out_specs=[blk, blk],
        compiler_params=_cparams(("parallel", "parallel")),
        name="dft_channels",
    )(proj, cs_mat)


def _dft_time_kernel(c_ref, s_ref, zc_ref, zs_ref, gate_ref, o_ref, *, scale):
    acc = _dot(c_ref[...], zc_ref[...]) - _dot(s_ref[...], zs_ref[...])
    o_ref[...] = (acc * scale * _silu(gate_ref[...])).astype(BF16)


def _dft_time(st, zc, zs, proj, cmat, smat):
    t_len = st.t_len
    tm = min(t_len, 512)
    mt = t_len // tm
    gate_cb = (SHIFT_COLS + A_WIDTH + B_WIDTH) // B_GROUP_CH
    scale = 1.0 / float(np.sqrt(t_len * B_GROUP_CH))
    return pl.pallas_call(
        functools.partial(_dft_time_kernel, scale=scale),
        out_shape=jax.ShapeDtypeStruct((st.rows, B_WIDTH), BF16),
        grid=(st.n_seq, mt, B_GROUPS),
        in_specs=[pl.BlockSpec((tm, t_len), lambda b, m, g: (m, 0)),
                  pl.BlockSpec((tm, t_len), lambda b, m, g: (m, 0)),
                  pl.BlockSpec((t_len, B_GROUP_CH), lambda b, m, g: (b, g)),
                  pl.BlockSpec((t_len, B_GROUP_CH), lambda b, m, g: (b, g)),
                  pl.BlockSpec((tm, B_GROUP_CH), lambda b, m, g: (b * mt + m, gate_cb + g))],
        out_specs=pl.BlockSpec((tm, B_GROUP_CH), lambda b, m, g: (b * mt + m, g)),
        compiler_params=_cparams(("parallel", "parallel", "parallel")),
        name="dft_time",
    )(cmat, smat, zc, zs, proj)


def _dft_mats(n):
    idx = jnp.arange(n, dtype=jnp.int32)
    prod = (idx[:, None] * idx[None, :]) % n
    ang = prod.astype(F32) * (2.0 * np.pi / n)
    return jnp.cos(ang), jnp.sin(ang)


def _out_kernel(a1_ref, a2_ref, w1_ref, w2_ref, x_ref, gate_ref, g_ref, o_ref):
    y = _dot(a1_ref[...], w1_ref[...]) + _dot(a2_ref[...], w2_ref[...])
    ms = jnp.mean(y * y, axis=-1, keepdims=True)
    yn = y * lax.rsqrt(ms + RMS_EPS) * g_ref[...]
    o_ref[...] = x_ref[...] + gate_ref[...] * yn


def _out_proj(st, a1, a1_cb, a2, a2_cb, w_bf16, li, x, mods4, norm_g, layer):
    tm = TM_ROW
    half = D_MODEL // 2
    cond = lambda i: _cond_of_row(st, i * tm)
    return pl.pallas_call(
        _out_kernel,
        out_shape=jax.ShapeDtypeStruct((st.rows, D_MODEL), F32),
        grid=(st.rows // tm,),
        in_specs=[pl.BlockSpec((tm, half), lambda i: (i, a1_cb)),
                  pl.BlockSpec((tm, half), lambda i: (i, a2_cb)),
                  pl.BlockSpec((None, half, D_MODEL), lambda i: (li, 0, 0)),
                  pl.BlockSpec((None, half, D_MODEL), lambda i: (li, 1, 0)),
                  pl.BlockSpec((tm, D_MODEL), lambda i: (i, 0)),
                  pl.BlockSpec((None, None, 1, D_MODEL), lambda i: (layer, cond(i), 0, 2)),
                  pl.BlockSpec((None, 1, D_MODEL), lambda i: (layer, 0, 0))],
        out_specs=pl.BlockSpec((tm, D_MODEL), lambda i: (i, 0)),
        compiler_params=_cparams(("parallel",)),
        name="out_proj",
    )(a1, a2, w_bf16, w_bf16, x, mods4, norm_g)


def _softmax_pv(scores, values, sink):
    m = sink
    for s in scores:
        m = jnp.maximum(m, jnp.max(s, axis=-1, keepdims=True))
    den = jnp.exp(sink - m)
    acc = None
    for s, v in zip(scores, values):
        p = jnp.exp(s - m)
        den = den + jnp.sum(p, axis=-1, keepdims=True)
        pv = _dot(p.astype(BF16), v)
        acc = pv if acc is None else acc + pv
    return acc / den


def _qk(q, k):
    return lax.dot_general(q, k, (((1,), (1,)), ((), ())), preferred_element_type=F32)


def _attn_ctx_kernel(sink_ref, q_ref, k_ref, v_ref, gate_ref, o_ref):
    kp = pl.program_id(1)
    q = q_ref[...] * (C_HEAD ** -0.5)
    k = k_ref[...]
    v = v_ref[...]
    outs = []
    for hh in range(2):
        kh = k[:, hh * C_HEAD:(hh + 1) * C_HEAD].astype(BF16)
        vh = v[:, hh * C_HEAD:(hh + 1) * C_HEAD].astype(BF16)
        for g in range(C_GROUP):
            c0 = (hh * C_GROUP + g) * C_HEAD
            qh = q[:, c0:c0 + C_HEAD].astype(BF16)
            sink = sink_ref[(kp * 2 + hh) * C_GROUP + g]
            outs.append(_softmax_pv([_qk(qh, kh)], [vh], sink))
    o = jnp.concatenate(outs, axis=-1)
    o_ref[...] = (o * _silu(gate_ref[...])).astype(BF16)


def _attn_context(proj, sink):
    qw = 2 * C_GROUP * C_HEAD
    k_cb = C_HEADS * C_HEAD // LANES
    v_cb = (C_HEADS + C_KV_HEADS) * C_HEAD // LANES
    gate_cb = (C_HEADS + 2 * C_KV_HEADS) * C_HEAD // qw
    return pl.pallas_call(
        _attn_ctx_kernel,
        out_shape=jax.ShapeDtypeStruct((PROMPT.rows, D_MODEL), BF16),
        grid=(BATCH, C_KV_HEADS // 2),
        in_specs=[pl.BlockSpec(memory_space=pltpu.SMEM),
                  pl.BlockSpec((SEQ, qw), lambda b, kp: (b, kp)),
                  pl.BlockSpec((SEQ, LANES), lambda b, kp: (b, k_cb + kp)),
                  pl.BlockSpec((SEQ, LANES), lambda b, kp: (b, v_cb + kp)),
                  pl.BlockSpec((SEQ, qw), lambda b, kp: (b, gate_cb + kp))],
        out_specs=pl.BlockSpec((SEQ, qw), lambda b, kp: (b, kp)),
        compiler_params=_cparams(("parallel", "parallel")),
        name="attn_context",
    )(sink, proj, proj, proj, proj)


def _rope(x, cos, sin_signed):
    lane = lax.broadcasted_iota(jnp.int32, x.shape, 1)
    first = (lane & 31) < 16
    partner = jnp.where(first, pltpu.roll(x, LANES - 16, axis=1), pltpu.roll(x, 16, axis=1))
    return x * cos + partner * sin_signed


def _attn_lat_kernel(sink_ref, q_ref, kp_ref, ko_ref, kn_ref, vp_ref, vo_ref, vn_ref, ck_ref, cv_ref,
                     cq_ref, sq_ref, cp_ref, sp_ref, cn_ref, sn_ref, gate_ref, o_ref):
    kpair = pl.program_id(1)
    qb = pl.program_id(2)
    cq = cq_ref[...]
    sq = sq_ref[...]
    q = q_ref[...] * (C_HEAD ** -0.5)
    qr = [_rope(q[:, n * LANES:(n + 1) * LANES], cq, sq) for n in range(4)]
    kband = jnp.concatenate([_rope(kp_ref[...], cp_ref[...], sp_ref[...]),
                             _rope(ko_ref[...], cq, sq),
                             _rope(kn_ref[...], cn_ref[...], sn_ref[...])], axis=0)
    vband = jnp.concatenate([vp_ref[...], vo_ref[...], vn_ref[...]], axis=0)
    ck = ck_ref[...]
    cv = cv_ref[...]
    qpos = qb * BLOCK + lax.broadcasted_iota(jnp.int32, (BLOCK, 3 * BLOCK), 0)
    kpos = (qb - 1) * BLOCK + lax.broadcasted_iota(jnp.int32, (BLOCK, 3 * BLOCK), 1)
    valid = (jnp.abs(qpos - kpos) <= WINDOW) & (kpos >= 0) & (kpos < DEC_SEQ)
    outs = []
    for hh in range(2):
        sl = slice(hh * C_HEAD, (hh + 1) * C_HEAD)
        kb = kband[:, sl].astype(BF16)
        vb = vband[:, sl].astype(BF16)
        ckh = ck[:, sl].astype(BF16)
        cvh = cv[:, sl].astype(BF16)
        for g in range(C_GROUP):
            n = hh * C_GROUP + g
            qh = qr[n // 2][:, (n % 2) * C_HEAD:(n % 2 + 1) * C_HEAD].astype(BF16)
            sink = sink_ref[(kpair * 2 + hh) * C_GROUP + g]
            s_band = jnp.where(valid, _qk(qh, kb), NEG_INF)
            outs.append(_softmax_pv([s_band, _qk(qh, ckh)], [vb, cvh], sink))
    o = jnp.concatenate(outs, axis=-1)
    o_ref[...] = (o * _silu(gate_ref[...])).astype(BF16)


def _attn_latent(proj, cache_k4, cache_v4, li, sink, cos_t, sin_t):
    qw = 2 * C_GROUP * C_HEAD
    k_cb = C_HEADS * C_HEAD // LANES
    v_cb = (C_HEADS + C_KV_HEADS) * C_HEAD // LANES
    gate_cb = (C_HEADS + 2 * C_KV_HEADS) * C_HEAD // qw
    nqb = DEC_SEQ // BLOCK

    def rows(delta):
        def idx(b, kp, qb):
            return b * nqb + jnp.clip(qb + delta, 0, nqb - 1)
        return idx

    def kv_spec(cb, delta):
        r = rows(delta)
        return pl.BlockSpec((BLOCK, LANES), lambda b, kp, qb: (r(b, kp, qb), cb + kp))

    def tab_spec(delta):
        return pl.BlockSpec((BLOCK, LANES), lambda b, kp, qb: (jnp.clip(qb + delta, 0, nqb - 1), 0))

    cache_spec = pl.BlockSpec((None, None, PAST_LEN, LANES), lambda b, kp, qb: (b, li, 0, kp))
    own = rows(0)
    return pl.pallas_call(
        _attn_lat_kernel,
        out_shape=jax.ShapeDtypeStruct((SAMPLE.rows, D_MODEL), BF16),
        grid=(DEC_BATCH, C_KV_HEADS // 2, nqb),
        in_specs=[pl.BlockSpec(memory_space=pltpu.SMEM),
                  pl.BlockSpec((BLOCK, qw), lambda b, kp, qb: (own(b, kp, qb), kp)),
                  kv_spec(k_cb, -1), kv_spec(k_cb, 0), kv_spec(k_cb, 1),
                  kv_spec(v_cb, -1), kv_spec(v_cb, 0), kv_spec(v_cb, 1),
                  cache_spec, cache_spec,
                  tab_spec(0), tab_spec(0), tab_spec(-1), tab_spec(-1), tab_spec(1), tab_spec(1),
                  pl.BlockSpec((BLOCK, qw), lambda b, kp, qb: (own(b, kp, qb), gate_cb + kp))],
        out_specs=pl.BlockSpec((BLOCK, qw), lambda b, kp, qb: (own(b, kp, qb), kp)),
        compiler_params=_cparams(("parallel", "parallel", "parallel")),
        name="attn_latent",
    )(sink, proj, proj, proj, proj, proj, proj, proj, cache_k4, cache_v4,
      cos_t, sin_t, cos_t, sin_t, cos_t, sin_t, proj)


def _rope_tables():
    t = jnp.arange(DEC_SEQ, dtype=jnp.int32)
    row = (t // GRID_W).astype(F32)
    col = (t % GRID_W).astype(F32)
    nf = C_HEAD // 4
    inv = 1.0 / (ROPE_BASE ** (jnp.arange(nf, dtype=F32) / nf))
    lane = np.arange(LANES)
    f_of_lane = lane % nf
    use_col = (lane % C_HEAD) >= C_HEAD // 2
    sign = np.where((lane % 32) < 16, -1.0, 1.0).astype(np.float32)
    pos = jnp.where(jnp.asarray(use_col)[None, :], col[:, None], row[:, None])
    ang = pos * inv[jnp.asarray(f_of_lane)][None, :]
    return jnp.cos(ang), jnp.sin(ang) * jnp.asarray(sign)[None, :]


def _even_mixer(st, proj, s0, p, gmat, cs_mat, dft):
    r, kk, v, w2, kka2, kd2, bonus = _wkv_prep(st, proj, p["mu"], p["w0"], p["wup"], p["a0"], p["aup"],
                                                p["k_k"], p["k_a"], p["r_k"], gmat)
    o, s_fin = _wkv(r, kk, v, w2, kka2, kd2, s0)
    ya = _ya(st, o, bonus, proj, p["gn_w"], p["gn_b"], gmat)
    zc, zs = _dft_channels(st, proj, cs_mat)
    yb = _dft_time(st, zc, zs, proj, dft[0], dft[1])
    return ya, yb, s_fin


def kernel(x_prompt, x_sample, c, state_wkv, cache_k, cache_v, c_ctx, mod_w, mod_b, norm_pre, norm_post,
           even_w_in, even_mu, even_w0, even_w_up, even_a0, even_a_up, even_k_k, even_k_a, even_r_k,
           even_gn_w, even_gn_b, even_w_out, odd_w_in, odd_sink, odd_w_out):
    n_odd = odd_w_in.shape[0]
    xs = {PROMPT: x_prompt.reshape(PROMPT.rows, D_MODEL), SAMPLE: x_sample.reshape(SAMPLE.rows, D_MODEL)}
    conds = jnp.concatenate([c_ctx[None, :], c, jnp.zeros((N_COND - 1 - DEC_BATCH, D_MODEL), F32)], axis=0)
    mods4 = _modulation(conds, mod_w, mod_b).reshape(DEPTH, N_COND, 1, 3 * D_MODEL)
    norm_pre3 = norm_pre.reshape(DEPTH, 1, D_MODEL)
    norm_post3 = norm_post.reshape(DEPTH, 1, D_MODEL)

    even_w_in_b = even_w_in.astype(BF16)
    even_w_out_b = even_w_out.astype(BF16)
    odd_w_in_b = odd_w_in.astype(BF16)
    odd_w_out_b = odd_w_out.astype(BF16)

    head_of_lane = np.arange(CW) // A_HEAD
    gmat = jnp.asarray((head_of_lane[:, None] == head_of_lane[None, :]).astype(np.float32)).astype(BF16)
    c_ch, s_ch = _dft_mats(B_GROUP_CH)
    cs_mat = jnp.concatenate([c_ch, s_ch], axis=1).astype(BF16)
    dft = {PROMPT: tuple(m.astype(BF16) for m in _dft_mats(SEQ)),
           SAMPLE: tuple(m.astype(BF16) for m in _dft_mats(DEC_SEQ))}
    cos_t, sin_t = _rope_tables()
    cache_k4 = cache_k.reshape(DEC_BATCH, n_odd, PAST_LEN, C_KV_HEADS * C_HEAD)
    cache_v4 = cache_v.reshape(DEC_BATCH, n_odd, PAST_LEN, C_KV_HEADS * C_HEAD)

    new_wkv, new_k, new_v = [], [], []
    for layer in range(DEPTH):
        i = layer // 2
        if layer % 2 == 0:
            zeros = jnp.zeros((LORA, A_WIDTH), F32)
            p = {
                "mu": even_mu[i][None, :], "w0": even_w0[i], "a0": even_a0[i],
                "wup": jnp.stack([jnp.concatenate([even_w_up[i, 0], zeros, zeros, zeros], axis=0),
                                  jnp.concatenate([zeros, even_w_up[i, 1], zeros, zeros], axis=0)]),
                "aup": jnp.stack([jnp.concatenate([zeros, zeros, even_a_up[i, 0], zeros], axis=0),
                                  jnp.concatenate([zeros, zeros, zeros, even_a_up[i, 1]], axis=0)]),
                "k_k": even_k_k[i][None, :], "k_a": even_k_a[i][None, :], "r_k": even_r_k[i].reshape(1, A_WIDTH),
                "gn_w": even_gn_w[i][None, :], "gn_b": even_gn_b[i][None, :],
            }
            s0s = state_wkv[:, i].reshape(DEC_BATCH, 2, A_HEADS // 2, 2, A_HEAD, A_HEAD)
            s0 = {PROMPT: jnp.zeros((2, BATCH // SEQ_PER_GROUP, A_HEAD, A_HEAD, CHAINS), F32),
                  SAMPLE: s0s.transpose(1, 5, 4, 3, 2, 0).reshape(2, 1, A_HEAD, A_HEAD, CHAINS)}
            for st in (PROMPT, SAMPLE):
                proj = _in_proj(st, xs[st], norm_pre3, mods4, even_w_in_b, layer, i)
                ya, yb, s_fin = _even_mixer(st, proj, s0[st], p, gmat, cs_mat, dft[st])
                xs[st] = _out_proj(st, ya, 0, yb, 0, even_w_out_b, i, xs[st], mods4, norm_post3, layer)
                if st is PROMPT:
                    s_fin = s_fin.reshape(2, BATCH // SEQ_PER_GROUP, A_HEAD, A_HEAD, 2, A_HEADS // 2, SEQ_PER_GROUP)
                    new_wkv.append(s_fin.transpose(1, 6, 0, 5, 4, 3, 2).reshape(BATCH, 2, A_HEADS, A_HEAD, A_HEAD))
        else:
            proj_p = _in_proj(PROMPT, xs[PROMPT], norm_pre3, mods4, odd_w_in_b, layer, i)
            proj_s = _in_proj(SAMPLE, xs[SAMPLE], norm_pre3, mods4, odd_w_in_b, layer, i)
            kv0 = C_HEADS * C_HEAD
            kvn = C_KV_HEADS * C_HEAD
            new_k.append(proj_p[:, kv0:kv0 + kvn].reshape(BATCH, SEQ, C_KV_HEADS, C_HEAD))
            new_v.append(proj_p[:, kv0 + kvn:kv0 + 2 * kvn].reshape(BATCH, SEQ, C_KV_HEADS, C_HEAD))
            att_p = _attn_context(proj_p, odd_sink[i])
            att_s = _attn_latent(proj_s, cache_k4, cache_v4, i, odd_sink[i], cos_t, sin_t)
            xs[PROMPT] = _out_proj(PROMPT, att_p, 0, att_p, 1, odd_w_out_b, i, xs[PROMPT], mods4, norm_post3, layer)
            xs[SAMPLE] = _out_proj(SAMPLE, att_s, 0, att_s, 1, odd_w_out_b, i, xs[SAMPLE], mods4, norm_post3, layer)

    y_prompt = xs[PROMPT].reshape(BATCH, SEQ, D_MODEL)
    y_sample = xs[SAMPLE].reshape(DEC_BATCH, DEC_SEQ, D_MODEL)
    return (y_prompt, y_sample, jnp.stack(new_wkv, axis=1), jnp.stack(new_k, axis=1), jnp.stack(new_v, axis=1))
```

```python
import collections
import functools

import numpy as np
import jax
import jax.numpy as jnp
from jax import lax
from jax.experimental import pallas as pl
from jax.experimental.pallas import tpu as pltpu

F32 = jnp.float32
BF16 = jnp.bfloat16

D_MODEL = 2048
BATCH = 32
SEQ = 256
DEPTH = 4
DEC_BATCH = 8
DEC_SEQ = 2048
PAST_LEN = 256
GRID_W = 64
RMS_EPS = 1e-6
A_WIDTH = 1024
A_HEAD = 64
A_HEADS = 16
LORA = 64
GN_EPS = 64e-5
B_WIDTH = 1024
B_GROUPS = 4
B_GROUP_CH = 256
SHIFT_COLS = 3 * A_WIDTH + 4 * LORA
EVEN_IN = SHIFT_COLS + A_WIDTH + 2 * B_WIDTH
C_HEAD = 64
C_HEADS = 32
C_KV_HEADS = 8
C_GROUP = 4
WINDOW = 128
BLOCK = 128
ROPE_BASE = 10000.0
ODD_IN = (C_HEADS + 2 * C_KV_HEADS) * C_HEAD + D_MODEL
NEG_INF = -1e30

N_COND = 16
SUBLANES = 8
LANES = 128
CHAINS = LANES
SEQ_PER_GROUP = CHAINS // A_HEADS
CW = 256
TM_IN = 512
TN_IN = 1280
TM_ROW = 256
WKV_TT = 16
WKV_VMEM_LIMIT = 56 * 1024 * 1024
WKV_ROWS = 16
WKV_ACCS = 2
VMEM_LIMIT = 48 * 1024 * 1024

Stream = collections.namedtuple("Stream", "rows t_len n_seq cond0 cond_per_seq")
PROMPT = Stream(BATCH * SEQ, SEQ, BATCH, 0, 0)
SAMPLE = Stream(DEC_BATCH * DEC_SEQ, DEC_SEQ, DEC_BATCH, 1, 1)


def _cparams(sem):
    return pltpu.CompilerParams(dimension_semantics=sem, vmem_limit_bytes=VMEM_LIMIT)


def _cond_of_row(st, row0):
    return st.cond0 + st.cond_per_seq * (row0 // st.t_len)


def _sigmoid(x):
    return 1.0 / (1.0 + jnp.exp(-x))


def _silu(x):
    return x * _sigmoid(x)


def _split(a):
    hi = a.astype(BF16)
    lo = (a - hi.astype(F32)).astype(BF16)
    return hi, lo


def _dot(a, b):
    return jnp.dot(a, b, preferred_element_type=F32)


def _dot3(a, b):
    ah, al = _split(a)
    bh, bl = _split(b)
    return _dot(ah, bh) + (_dot(ah, bl) + _dot(al, bh))


def _gsum(x, gmat):
    xh, xl = _split(x)
    return _dot(xh, gmat) + _dot(xl, gmat)


def _mod_kernel(c_ref, w_ref, b_ref, o_ref):
    o_ref[...] = _dot3(_silu(c_ref[...]), w_ref[...]) + b_ref[...]


def _modulation(conds, mod_w, mod_b):
    tn = 512
    n = 3 * D_MODEL
    return pl.pallas_call(
        _mod_kernel,
        out_shape=jax.ShapeDtypeStruct((DEPTH, N_COND, n), F32),
        grid=(DEPTH, n // tn),
        in_specs=[
            pl.BlockSpec((N_COND, D_MODEL), lambda l, j: (0, 0)),
            pl.BlockSpec((None, D_MODEL, tn), lambda l, j: (l, 0, j)),
            pl.BlockSpec((None, 1, tn), lambda l, j: (l, 0, j)),
        ],
        out_specs=pl.BlockSpec((None, N_COND, tn), lambda l, j: (l, 0, j)),
        compiler_params=_cparams(("parallel", "parallel")),
        name="modulation",
    )(conds, mod_w, mod_b.reshape(DEPTH, 1, n))


def _in_kernel(x_ref, g_ref, sh_ref, sc_ref, w_ref, o_ref, h_ref):
    @pl.when(pl.program_id(1) == 0)
    def _():
        x = x_ref[...]
        ms = jnp.mean(x * x, axis=-1, keepdims=True)
        y = x * lax.rsqrt(ms + RMS_EPS) * g_ref[...]
        h_ref[...] = (y * (1.0 + sc_ref[...]) + sh_ref[...]).astype(BF16)

    o_ref[...] = _dot(h_ref[...], w_ref[...])


def _in_proj(st, x, norm_g, mods4, w_bf16, layer, li):
    n = w_bf16.shape[-1]
    cond = lambda i: _cond_of_row(st, i * TM_IN)
    return pl.pallas_call(
        _in_kernel,
        out_shape=jax.ShapeDtypeStruct((st.rows, n), F32),
        grid=(st.rows // TM_IN, n // TN_IN),
        in_specs=[
            pl.BlockSpec((TM_IN, D_MODEL), lambda i, j: (i, 0)),
            pl.BlockSpec((None, 1, D_MODEL), lambda i, j: (layer, 0, 0)),
            pl.BlockSpec((None, None, 1, D_MODEL), lambda i, j: (layer, cond(i), 0, 0)),
            pl.BlockSpec((None, None, 1, D_MODEL), lambda i, j: (layer, cond(i), 0, 1)),
            pl.BlockSpec((None, D_MODEL, TN_IN), lambda i, j: (li, 0, j)),
        ],
        out_specs=pl.BlockSpec((TM_IN, TN_IN), lambda i, j: (i, j)),
        scratch_shapes=[pltpu.VMEM((TM_IN, D_MODEL), BF16)],
        compiler_params=_cparams(("parallel", "arbitrary")),
        name="in_proj",
    )(x, norm_g, mods4, mods4, w_bf16)


def _token_shift(cur, prev8, next8, mu, first, last):
    tm = cur.shape[0]
    rows = lax.broadcasted_iota(jnp.int32, cur.shape, 0)
    prow = jnp.where(first, 0.0, prev8[7:8, :])
    nrow = jnp.where(last, 0.0, next8[0:1, :])
    up = jnp.where(rows == 0, prow, pltpu.roll(cur, 1, axis=0))
    dn = jnp.where(rows == tm - 1, nrow, pltpu.roll(cur, tm - 1, axis=0))
    return cur + mu * (0.5 * (up + dn) - cur)


def _prep_kernel(r_ref, k_ref, v_ref, lo_ref, rp_ref, kp_ref, vp_ref, lp_ref, rn_ref, kn_ref, vn_ref, ln_ref,
                 mur_ref, muk_ref, muv_ref, mul_ref, w0_ref, wup_ref, a0_ref, aup_ref, kkw_ref, ka_ref, rk_ref,
                 gm_ref,
                 r_o, kk_o, v_o, w_o, kka_o, kd_o, bon_o, *, tiles_per_seq):
    i = pl.program_id(0)
    first = lax.rem(i, tiles_per_seq) == 0
    last = lax.rem(i, tiles_per_seq) == tiles_per_seq - 1

    r = _token_shift(r_ref[...], rp_ref[...], rn_ref[...], mur_ref[...], first, last)
    k = _token_shift(k_ref[...], kp_ref[...], kn_ref[...], muk_ref[...], first, last)
    v = _token_shift(v_ref[...], vp_ref[...], vn_ref[...], muv_ref[...], first, last)
    low = _token_shift(lo_ref[...], lp_ref[...], ln_ref[...], mul_ref[...], first, last)
    low_t = jnp.tanh(low)
    gm = gm_ref[...]

    kk = k * kkw_ref[...]
    nrm = jnp.sqrt(_gsum(kk * kk, gm))
    kk = kk / jnp.maximum(nrm, 1e-12)
    ka = ka_ref[...]
    kd_sum = None
    for d in range(2):
        w_raw = w0_ref[d:d + 1, :] + _dot3(low_t, wup_ref[d])
        z = -w_raw
        sp = jnp.maximum(z, 0.0) + jnp.log(1.0 + jnp.exp(-jnp.abs(z)))
        a = _sigmoid(a0_ref[d:d + 1, :] + _dot3(low, aup_ref[d]))
        kd = k * (1.0 + (a - 1.0) * ka)
        w_o[d] = jnp.exp(-jnp.exp(-sp - 0.5))
        kka_o[d] = kk * a
        kd_o[d] = kd
        kd_sum = kd if kd_sum is None else kd_sum + kd
    r_o[...] = r
    kk_o[...] = kk
    v_o[...] = v
    bon_o[...] = _gsum(r * (0.5 * kd_sum) * rk_ref[...], gm) * v


def _wkv_prep(st, proj, mu, w0, wup_pad, a0, aup_pad, k_k, k_a, r_k, gmat):
    tm = TM_ROW
    nb8 = st.rows // SUBLANES
    ncb = A_WIDTH // CW
    low_cb = 3 * A_WIDTH // CW
    assert 4 * LORA == CW and st.t_len % tm == 0

    def main(off):
        return pl.BlockSpec((tm, CW), lambda i, c: (i, off + c))

    def prev(off):
        return pl.BlockSpec((SUBLANES, CW), lambda i, c: (jnp.maximum(i * (tm // SUBLANES) - 1, 0), off + c))

    def nxt(off):
        return pl.BlockSpec((SUBLANES, CW),
                            lambda i, c: (jnp.minimum((i + 1) * (tm // SUBLANES), nb8 - 1), off + c))

    def fixed(spec_fn):
        return [spec_fn(0), spec_fn(ncb), spec_fn(2 * ncb)]

    low_main = pl.BlockSpec((tm, CW), lambda i, c: (i, low_cb))
    low_prev = pl.BlockSpec((SUBLANES, CW), lambda i, c: (jnp.maximum(i * (tm // SUBLANES) - 1, 0), low_cb))
    low_next = pl.BlockSpec((SUBLANES, CW),
                            lambda i, c: (jnp.minimum((i + 1) * (tm // SUBLANES), nb8 - 1), low_cb))

    def vec(off):
        return pl.BlockSpec((1, CW), lambda i, c: (0, off + c))

    in_specs = (fixed(main) + [low_main] + fixed(prev) + [low_prev] + fixed(nxt) + [low_next]
                + [vec(0), vec(ncb), vec(2 * ncb), pl.BlockSpec((1, CW), lambda i, c: (0, low_cb))]
                + [pl.BlockSpec((2, CW), lambda i, c: (0, c)),
                   pl.BlockSpec((2, CW, CW), lambda i, c: (0, 0, c)),
                   pl.BlockSpec((2, CW), lambda i, c: (0, c)),
                   pl.BlockSpec((2, CW, CW), lambda i, c: (0, 0, c)),
                   vec(0), vec(0), vec(0),
                   pl.BlockSpec((CW, CW), lambda i, c: (0, 0))])
    tps = st.t_len // tm
    g_n = st.n_seq // SEQ_PER_GROUP

    def til(i, c):
        s = i // tps
        return (s // SEQ_PER_GROUP, lax.rem(i, tps), lax.rem(s, SEQ_PER_GROUP) * ncb + c)

    one = pl.BlockSpec((tm, CW), lambda i, c: (i, c))
    til1 = pl.BlockSpec((None, tm, CW), lambda i, c: til(i, c))
    til2 = pl.BlockSpec((2, None, tm, CW), lambda i, c: (0,) + til(i, c))
    one_sds = jax.ShapeDtypeStruct((st.rows, A_WIDTH), F32)
    til1_sds = jax.ShapeDtypeStruct((g_n, st.t_len, SEQ_PER_GROUP * A_WIDTH), F32)
    til2_sds = jax.ShapeDtypeStruct((2, g_n, st.t_len, SEQ_PER_GROUP * A_WIDTH), F32)
    return pl.pallas_call(
        functools.partial(_prep_kernel, tiles_per_seq=tps),
        out_shape=[til1_sds, til1_sds, til1_sds, til2_sds, til2_sds, til2_sds, one_sds],
        grid=(st.rows // tm, ncb),
        in_specs=in_specs,
        out_specs=[til1, til1, til1, til2, til2, til2, one],
        compiler_params=_cparams(("parallel", "parallel")),
        name="wkv_prep",
    )(*([proj] * 12), mu, mu, mu, mu, w0, wup_pad, a0, aup_pad, k_k, k_a, r_k, gmat)


def _wkv_kernel(rf_ref, rb_ref, kkf_ref, kkb_ref, vf_ref, vb_ref, wf_ref, wb_ref, kkaf_ref, kkab_ref, kdf_ref, kdb_ref,
                s0_ref, of_ref, ob_ref, sf_ref,
                s2_ref, r2_ref, kk2_ref, v2_ref, w2_ref, kka2_ref, kd2_ref, o_ref, *, tt_steps):
    c = pl.program_id(1)
    nblk = WKV_ROWS // SUBLANES
    n_hv = A_HEADS // 2
    pairs = ((rf_ref, rb_ref, r2_ref), (kkf_ref, kkb_ref, kk2_ref), (vf_ref, vb_ref, v2_ref),
             (wf_ref, wb_ref, w2_ref), (kkaf_ref, kkab_ref, kka2_ref), (kdf_ref, kdb_ref, kd2_ref))

    @pl.when(c == 0)
    def _():
        s2_ref[...] = s0_ref[...]

    def to_chain_tiles(u, carry):
        ub = tt_steps - 1 - u
        for xf_ref, xb_ref, tile_ref in pairs:
            x2 = jnp.concatenate([xf_ref[u, :, hv * LANES:(hv + 1) * LANES] for hv in range(n_hv)]
                                 + [xb_ref[ub, :, hv * LANES:(hv + 1) * LANES] for hv in range(n_hv)], axis=0)
            y = x2.T
            tile_ref[0, u] = y[:A_HEAD]
            tile_ref[1, u] = y[A_HEAD:]
        return carry

    lax.fori_loop(0, tt_steps, to_chain_tiles, 0)

    def bcast_row(ref, t, j):
        return ref[t, pl.ds(j, SUBLANES, stride=0), :]

    def time_step(hp, t):
        s_ref = s2_ref.at[hp]
        r_ref, kk_ref, v_ref, w_ref, kka_ref, kd_ref = (p[2].at[hp] for p in pairs)
        for ib in range(A_HEAD // WKV_ROWS):
            i0 = ib * WKV_ROWS
            rows = [pl.ds(i0 + SUBLANES * b, SUBLANES) for b in range(nblk)]
            sa = [[None] * WKV_ACCS for _ in range(nblk)]
            for j in range(A_HEAD):
                kkj = bcast_row(kk_ref, t, j)
                for b in range(nblk):
                    p = s_ref[j, rows[b], :] * kkj
                    sa[b][j % WKV_ACCS] = p if sa[b][j % WKV_ACCS] is None else sa[b][j % WKV_ACCS] + p
            sa = [functools.reduce(lambda x, y: x + y, parts) for parts in sa]
            v8 = [v_ref[t, rows[b], :] for b in range(nblk)]
            out = [[None] * WKV_ACCS for _ in range(nblk)]
            for j in range(A_HEAD):
                wj = bcast_row(w_ref, t, j)
                kkaj = bcast_row(kka_ref, t, j)
                kdj = bcast_row(kd_ref, t, j)
                rj = bcast_row(r_ref, t, j)
                for b in range(nblk):
                    sn = s_ref[j, rows[b], :] * wj - sa[b] * kkaj + v8[b] * kdj
                    s_ref[j, rows[b], :] = sn
                    q = sn * rj
                    out[b][j % WKV_ACCS] = q if out[b][j % WKV_ACCS] is None else out[b][j % WKV_ACCS] + q
            for b in range(nblk):
                o_ref[hp, rows[b], :] = functools.reduce(lambda x, y: x + y, out[b])

    def step(u, carry):
        for hp in range(2):
            time_step(hp, u)
        x2 = jnp.concatenate([o_ref[0], o_ref[1]], axis=0).T
        ub = tt_steps - 1 - u
        for hv in range(n_hv):
            of_ref[u, :, hv * LANES:(hv + 1) * LANES] = x2[hv * SUBLANES:(hv + 1) * SUBLANES, :]
            ob_ref[ub, :, hv * LANES:(hv + 1) * LANES] = x2[(n_hv + hv) * SUBLANES:(n_hv + hv + 1) * SUBLANES, :]
        return carry

    lax.fori_loop(0, tt_steps, step, 0)

    @pl.when(c == pl.num_programs(1) - 1)
    def _():
        sf_ref[...] = s2_ref[...]


def _wkv(r, kk, v, w, kka, kd, s0):
    g_n, t_n = r.shape[0], r.shape[1]
    tt = WKV_TT
    nc = t_n // tt
    split = lambda a: a.reshape(a.shape[:-1] + (SEQ_PER_GROUP, A_WIDTH))
    blk = (tt, SEQ_PER_GROUP, A_WIDTH)

    fwd = pl.BlockSpec((None,) + blk, lambda g, c: (g, c, 0, 0))
    bwd = pl.BlockSpec((None,) + blk, lambda g, c: (g, nc - 1 - c, 0, 0))
    fwd_d = pl.BlockSpec((None, None) + blk, lambda g, c: (0, g, c, 0, 0))
    bwd_d = pl.BlockSpec((None, None) + blk, lambda g, c: (1, g, nc - 1 - c, 0, 0))
    state = pl.BlockSpec((None, 2, A_HEAD, A_HEAD, CHAINS), lambda g, c: (g, 0, 0, 0, 0))
    tiles = pltpu.VMEM((2, tt, A_HEAD, CHAINS), F32)
    o_sds = jax.ShapeDtypeStruct((g_n, t_n, SEQ_PER_GROUP, A_WIDTH), F32)
    r, kk, v, w, kka, kd = (split(a) for a in (r, kk, v, w, kka, kd))
    o_f, o_b, s_fin = pl.pallas_call(
        functools.partial(_wkv_kernel, tt_steps=tt),
        out_shape=[o_sds, o_sds, jax.ShapeDtypeStruct((g_n, 2, A_HEAD, A_HEAD, CHAINS), F32)],
        grid=(g_n, nc),
        in_specs=[fwd, bwd, fwd, bwd, fwd, bwd, fwd_d, bwd_d, fwd_d, bwd_d, fwd_d, bwd_d, state],
        out_specs=[fwd, bwd, state],
        scratch_shapes=([pltpu.VMEM((2, A_HEAD, A_HEAD, CHAINS), F32)] + [tiles] * 6
                        + [pltpu.VMEM((2, A_HEAD, CHAINS), F32)]),
        compiler_params=pltpu.CompilerParams(dimension_semantics=("parallel", "arbitrary"),
                                             vmem_limit_bytes=WKV_VMEM_LIMIT),
        name="wkv",
    )(r, r, kk, kk, v, v, w, w, kka, kka, kd, kd, s0)
    merge = lambda a: a.reshape(g_n, t_n, SEQ_PER_GROUP * A_WIDTH)
    return merge(o_f), merge(o_b), s_fin


def _ya_kernel(of_ref, ob_ref, bon_ref, gate_ref, gw_ref, gb_ref, gm_ref, o_ref):
    gm = gm_ref[...]
    o = of_ref[...] + ob_ref[...]
    mean = _gsum(o, gm) * (1.0 / A_HEAD)
    dev = o - mean
    var = _gsum(dev * dev, gm) * (1.0 / A_HEAD)
    y = dev * lax.rsqrt(var + GN_EPS) * gw_ref[...] + gb_ref[...]
    o_ref[...] = ((y + bon_ref[...]) * _silu(gate_ref[...])).astype(BF16)


def _ya(st, o_f, o_b, bonus, proj, gn_w, gn_b, gmat):
    tm = TM_ROW
    gate_cb = (SHIFT_COLS) // CW
    blk = pl.BlockSpec((tm, CW), lambda i, c: (i, c))
    vec = pl.BlockSpec((1, CW), lambda i, c: (0, c))
    tps = st.t_len // tm
    ncb = A_WIDTH // CW

    def til_idx(i, c):
        s = i // tps
        return (s // SEQ_PER_GROUP, lax.rem(i, tps), lax.rem(s, SEQ_PER_GROUP) * ncb + c)

    til = pl.BlockSpec((None, tm, CW), til_idx)
    return pl.pallas_call(
        _ya_kernel,
        out_shape=jax.ShapeDtypeStruct((st.rows, A_WIDTH), BF16),
        grid=(st.rows // tm, A_WIDTH // CW),
        in_specs=[til, til,
                  blk,
                  pl.BlockSpec((tm, CW), lambda i, c: (i, gate_cb + c)),
                  vec, vec,
                  pl.BlockSpec((CW, CW), lambda i, c: (0, 0))],
        out_specs=blk,
        compiler_params=_cparams(("parallel", "parallel")),
        name="wkv_post",
    )(o_f, o_b, bonus, proj, gn_w, gn_b, gmat)


def _dft_ch_kernel(u_ref, cs_ref, zc_ref, zs_ref):
    z = _dot(u_ref[...].astype(BF16), cs_ref[...])
    zc_ref[...] = z[:, :B_GROUP_CH].astype(BF16)
    zs_ref[...] = z[:, B_GROUP_CH:].astype(BF16)


def _dft_channels(st, proj, cs_mat):
    tm = 512
    u_cb = (SHIFT_COLS + A_WIDTH) // B_GROUP_CH
    blk = pl.BlockSpec((tm, B_GROUP_CH), lambda i, g: (i, g))
    sds = jax.ShapeDtypeStruct((st.rows, B_WIDTH), BF16)
    return pl.pallas_call(
        _dft_ch_kernel,
        out_shape=[sds, sds],
        grid=(st.rows // tm, B_GROUPS),
        in_specs=[pl.BlockSpec((tm, B_GROUP_CH), lambda i, g: (i, u_cb + g)),
                  pl.BlockSpec((B_GROUP_CH, 2 * B_GROUP_CH), lambda i, g: (0, 0))],
        out_specs=[blk, blk],
        compiler_params=_cparams(("parallel", "parallel")),
        name="dft_channels",
    )(proj, cs_mat)


def _dft_time_kernel(c_ref, s_ref, zc_ref, zs_ref, gate_ref, o_ref, *, scale):
    acc = _dot(c_ref[...], zc_ref[...]) - _dot(s_ref[...], zs_ref[...])
    o_ref[...] = (acc * scale * _silu(gate_ref[...])).astype(BF16)


def _dft_time(st, zc, zs, proj, cmat, smat):
    t_len = st.t_len
    tm = min(t_len, 512)
    mt = t_len // tm
    gate_cb = (SHIFT_COLS + A_WIDTH + B_WIDTH) // B_GROUP_CH
    scale = 1.0 / float(np.sqrt(t_len * B_GROUP_CH))
    return pl.pallas_call(
        functools.partial(_dft_time_kernel, scale=scale),
        out_shape=jax.ShapeDtypeStruct((st.rows, B_WIDTH), BF16),
        grid=(st.n_seq, mt, B_GROUPS),
        in_specs=[pl.BlockSpec((tm, t_len), lambda b, m, g: (m, 0)),
                  pl.BlockSpec((tm, t_len), lambda b, m, g: (m, 0)),
                  pl.BlockSpec((t_len, B_GROUP_CH), lambda b, m, g: (b, g)),
                  pl.BlockSpec((t_len, B_GROUP_CH), lambda b, m, g: (b, g)),
                  pl.BlockSpec((tm, B_GROUP_CH), lambda b, m, g: (b * mt + m, gate_cb + g))],
        out_specs=pl.BlockSpec((tm, B_GROUP_CH), lambda b, m, g: (b * mt + m, g)),
        compiler_params=_cparams(("parallel", "parallel", "parallel")),
        name="dft_time",
    )(cmat, smat, zc, zs, proj)


def _dft_mats(n):
    idx = jnp.arange(n, dtype=jnp.int32)
    prod = (idx[:, None] * idx[None, :]) % n
    ang = prod.astype(F32) * (2.0 * np.pi / n)
    return jnp.cos(ang), jnp.sin(ang)


def _out_kernel(a1_ref, a2_ref, w1_ref, w2_ref, x_ref, gate_ref, g_ref, o_ref):
    y = _dot(a1_ref[...], w1_ref[...]) + _dot(a2_ref[...], w2_ref[...])
    ms = jnp.mean(y * y, axis=-1, keepdims=True)
    yn = y * lax.rsqrt(ms + RMS_EPS) * g_ref[...]
    o_ref[...] = x_ref[...] + gate_ref[...] * yn


def _out_proj(st, a1, a1_cb, a2, a2_cb, w_bf16, li, x, mods4, norm_g, layer):
    tm = TM_ROW
    half = D_MODEL // 2
    cond = lambda i: _cond_of_row(st, i * tm)
    return pl.pallas_call(
        _out_kernel,
        out_shape=jax.ShapeDtypeStruct((st.rows, D_MODEL), F32),
        grid=(st.rows // tm,),
        in_specs=[pl.BlockSpec((tm, half), lambda i: (i, a1_cb)),
                  pl.BlockSpec((tm, half), lambda i: (i, a2_cb)),
                  pl.BlockSpec((None, half, D_MODEL), lambda i: (li, 0, 0)),
                  pl.BlockSpec((None, half, D_MODEL), lambda i: (li, 1, 0)),
                  pl.BlockSpec((tm, D_MODEL), lambda i: (i, 0)),
                  pl.BlockSpec((None, None, 1, D_MODEL), lambda i: (layer, cond(i), 0, 2)),
                  pl.BlockSpec((None, 1, D_MODEL), lambda i: (layer, 0, 0))],
        out_specs=pl.BlockSpec((tm, D_MODEL), lambda i: (i, 0)),
        compiler_params=_cparams(("parallel",)),
        name="out_proj",
    )(a1, a2, w_bf16, w_bf16, x, mods4, norm_g)


def _softmax_pv(scores, values, sink):
    m = sink
    for s in scores:
        m = jnp.maximum(m, jnp.max(s, axis=-1, keepdims=True))
    den = jnp.exp(sink - m)
    acc = None
    for s, v in zip(scores, values):
        p = jnp.exp(s - m)
        den = den + jnp.sum(p, axis=-1, keepdims=True)
        pv = _dot(p.astype(BF16), v)
        acc = pv if acc is None else acc + pv
    return acc / den


def _qk(q, k):
    return lax.dot_general(q, k, (((1,), (1,)), ((), ())), preferred_element_type=F32)


def _attn_ctx_kernel(sink_ref, q_ref, k_ref, v_ref, gate_ref, o_ref):
    kp = pl.program_id(1)
    q = q_ref[...] * (C_HEAD ** -0.5)
    k = k_ref[...]
    v = v_ref[...]
    outs = []
    for hh in range(2):
        kh = k[:, hh * C_HEAD:(hh + 1) * C_HEAD].astype(BF16)
        vh = v[:, hh * C_HEAD:(hh + 1) * C_HEAD].astype(BF16)
        for g in range(C_GROUP):
            c0 = (hh * C_GROUP + g) * C_HEAD
            qh = q[:, c0:c0 + C_HEAD].astype(BF16)
            sink = sink_ref[(kp * 2 + hh) * C_GROUP + g]
            outs.append(_softmax_pv([_qk(qh, kh)], [vh], sink))
    o = jnp.concatenate(outs, axis=-1)
    o_ref[...] = (o * _silu(gate_ref[...])).astype(BF16)


def _attn_context(proj, sink):
    qw = 2 * C_GROUP * C_HEAD
    k_cb = C_HEADS * C_HEAD // LANES
    v_cb = (C_HEADS + C_KV_HEADS) * C_HEAD // LANES
    gate_cb = (C_HEADS + 2 * C_KV_HEADS) * C_HEAD // qw
    return pl.pallas_call(
        _attn_ctx_kernel,
        out_shape=jax.ShapeDtypeStruct((PROMPT.rows, D_MODEL), BF16),
        grid=(BATCH, C_KV_HEADS // 2),
        in_specs=[pl.BlockSpec(memory_space=pltpu.SMEM),
                  pl.BlockSpec((SEQ, qw), lambda b, kp: (b, kp)),
                  pl.BlockSpec((SEQ, LANES), lambda b, kp: (b, k_cb + kp)),
                  pl.BlockSpec((SEQ, LANES), lambda b, kp: (b, v_cb + kp)),
                  pl.BlockSpec((SEQ, qw), lambda b, kp: (b, gate_cb + kp))],
        out_specs=pl.BlockSpec((SEQ, qw), lambda b, kp: (b, kp)),
        compiler_params=_cparams(("parallel", "parallel")),
        name="attn_context",
    )(sink, proj, proj, proj, proj)


def _rope(x, cos, sin_signed):
    lane = lax.broadcasted_iota(jnp.int32, x.shape, 1)
    first = (lane & 31) < 16
    partner = jnp.where(first, pltpu.roll(x, LANES - 16, axis=1), pltpu.roll(x, 16, axis=1))
    return x * cos + partner * sin_signed


def _attn_lat_kernel(sink_ref, q_ref, kp_ref, ko_ref, kn_ref, vp_ref, vo_ref, vn_ref, ck_ref, cv_ref,
                     cq_ref, sq_ref, cp_ref, sp_ref, cn_ref, sn_ref, gate_ref, o_ref):
    kpair = pl.program_id(1)
    qb = pl.program_id(2)
    cq = cq_ref[...]
    sq = sq_ref[...]
    q = q_ref[...] * (C_HEAD ** -0.5)
    qr = [_rope(q[:, n * LANES:(n + 1) * LANES], cq, sq) for n in range(4)]
    kband = jnp.concatenate([_rope(kp_ref[...], cp_ref[...], sp_ref[...]),
                             _rope(ko_ref[...], cq, sq),
                             _rope(kn_ref[...], cn_ref[...], sn_ref[...])], axis=0)
    vband = jnp.concatenate([vp_ref[...], vo_ref[...], vn_ref[...]], axis=0)
    ck = ck_ref[...]
    cv = cv_ref[...]
    qpos = qb * BLOCK + lax.broadcasted_iota(jnp.int32, (BLOCK, 3 * BLOCK), 0)
    kpos = (qb - 1) * BLOCK + lax.broadcasted_iota(jnp.int32, (BLOCK, 3 * BLOCK), 1)
    valid = (jnp.abs(qpos - kpos) <= WINDOW) & (kpos >= 0) & (kpos < DEC_SEQ)
    outs = []
    for hh in range(2):
        sl = slice(hh * C_HEAD, (hh + 1) * C_HEAD)
        kb = kband[:, sl].astype(BF16)
        vb = vband[:, sl].astype(BF16)
        ckh = ck[:, sl].astype(BF16)
        cvh = cv[:, sl].astype(BF16)
        for g in range(C_GROUP):
            n = hh * C_GROUP + g
            qh = qr[n // 2][:, (n % 2) * C_HEAD:(n % 2 + 1) * C_HEAD].astype(BF16)
            sink = sink_ref[(kpair * 2 + hh) * C_GROUP + g]
            s_band = jnp.where(valid, _qk(qh, kb), NEG_INF)
            outs.append(_softmax_pv([s_band, _qk(qh, ckh)], [vb, cvh], sink))
    o = jnp.concatenate(outs, axis=-1)
    o_ref[...] = (o * _silu(gate_ref[...])).astype(BF16)


def _attn_latent(proj, cache_k4, cache_v4, li, sink, cos_t, sin_t):
    qw = 2 * C_GROUP * C_HEAD
    k_cb = C_HEADS * C_HEAD // LANES
    v_cb = (C_HEADS + C_KV_HEADS) * C_HEAD // LANES
    gate_cb = (C_HEADS + 2 * C_KV_HEADS) * C_HEAD // qw
    nqb = DEC_SEQ // BLOCK

    def rows(delta):
        def idx(b, kp, qb):
            return b * nqb + jnp.clip(qb + delta, 0, nqb - 1)
        return idx

    def kv_spec(cb, delta):
        r = rows(delta)
        return pl.BlockSpec((BLOCK, LANES), lambda b, kp, qb: (r(b, kp, qb), cb + kp))

    def tab_spec(delta):
        return pl.BlockSpec((BLOCK, LANES), lambda b, kp, qb: (jnp.clip(qb + delta, 0, nqb - 1), 0))

    cache_spec = pl.BlockSpec((None, None, PAST_LEN, LANES), lambda b, kp, qb: (b, li, 0, kp))
    own = rows(0)
    return pl.pallas_call(
        _attn_lat_kernel,
        out_shape=jax.ShapeDtypeStruct((SAMPLE.rows, D_MODEL), BF16),
        grid=(DEC_BATCH, C_KV_HEADS // 2, nqb),
        in_specs=[pl.BlockSpec(memory_space=pltpu.SMEM),
                  pl.BlockSpec((BLOCK, qw), lambda b, kp, qb: (own(b, kp, qb), kp)),
                  kv_spec(k_cb, -1), kv_spec(k_cb, 0), kv_spec(k_cb, 1),
                  kv_spec(v_cb, -1), kv_spec(v_cb, 0), kv_spec(v_cb, 1),
                  cache_spec, cache_spec,
                  tab_spec(0), tab_spec(0), tab_spec(-1), tab_spec(-1), tab_spec(1), tab_spec(1),
                  pl.BlockSpec((BLOCK, qw), lambda b, kp, qb: (own(b, kp, qb), gate_cb + kp))],
        out_specs=pl.BlockSpec((BLOCK, qw), lambda b, kp, qb: (own(b, kp, qb), kp)),
        compiler_params=_cparams(("parallel", "parallel", "parallel")),
        name="attn_latent",
    )(sink, proj, proj, proj, proj, proj, proj, proj, cache_k4, cache_v4,
      cos_t, sin_t, cos_t, sin_t, cos_t, sin_t, proj)


def _rope_tables():
    t = jnp.arange(DEC_SEQ, dtype=jnp.int32)
    row = (t // GRID_W).astype(F32)
    col = (t % GRID_W).astype(F32)
    nf = C_HEAD // 4
    inv = 1.0 / (ROPE_BASE ** (jnp.arange(nf, dtype=F32) / nf))
    lane = np.arange(LANES)
    f_of_lane = lane % nf
    use_col = (lane % C_HEAD) >= C_HEAD // 2
    sign = np.where((lane % 32) < 16, -1.0, 1.0).astype(np.float32)
    pos = jnp.where(jnp.asarray(use_col)[None, :], col[:, None], row[:, None])
    ang = pos * inv[jnp.asarray(f_of_lane)][None, :]
    return jnp.cos(ang), jnp.sin(ang) * jnp.asarray(sign)[None, :]


def _even_mixer(st, proj, s0, p, gmat, cs_mat, dft):
    r, kk, v, w2, kka2, kd2, bonus = _wkv_prep(st, proj, p["mu"], p["w0"], p["wup"], p["a0"], p["aup"],
                                                p["k_k"], p["k_a"], p["r_k"], gmat)
    o_f, o_b, s_fin = _wkv(r, kk, v, w2, kka2, kd2, s0)
    ya = _ya(st, o_f, o_b, bonus, proj, p["gn_w"], p["gn_b"], gmat)
    zc, zs = _dft_channels(st, proj, cs_mat)
    yb = _dft_time(st, zc, zs, proj, dft[0], dft[1])
    return ya, yb, s_fin


def kernel(x_prompt, x_sample, c, state_wkv, cache_k, cache_v, c_ctx, mod_w, mod_b, norm_pre, norm_post,
           even_w_in, even_mu, even_w0, even_w_up, even_a0, even_a_up, even_k_k, even_k_a, even_r_k,
           even_gn_w, even_gn_b, even_w_out, odd_w_in, odd_sink, odd_w_out):
    n_odd = odd_w_in.shape[0]
    xs = {PROMPT: x_prompt.reshape(PROMPT.rows, D_MODEL), SAMPLE: x_sample.reshape(SAMPLE.rows, D_MODEL)}
    conds = jnp.concatenate([c_ctx[None, :], c, jnp.zeros((N_COND - 1 - DEC_BATCH, D_MODEL), F32)], axis=0)
    mods4 = _modulation(conds, mod_w, mod_b).reshape(DEPTH, N_COND, 1, 3 * D_MODEL)
    norm_pre3 = norm_pre.reshape(DEPTH, 1, D_MODEL)
    norm_post3 = norm_post.reshape(DEPTH, 1, D_MODEL)

    even_w_in_b = even_w_in.astype(BF16)
    even_w_out_b = even_w_out.astype(BF16)
    odd_w_in_b = odd_w_in.astype(BF16)
    odd_w_out_b = odd_w_out.astype(BF16)

    head_of_lane = np.arange(CW) // A_HEAD
    gmat = jnp.asarray((head_of_lane[:, None] == head_of_lane[None, :]).astype(np.float32)).astype(BF16)
    c_ch, s_ch = _dft_mats(B_GROUP_CH)
    cs_mat = jnp.concatenate([c_ch, s_ch], axis=1).astype(BF16)
    dft = {PROMPT: tuple(m.astype(BF16) for m in _dft_mats(SEQ)),
           SAMPLE: tuple(m.astype(BF16) for m in _dft_mats(DEC_SEQ))}
    cos_t, sin_t = _rope_tables()
    cache_k4 = cache_k.reshape(DEC_BATCH, n_odd, PAST_LEN, C_KV_HEADS * C_HEAD)
    cache_v4 = cache_v.reshape(DEC_BATCH, n_odd, PAST_LEN, C_KV_HEADS * C_HEAD)

    new_wkv, new_k, new_v = [], [], []
    for layer in range(DEPTH):
        i = layer // 2
        if layer % 2 == 0:
            zeros = jnp.zeros((LORA, A_WIDTH), F32)
            p = {
                "mu": even_mu[i][None, :], "w0": even_w0[i], "a0": even_a0[i],
                "wup": jnp.stack([jnp.concatenate([even_w_up[i, 0], zeros, zeros, zeros], axis=0),
                                  jnp.concatenate([zeros, even_w_up[i, 1], zeros, zeros], axis=0)]),
                "aup": jnp.stack([jnp.concatenate([zeros, zeros, even_a_up[i, 0], zeros], axis=0),
                                  jnp.concatenate([zeros, zeros, zeros, even_a_up[i, 1]], axis=0)]),
                "k_k": even_k_k[i][None, :], "k_a": even_k_a[i][None, :], "r_k": even_r_k[i].reshape(1, A_WIDTH),
                "gn_w": even_gn_w[i][None, :], "gn_b": even_gn_b[i][None, :],
            }
            s0s = state_wkv[:, i].reshape(DEC_BATCH, 2, A_HEADS // 2, 2, A_HEAD, A_HEAD)
            s0 = {PROMPT: jnp.zeros((BATCH // SEQ_PER_GROUP, 2, A_HEAD, A_HEAD, CHAINS), F32),
                  SAMPLE: s0s.transpose(3, 5, 4, 1, 2, 0).reshape(1, 2, A_HEAD, A_HEAD, CHAINS)}
            for st in (PROMPT, SAMPLE):
                proj = _in_proj(st, xs[st], norm_pre3, mods4, even_w_in_b, layer, i)
                ya, yb, s_fin = _even_mixer(st, proj, s0[st], p, gmat, cs_mat, dft[st])
                xs[st] = _out_proj(st, ya, 0, yb, 0, even_w_out_b, i, xs[st], mods4, norm_post3, layer)
                if st is PROMPT:
                    s_fin = s_fin.reshape(BATCH // SEQ_PER_GROUP, 2, A_HEAD, A_HEAD, 2, A_HEADS // 2, SEQ_PER_GROUP)
                    new_wkv.append(s_fin.transpose(0, 6, 4, 5, 1, 3, 2).reshape(BATCH, 2, A_HEADS, A_HEAD, A_HEAD))
        else:
            proj_p = _in_proj(PROMPT, xs[PROMPT], norm_pre3, mods4, odd_w_in_b, layer, i)
            proj_s = _in_proj(SAMPLE, xs[SAMPLE], norm_pre3, mods4, odd_w_in_b, layer, i)
            kv0 = C_HEADS * C_HEAD
            kvn = C_KV_HEADS * C_HEAD
            new_k.append(proj_p[:, kv0:kv0 + kvn].reshape(BATCH, SEQ, C_KV_HEADS, C_HEAD))
            new_v.append(proj_p[:, kv0 + kvn:kv0 + 2 * kvn].reshape(BATCH, SEQ, C_KV_HEADS, C_HEAD))
            att_p = _attn_context(proj_p, odd_sink[i])
            att_s = _attn_latent(proj_s, cache_k4, cache_v4, i, odd_sink[i], cos_t, sin_t)
            xs[PROMPT] = _out_proj(PROMPT, att_p, 0, att_p, 1, odd_w_out_b, i, xs[PROMPT], mods4, norm_post3, layer)
            xs[SAMPLE] = _out_proj(SAMPLE, att_s, 0, att_s, 1, odd_w_out_b, i, xs[SAMPLE], mods4, norm_post3, layer)

    y_prompt = xs[PROMPT].reshape(BATCH, SEQ, D_MODEL)
    y_sample = xs[SAMPLE].reshape(DEC_BATCH, DEC_SEQ, D_MODEL)
    return (y_prompt, y_sample, jnp.stack(new_wkv, axis=1), jnp.stack(new_k, axis=1), jnp.stack(new_v, axis=1))
```

```python
import collections
import functools

import numpy as np
import jax
import jax.numpy as jnp
from jax import lax
from jax.experimental import pallas as pl
from jax.experimental.pallas import tpu as pltpu

F32 = jnp.float32
BF16 = jnp.bfloat16

D_MODEL = 2048
BATCH = 32
SEQ = 256
DEPTH = 4
DEC_BATCH = 8
DEC_SEQ = 2048
PAST_LEN = 256
GRID_W = 64
RMS_EPS = 1e-6
A_WIDTH = 1024
A_HEAD = 64
A_HEADS = 16
LORA = 64
GN_EPS = 64e-5
B_WIDTH = 1024
B_GROUPS = 4
B_GROUP_CH = 256
SHIFT_COLS = 3 * A_WIDTH + 4 * LORA
EVEN_IN = SHIFT_COLS + A_WIDTH + 2 * B_WIDTH
C_HEAD = 64
C_HEADS = 32
C_KV_HEADS = 8
C_GROUP = 4
WINDOW = 128
BLOCK = 128
ROPE_BASE = 10000.0
ODD_IN = (C_HEADS + 2 * C_KV_HEADS) * C_HEAD + D_MODEL
NEG_INF = -1e30

N_COND = 16
SUBLANES = 8
LANES = 128
CHAINS = LANES
SEQ_PER_GROUP = CHAINS // A_HEADS
CW = 256
TM_IN = 512
TN_IN = 1280
TM_ROW = 256
WKV_TT = 16
WKV_VMEM_LIMIT = 56 * 1024 * 1024
WKV_ROWS = 16
WKV_ACCS = 2
VMEM_LIMIT = 48 * 1024 * 1024

Stream = collections.namedtuple("Stream", "rows t_len n_seq cond0 cond_per_seq")
PROMPT = Stream(BATCH * SEQ, SEQ, BATCH, 0, 0)
SAMPLE = Stream(DEC_BATCH * DEC_SEQ, DEC_SEQ, DEC_BATCH, 1, 1)


def _cparams(sem):
    return pltpu.CompilerParams(dimension_semantics=sem, vmem_limit_bytes=VMEM_LIMIT)


def _cond_of_row(st, row0):
    return st.cond0 + st.cond_per_seq * (row0 // st.t_len)


def _sigmoid(x):
    return 1.0 / (1.0 + jnp.exp(-x))


def _silu(x):
    return x * _sigmoid(x)


def _split(a):
    hi = a.astype(BF16)
    lo = (a - hi.astype(F32)).astype(BF16)
    return hi, lo


def _dot(a, b):
    return jnp.dot(a, b, preferred_element_type=F32)


def _dot3(a, b):
    ah, al = _split(a)
    bh, bl = _split(b)
    return _dot(ah, bh) + (_dot(ah, bl) + _dot(al, bh))


def _gsum(x, gmat):
    xh, xl = _split(x)
    return _dot(xh, gmat) + _dot(xl, gmat)


def _mod_kernel(c_ref, w_ref, b_ref, o_ref):
    o_ref[...] = _dot3(_silu(c_ref[...]), w_ref[...]) + b_ref[...]


def _modulation(conds, mod_w, mod_b):
    tn = 512
    n = 3 * D_MODEL
    return pl.pallas_call(
        _mod_kernel,
        out_shape=jax.ShapeDtypeStruct((DEPTH, N_COND, n), F32),
        grid=(DEPTH, n // tn),
        in_specs=[
            pl.BlockSpec((N_COND, D_MODEL), lambda l, j: (0, 0)),
            pl.BlockSpec((None, D_MODEL, tn), lambda l, j: (l, 0, j)),
            pl.BlockSpec((None, 1, tn), lambda l, j: (l, 0, j)),
        ],
        out_specs=pl.BlockSpec((None, N_COND, tn), lambda l, j: (l, 0, j)),
        compiler_params=_cparams(("parallel", "parallel")),
        name="modulation",
    )(conds, mod_w, mod_b.reshape(DEPTH, 1, n))


def _in_kernel(x_ref, g_ref, sh_ref, sc_ref, w_ref, o_ref, h_ref):
    @pl.when(pl.program_id(1) == 0)
    def _():
        x = x_ref[...]
        ms = jnp.mean(x * x, axis=-1, keepdims=True)
        y = x * lax.rsqrt(ms + RMS_EPS) * g_ref[...]
        h_ref[...] = (y * (1.0 + sc_ref[...]) + sh_ref[...]).astype(BF16)

    o_ref[...] = _dot(h_ref[...], w_ref[...])


def _in_proj(st, x, norm_g, mods4, w_bf16, layer, li):
    n = w_bf16.shape[-1]
    cond = lambda i: _cond_of_row(st, i * TM_IN)
    return pl.pallas_call(
        _in_kernel,
        out_shape=jax.ShapeDtypeStruct((st.rows, n), F32),
        grid=(st.rows // TM_IN, n // TN_IN),
        in_specs=[
            pl.BlockSpec((TM_IN, D_MODEL), lambda i, j: (i, 0)),
            pl.BlockSpec((None, 1, D_MODEL), lambda i, j: (layer, 0, 0)),
            pl.BlockSpec((None, None, 1, D_MODEL), lambda i, j: (layer, cond(i), 0, 0)),
            pl.BlockSpec((None, None, 1, D_MODEL), lambda i, j: (layer, cond(i), 0, 1)),
            pl.BlockSpec((None, D_MODEL, TN_IN), lambda i, j: (li, 0, j)),
        ],
        out_specs=pl.BlockSpec((TM_IN, TN_IN), lambda i, j: (i, j)),
        scratch_shapes=[pltpu.VMEM((TM_IN, D_MODEL), BF16)],
        compiler_params=_cparams(("parallel", "arbitrary")),
        name="in_proj",
    )(x, norm_g, mods4, mods4, w_bf16)


def _token_shift(cur, prev8, next8, mu, first, last):
    tm = cur.shape[0]
    rows = lax.broadcasted_iota(jnp.int32, cur.shape, 0)
    prow = jnp.where(first, 0.0, prev8[7:8, :])
    nrow = jnp.where(last, 0.0, next8[0:1, :])
    up = jnp.where(rows == 0, prow, pltpu.roll(cur, 1, axis=0))
    dn = jnp.where(rows == tm - 1, nrow, pltpu.roll(cur, tm - 1, axis=0))
    return cur + mu * (0.5 * (up + dn) - cur)


def _prep_kernel(r_ref, k_ref, v_ref, lo_ref, rp_ref, kp_ref, vp_ref, lp_ref, rn_ref, kn_ref, vn_ref, ln_ref,
                 mur_ref, muk_ref, muv_ref, mul_ref, w0_ref, wup_ref, a0_ref, aup_ref, kkw_ref, ka_ref, rk_ref,
                 gm_ref,
                 r_o, kk_o, v_o, w_o, kka_o, kd_o, bon_o, *, tiles_per_seq):
    i = pl.program_id(0)
    first = lax.rem(i, tiles_per_seq) == 0
    last = lax.rem(i, tiles_per_seq) == tiles_per_seq - 1

    r = _token_shift(r_ref[...], rp_ref[...], rn_ref[...], mur_ref[...], first, last)
    k = _token_shift(k_ref[...], kp_ref[...], kn_ref[...], muk_ref[...], first, last)
    v = _token_shift(v_ref[...], vp_ref[...], vn_ref[...], muv_ref[...], first, last)
    low = _token_shift(lo_ref[...], lp_ref[...], ln_ref[...], mul_ref[...], first, last)
    low_t = jnp.tanh(low)
    gm = gm_ref[...]

    kk = k * kkw_ref[...]
    nrm = jnp.sqrt(_gsum(kk * kk, gm))
    kk = kk / jnp.maximum(nrm, 1e-12)
    ka = ka_ref[...]
    kd_sum = None
    for d in range(2):
        w_raw = w0_ref[d:d + 1, :] + _dot3(low_t, wup_ref[d])
        z = -w_raw
        sp = jnp.maximum(z, 0.0) + jnp.log(1.0 + jnp.exp(-jnp.abs(z)))
        a = _sigmoid(a0_ref[d:d + 1, :] + _dot3(low, aup_ref[d]))
        kd = k * (1.0 + (a - 1.0) * ka)
        w_o[d] = jnp.exp(-jnp.exp(-sp - 0.5))
        kka_o[d] = kk * a
        kd_o[d] = kd
        kd_sum = kd if kd_sum is None else kd_sum + kd
    r_o[...] = r
    kk_o[...] = kk
    v_o[...] = v
    bon_o[...] = _gsum(r * (0.5 * kd_sum) * rk_ref[...], gm) * v


def _wkv_prep(st, proj, mu, w0, wup_pad, a0, aup_pad, k_k, k_a, r_k, gmat):
    tm = TM_ROW
    nb8 = st.rows // SUBLANES
    ncb = A_WIDTH // CW
    low_cb = 3 * A_WIDTH // CW
    assert 4 * LORA == CW and st.t_len % tm == 0

    def main(off):
        return pl.BlockSpec((tm, CW), lambda i, c: (i, off + c))

    def prev(off):
        return pl.BlockSpec((SUBLANES, CW), lambda i, c: (jnp.maximum(i * (tm // SUBLANES) - 1, 0), off + c))

    def nxt(off):
        return pl.BlockSpec((SUBLANES, CW),
                            lambda i, c: (jnp.minimum((i + 1) * (tm // SUBLANES), nb8 - 1), off + c))

    def fixed(spec_fn):
        return [spec_fn(0), spec_fn(ncb), spec_fn(2 * ncb)]

    low_main = pl.BlockSpec((tm, CW), lambda i, c: (i, low_cb))
    low_prev = pl.BlockSpec((SUBLANES, CW), lambda i, c: (jnp.maximum(i * (tm // SUBLANES) - 1, 0), low_cb))
    low_next = pl.BlockSpec((SUBLANES, CW),
                            lambda i, c: (jnp.minimum((i + 1) * (tm // SUBLANES), nb8 - 1), low_cb))

    def vec(off):
        return pl.BlockSpec((1, CW), lambda i, c: (0, off + c))

    in_specs = (fixed(main) + [low_main] + fixed(prev) + [low_prev] + fixed(nxt) + [low_next]
                + [vec(0), vec(ncb), vec(2 * ncb), pl.BlockSpec((1, CW), lambda i, c: (0, low_cb))]
                + [pl.BlockSpec((2, CW), lambda i, c: (0, c)),
                   pl.BlockSpec((2, CW, CW), lambda i, c: (0, 0, c)),
                   pl.BlockSpec((2, CW), lambda i, c: (0, c)),
                   pl.BlockSpec((2, CW, CW), lambda i, c: (0, 0, c)),
                   vec(0), vec(0), vec(0),
                   pl.BlockSpec((CW, CW), lambda i, c: (0, 0))])
    tps = st.t_len // tm
    g_n = st.n_seq // SEQ_PER_GROUP

    def til(i, c):
        s = i // tps
        return (s // SEQ_PER_GROUP, lax.rem(i, tps), lax.rem(s, SEQ_PER_GROUP) * ncb + c)

    one = pl.BlockSpec((tm, CW), lambda i, c: (i, c))
    til1 = pl.BlockSpec((None, tm, CW), lambda i, c: til(i, c))
    til2 = pl.BlockSpec((2, None, tm, CW), lambda i, c: (0,) + til(i, c))
    one_sds = jax.ShapeDtypeStruct((st.rows, A_WIDTH), F32)
    til1_sds = jax.ShapeDtypeStruct((g_n, st.t_len, SEQ_PER_GROUP * A_WIDTH), F32)
    til2_sds = jax.ShapeDtypeStruct((2, g_n, st.t_len, SEQ_PER_GROUP * A_WIDTH), F32)
    return pl.pallas_call(
        functools.partial(_prep_kernel, tiles_per_seq=tps),
        out_shape=[til1_sds, til1_sds, til1_sds, til2_sds, til2_sds, til2_sds, one_sds],
        grid=(st.rows // tm, ncb),
        in_specs=in_specs,
        out_specs=[til1, til1, til1, til2, til2, til2, one],
        compiler_params=_cparams(("parallel", "parallel")),
        name="wkv_prep",
    )(*([proj] * 12), mu, mu, mu, mu, w0, wup_pad, a0, aup_pad, k_k, k_a, r_k, gmat)


def _wkv_kernel(rf_ref, rb_ref, kkf_ref, kkb_ref, vf_ref, vb_ref, wf_ref, wb_ref, kkaf_ref, kkab_ref, kdf_ref, kdb_ref,
                s0_ref, of_ref, ob_ref, sf_ref,
                s2_ref, r2_ref, kk2_ref, v2_ref, kka2_ref, kd2_ref, g_ref, o_ref, *, tt_steps):
    c = pl.program_id(1)
    nblk = WKV_ROWS // SUBLANES
    n_hv = A_HEADS // 2
    pairs = ((rf_ref, rb_ref, r2_ref), (kkf_ref, kkb_ref, kk2_ref), (vf_ref, vb_ref, v2_ref),
             (wf_ref, wb_ref, None), (kkaf_ref, kkab_ref, kka2_ref), (kdf_ref, kdb_ref, kd2_ref))

    @pl.when(c == 0)
    def _():
        s2_ref[...] = s0_ref[...]

    def to_chain_tiles(u, gammas):
        ub = tt_steps - 1 - u
        halves = []
        for xf_ref, xb_ref, _ in pairs:
            x2 = jnp.concatenate([xf_ref[u, :, hv * LANES:(hv + 1) * LANES] for hv in range(n_hv)]
                                 + [xb_ref[ub, :, hv * LANES:(hv + 1) * LANES] for hv in range(n_hv)], axis=0)
            y = x2.T
            halves.append((y[:A_HEAD], y[A_HEAD:]))
        r_h, kk_h, v_h, w_h, kka_h, kd_h = halves
        new_gammas = []
        for hp in range(2):
            gamma = gammas[hp] * w_h[hp]
            inv = 1.0 / gamma
            kk2_ref[hp, u] = gammas[hp] * kk_h[hp]
            kka2_ref[hp, u] = kka_h[hp] * inv
            kd2_ref[hp, u] = kd_h[hp] * inv
            r2_ref[hp, u] = r_h[hp] * gamma
            v2_ref[hp, u] = v_h[hp]
            new_gammas.append(gamma)
        return tuple(new_gammas)

    one = jnp.ones((A_HEAD, CHAINS), F32)
    gammas = lax.fori_loop(0, tt_steps, to_chain_tiles, (one, one))
    for hp in range(2):
        g_ref[hp] = gammas[hp]

    def bcast_row(ref, t, j):
        return ref[t, pl.ds(j, SUBLANES, stride=0), :]

    def time_step(hp, t):
        s_ref = s2_ref.at[hp]
        r_ref, kk_ref, v_ref, kka_ref, kd_ref = (ref.at[hp] for ref in (r2_ref, kk2_ref, v2_ref, kka2_ref, kd2_ref))
        for ib in range(A_HEAD // WKV_ROWS):
            i0 = ib * WKV_ROWS
            rows = [pl.ds(i0 + SUBLANES * b, SUBLANES) for b in range(nblk)]
            sa = [[None] * WKV_ACCS for _ in range(nblk)]
            for j in range(A_HEAD):
                kkj = bcast_row(kk_ref, t, j)
                for b in range(nblk):
                    p = s_ref[j, rows[b], :] * kkj
                    sa[b][j % WKV_ACCS] = p if sa[b][j % WKV_ACCS] is None else sa[b][j % WKV_ACCS] + p
            sa = [functools.reduce(lambda x, y: x + y, parts) for parts in sa]
            v8 = [v_ref[t, rows[b], :] for b in range(nblk)]
            out = [[None] * WKV_ACCS for _ in range(nblk)]
            for j in range(A_HEAD):
                kkaj = bcast_row(kka_ref, t, j)
                kdj = bcast_row(kd_ref, t, j)
                rj = bcast_row(r_ref, t, j)
                for b in range(nblk):
                    sn = s_ref[j, rows[b], :] - sa[b] * kkaj + v8[b] * kdj
                    s_ref[j, rows[b], :] = sn
                    q = sn * rj
                    out[b][j % WKV_ACCS] = q if out[b][j % WKV_ACCS] is None else out[b][j % WKV_ACCS] + q
            for b in range(nblk):
                o_ref[hp, rows[b], :] = functools.reduce(lambda x, y: x + y, out[b])

    def step(u, carry):
        for hp in range(2):
            time_step(hp, u)
        x2 = jnp.concatenate([o_ref[0], o_ref[1]], axis=0).T
        ub = tt_steps - 1 - u
        for hv in range(n_hv):
            of_ref[u, :, hv * LANES:(hv + 1) * LANES] = x2[hv * SUBLANES:(hv + 1) * SUBLANES, :]
            ob_ref[ub, :, hv * LANES:(hv + 1) * LANES] = x2[(n_hv + hv) * SUBLANES:(n_hv + hv + 1) * SUBLANES, :]
        return carry

    lax.fori_loop(0, tt_steps, step, 0)

    def rescale(j, carry):
        for hp in range(2):
            gj = g_ref[hp, pl.ds(j, SUBLANES, stride=0), :]
            for b in range(A_HEAD // SUBLANES):
                rows = pl.ds(b * SUBLANES, SUBLANES)
                s2_ref[hp, j, rows, :] = s2_ref[hp, j, rows, :] * gj
        return carry

    lax.fori_loop(0, A_HEAD, rescale, 0)

    @pl.when(c == pl.num_programs(1) - 1)
    def _():
        sf_ref[...] = s2_ref[...]


def _wkv(r, kk, v, w, kka, kd, s0):
    g_n, t_n = r.shape[0], r.shape[1]
    tt = WKV_TT
    nc = t_n // tt
    split = lambda a: a.reshape(a.shape[:-1] + (SEQ_PER_GROUP, A_WIDTH))
    blk = (tt, SEQ_PER_GROUP, A_WIDTH)

    fwd = pl.BlockSpec((None,) + blk, lambda g, c: (g, c, 0, 0))
    bwd = pl.BlockSpec((None,) + blk, lambda g, c: (g, nc - 1 - c, 0, 0))
    fwd_d = pl.BlockSpec((None, None) + blk, lambda g, c: (0, g, c, 0, 0))
    bwd_d = pl.BlockSpec((None, None) + blk, lambda g, c: (1, g, nc - 1 - c, 0, 0))
    state = pl.BlockSpec((None, 2, A_HEAD, A_HEAD, CHAINS), lambda g, c: (g, 0, 0, 0, 0))
    tiles = pltpu.VMEM((2, tt, A_HEAD, CHAINS), F32)
    o_sds = jax.ShapeDtypeStruct((g_n, t_n, SEQ_PER_GROUP, A_WIDTH), F32)
    r, kk, v, w, kka, kd = (split(a) for a in (r, kk, v, w, kka, kd))
    o_f, o_b, s_fin = pl.pallas_call(
        functools.partial(_wkv_kernel, tt_steps=tt),
        out_shape=[o_sds, o_sds, jax.ShapeDtypeStruct((g_n, 2, A_HEAD, A_HEAD, CHAINS), F32)],
        grid=(g_n, nc),
        in_specs=[fwd, bwd, fwd, bwd, fwd, bwd, fwd_d, bwd_d, fwd_d, bwd_d, fwd_d, bwd_d, state],
        out_specs=[fwd, bwd, state],
        scratch_shapes=([pltpu.VMEM((2, A_HEAD, A_HEAD, CHAINS), F32)] + [tiles] * 5
                        + [pltpu.VMEM((2, A_HEAD, CHAINS), F32)] * 2),
        compiler_params=pltpu.CompilerParams(dimension_semantics=("parallel", "arbitrary"),
                                             vmem_limit_bytes=WKV_VMEM_LIMIT),
        name="wkv",
    )(r, r, kk, kk, v, v, w, w, kka, kka, kd, kd, s0)
    merge = lambda a: a.reshape(g_n, t_n, SEQ_PER_GROUP * A_WIDTH)
    return merge(o_f), merge(o_b), s_fin


def _ya_kernel(of_ref, ob_ref, bon_ref, gate_ref, gw_ref, gb_ref, gm_ref, o_ref):
    gm = gm_ref[...]
    o = of_ref[...] + ob_ref[...]
    mean = _gsum(o, gm) * (1.0 / A_HEAD)
    dev = o - mean
    var = _gsum(dev * dev, gm) * (1.0 / A_HEAD)
    y = dev * lax.rsqrt(var + GN_EPS) * gw_ref[...] + gb_ref[...]
    o_ref[...] = ((y + bon_ref[...]) * _silu(gate_ref[...])).astype(BF16)


def _ya(st, o_f, o_b, bonus, proj, gn_w, gn_b, gmat):
    tm = TM_ROW
    gate_cb = (SHIFT_COLS) // CW
    blk = pl.BlockSpec((tm, CW), lambda i, c: (i, c))
    vec = pl.BlockSpec((1, CW), lambda i, c: (0, c))
    tps = st.t_len // tm
    ncb = A_WIDTH // CW

    def til_idx(i, c):
        s = i // tps
        return (s // SEQ_PER_GROUP, lax.rem(i, tps), lax.rem(s, SEQ_PER_GROUP) * ncb + c)

    til = pl.BlockSpec((None, tm, CW), til_idx)
    return pl.pallas_call(
        _ya_kernel,
        out_shape=jax.ShapeDtypeStruct((st.rows, A_WIDTH), BF16),
        grid=(st.rows // tm, A_WIDTH // CW),
        in_specs=[til, til,
                  blk,
                  pl.BlockSpec((tm, CW), lambda i, c: (i, gate_cb + c)),
                  vec, vec,
                  pl.BlockSpec((CW, CW), lambda i, c: (0, 0))],
        out_specs=blk,
        compiler_params=_cparams(("parallel", "parallel")),
        name="wkv_post",
    )(o_f, o_b, bonus, proj, gn_w, gn_b, gmat)


def _dft_ch_kernel(u_ref, cs_ref, zc_ref, zs_ref):
    z = _dot(u_ref[...].astype(BF16), cs_ref[...])
    zc_ref[...] = z[:, :B_GROUP_CH].astype(BF16)
    zs_ref[...] = z[:, B_GROUP_CH:].astype(BF16)


def _dft_channels(st, proj, cs_mat):
    tm = 512
    u_cb = (SHIFT_COLS + A_WIDTH) // B_GROUP_CH
    blk = pl.BlockSpec((tm, B_GROUP_CH), lambda i, g: (i, g))
    sds = jax.ShapeDtypeStruct((st.rows, B_WIDTH), BF16)
    return pl.pallas_call(
        _dft_ch_kernel,
        out_shape=[sds, sds],
        grid=(st.rows // tm, B_GROUPS),
        in_specs=[pl.BlockSpec((tm, B_GROUP_CH), lambda i, g: (i, u_cb + g)),
                  pl.BlockSpec((B_GROUP_CH, 2 * B_GROUP_CH), lambda i, g: (0, 0))],
        out_specs=[blk, blk],
        compiler_params=_cparams(("parallel", "parallel")),
        name="dft_channels",
    )(proj, cs_mat)


def _dft_time_kernel(c_ref, s_ref, zc_ref, zs_ref, gate_ref, o_ref, *, scale):
    acc = _dot(c_ref[...], zc_ref[...]) - _dot(s_ref[...], zs_ref[...])
    o_ref[...] = (acc * scale * _silu(gate_ref[...])).astype(BF16)


def _dft_time(st, zc, zs, proj, cmat, smat):
    t_len = st.t_len
    tm = min(t_len, 512)
    mt = t_len // tm
    gate_cb = (SHIFT_COLS + A_WIDTH + B_WIDTH) // B_GROUP_CH
    scale = 1.0 / float(np.sqrt(t_len * B_GROUP_CH))
    return pl.pallas_call(
        functools.partial(_dft_time_kernel, scale=scale),
        out_shape=jax.ShapeDtypeStruct((st.rows, B_WIDTH), BF16),
        grid=(st.n_seq, mt, B_GROUPS),
        in_specs=[pl.BlockSpec((tm, t_len), lambda b, m, g: (m, 0)),
                  pl.BlockSpec((tm, t_len), lambda b, m, g: (m, 0)),
                  pl.BlockSpec((t_len, B_GROUP_CH), lambda b, m, g: (b, g)),
                  pl.BlockSpec((t_len, B_GROUP_CH), lambda b, m, g: (b, g)),
                  pl.BlockSpec((tm, B_GROUP_CH), lambda b, m, g: (b * mt + m, gate_cb + g))],
        out_specs=pl.BlockSpec((tm, B_GROUP_CH), lambda b, m, g: (b * mt + m, g)),
        compiler_params=_cparams(("parallel", "parallel", "parallel")),
        name="dft_time",
    )(cmat, smat, zc, zs, proj)


def _dft_mats(n):
    idx = jnp.arange(n, dtype=jnp.int32)
    prod = (idx[:, None] * idx[None, :]) % n
    ang = prod.astype(F32) * (2.0 * np.pi / n)
    return jnp.cos(ang), jnp.sin(ang)


def _out_kernel(a1_ref, a2_ref, w1_ref, w2_ref, x_ref, gate_ref, g_ref, o_ref):
    y = _dot(a1_ref[...], w1_ref[...]) + _dot(a2_ref[...], w2_ref[...])
    ms = jnp.mean(y * y, axis=-1, keepdims=True)
    yn = y * lax.rsqrt(ms + RMS_EPS) * g_ref[...]
    o_ref[...] = x_ref[...] + gate_ref[...] * yn


def _out_proj(st, a1, a1_cb, a2, a2_cb, w_bf16, li, x, mods4, norm_g, layer):
    tm = TM_ROW
    half = D_MODEL // 2
    cond = lambda i: _cond_of_row(st, i * tm)
    return pl.pallas_call(
        _out_kernel,
        out_shape=jax.ShapeDtypeStruct((st.rows, D_MODEL), F32),
        grid=(st.rows // tm,),
        in_specs=[pl.BlockSpec((tm, half), lambda i: (i, a1_cb)),
                  pl.BlockSpec((tm, half), lambda i: (i, a2_cb)),
                  pl.BlockSpec((None, half, D_MODEL), lambda i: (li, 0, 0)),
                  pl.BlockSpec((None, half, D_MODEL), lambda i: (li, 1, 0)),
                  pl.BlockSpec((tm, D_MODEL), lambda i: (i, 0)),
                  pl.BlockSpec((None, None, 1, D_MODEL), lambda i: (layer, cond(i), 0, 2)),
                  pl.BlockSpec((None, 1, D_MODEL), lambda i: (layer, 0, 0))],
        out_specs=pl.BlockSpec((tm, D_MODEL), lambda i: (i, 0)),
        compiler_params=_cparams(("parallel",)),
        name="out_proj",
    )(a1, a2, w_bf16, w_bf16, x, mods4, norm_g)


def _softmax_pv(scores, values, sink):
    m = sink
    for s in scores:
        m = jnp.maximum(m, jnp.max(s, axis=-1, keepdims=True))
    den = jnp.exp(sink - m)
    acc = None
    for s, v in zip(scores, values):
        p = jnp.exp(s - m)
        den = den + jnp.sum(p, axis=-1, keepdims=True)
        pv = _dot(p.astype(BF16), v)
        acc = pv if acc is None else acc + pv
    return acc / den


def _qk(q, k):
    return lax.dot_general(q, k, (((1,), (1,)), ((), ())), preferred_element_type=F32)


def _upper_half(rows):
    return lax.broadcasted_iota(jnp.int32, (rows, LANES), 1) >= C_HEAD


def _both_halves(x, hh):
    upper = _upper_half(x.shape[0])
    keep = upper if hh == 1 else jnp.logical_not(upper)
    return jnp.where(keep, x, pltpu.roll(x, C_HEAD, axis=1)).astype(BF16)


def _pair_heads(q_chunks, keys, values, sinks, masks):
    rows = q_chunks[0].shape[0]
    upper = _upper_half(rows)
    lower = jnp.logical_not(upper)
    outs = [None] * (2 * C_GROUP)
    for hh in range(2):
        kd = [_both_halves(k, hh) for k in keys]
        vd = [_both_halves(v, hh) for v in values]
        heads = [hh * C_GROUP + g for g in range(C_GROUP)]
        q4 = jnp.concatenate([jnp.where(upper if n % 2 else lower, q_chunks[n // 2], 0.0) for n in heads],
                             axis=0).astype(BF16)
        s4 = [_qk(q4, k) for k in kd]
        p4 = [[] for _ in kd]
        dens = []
        for g, n in enumerate(heads):
            sl = slice(g * rows, (g + 1) * rows)
            scores = [s[sl] if mask is None else jnp.where(mask, s[sl], NEG_INF) for s, mask in zip(s4, masks)]
            sink = sinks(n)
            m = sink
            for s in scores:
                m = jnp.maximum(m, jnp.max(s, axis=-1, keepdims=True))
            den = jnp.exp(sink - m)
            for t, s in enumerate(scores):
                p = jnp.exp(s - m)
                den = den + jnp.sum(p, axis=-1, keepdims=True)
                p4[t].append(p.astype(BF16))
            dens.append(den)
        pv4 = None
        for p_parts, v in zip(p4, vd):
            pv = _dot(jnp.concatenate(p_parts, axis=0), v)
            pv4 = pv if pv4 is None else pv4 + pv
        for g, n in enumerate(heads):
            outs[n] = pv4[g * rows:(g + 1) * rows] / dens[g]
    return jnp.concatenate([jnp.where(lower, outs[2 * m], outs[2 * m + 1]) for m in range(C_GROUP)], axis=-1)


def _attn_ctx_kernel(sink_ref, q_ref, k_ref, v_ref, gate_ref, o_ref):
    kp = pl.program_id(1)
    q = q_ref[...] * (C_HEAD ** -0.5)
    q_chunks = [q[:, m * LANES:(m + 1) * LANES] for m in range(C_GROUP)]
    o = _pair_heads(q_chunks, [k_ref[...]], [v_ref[...]], lambda n: sink_ref[kp * 2 * C_GROUP + n], [None])
    o_ref[...] = (o * _silu(gate_ref[...])).astype(BF16)


def _attn_context(proj, sink):
    qw = 2 * C_GROUP * C_HEAD
    k_cb = C_HEADS * C_HEAD // LANES
    v_cb = (C_HEADS + C_KV_HEADS) * C_HEAD // LANES
    gate_cb = (C_HEADS + 2 * C_KV_HEADS) * C_HEAD // qw
    return pl.pallas_call(
        _attn_ctx_kernel,
        out_shape=jax.ShapeDtypeStruct((PROMPT.rows, D_MODEL), BF16),
        grid=(BATCH, C_KV_HEADS // 2),
        in_specs=[pl.BlockSpec(memory_space=pltpu.SMEM),
                  pl.BlockSpec((SEQ, qw), lambda b, kp: (b, kp)),
                  pl.BlockSpec((SEQ, LANES), lambda b, kp: (b, k_cb + kp)),
                  pl.BlockSpec((SEQ, LANES), lambda b, kp: (b, v_cb + kp)),
                  pl.BlockSpec((SEQ, qw), lambda b, kp: (b, gate_cb + kp))],
        out_specs=pl.BlockSpec((SEQ, qw), lambda b, kp: (b, kp)),
        compiler_params=_cparams(("parallel", "parallel")),
        name="attn_context",
    )(sink, proj, proj, proj, proj)


def _rope(x, cos, sin_signed):
    lane = lax.broadcasted_iota(jnp.int32, x.shape, 1)
    first = (lane & 31) < 16
    partner = jnp.where(first, pltpu.roll(x, LANES - 16, axis=1), pltpu.roll(x, 16, axis=1))
    return x * cos + partner * sin_signed


def _attn_lat_kernel(sink_ref, q_ref, kp_ref, ko_ref, kn_ref, vp_ref, vo_ref, vn_ref, ck_ref, cv_ref,
                     cq_ref, sq_ref, cp_ref, sp_ref, cn_ref, sn_ref, gate_ref, o_ref):
    kpair = pl.program_id(1)
    qb = pl.program_id(2)
    cq = cq_ref[...]
    sq = sq_ref[...]
    q = q_ref[...] * (C_HEAD ** -0.5)
    qr = [_rope(q[:, n * LANES:(n + 1) * LANES], cq, sq) for n in range(4)]
    kband = jnp.concatenate([_rope(kp_ref[...], cp_ref[...], sp_ref[...]),
                             _rope(ko_ref[...], cq, sq),
                             _rope(kn_ref[...], cn_ref[...], sn_ref[...])], axis=0)
    vband = jnp.concatenate([vp_ref[...], vo_ref[...], vn_ref[...]], axis=0)
    qpos = qb * BLOCK + lax.broadcasted_iota(jnp.int32, (BLOCK, 3 * BLOCK), 0)
    kpos = (qb - 1) * BLOCK + lax.broadcasted_iota(jnp.int32, (BLOCK, 3 * BLOCK), 1)
    valid = (jnp.abs(qpos - kpos) <= WINDOW) & (kpos >= 0) & (kpos < DEC_SEQ)
    o = _pair_heads(qr, [kband, ck_ref[...]], [vband, cv_ref[...]],
                    lambda n: sink_ref[kpair * 2 * C_GROUP + n], [valid, None])
    o_ref[...] = (o * _silu(gate_ref[...])).astype(BF16)


def _attn_latent(proj, cache_k4, cache_v4, li, sink, cos_t, sin_t):
    qw = 2 * C_GROUP * C_HEAD
    k_cb = C_HEADS * C_HEAD // LANES
    v_cb = (C_HEADS + C_KV_HEADS) * C_HEAD // LANES
    gate_cb = (C_HEADS + 2 * C_KV_HEADS) * C_HEAD // qw
    nqb = DEC_SEQ // BLOCK

    def rows(delta):
        def idx(b, kp, qb):
            return b * nqb + jnp.clip(qb + delta, 0, nqb - 1)
        return idx

    def kv_spec(cb, delta):
        r = rows(delta)
        return pl.BlockSpec((BLOCK, LANES), lambda b, kp, qb: (r(b, kp, qb), cb + kp))

    def tab_spec(delta):
        return pl.BlockSpec((BLOCK, LANES), lambda b, kp, qb: (jnp.clip(qb + delta, 0, nqb - 1), 0))

    cache_spec = pl.BlockSpec((None, None, PAST_LEN, LANES), lambda b, kp, qb: (b, li, 0, kp))
    own = rows(0)
    return pl.pallas_call(
        _attn_lat_kernel,
        out_shape=jax.ShapeDtypeStruct((SAMPLE.rows, D_MODEL), BF16),
        grid=(DEC_BATCH, C_KV_HEADS // 2, nqb),
        in_specs=[pl.BlockSpec(memory_space=pltpu.SMEM),
                  pl.BlockSpec((BLOCK, qw), lambda b, kp, qb: (own(b, kp, qb), kp)),
                  kv_spec(k_cb, -1), kv_spec(k_cb, 0), kv_spec(k_cb, 1),
                  kv_spec(v_cb, -1), kv_spec(v_cb, 0), kv_spec(v_cb, 1),
                  cache_spec, cache_spec,
                  tab_spec(0), tab_spec(0), tab_spec(-1), tab_spec(-1), tab_spec(1), tab_spec(1),
                  pl.BlockSpec((BLOCK, qw), lambda b, kp, qb: (own(b, kp, qb), gate_cb + kp))],
        out_specs=pl.BlockSpec((BLOCK, qw), lambda b, kp, qb: (own(b, kp, qb), kp)),
        compiler_params=_cparams(("parallel", "parallel", "parallel")),
        name="attn_latent",
    )(sink, proj, proj, proj, proj, proj, proj, proj, cache_k4, cache_v4,
      cos_t, sin_t, cos_t, sin_t, cos_t, sin_t, proj)


def _rope_tables():
    t = jnp.arange(DEC_SEQ, dtype=jnp.int32)
    row = (t // GRID_W).astype(F32)
    col = (t % GRID_W).astype(F32)
    nf = C_HEAD // 4
    inv = 1.0 / (ROPE_BASE ** (jnp.arange(nf, dtype=F32) / nf))
    lane = np.arange(LANES)
    f_of_lane = lane % nf
    use_col = (lane % C_HEAD) >= C_HEAD // 2
    sign = np.where((lane % 32) < 16, -1.0, 1.0).astype(np.float32)
    pos = jnp.where(jnp.asarray(use_col)[None, :], col[:, None], row[:, None])
    ang = pos * inv[jnp.asarray(f_of_lane)][None, :]
    return jnp.cos(ang), jnp.sin(ang) * jnp.asarray(sign)[None, :]


def _even_mixer(st, proj, s0, p, gmat, cs_mat, dft):
    r, kk, v, w2, kka2, kd2, bonus = _wkv_prep(st, proj, p["mu"], p["w0"], p["wup"], p["a0"], p["aup"],
                                                p["k_k"], p["k_a"], p["r_k"], gmat)
    o_f, o_b, s_fin = _wkv(r, kk, v, w2, kka2, kd2, s0)
    ya = _ya(st, o_f, o_b, bonus, proj, p["gn_w"], p["gn_b"], gmat)
    zc, zs = _dft_channels(st, proj, cs_mat)
    yb = _dft_time(st, zc, zs, proj, dft[0], dft[1])
    return ya, yb, s_fin


def kernel(x_prompt, x_sample, c, state_wkv, cache_k, cache_v, c_ctx, mod_w, mod_b, norm_pre, norm_post,
           even_w_in, even_mu, even_w0, even_w_up, even_a0, even_a_up, even_k_k, even_k_a, even_r_k,
           even_gn_w, even_gn_b, even_w_out, odd_w_in, odd_sink, odd_w_out):
    n_odd = odd_w_in.shape[0]
    xs = {PROMPT: x_prompt.reshape(PROMPT.rows, D_MODEL), SAMPLE: x_sample.reshape(SAMPLE.rows, D_MODEL)}
    conds = jnp.concatenate([c_ctx[None, :], c, jnp.zeros((N_COND - 1 - DEC_BATCH, D_MODEL), F32)], axis=0)
    mods4 = _modulation(conds, mod_w, mod_b).reshape(DEPTH, N_COND, 1, 3 * D_MODEL)
    norm_pre3 = norm_pre.reshape(DEPTH, 1, D_MODEL)
    norm_post3 = norm_post.reshape(DEPTH, 1, D_MODEL)

    even_w_in_b = even_w_in.astype(BF16)
    even_w_out_b = even_w_out.astype(BF16)
    odd_w_in_b = odd_w_in.astype(BF16)
    odd_w_out_b = odd_w_out.astype(BF16)

    head_of_lane = np.arange(CW) // A_HEAD
    gmat = jnp.asarray((head_of_lane[:, None] == head_of_lane[None, :]).astype(np.float32)).astype(BF16)
    c_ch, s_ch = _dft_mats(B_GROUP_CH)
    cs_mat = jnp.concatenate([c_ch, s_ch], axis=1).astype(BF16)
    dft = {PROMPT: tuple(m.astype(BF16) for m in _dft_mats(SEQ)),
           SAMPLE: tuple(m.astype(BF16) for m in _dft_mats(DEC_SEQ))}
    cos_t, sin_t = _rope_tables()
    cache_k4 = cache_k.reshape(DEC_BATCH, n_odd, PAST_LEN, C_KV_HEADS * C_HEAD)
    cache_v4 = cache_v.reshape(DEC_BATCH, n_odd, PAST_LEN, C_KV_HEADS * C_HEAD)

    new_wkv, new_k, new_v = [], [], []
    for layer in range(DEPTH):
        i = layer // 2
        if layer % 2 == 0:
            zeros = jnp.zeros((LORA, A_WIDTH), F32)
            p = {
                "mu": even_mu[i][None, :], "w0": even_w0[i], "a0": even_a0[i],
                "wup": jnp.stack([jnp.concatenate([even_w_up[i, 0], zeros, zeros, zeros], axis=0),
                                  jnp.concatenate([zeros, even_w_up[i, 1], zeros, zeros], axis=0)]),
                "aup": jnp.stack([jnp.concatenate([zeros, zeros, even_a_up[i, 0], zeros], axis=0),
                                  jnp.concatenate([zeros, zeros, zeros, even_a_up[i, 1]], axis=0)]),
                "k_k": even_k_k[i][None, :], "k_a": even_k_a[i][None, :], "r_k": even_r_k[i].reshape(1, A_WIDTH),
                "gn_w": even_gn_w[i][None, :], "gn_b": even_gn_b[i][None, :],
            }
            s0s = state_wkv[:, i].reshape(DEC_BATCH, 2, A_HEADS // 2, 2, A_HEAD, A_HEAD)
            s0 = {PROMPT: jnp.zeros((BATCH // SEQ_PER_GROUP, 2, A_HEAD, A_HEAD, CHAINS), F32),
                  SAMPLE: s0s.transpose(3, 5, 4, 1, 2, 0).reshape(1, 2, A_HEAD, A_HEAD, CHAINS)}
            for st in (PROMPT, SAMPLE):
                proj = _in_proj(st, xs[st], norm_pre3, mods4, even_w_in_b, layer, i)
                ya, yb, s_fin = _even_mixer(st, proj, s0[st], p, gmat, cs_mat, dft[st])
                xs[st] = _out_proj(st, ya, 0, yb, 0, even_w_out_b, i, xs[st], mods4, norm_post3, layer)
                if st is PROMPT:
                    s_fin = s_fin.reshape(BATCH // SEQ_PER_GROUP, 2, A_HEAD, A_HEAD, 2, A_HEADS // 2, SEQ_PER_GROUP)
                    new_wkv.append(s_fin.transpose(0, 6, 4, 5, 1, 3, 2).reshape(BATCH, 2, A_HEADS, A_HEAD, A_HEAD))
        else:
            proj_p = _in_proj(PROMPT, xs[PROMPT], norm_pre3, mods4, odd_w_in_b, layer, i)
            proj_s = _in_proj(SAMPLE, xs[SAMPLE], norm_pre3, mods4, odd_w_in_b, layer, i)
            kv0 = C_HEADS * C_HEAD
            kvn = C_KV_HEADS * C_HEAD
            new_k.append(proj_p[:, kv0:kv0 + kvn].reshape(BATCH, SEQ, C_KV_HEADS, C_HEAD))
            new_v.append(proj_p[:, kv0 + kvn:kv0 + 2 * kvn].reshape(BATCH, SEQ, C_KV_HEADS, C_HEAD))
            att_p = _attn_context(proj_p, odd_sink[i])
            att_s = _attn_latent(proj_s, cache_k4, cache_v4, i, odd_sink[i], cos_t, sin_t)
            xs[PROMPT] = _out_proj(PROMPT, att_p, 0, att_p, 1, odd_w_out_b, i, xs[PROMPT], mods4, norm_post3, layer)
            xs[SAMPLE] = _out_proj(SAMPLE, att_s, 0, att_s, 1, odd_w_out_b, i, xs[SAMPLE], mods4, norm_post3, layer)

    y_prompt = xs[PROMPT].reshape(BATCH, SEQ, D_MODEL)
    y_sample = xs[SAMPLE].reshape(DEC_BATCH, DEC_SEQ, D_MODEL)
    return (y_prompt, y_sample, jnp.stack(new_wkv, axis=1), jnp.stack(new_k, axis=1), jnp.stack(new_v, axis=1))
```

```python
import collections
import functools

import numpy as np
import jax
import jax.numpy as jnp
from jax import lax
from jax.experimental import pallas as pl
from jax.experimental.pallas import tpu as pltpu

F32 = jnp.float32
BF16 = jnp.bfloat16

D_MODEL = 2048
BATCH = 32
SEQ = 256
DEPTH = 4
DEC_BATCH = 8
DEC_SEQ = 2048
PAST_LEN = 256
GRID_W = 64
RMS_EPS = 1e-6
A_WIDTH = 1024
A_HEAD = 64
A_HEADS = 16
LORA = 64
GN_EPS = 64e-5
B_WIDTH = 1024
B_GROUPS = 4
B_GROUP_CH = 256
SHIFT_COLS = 3 * A_WIDTH + 4 * LORA
EVEN_IN = SHIFT_COLS + A_WIDTH + 2 * B_WIDTH
C_HEAD = 64
C_HEADS = 32
C_KV_HEADS = 8
C_GROUP = 4
WINDOW = 128
BLOCK = 128
ROPE_BASE = 10000.0
ODD_IN = (C_HEADS + 2 * C_KV_HEADS) * C_HEAD + D_MODEL
NEG_INF = -1e30

N_COND = 16
SUBLANES = 8
LANES = 128
CHAINS = LANES
SEQ_PER_GROUP = CHAINS // A_HEADS
CW = 256
TM_IN = 512
TN_IN = 1280
TM_ROW = 256
WKV_TT = 16
WKV_VMEM_LIMIT = 56 * 1024 * 1024
WKV_ROWS = 16
WKV_ACCS = 2
VMEM_LIMIT = 48 * 1024 * 1024

Stream = collections.namedtuple("Stream", "rows t_len n_seq cond0 cond_per_seq")
PROMPT = Stream(BATCH * SEQ, SEQ, BATCH, 0, 0)
SAMPLE = Stream(DEC_BATCH * DEC_SEQ, DEC_SEQ, DEC_BATCH, 1, 1)


def _cparams(sem):
    return pltpu.CompilerParams(dimension_semantics=sem, vmem_limit_bytes=VMEM_LIMIT)


def _cond_of_row(st, row0):
    return st.cond0 + st.cond_per_seq * (row0 // st.t_len)


def _sigmoid(x):
    return 1.0 / (1.0 + jnp.exp(-x))


def _silu(x):
    return x * _sigmoid(x)


def _split(a):
    hi = a.astype(BF16)
    lo = (a - hi.astype(F32)).astype(BF16)
    return hi, lo


def _dot(a, b):
    return jnp.dot(a, b, preferred_element_type=F32)


def _dot3(a, b):
    ah, al = _split(a)
    bh, bl = _split(b)
    return _dot(ah, bh) + (_dot(ah, bl) + _dot(al, bh))


def _gsum(x, gmat):
    xh, xl = _split(x)
    return _dot(xh, gmat) + _dot(xl, gmat)


def _mod_kernel(c_ref, w_ref, b_ref, o_ref):
    o_ref[...] = _dot3(_silu(c_ref[...]), w_ref[...]) + b_ref[...]


def _modulation(conds, mod_w, mod_b):
    tn = 512
    n = 3 * D_MODEL
    return pl.pallas_call(
        _mod_kernel,
        out_shape=jax.ShapeDtypeStruct((DEPTH, N_COND, n), F32),
        grid=(DEPTH, n // tn),
        in_specs=[
            pl.BlockSpec((N_COND, D_MODEL), lambda l, j: (0, 0)),
            pl.BlockSpec((None, D_MODEL, tn), lambda l, j: (l, 0, j)),
            pl.BlockSpec((None, 1, tn), lambda l, j: (l, 0, j)),
        ],
        out_specs=pl.BlockSpec((None, N_COND, tn), lambda l, j: (l, 0, j)),
        compiler_params=_cparams(("parallel", "parallel")),
        name="modulation",
    )(conds, mod_w, mod_b.reshape(DEPTH, 1, n))


def _in_kernel(x_ref, g_ref, sh_ref, sc_ref, w_ref, o_ref, h_ref):
    @pl.when(pl.program_id(1) == 0)
    def _():
        x = x_ref[...]
        ms = jnp.mean(x * x, axis=-1, keepdims=True)
        y = x * lax.rsqrt(ms + RMS_EPS) * g_ref[...]
        h_ref[...] = (y * (1.0 + sc_ref[...]) + sh_ref[...]).astype(BF16)

    o_ref[...] = _dot(h_ref[...], w_ref[...])


def _in_proj(st, x, norm_g, mods4, w_bf16, layer, li):
    n = w_bf16.shape[-1]
    cond = lambda i: _cond_of_row(st, i * TM_IN)
    return pl.pallas_call(
        _in_kernel,
        out_shape=jax.ShapeDtypeStruct((st.rows, n), F32),
        grid=(st.rows // TM_IN, n // TN_IN),
        in_specs=[
            pl.BlockSpec((TM_IN, D_MODEL), lambda i, j: (i, 0)),
            pl.BlockSpec((None, 1, D_MODEL), lambda i, j: (layer, 0, 0)),
            pl.BlockSpec((None, None, 1, D_MODEL), lambda i, j: (layer, cond(i), 0, 0)),
            pl.BlockSpec((None, None, 1, D_MODEL), lambda i, j: (layer, cond(i), 0, 1)),
            pl.BlockSpec((None, D_MODEL, TN_IN), lambda i, j: (li, 0, j)),
        ],
        out_specs=pl.BlockSpec((TM_IN, TN_IN), lambda i, j: (i, j)),
        scratch_shapes=[pltpu.VMEM((TM_IN, D_MODEL), BF16)],
        compiler_params=_cparams(("parallel", "arbitrary")),
        name="in_proj",
    )(x, norm_g, mods4, mods4, w_bf16)


def _token_shift(cur, prev8, next8, mu, first, last):
    tm = cur.shape[0]
    rows = lax.broadcasted_iota(jnp.int32, cur.shape, 0)
    prow = jnp.where(first, 0.0, prev8[7:8, :])
    nrow = jnp.where(last, 0.0, next8[0:1, :])
    up = jnp.where(rows == 0, prow, pltpu.roll(cur, 1, axis=0))
    dn = jnp.where(rows == tm - 1, nrow, pltpu.roll(cur, tm - 1, axis=0))
    return cur + mu * (0.5 * (up + dn) - cur)


def _prep_kernel(r_ref, k_ref, v_ref, lo_ref, rp_ref, kp_ref, vp_ref, lp_ref, rn_ref, kn_ref, vn_ref, ln_ref,
                 mur_ref, muk_ref, muv_ref, mul_ref, w0_ref, wup_ref, a0_ref, aup_ref, kkw_ref, ka_ref, rk_ref,
                 gm_ref,
                 r_o, kk_o, v_o, w_o, kka_o, kd_o, bon_o, *, tiles_per_seq):
    i = pl.program_id(0)
    first = lax.rem(i, tiles_per_seq) == 0
    last = lax.rem(i, tiles_per_seq) == tiles_per_seq - 1

    low = _token_shift(lo_ref[...], lp_ref[...], ln_ref[...], mul_ref[...], first, last)
    low_t = jnp.tanh(low)
    gm = gm_ref[...]

    for c in range(A_WIDTH // CW):
        cols = slice(c * CW, (c + 1) * CW)
        r = _token_shift(r_ref[:, cols], rp_ref[:, cols], rn_ref[:, cols], mur_ref[:, cols], first, last)
        k = _token_shift(k_ref[:, cols], kp_ref[:, cols], kn_ref[:, cols], muk_ref[:, cols], first, last)
        v = _token_shift(v_ref[:, cols], vp_ref[:, cols], vn_ref[:, cols], muv_ref[:, cols], first, last)
        kk = k * kkw_ref[:, cols]
        nrm = jnp.sqrt(_gsum(kk * kk, gm))
        kk = kk / jnp.maximum(nrm, 1e-12)
        ka = ka_ref[:, cols]
        kd_sum = None
        for d in range(2):
            w_raw = w0_ref[d:d + 1, cols] + _dot3(low_t, wup_ref[d, :, cols])
            z = -w_raw
            sp = jnp.maximum(z, 0.0) + jnp.log(1.0 + jnp.exp(-jnp.abs(z)))
            a = _sigmoid(a0_ref[d:d + 1, cols] + _dot3(low, aup_ref[d, :, cols]))
            kd = k * (1.0 + (a - 1.0) * ka)
            w_o[d, :, cols] = jnp.exp(-jnp.exp(-sp - 0.5))
            kka_o[d, :, cols] = kk * a
            kd_o[d, :, cols] = kd
            kd_sum = kd if kd_sum is None else kd_sum + kd
        r_o[:, cols] = r
        kk_o[:, cols] = kk
        v_o[:, cols] = v
        bon_o[:, cols] = _gsum(r * (0.5 * kd_sum) * rk_ref[:, cols], gm) * v


def _wkv_prep(st, proj, mu, w0, wup_pad, a0, aup_pad, k_k, k_a, r_k, gmat):
    tm = TM_ROW
    nb8 = st.rows // SUBLANES
    ncb = A_WIDTH // CW
    low_cb = 3 * A_WIDTH // CW
    assert 4 * LORA == CW and st.t_len % tm == 0

    prev_row = lambda i: jnp.maximum(i * (tm // SUBLANES) - 1, 0)
    next_row = lambda i: jnp.minimum((i + 1) * (tm // SUBLANES), nb8 - 1)

    def main(cb):
        return pl.BlockSpec((tm, A_WIDTH), lambda i: (i, cb))

    def prev(cb):
        return pl.BlockSpec((SUBLANES, A_WIDTH), lambda i: (prev_row(i), cb))

    def nxt(cb):
        return pl.BlockSpec((SUBLANES, A_WIDTH), lambda i: (next_row(i), cb))

    def fixed(spec_fn):
        return [spec_fn(0), spec_fn(1), spec_fn(2)]

    low_main = pl.BlockSpec((tm, CW), lambda i: (i, low_cb))
    low_prev = pl.BlockSpec((SUBLANES, CW), lambda i: (prev_row(i), low_cb))
    low_next = pl.BlockSpec((SUBLANES, CW), lambda i: (next_row(i), low_cb))

    def vec(cb):
        return pl.BlockSpec((1, A_WIDTH), lambda i: (0, cb))

    in_specs = (fixed(main) + [low_main] + fixed(prev) + [low_prev] + fixed(nxt) + [low_next]
                + [vec(0), vec(1), vec(2), pl.BlockSpec((1, CW), lambda i: (0, low_cb))]
                + [pl.BlockSpec((2, A_WIDTH), lambda i: (0, 0)),
                   pl.BlockSpec((2, CW, A_WIDTH), lambda i: (0, 0, 0)),
                   pl.BlockSpec((2, A_WIDTH), lambda i: (0, 0)),
                   pl.BlockSpec((2, CW, A_WIDTH), lambda i: (0, 0, 0)),
                   vec(0), vec(0), vec(0),
                   pl.BlockSpec((CW, CW), lambda i: (0, 0))])
    tps = st.t_len // tm
    g_n = st.n_seq // SEQ_PER_GROUP

    def til(i):
        s = i // tps
        return (s // SEQ_PER_GROUP, lax.rem(i, tps), lax.rem(s, SEQ_PER_GROUP))

    one = pl.BlockSpec((tm, A_WIDTH), lambda i: (i, 0))
    til1 = pl.BlockSpec((None, tm, A_WIDTH), lambda i: til(i))
    til2 = pl.BlockSpec((2, None, tm, A_WIDTH), lambda i: (0,) + til(i))
    one_sds = jax.ShapeDtypeStruct((st.rows, A_WIDTH), F32)
    til1_sds = jax.ShapeDtypeStruct((g_n, st.t_len, SEQ_PER_GROUP * A_WIDTH), F32)
    til2_sds = jax.ShapeDtypeStruct((2, g_n, st.t_len, SEQ_PER_GROUP * A_WIDTH), F32)
    return pl.pallas_call(
        functools.partial(_prep_kernel, tiles_per_seq=tps),
        out_shape=[til1_sds, til1_sds, til1_sds, til2_sds, til2_sds, til2_sds, one_sds],
        grid=(st.rows // tm,),
        in_specs=in_specs,
        out_specs=[til1, til1, til1, til2, til2, til2, one],
        compiler_params=_cparams(("parallel",)),
        name="wkv_prep",
    )(*([proj] * 12), mu, mu, mu, mu, w0, wup_pad, a0, aup_pad, k_k, k_a, r_k, gmat)


def _wkv_kernel(rf_ref, rb_ref, kkf_ref, kkb_ref, vf_ref, vb_ref, wf_ref, wb_ref, kkaf_ref, kkab_ref, kdf_ref, kdb_ref,
                s0_ref, of_ref, ob_ref, sf_ref,
                s2_ref, r2_ref, kk2_ref, v2_ref, kka2_ref, kd2_ref, g_ref, o_ref, *, tt_steps):
    c = pl.program_id(1)
    nblk = WKV_ROWS // SUBLANES
    n_hv = A_HEADS // 2
    pairs = ((rf_ref, rb_ref, r2_ref), (kkf_ref, kkb_ref, kk2_ref), (vf_ref, vb_ref, v2_ref),
             (wf_ref, wb_ref, None), (kkaf_ref, kkab_ref, kka2_ref), (kdf_ref, kdb_ref, kd2_ref))

    @pl.when(c == 0)
    def _():
        s2_ref[...] = s0_ref[...]

    def to_chain_tiles(u, gammas):
        ub = tt_steps - 1 - u
        halves = []
        for xf_ref, xb_ref, _ in pairs:
            x2 = jnp.concatenate([xf_ref[u, :, hv * LANES:(hv + 1) * LANES] for hv in range(n_hv)]
                                 + [xb_ref[ub, :, hv * LANES:(hv + 1) * LANES] for hv in range(n_hv)], axis=0)
            y = x2.T
            halves.append((y[:A_HEAD], y[A_HEAD:]))
        r_h, kk_h, v_h, w_h, kka_h, kd_h = halves
        new_gammas = []
        for hp in range(2):
            gamma = gammas[hp] * w_h[hp]
            inv = 1.0 / gamma
            kk2_ref[hp, u] = gammas[hp] * kk_h[hp]
            kka2_ref[hp, u] = kka_h[hp] * inv
            kd2_ref[hp, u] = kd_h[hp] * inv
            r2_ref[hp, u] = r_h[hp] * gamma
            v2_ref[hp, u] = v_h[hp]
            new_gammas.append(gamma)
        return tuple(new_gammas)

    one = jnp.ones((A_HEAD, CHAINS), F32)
    gammas = lax.fori_loop(0, tt_steps, to_chain_tiles, (one, one))
    for hp in range(2):
        g_ref[hp] = gammas[hp]

    def bcast_row(ref, t, j):
        return ref[t, pl.ds(j, SUBLANES, stride=0), :]

    def time_step(hp, t):
        s_ref = s2_ref.at[hp]
        r_ref, kk_ref, v_ref, kka_ref, kd_ref = (ref.at[hp] for ref in (r2_ref, kk2_ref, v2_ref, kka2_ref, kd2_ref))
        for ib in range(A_HEAD // WKV_ROWS):
            i0 = ib * WKV_ROWS
            rows = [pl.ds(i0 + SUBLANES * b, SUBLANES) for b in range(nblk)]
            sa = [[None] * WKV_ACCS for _ in range(nblk)]
            for j in range(A_HEAD):
                kkj = bcast_row(kk_ref, t, j)
                for b in range(nblk):
                    p = s_ref[j, rows[b], :] * kkj
                    sa[b][j % WKV_ACCS] = p if sa[b][j % WKV_ACCS] is None else sa[b][j % WKV_ACCS] + p
            sa = [functools.reduce(lambda x, y: x + y, parts) for parts in sa]
            v8 = [v_ref[t, rows[b], :] for b in range(nblk)]
            out = [[None] * WKV_ACCS for _ in range(nblk)]
            for j in range(A_HEAD):
                kkaj = bcast_row(kka_ref, t, j)
                kdj = bcast_row(kd_ref, t, j)
                rj = bcast_row(r_ref, t, j)
                for b in range(nblk):
                    sn = s_ref[j, rows[b], :] - sa[b] * kkaj + v8[b] * kdj
                    s_ref[j, rows[b], :] = sn
                    q = sn * rj
                    out[b][j % WKV_ACCS] = q if out[b][j % WKV_ACCS] is None else out[b][j % WKV_ACCS] + q
            for b in range(nblk):
                o_ref[hp, rows[b], :] = functools.reduce(lambda x, y: x + y, out[b])

    def step(u, carry):
        for hp in range(2):
            time_step(hp, u)
        x2 = jnp.concatenate([o_ref[0], o_ref[1]], axis=0).T
        ub = tt_steps - 1 - u
        for hv in range(n_hv):
            of_ref[u, :, hv * LANES:(hv + 1) * LANES] = x2[hv * SUBLANES:(hv + 1) * SUBLANES, :]
            ob_ref[ub, :, hv * LANES:(hv + 1) * LANES] = x2[(n_hv + hv) * SUBLANES:(n_hv + hv + 1) * SUBLANES, :]
        return carry

    lax.fori_loop(0, tt_steps, step, 0)

    def rescale(j, carry):
        for hp in range(2):
            gj = g_ref[hp, pl.ds(j, SUBLANES, stride=0), :]
            for b in range(A_HEAD // SUBLANES):
                rows = pl.ds(b * SUBLANES, SUBLANES)
                s2_ref[hp, j, rows, :] = s2_ref[hp, j, rows, :] * gj
        return carry

    lax.fori_loop(0, A_HEAD, rescale, 0)

    @pl.when(c == pl.num_programs(1) - 1)
    def _():
        sf_ref[...] = s2_ref[...]


def _wkv(r, kk, v, w, kka, kd, s0):
    g_n, t_n = r.shape[0], r.shape[1]
    tt = WKV_TT
    nc = t_n // tt
    split = lambda a: a.reshape(a.shape[:-1] + (SEQ_PER_GROUP, A_WIDTH))
    blk = (tt, SEQ_PER_GROUP, A_WIDTH)

    fwd = pl.BlockSpec((None,) + blk, lambda g, c: (g, c, 0, 0))
    bwd = pl.BlockSpec((None,) + blk, lambda g, c: (g, nc - 1 - c, 0, 0))
    fwd_d = pl.BlockSpec((None, None) + blk, lambda g, c: (0, g, c, 0, 0))
    bwd_d = pl.BlockSpec((None, None) + blk, lambda g, c: (1, g, nc - 1 - c, 0, 0))
    state = pl.BlockSpec((None, 2, A_HEAD, A_HEAD, CHAINS), lambda g, c: (g, 0, 0, 0, 0))
    tiles = pltpu.VMEM((2, tt, A_HEAD, CHAINS), F32)
    o_sds = jax.ShapeDtypeStruct((g_n, t_n, SEQ_PER_GROUP, A_WIDTH), F32)
    r, kk, v, w, kka, kd = (split(a) for a in (r, kk, v, w, kka, kd))
    o_f, o_b, s_fin = pl.pallas_call(
        functools.partial(_wkv_kernel, tt_steps=tt),
        out_shape=[o_sds, o_sds, jax.ShapeDtypeStruct((g_n, 2, A_HEAD, A_HEAD, CHAINS), F32)],
        grid=(g_n, nc),
        in_specs=[fwd, bwd, fwd, bwd, fwd, bwd, fwd_d, bwd_d, fwd_d, bwd_d, fwd_d, bwd_d, state],
        out_specs=[fwd, bwd, state],
        scratch_shapes=([pltpu.VMEM((2, A_HEAD, A_HEAD, CHAINS), F32)] + [tiles] * 5
                        + [pltpu.VMEM((2, A_HEAD, CHAINS), F32)] * 2),
        compiler_params=pltpu.CompilerParams(dimension_semantics=("parallel", "arbitrary"),
                                             vmem_limit_bytes=WKV_VMEM_LIMIT),
        name="wkv",
    )(r, r, kk, kk, v, v, w, w, kka, kka, kd, kd, s0)
    merge = lambda a: a.reshape(g_n, t_n, SEQ_PER_GROUP * A_WIDTH)
    return merge(o_f), merge(o_b), s_fin


def _ya_kernel(of_ref, ob_ref, bon_ref, g0_ref, g1_ref, g2_ref, g3_ref, gw_ref, gb_ref, gm_ref, o_ref):
    gm = gm_ref[...]
    for c, gate_ref in enumerate((g0_ref, g1_ref, g2_ref, g3_ref)):
        cols = slice(c * CW, (c + 1) * CW)
        o = of_ref[:, cols] + ob_ref[:, cols]
        mean = _gsum(o, gm) * (1.0 / A_HEAD)
        dev = o - mean
        var = _gsum(dev * dev, gm) * (1.0 / A_HEAD)
        y = dev * lax.rsqrt(var + GN_EPS) * gw_ref[:, cols] + gb_ref[:, cols]
        o_ref[:, cols] = ((y + bon_ref[:, cols]) * _silu(gate_ref[...])).astype(BF16)


def _ya(st, o_f, o_b, bonus, proj, gn_w, gn_b, gmat):
    tm = TM_ROW
    gate_cb = (SHIFT_COLS) // CW
    blk = pl.BlockSpec((tm, A_WIDTH), lambda i: (i, 0))
    vec = pl.BlockSpec((1, A_WIDTH), lambda i: (0, 0))
    tps = st.t_len // tm

    def til_idx(i):
        s = i // tps
        return (s // SEQ_PER_GROUP, lax.rem(i, tps), lax.rem(s, SEQ_PER_GROUP))

    til = pl.BlockSpec((None, tm, A_WIDTH), til_idx)
    gates = [pl.BlockSpec((tm, CW), lambda i, c=c: (i, gate_cb + c)) for c in range(A_WIDTH // CW)]
    return pl.pallas_call(
        _ya_kernel,
        out_shape=jax.ShapeDtypeStruct((st.rows, A_WIDTH), BF16),
        grid=(st.rows // tm,),
        in_specs=[til, til, blk] + gates + [vec, vec, pl.BlockSpec((CW, CW), lambda i: (0, 0))],
        out_specs=blk,
        compiler_params=_cparams(("parallel",)),
        name="wkv_post",
    )(o_f, o_b, bonus, proj, proj, proj, proj, gn_w, gn_b, gmat)


def _dft_ch_kernel(u_ref, cs_ref, zc_ref, zs_ref):
    z = _dot(u_ref[...].astype(BF16), cs_ref[...])
    zc_ref[...] = z[:, :B_GROUP_CH].astype(BF16)
    zs_ref[...] = z[:, B_GROUP_CH:].astype(BF16)


def _dft_channels(st, proj, cs_mat):
    tm = 2048
    u_cb = (SHIFT_COLS + A_WIDTH) // B_GROUP_CH
    blk = pl.BlockSpec((tm, B_GROUP_CH), lambda i, g: (i, g))
    sds = jax.ShapeDtypeStruct((st.rows, B_WIDTH), BF16)
    return pl.pallas_call(
        _dft_ch_kernel,
        out_shape=[sds, sds],
        grid=(st.rows // tm, B_GROUPS),
        in_specs=[pl.BlockSpec((tm, B_GROUP_CH), lambda i, g: (i, u_cb + g)),
                  pl.BlockSpec((B_GROUP_CH, 2 * B_GROUP_CH), lambda i, g: (0, 0))],
        out_specs=[blk, blk],
        compiler_params=_cparams(("parallel", "parallel")),
        name="dft_channels",
    )(proj, cs_mat)


def _dft_time_kernel(c_ref, s_ref, zc_ref, zs_ref, g0_ref, g1_ref, g2_ref, g3_ref, o_ref, *, scale):
    acc = _dot(c_ref[...], zc_ref[...]) - _dot(s_ref[...], zs_ref[...])
    for g, gate_ref in enumerate((g0_ref, g1_ref, g2_ref, g3_ref)):
        cols = slice(g * B_GROUP_CH, (g + 1) * B_GROUP_CH)
        o_ref[:, cols] = (acc[:, cols] * scale * _silu(gate_ref[...])).astype(BF16)


def _dft_time(st, zc, zs, proj, cmat, smat):
    t_len = st.t_len
    tm = min(t_len, 512)
    mt = t_len // tm
    gate_cb = (SHIFT_COLS + A_WIDTH + B_WIDTH) // B_GROUP_CH
    gate_specs = [pl.BlockSpec((tm, B_GROUP_CH), lambda b, m, g=g: (b * mt + m, gate_cb + g)) for g in range(B_GROUPS)]
    scale = 1.0 / float(np.sqrt(t_len * B_GROUP_CH))
    return pl.pallas_call(
        functools.partial(_dft_time_kernel, scale=scale),
        out_shape=jax.ShapeDtypeStruct((st.rows, B_WIDTH), BF16),
        grid=(st.n_seq, mt),
        in_specs=[pl.BlockSpec((tm, t_len), lambda b, m: (m, 0)),
                  pl.BlockSpec((tm, t_len), lambda b, m: (m, 0)),
                  pl.BlockSpec((t_len, B_WIDTH), lambda b, m: (b, 0)),
                  pl.BlockSpec((t_len, B_WIDTH), lambda b, m: (b, 0))] + gate_specs,
        out_specs=pl.BlockSpec((tm, B_WIDTH), lambda b, m: (b * mt + m, 0)),
        compiler_params=_cparams(("parallel", "parallel")),
        name="dft_time",
    )(cmat, smat, zc, zs, proj, proj, proj, proj)


def _dft_mats(n):
    idx = jnp.arange(n, dtype=jnp.int32)
    prod = (idx[:, None] * idx[None, :]) % n
    ang = prod.astype(F32) * (2.0 * np.pi / n)
    return jnp.cos(ang), jnp.sin(ang)


def _out_kernel(a1_ref, a2_ref, w1_ref, w2_ref, x_ref, gate_ref, g_ref, o_ref):
    y = _dot(a1_ref[...], w1_ref[...]) + _dot(a2_ref[...], w2_ref[...])
    ms = jnp.mean(y * y, axis=-1, keepdims=True)
    yn = y * lax.rsqrt(ms + RMS_EPS) * g_ref[...]
    o_ref[...] = x_ref[...] + gate_ref[...] * yn


def _out_proj(st, a1, a1_cb, a2, a2_cb, w_bf16, li, x, mods4, norm_g, layer):
    tm = TM_ROW
    half = D_MODEL // 2
    cond = lambda i: _cond_of_row(st, i * tm)
    return pl.pallas_call(
        _out_kernel,
        out_shape=jax.ShapeDtypeStruct((st.rows, D_MODEL), F32),
        grid=(st.rows // tm,),
        in_specs=[pl.BlockSpec((tm, half), lambda i: (i, a1_cb)),
                  pl.BlockSpec((tm, half), lambda i: (i, a2_cb)),
                  pl.BlockSpec((None, half, D_MODEL), lambda i: (li, 0, 0)),
                  pl.BlockSpec((None, half, D_MODEL), lambda i: (li, 1, 0)),
                  pl.BlockSpec((tm, D_MODEL), lambda i: (i, 0)),
                  pl.BlockSpec((None, None, 1, D_MODEL), lambda i: (layer, cond(i), 0, 2)),
                  pl.BlockSpec((None, 1, D_MODEL), lambda i: (layer, 0, 0))],
        out_specs=pl.BlockSpec((tm, D_MODEL), lambda i: (i, 0)),
        compiler_params=_cparams(("parallel",)),
        name="out_proj",
    )(a1, a2, w_bf16, w_bf16, x, mods4, norm_g)


def _softmax_pv(scores, values, sink):
    m = sink
    for s in scores:
        m = jnp.maximum(m, jnp.max(s, axis=-1, keepdims=True))
    den = jnp.exp(sink - m)
    acc = None
    for s, v in zip(scores, values):
        p = jnp.exp(s - m)
        den = den + jnp.sum(p, axis=-1, keepdims=True)
        pv = _dot(p.astype(BF16), v)
        acc = pv if acc is None else acc + pv
    return acc / den


def _qk(q, k):
    return lax.dot_general(q, k, (((1,), (1,)), ((), ())), preferred_element_type=F32)


def _upper_half(rows):
    return lax.broadcasted_iota(jnp.int32, (rows, LANES), 1) >= C_HEAD


def _both_halves(x, hh):
    upper = _upper_half(x.shape[0])
    keep = upper if hh == 1 else jnp.logical_not(upper)
    return jnp.where(keep, x, pltpu.roll(x, C_HEAD, axis=1)).astype(BF16)


def _pair_heads(q_chunks, keys, values, sinks, masks):
    rows = q_chunks[0].shape[0]
    upper = _upper_half(rows)
    lower = jnp.logical_not(upper)
    outs = [None] * (2 * C_GROUP)
    for hh in range(2):
        kd = [_both_halves(k, hh) for k in keys]
        vd = [_both_halves(v, hh) for v in values]
        heads = [hh * C_GROUP + g for g in range(C_GROUP)]
        q4 = jnp.concatenate([jnp.where(upper if n % 2 else lower, q_chunks[n // 2], 0.0) for n in heads],
                             axis=0).astype(BF16)
        s4 = [_qk(q4, k) for k in kd]
        p4 = [[] for _ in kd]
        dens = []
        for g, n in enumerate(heads):
            sl = slice(g * rows, (g + 1) * rows)
            scores = [s[sl] if mask is None else jnp.where(mask, s[sl], NEG_INF) for s, mask in zip(s4, masks)]
            sink = sinks(n)
            m = sink
            for s in scores:
                m = jnp.maximum(m, jnp.max(s, axis=-1, keepdims=True))
            den = jnp.exp(sink - m)
            for t, s in enumerate(scores):
                p = jnp.exp(s - m)
                den = den + jnp.sum(p, axis=-1, keepdims=True)
                p4[t].append(p.astype(BF16))
            dens.append(den)
        pv4 = None
        for p_parts, v in zip(p4, vd):
            pv = _dot(jnp.concatenate(p_parts, axis=0), v)
            pv4 = pv if pv4 is None else pv4 + pv
        for g, n in enumerate(heads):
            outs[n] = pv4[g * rows:(g + 1) * rows] / dens[g]
    return jnp.concatenate([jnp.where(lower, outs[2 * m], outs[2 * m + 1]) for m in range(C_GROUP)], axis=-1)


def _attn_ctx_kernel(sink_ref, q_ref, k_ref, v_ref, gate_ref, o_ref):
    kp = pl.program_id(1)
    q = q_ref[...] * (C_HEAD ** -0.5)
    q_chunks = [q[:, m * LANES:(m + 1) * LANES] for m in range(C_GROUP)]
    o = _pair_heads(q_chunks, [k_ref[...]], [v_ref[...]], lambda n: sink_ref[kp * 2 * C_GROUP + n], [None])
    o_ref[...] = (o * _silu(gate_ref[...])).astype(BF16)


def _attn_context(proj, sink):
    qw = 2 * C_GROUP * C_HEAD
    k_cb = C_HEADS * C_HEAD // LANES
    v_cb = (C_HEADS + C_KV_HEADS) * C_HEAD // LANES
    gate_cb = (C_HEADS + 2 * C_KV_HEADS) * C_HEAD // qw
    return pl.pallas_call(
        _attn_ctx_kernel,
        out_shape=jax.ShapeDtypeStruct((PROMPT.rows, D_MODEL), BF16),
        grid=(BATCH, C_KV_HEADS // 2),
        in_specs=[pl.BlockSpec(memory_space=pltpu.SMEM),
                  pl.BlockSpec((SEQ, qw), lambda b, kp: (b, kp)),
                  pl.BlockSpec((SEQ, LANES), lambda b, kp: (b, k_cb + kp)),
                  pl.BlockSpec((SEQ, LANES), lambda b, kp: (b, v_cb + kp)),
                  pl.BlockSpec((SEQ, qw), lambda b, kp: (b, gate_cb + kp))],
        out_specs=pl.BlockSpec((SEQ, qw), lambda b, kp: (b, kp)),
        compiler_params=_cparams(("parallel", "parallel")),
        name="attn_context",
    )(sink, proj, proj, proj, proj)


def _rope(x, cos, sin_signed):
    lane = lax.broadcasted_iota(jnp.int32, x.shape, 1)
    first = (lane & 31) < 16
    partner = jnp.where(first, pltpu.roll(x, LANES - 16, axis=1), pltpu.roll(x, 16, axis=1))
    return x * cos + partner * sin_signed


def _attn_lat_kernel(sink_ref, q_ref, kp_ref, ko_ref, kn_ref, vp_ref, vo_ref, vn_ref, ck_ref, cv_ref,
                     cq_ref, sq_ref, cp_ref, sp_ref, cn_ref, sn_ref, gate_ref, o_ref):
    kpair = pl.program_id(1)
    qb = pl.program_id(2)
    cq = cq_ref[...]
    sq = sq_ref[...]
    q = q_ref[...] * (C_HEAD ** -0.5)
    qr = [_rope(q[:, n * LANES:(n + 1) * LANES], cq, sq) for n in range(4)]
    kband = jnp.concatenate([_rope(kp_ref[...], cp_ref[...], sp_ref[...]),
                             _rope(ko_ref[...], cq, sq),
                             _rope(kn_ref[...], cn_ref[...], sn_ref[...])], axis=0)
    vband = jnp.concatenate([vp_ref[...], vo_ref[...], vn_ref[...]], axis=0)
    qpos = qb * BLOCK + lax.broadcasted_iota(jnp.int32, (BLOCK, 3 * BLOCK), 0)
    kpos = (qb - 1) * BLOCK + lax.broadcasted_iota(jnp.int32, (BLOCK, 3 * BLOCK), 1)
    valid = (jnp.abs(qpos - kpos) <= WINDOW) & (kpos >= 0) & (kpos < DEC_SEQ)
    o = _pair_heads(qr, [kband, ck_ref[...]], [vband, cv_ref[...]],
                    lambda n: sink_ref[kpair * 2 * C_GROUP + n], [valid, None])
    o_ref[...] = (o * _silu(gate_ref[...])).astype(BF16)


def _attn_latent(proj, cache_k4, cache_v4, li, sink, cos_t, sin_t):
    qw = 2 * C_GROUP * C_HEAD
    k_cb = C_HEADS * C_HEAD // LANES
    v_cb = (C_HEADS + C_KV_HEADS) * C_HEAD // LANES
    gate_cb = (C_HEADS + 2 * C_KV_HEADS) * C_HEAD // qw
    nqb = DEC_SEQ // BLOCK

    def rows(delta):
        def idx(b, kp, qb):
            return b * nqb + jnp.clip(qb + delta, 0, nqb - 1)
        return idx

    def kv_spec(cb, delta):
        r = rows(delta)
        return pl.BlockSpec((BLOCK, LANES), lambda b, kp, qb: (r(b, kp, qb), cb + kp))

    def tab_spec(delta):
        return pl.BlockSpec((BLOCK, LANES), lambda b, kp, qb: (jnp.clip(qb + delta, 0, nqb - 1), 0))

    cache_spec = pl.BlockSpec((None, None, PAST_LEN, LANES), lambda b, kp, qb: (b, li, 0, kp))
    own = rows(0)
    return pl.pallas_call(
        _attn_lat_kernel,
        out_shape=jax.ShapeDtypeStruct((SAMPLE.rows, D_MODEL), BF16),
        grid=(DEC_BATCH, C_KV_HEADS // 2, nqb),
        in_specs=[pl.BlockSpec(memory_space=pltpu.SMEM),
                  pl.BlockSpec((BLOCK, qw), lambda b, kp, qb: (own(b, kp, qb), kp)),
                  kv_spec(k_cb, -1), kv_spec(k_cb, 0), kv_spec(k_cb, 1),
                  kv_spec(v_cb, -1), kv_spec(v_cb, 0), kv_spec(v_cb, 1),
                  cache_spec, cache_spec,
                  tab_spec(0), tab_spec(0), tab_spec(-1), tab_spec(-1), tab_spec(1), tab_spec(1),
                  pl.BlockSpec((BLOCK, qw), lambda b, kp, qb: (own(b, kp, qb), gate_cb + kp))],
        out_specs=pl.BlockSpec((BLOCK, qw), lambda b, kp, qb: (own(b, kp, qb), kp)),
        compiler_params=_cparams(("parallel", "parallel", "parallel")),
        name="attn_latent",
    )(sink, proj, proj, proj, proj, proj, proj, proj, cache_k4, cache_v4,
      cos_t, sin_t, cos_t, sin_t, cos_t, sin_t, proj)


def _rope_tables():
    t = jnp.arange(DEC_SEQ, dtype=jnp.int32)
    row = (t // GRID_W).astype(F32)
    col = (t % GRID_W).astype(F32)
    nf = C_HEAD // 4
    inv = 1.0 / (ROPE_BASE ** (jnp.arange(nf, dtype=F32) / nf))
    lane = np.arange(LANES)
    f_of_lane = lane % nf
    use_col = (lane % C_HEAD) >= C_HEAD // 2
    sign = np.where((lane % 32) < 16, -1.0, 1.0).astype(np.float32)
    pos = jnp.where(jnp.asarray(use_col)[None, :], col[:, None], row[:, None])
    ang = pos * inv[jnp.asarray(f_of_lane)][None, :]
    return jnp.cos(ang), jnp.sin(ang) * jnp.asarray(sign)[None, :]


def _even_mixer(st, proj, s0, p, gmat, cs_mat, dft):
    r, kk, v, w2, kka2, kd2, bonus = _wkv_prep(st, proj, p["mu"], p["w0"], p["wup"], p["a0"], p["aup"],
                                                p["k_k"], p["k_a"], p["r_k"], gmat)
    o_f, o_b, s_fin = _wkv(r, kk, v, w2, kka2, kd2, s0)
    ya = _ya(st, o_f, o_b, bonus, proj, p["gn_w"], p["gn_b"], gmat)
    zc, zs = _dft_channels(st, proj, cs_mat)
    yb = _dft_time(st, zc, zs, proj, dft[0], dft[1])
    return ya, yb, s_fin


def kernel(x_prompt, x_sample, c, state_wkv, cache_k, cache_v, c_ctx, mod_w, mod_b, norm_pre, norm_post,
           even_w_in, even_mu, even_w0, even_w_up, even_a0, even_a_up, even_k_k, even_k_a, even_r_k,
           even_gn_w, even_gn_b, even_w_out, odd_w_in, odd_sink, odd_w_out):
    n_odd = odd_w_in.shape[0]
    xs = {PROMPT: x_prompt.reshape(PROMPT.rows, D_MODEL), SAMPLE: x_sample.reshape(SAMPLE.rows, D_MODEL)}
    conds = jnp.concatenate([c_ctx[None, :], c, jnp.zeros((N_COND - 1 - DEC_BATCH, D_MODEL), F32)], axis=0)
    mods4 = _modulation(conds, mod_w, mod_b).reshape(DEPTH, N_COND, 1, 3 * D_MODEL)
    norm_pre3 = norm_pre.reshape(DEPTH, 1, D_MODEL)
    norm_post3 = norm_post.reshape(DEPTH, 1, D_MODEL)

    even_w_in_b = even_w_in.astype(BF16)
    even_w_out_b = even_w_out.astype(BF16)
    odd_w_in_b = odd_w_in.astype(BF16)
    odd_w_out_b = odd_w_out.astype(BF16)

    head_of_lane = np.arange(CW) // A_HEAD
    gmat = jnp.asarray((head_of_lane[:, None] == head_of_lane[None, :]).astype(np.float32)).astype(BF16)
    c_ch, s_ch = _dft_mats(B_GROUP_CH)
    cs_mat = jnp.concatenate([c_ch, s_ch], axis=1).astype(BF16)
    dft = {PROMPT: tuple(m.astype(BF16) for m in _dft_mats(SEQ)),
           SAMPLE: tuple(m.astype(BF16) for m in _dft_mats(DEC_SEQ))}
    cos_t, sin_t = _rope_tables()
    cache_k4 = cache_k.reshape(DEC_BATCH, n_odd, PAST_LEN, C_KV_HEADS * C_HEAD)
    cache_v4 = cache_v.reshape(DEC_BATCH, n_odd, PAST_LEN, C_KV_HEADS * C_HEAD)

    new_wkv, new_k, new_v = [], [], []
    for layer in range(DEPTH):
        i = layer // 2
        if layer % 2 == 0:
            zeros = jnp.zeros((LORA, A_WIDTH), F32)
            p = {
                "mu": even_mu[i][None, :], "w0": even_w0[i], "a0": even_a0[i],
                "wup": jnp.stack([jnp.concatenate([even_w_up[i, 0], zeros, zeros, zeros], axis=0),
                                  jnp.concatenate([zeros, even_w_up[i, 1], zeros, zeros], axis=0)]),
                "aup": jnp.stack([jnp.concatenate([zeros, zeros, even_a_up[i, 0], zeros], axis=0),
                                  jnp.concatenate([zeros, zeros, zeros, even_a_up[i, 1]], axis=0)]),
                "k_k": even_k_k[i][None, :], "k_a": even_k_a[i][None, :], "r_k": even_r_k[i].reshape(1, A_WIDTH),
                "gn_w": even_gn_w[i][None, :], "gn_b": even_gn_b[i][None, :],
            }
            s0s = state_wkv[:, i].reshape(DEC_BATCH, 2, A_HEADS // 2, 2, A_HEAD, A_HEAD)
            s0 = {PROMPT: jnp.zeros((BATCH // SEQ_PER_GROUP, 2, A_HEAD, A_HEAD, CHAINS), F32),
                  SAMPLE: s0s.transpose(3, 5, 4, 1, 2, 0).reshape(1, 2, A_HEAD, A_HEAD, CHAINS)}
            for st in (PROMPT, SAMPLE):
                proj = _in_proj(st, xs[st], norm_pre3, mods4, even_w_in_b, layer, i)
                ya, yb, s_fin = _even_mixer(st, proj, s0[st], p, gmat, cs_mat, dft[st])
                xs[st] = _out_proj(st, ya, 0, yb, 0, even_w_out_b, i, xs[st], mods4, norm_post3, layer)
                if st is PROMPT:
                    s_fin = s_fin.reshape(BATCH // SEQ_PER_GROUP, 2, A_HEAD, A_HEAD, 2, A_HEADS // 2, SEQ_PER_GROUP)
                    new_wkv.append(s_fin.transpose(0, 6, 4, 5, 1, 3, 2).reshape(BATCH, 2, A_HEADS, A_HEAD, A_HEAD))
        else:
            proj_p = _in_proj(PROMPT, xs[PROMPT], norm_pre3, mods4, odd_w_in_b, layer, i)
            proj_s = _in_proj(SAMPLE, xs[SAMPLE], norm_pre3, mods4, odd_w_in_b, layer, i)
            kv0 = C_HEADS * C_HEAD
            kvn = C_KV_HEADS * C_HEAD
            new_k.append(proj_p[:, kv0:kv0 + kvn].reshape(BATCH, SEQ, C_KV_HEADS, C_HEAD))
            new_v.append(proj_p[:, kv0 + kvn:kv0 + 2 * kvn].reshape(BATCH, SEQ, C_KV_HEADS, C_HEAD))
            att_p = _attn_context(proj_p, odd_sink[i])
            att_s = _attn_latent(proj_s, cache_k4, cache_v4, i, odd_sink[i], cos_t, sin_t)
            xs[PROMPT] = _out_proj(PROMPT, att_p, 0, att_p, 1, odd_w_out_b, i, xs[PROMPT], mods4, norm_post3, layer)
            xs[SAMPLE] = _out_proj(SAMPLE, att_s, 0, att_s, 1, odd_w_out_b, i, xs[SAMPLE], mods4, norm_post3, layer)

    y_prompt = xs[PROMPT].reshape(BATCH, SEQ, D_MODEL)
    y_sample = xs[SAMPLE].reshape(DEC_BATCH, DEC_SEQ, D_MODEL)
    return (y_prompt, y_sample, jnp.stack(new_wkv, axis=1), jnp.stack(new_k, axis=1), jnp.stack(new_v, axis=1))
```

```python
import collections
import functools

import numpy as np
import jax
import jax.numpy as jnp
from jax import lax
from jax.experimental import pallas as pl
from jax.experimental.pallas import tpu as pltpu

F32 = jnp.float32
BF16 = jnp.bfloat16

D_MODEL = 2048
BATCH = 32
SEQ = 256
DEPTH = 4
DEC_BATCH = 8
DEC_SEQ = 2048
PAST_LEN = 256
GRID_W = 64
RMS_EPS = 1e-6
A_WIDTH = 1024
A_HEAD = 64
A_HEADS = 16
LORA = 64
GN_EPS = 64e-5
B_WIDTH = 1024
B_GROUPS = 4
B_GROUP_CH = 256
SHIFT_COLS = 3 * A_WIDTH + 4 * LORA
EVEN_IN = SHIFT_COLS + A_WIDTH + 2 * B_WIDTH
C_HEAD = 64
C_HEADS = 32
C_KV_HEADS = 8
C_GROUP = 4
WINDOW = 128
BLOCK = 128
ROPE_BASE = 10000.0
ODD_IN = (C_HEADS + 2 * C_KV_HEADS) * C_HEAD + D_MODEL
NEG_INF = -1e30

N_COND = 16
SUBLANES = 8
LANES = 128
CHAINS = LANES
SEQ_PER_GROUP = CHAINS // A_HEADS
CW = 256
TM_IN = 512
TN_IN = 1280
TM_ROW = 256
WKV_TT = 16
WKV_VMEM_LIMIT = 56 * 1024 * 1024
WKV_ROWS = 16
WKV_ACCS = 2
VMEM_LIMIT = 48 * 1024 * 1024

Stream = collections.namedtuple("Stream", "rows t_len n_seq cond0 cond_per_seq")
PROMPT = Stream(BATCH * SEQ, SEQ, BATCH, 0, 0)
SAMPLE = Stream(DEC_BATCH * DEC_SEQ, DEC_SEQ, DEC_BATCH, 1, 1)


def _cparams(sem):
    return pltpu.CompilerParams(dimension_semantics=sem, vmem_limit_bytes=VMEM_LIMIT)


def _cond_of_row(st, row0):
    return st.cond0 + st.cond_per_seq * (row0 // st.t_len)


def _sigmoid(x):
    return 1.0 / (1.0 + jnp.exp(-x))


def _silu(x):
    return x * _sigmoid(x)


def _split(a):
    hi = a.astype(BF16)
    lo = (a - hi.astype(F32)).astype(BF16)
    return hi, lo


def _dot(a, b):
    return jnp.dot(a, b, preferred_element_type=F32)


def _dot3(a, b):
    ah, al = _split(a)
    bh, bl = _split(b)
    return _dot(ah, bh) + (_dot(ah, bl) + _dot(al, bh))


def _gsum(x, gmat):
    xh, xl = _split(x)
    return _dot(xh, gmat) + _dot(xl, gmat)


def _mod_kernel(c_ref, w_ref, b_ref, o_ref):
    o_ref[...] = _dot3(_silu(c_ref[...]), w_ref[...]) + b_ref[...]


def _modulation(conds, mod_w, mod_b):
    tn = 512
    n = 3 * D_MODEL
    return pl.pallas_call(
        _mod_kernel,
        out_shape=jax.ShapeDtypeStruct((DEPTH, N_COND, n), F32),
        grid=(DEPTH, n // tn),
        in_specs=[
            pl.BlockSpec((N_COND, D_MODEL), lambda l, j: (0, 0)),
            pl.BlockSpec((None, D_MODEL, tn), lambda l, j: (l, 0, j)),
            pl.BlockSpec((None, 1, tn), lambda l, j: (l, 0, j)),
        ],
        out_specs=pl.BlockSpec((None, N_COND, tn), lambda l, j: (l, 0, j)),
        compiler_params=_cparams(("parallel", "parallel")),
        name="modulation",
    )(conds, mod_w, mod_b.reshape(DEPTH, 1, n))


def _norm_modulate(x, g, sh, sc):
    ms = jnp.mean(x * x, axis=-1, keepdims=True)
    y = x * lax.rsqrt(ms + RMS_EPS) * g
    return (y * (1.0 + sc) + sh).astype(BF16)


def _pre_kernel(x_ref, g_ref, sh_ref, sc_ref, h_ref):
    h_ref[...] = _norm_modulate(x_ref[...], g_ref[...], sh_ref[...], sc_ref[...])


def _pre_norm(st, x, norm_g, mods4, layer):
    tm = TM_ROW
    cond = lambda i: _cond_of_row(st, i * tm)
    return pl.pallas_call(
        _pre_kernel,
        out_shape=jax.ShapeDtypeStruct((st.rows, D_MODEL), BF16),
        grid=(st.rows // tm,),
        in_specs=[pl.BlockSpec((tm, D_MODEL), lambda i: (i, 0)),
                  pl.BlockSpec((None, 1, D_MODEL), lambda i: (layer, 0, 0)),
                  pl.BlockSpec((None, None, 1, D_MODEL), lambda i: (layer, cond(i), 0, 0)),
                  pl.BlockSpec((None, None, 1, D_MODEL), lambda i: (layer, cond(i), 0, 1))],
        out_specs=pl.BlockSpec((tm, D_MODEL), lambda i: (i, 0)),
        compiler_params=_cparams(("parallel",)),
        name="pre_norm",
    )(x, norm_g, mods4, mods4)


def _in_kernel(h_ref, w_ref, o_ref):
    o_ref[...] = _dot(h_ref[...], w_ref[...])


def _in_proj(st, h, w_bf16, li):
    n = w_bf16.shape[-1]
    return pl.pallas_call(
        _in_kernel,
        out_shape=jax.ShapeDtypeStruct((st.rows, n), F32),
        grid=(n // TN_IN, st.rows // TM_IN),
        in_specs=[pl.BlockSpec((TM_IN, D_MODEL), lambda j, i: (i, 0)),
                  pl.BlockSpec((None, D_MODEL, TN_IN), lambda j, i: (li, 0, j))],
        out_specs=pl.BlockSpec((TM_IN, TN_IN), lambda j, i: (i, j)),
        compiler_params=_cparams(("parallel", "parallel")),
        name="in_proj",
    )(h, w_bf16)


def _token_shift(cur, prev8, next8, mu, first, last):
    tm = cur.shape[0]
    rows = lax.broadcasted_iota(jnp.int32, cur.shape, 0)
    prow = jnp.where(first, 0.0, prev8[7:8, :])
    nrow = jnp.where(last, 0.0, next8[0:1, :])
    up = jnp.where(rows == 0, prow, pltpu.roll(cur, 1, axis=0))
    dn = jnp.where(rows == tm - 1, nrow, pltpu.roll(cur, tm - 1, axis=0))
    return cur + mu * (0.5 * (up + dn) - cur)


def _prep_kernel(r_ref, k_ref, v_ref, lo_ref, rp_ref, kp_ref, vp_ref, lp_ref, rn_ref, kn_ref, vn_ref, ln_ref,
                 mur_ref, muk_ref, muv_ref, mul_ref, w0_ref, wup_ref, a0_ref, aup_ref, kkw_ref, ka_ref, rk_ref,
                 gm_ref,
                 r_o, kk_o, v_o, w_o, kka_o, kd_o, bon_o, *, tiles_per_seq):
    i = pl.program_id(0)
    first = lax.rem(i, tiles_per_seq) == 0
    last = lax.rem(i, tiles_per_seq) == tiles_per_seq - 1

    low = _token_shift(lo_ref[...], lp_ref[...], ln_ref[...], mul_ref[...], first, last)
    low_t = jnp.tanh(low)
    gm = gm_ref[...]

    for c in range(A_WIDTH // CW):
        cols = slice(c * CW, (c + 1) * CW)
        r = _token_shift(r_ref[:, cols], rp_ref[:, cols], rn_ref[:, cols], mur_ref[:, cols], first, last)
        k = _token_shift(k_ref[:, cols], kp_ref[:, cols], kn_ref[:, cols], muk_ref[:, cols], first, last)
        v = _token_shift(v_ref[:, cols], vp_ref[:, cols], vn_ref[:, cols], muv_ref[:, cols], first, last)
        kk = k * kkw_ref[:, cols]
        nrm = jnp.sqrt(_gsum(kk * kk, gm))
        kk = kk / jnp.maximum(nrm, 1e-12)
        ka = ka_ref[:, cols]
        kd_sum = None
        for d in range(2):
            w_raw = w0_ref[d:d + 1, cols] + _dot3(low_t, wup_ref[d, :, cols])
            z = -w_raw
            sp = jnp.maximum(z, 0.0) + jnp.log(1.0 + jnp.exp(-jnp.abs(z)))
            a = _sigmoid(a0_ref[d:d + 1, cols] + _dot3(low, aup_ref[d, :, cols]))
            kd = k * (1.0 + (a - 1.0) * ka)
            w_o[d, :, cols] = jnp.exp(-jnp.exp(-sp - 0.5))
            kka_o[d, :, cols] = kk * a
            kd_o[d, :, cols] = kd
            kd_sum = kd if kd_sum is None else kd_sum + kd
        r_o[:, cols] = r
        kk_o[:, cols] = kk
        v_o[:, cols] = v
        bon_o[:, cols] = _gsum(r * (0.5 * kd_sum) * rk_ref[:, cols], gm) * v


def _wkv_prep(st, proj, mu, w0, wup_pad, a0, aup_pad, k_k, k_a, r_k, gmat):
    tm = TM_ROW
    nb8 = st.rows // SUBLANES
    ncb = A_WIDTH // CW
    low_cb = 3 * A_WIDTH // CW
    assert 4 * LORA == CW and st.t_len % tm == 0

    prev_row = lambda i: jnp.maximum(i * (tm // SUBLANES) - 1, 0)
    next_row = lambda i: jnp.minimum((i + 1) * (tm // SUBLANES), nb8 - 1)

    def main(cb):
        return pl.BlockSpec((tm, A_WIDTH), lambda i: (i, cb))

    def prev(cb):
        return pl.BlockSpec((SUBLANES, A_WIDTH), lambda i: (prev_row(i), cb))

    def nxt(cb):
        return pl.BlockSpec((SUBLANES, A_WIDTH), lambda i: (next_row(i), cb))

    def fixed(spec_fn):
        return [spec_fn(0), spec_fn(1), spec_fn(2)]

    low_main = pl.BlockSpec((tm, CW), lambda i: (i, low_cb))
    low_prev = pl.BlockSpec((SUBLANES, CW), lambda i: (prev_row(i), low_cb))
    low_next = pl.BlockSpec((SUBLANES, CW), lambda i: (next_row(i), low_cb))

    def vec(cb):
        return pl.BlockSpec((1, A_WIDTH), lambda i: (0, cb))

    in_specs = (fixed(main) + [low_main] + fixed(prev) + [low_prev] + fixed(nxt) + [low_next]
                + [vec(0), vec(1), vec(2), pl.BlockSpec((1, CW), lambda i: (0, low_cb))]
                + [pl.BlockSpec((2, A_WIDTH), lambda i: (0, 0)),
                   pl.BlockSpec((2, CW, A_WIDTH), lambda i: (0, 0, 0)),
                   pl.BlockSpec((2, A_WIDTH), lambda i: (0, 0)),
                   pl.BlockSpec((2, CW, A_WIDTH), lambda i: (0, 0, 0)),
                   vec(0), vec(0), vec(0),
                   pl.BlockSpec((CW, CW), lambda i: (0, 0))])
    tps = st.t_len // tm
    g_n = st.n_seq // SEQ_PER_GROUP

    def til(i):
        s = i // tps
        return (s // SEQ_PER_GROUP, lax.rem(i, tps), lax.rem(s, SEQ_PER_GROUP))

    one = pl.BlockSpec((tm, A_WIDTH), lambda i: (i, 0))
    til1 = pl.BlockSpec((None, tm, A_WIDTH), lambda i: til(i))
    til2 = pl.BlockSpec((2, None, tm, A_WIDTH), lambda i: (0,) + til(i))
    one_sds = jax.ShapeDtypeStruct((st.rows, A_WIDTH), F32)
    til1_sds = jax.ShapeDtypeStruct((g_n, st.t_len, SEQ_PER_GROUP * A_WIDTH), F32)
    til2_sds = jax.ShapeDtypeStruct((2, g_n, st.t_len, SEQ_PER_GROUP * A_WIDTH), F32)
    return pl.pallas_call(
        functools.partial(_prep_kernel, tiles_per_seq=tps),
        out_shape=[til1_sds, til1_sds, til1_sds, til2_sds, til2_sds, til2_sds, one_sds],
        grid=(st.rows // tm,),
        in_specs=in_specs,
        out_specs=[til1, til1, til1, til2, til2, til2, one],
        compiler_params=_cparams(("parallel",)),
        name="wkv_prep",
    )(*([proj] * 12), mu, mu, mu, mu, w0, wup_pad, a0, aup_pad, k_k, k_a, r_k, gmat)


def _wkv_kernel(rf_ref, rb_ref, kkf_ref, kkb_ref, vf_ref, vb_ref, wf_ref, wb_ref, kkaf_ref, kkab_ref, kdf_ref, kdb_ref,
                s0_ref, of_ref, ob_ref, sf_ref,
                s2_ref, r2_ref, kk2_ref, v2_ref, kka2_ref, kd2_ref, g_ref, o_ref, *, tt_steps):
    c = pl.program_id(1)
    nblk = WKV_ROWS // SUBLANES
    n_hv = A_HEADS // 2
    pairs = ((rf_ref, rb_ref, r2_ref), (kkf_ref, kkb_ref, kk2_ref), (vf_ref, vb_ref, v2_ref),
             (wf_ref, wb_ref, None), (kkaf_ref, kkab_ref, kka2_ref), (kdf_ref, kdb_ref, kd2_ref))

    @pl.when(c == 0)
    def _():
        s2_ref[...] = s0_ref[...]

    def to_chain_tiles(u, gammas):
        ub = tt_steps - 1 - u
        halves = []
        for xf_ref, xb_ref, _ in pairs:
            x2 = jnp.concatenate([xf_ref[u, :, hv * LANES:(hv + 1) * LANES] for hv in range(n_hv)]
                                 + [xb_ref[ub, :, hv * LANES:(hv + 1) * LANES] for hv in range(n_hv)], axis=0)
            y = x2.T
            halves.append((y[:A_HEAD], y[A_HEAD:]))
        r_h, kk_h, v_h, w_h, kka_h, kd_h = halves
        new_gammas = []
        for hp in range(2):
            gamma = gammas[hp] * w_h[hp]
            inv = 1.0 / gamma
            kk2_ref[hp, u] = gammas[hp] * kk_h[hp]
            kka2_ref[hp, u] = kka_h[hp] * inv
            kd2_ref[hp, u] = kd_h[hp] * inv
            r2_ref[hp, u] = r_h[hp] * gamma
            v2_ref[hp, u] = v_h[hp]
            new_gammas.append(gamma)
        return tuple(new_gammas)

    one = jnp.ones((A_HEAD, CHAINS), F32)
    gammas = lax.fori_loop(0, tt_steps, to_chain_tiles, (one, one))
    for hp in range(2):
        g_ref[hp] = gammas[hp]

    def bcast_row(ref, t, j):
        return ref[t, pl.ds(j, SUBLANES, stride=0), :]

    def time_step(hp, t):
        s_ref = s2_ref.at[hp]
        r_ref, kk_ref, v_ref, kka_ref, kd_ref = (ref.at[hp] for ref in (r2_ref, kk2_ref, v2_ref, kka2_ref, kd2_ref))
        for ib in range(A_HEAD // WKV_ROWS):
            i0 = ib * WKV_ROWS
            rows = [pl.ds(i0 + SUBLANES * b, SUBLANES) for b in range(nblk)]
            sa = [[None] * WKV_ACCS for _ in range(nblk)]
            for j in range(A_HEAD):
                kkj = bcast_row(kk_ref, t, j)
                for b in range(nblk):
                    p = s_ref[j, rows[b], :] * kkj
                    sa[b][j % WKV_ACCS] = p if sa[b][j % WKV_ACCS] is None else sa[b][j % WKV_ACCS] + p
            sa = [functools.reduce(lambda x, y: x + y, parts) for parts in sa]
            v8 = [v_ref[t, rows[b], :] for b in range(nblk)]
            out = [[None] * WKV_ACCS for _ in range(nblk)]
            for j in range(A_HEAD):
                kkaj = bcast_row(kka_ref, t, j)
                kdj = bcast_row(kd_ref, t, j)
                rj = bcast_row(r_ref, t, j)
                for b in range(nblk):
                    sn = s_ref[j, rows[b], :] - sa[b] * kkaj + v8[b] * kdj
                    s_ref[j, rows[b], :] = sn
                    q = sn * rj
                    out[b][j % WKV_ACCS] = q if out[b][j % WKV_ACCS] is None else out[b][j % WKV_ACCS] + q
            for b in range(nblk):
                o_ref[hp, rows[b], :] = functools.reduce(lambda x, y: x + y, out[b])

    def step(u, carry):
        for hp in range(2):
            time_step(hp, u)
        x2 = jnp.concatenate([o_ref[0], o_ref[1]], axis=0).T
        ub = tt_steps - 1 - u
        for hv in range(n_hv):
            of_ref[u, :, hv * LANES:(hv + 1) * LANES] = x2[hv * SUBLANES:(hv + 1) * SUBLANES, :]
            ob_ref[ub, :, hv * LANES:(hv + 1) * LANES] = x2[(n_hv + hv) * SUBLANES:(n_hv + hv + 1) * SUBLANES, :]
        return carry

    lax.fori_loop(0, tt_steps, step, 0)

    def rescale(j, carry):
        for hp in range(2):
            gj = g_ref[hp, pl.ds(j, SUBLANES, stride=0), :]
            for b in range(A_HEAD // SUBLANES):
                rows = pl.ds(b * SUBLANES, SUBLANES)
                s2_ref[hp, j, rows, :] = s2_ref[hp, j, rows, :] * gj
        return carry

    lax.fori_loop(0, A_HEAD, rescale, 0)

    @pl.when(c == pl.num_programs(1) - 1)
    def _():
        sf_ref[...] = s2_ref[...]


def _wkv(r, kk, v, w, kka, kd, s0):
    g_n, t_n = r.shape[0], r.shape[1]
    tt = WKV_TT
    nc = t_n // tt
    split = lambda a: a.reshape(a.shape[:-1] + (SEQ_PER_GROUP, A_WIDTH))
    blk = (tt, SEQ_PER_GROUP, A_WIDTH)

    fwd = pl.BlockSpec((None,) + blk, lambda g, c: (g, c, 0, 0))
    bwd = pl.BlockSpec((None,) + blk, lambda g, c: (g, nc - 1 - c, 0, 0))
    fwd_d = pl.BlockSpec((None, None) + blk, lambda g, c: (0, g, c, 0, 0))
    bwd_d = pl.BlockSpec((None, None) + blk, lambda g, c: (1, g, nc - 1 - c, 0, 0))
    state = pl.BlockSpec((None, 2, A_HEAD, A_HEAD, CHAINS), lambda g, c: (g, 0, 0, 0, 0))
    tiles = pltpu.VMEM((2, tt, A_HEAD, CHAINS), F32)
    o_sds = jax.ShapeDtypeStruct((g_n, t_n, SEQ_PER_GROUP, A_WIDTH), F32)
    r, kk, v, w, kka, kd = (split(a) for a in (r, kk, v, w, kka, kd))
    o_f, o_b, s_fin = pl.pallas_call(
        functools.partial(_wkv_kernel, tt_steps=tt),
        out_shape=[o_sds, o_sds, jax.ShapeDtypeStruct((g_n, 2, A_HEAD, A_HEAD, CHAINS), F32)],
        grid=(g_n, nc),
        in_specs=[fwd, bwd, fwd, bwd, fwd, bwd, fwd_d, bwd_d, fwd_d, bwd_d, fwd_d, bwd_d, state],
        out_specs=[fwd, bwd, state],
        scratch_shapes=([pltpu.VMEM((2, A_HEAD, A_HEAD, CHAINS), F32)] + [tiles] * 5
                        + [pltpu.VMEM((2, A_HEAD, CHAINS), F32)] * 2),
        compiler_params=pltpu.CompilerParams(dimension_semantics=("parallel", "arbitrary"),
                                             vmem_limit_bytes=WKV_VMEM_LIMIT),
        name="wkv",
    )(r, r, kk, kk, v, v, w, w, kka, kka, kd, kd, s0)
    merge = lambda a: a.reshape(g_n, t_n, SEQ_PER_GROUP * A_WIDTH)
    return merge(o_f), merge(o_b), s_fin


def _ya_kernel(of_ref, ob_ref, bon_ref, g0_ref, g1_ref, g2_ref, g3_ref, gw_ref, gb_ref, gm_ref, o_ref):
    gm = gm_ref[...]
    for c, gate_ref in enumerate((g0_ref, g1_ref, g2_ref, g3_ref)):
        cols = slice(c * CW, (c + 1) * CW)
        o = of_ref[:, cols] + ob_ref[:, cols]
        mean = _gsum(o, gm) * (1.0 / A_HEAD)
        dev = o - mean
        var = _gsum(dev * dev, gm) * (1.0 / A_HEAD)
        y = dev * lax.rsqrt(var + GN_EPS) * gw_ref[:, cols] + gb_ref[:, cols]
        o_ref[:, cols] = ((y + bon_ref[:, cols]) * _silu(gate_ref[...])).astype(BF16)


def _ya(st, o_f, o_b, bonus, proj, gn_w, gn_b, gmat):
    tm = TM_ROW
    gate_cb = (SHIFT_COLS) // CW
    blk = pl.BlockSpec((tm, A_WIDTH), lambda i: (i, 0))
    vec = pl.BlockSpec((1, A_WIDTH), lambda i: (0, 0))
    tps = st.t_len // tm

    def til_idx(i):
        s = i // tps
        return (s // SEQ_PER_GROUP, lax.rem(i, tps), lax.rem(s, SEQ_PER_GROUP))

    til = pl.BlockSpec((None, tm, A_WIDTH), til_idx)
    gates = [pl.BlockSpec((tm, CW), lambda i, c=c: (i, gate_cb + c)) for c in range(A_WIDTH // CW)]
    return pl.pallas_call(
        _ya_kernel,
        out_shape=jax.ShapeDtypeStruct((st.rows, A_WIDTH), BF16),
        grid=(st.rows // tm,),
        in_specs=[til, til, blk] + gates + [vec, vec, pl.BlockSpec((CW, CW), lambda i: (0, 0))],
        out_specs=blk,
        compiler_params=_cparams(("parallel",)),
        name="wkv_post",
    )(o_f, o_b, bonus, proj, proj, proj, proj, gn_w, gn_b, gmat)


def _dft_ch_kernel(u_ref, cs_ref, zc_ref, zs_ref):
    z = _dot(u_ref[...].astype(BF16), cs_ref[...])
    zc_ref[...] = z[:, :B_GROUP_CH].astype(BF16)
    zs_ref[...] = z[:, B_GROUP_CH:].astype(BF16)


def _dft_channels(st, proj, cs_mat):
    tm = 2048
    u_cb = (SHIFT_COLS + A_WIDTH) // B_GROUP_CH
    blk = pl.BlockSpec((tm, B_GROUP_CH), lambda i, g: (i, g))
    sds = jax.ShapeDtypeStruct((st.rows, B_WIDTH), BF16)
    return pl.pallas_call(
        _dft_ch_kernel,
        out_shape=[sds, sds],
        grid=(st.rows // tm, B_GROUPS),
        in_specs=[pl.BlockSpec((tm, B_GROUP_CH), lambda i, g: (i, u_cb + g)),
                  pl.BlockSpec((B_GROUP_CH, 2 * B_GROUP_CH), lambda i, g: (0, 0))],
        out_specs=[blk, blk],
        compiler_params=_cparams(("parallel", "parallel")),
        name="dft_channels",
    )(proj, cs_mat)


def _dft_time_kernel(c_ref, s_ref, zc_ref, zs_ref, g0_ref, g1_ref, g2_ref, g3_ref, o_ref, *, scale):
    acc = _dot(c_ref[...], zc_ref[...]) - _dot(s_ref[...], zs_ref[...])
    for g, gate_ref in enumerate((g0_ref, g1_ref, g2_ref, g3_ref)):
        cols = slice(g * B_GROUP_CH, (g + 1) * B_GROUP_CH)
        o_ref[:, cols] = (acc[:, cols] * scale * _silu(gate_ref[...])).astype(BF16)


def _dft_time(st, zc, zs, proj, cmat, smat):
    t_len = st.t_len
    tm = min(t_len, 512)
    mt = t_len // tm
    gate_cb = (SHIFT_COLS + A_WIDTH + B_WIDTH) // B_GROUP_CH
    gate_specs = [pl.BlockSpec((tm, B_GROUP_CH), lambda b, m, g=g: (b * mt + m, gate_cb + g)) for g in range(B_GROUPS)]
    scale = 1.0 / float(np.sqrt(t_len * B_GROUP_CH))
    return pl.pallas_call(
        functools.partial(_dft_time_kernel, scale=scale),
        out_shape=jax.ShapeDtypeStruct((st.rows, B_WIDTH), BF16),
        grid=(st.n_seq, mt),
        in_specs=[pl.BlockSpec((tm, t_len), lambda b, m: (m, 0)),
                  pl.BlockSpec((tm, t_len), lambda b, m: (m, 0)),
                  pl.BlockSpec((t_len, B_WIDTH), lambda b, m: (b, 0)),
                  pl.BlockSpec((t_len, B_WIDTH), lambda b, m: (b, 0))] + gate_specs,
        out_specs=pl.BlockSpec((tm, B_WIDTH), lambda b, m: (b * mt + m, 0)),
        compiler_params=_cparams(("parallel", "parallel")),
        name="dft_time",
    )(cmat, smat, zc, zs, proj, proj, proj, proj)


def _dft_mats(n):
    idx = jnp.arange(n, dtype=jnp.int32)
    prod = (idx[:, None] * idx[None, :]) % n
    ang = prod.astype(F32) * (2.0 * np.pi / n)
    return jnp.cos(ang), jnp.sin(ang)


def _out_kernel(a1_ref, a2_ref, w1_ref, w2_ref, x_ref, gate_ref, g_ref, *rest, emit_next):
    y = _dot(a1_ref[...], w1_ref[...]) + _dot(a2_ref[...], w2_ref[...])
    ms = jnp.mean(y * y, axis=-1, keepdims=True)
    yn = y * lax.rsqrt(ms + RMS_EPS) * g_ref[...]
    x_new = x_ref[...] + gate_ref[...] * yn
    if emit_next:
        ng_ref, nsh_ref, nsc_ref, o_ref, h_ref = rest
        h_ref[...] = _norm_modulate(x_new, ng_ref[...], nsh_ref[...], nsc_ref[...])
    else:
        o_ref, = rest
    o_ref[...] = x_new


def _out_proj(st, a1, a1_cb, a2, a2_cb, w_bf16, li, x, mods4, norm_post, norm_pre, layer):
    tm = TM_ROW
    half = D_MODEL // 2
    cond = lambda i: _cond_of_row(st, i * tm)
    emit_next = layer + 1 < DEPTH
    row = pl.BlockSpec((tm, D_MODEL), lambda i: (i, 0))
    in_specs = [pl.BlockSpec((tm, half), lambda i: (i, a1_cb)),
                pl.BlockSpec((tm, half), lambda i: (i, a2_cb)),
                pl.BlockSpec((None, half, D_MODEL), lambda i: (li, 0, 0)),
                pl.BlockSpec((None, half, D_MODEL), lambda i: (li, 1, 0)),
                row,
                pl.BlockSpec((None, None, 1, D_MODEL), lambda i: (layer, cond(i), 0, 2)),
                pl.BlockSpec((None, 1, D_MODEL), lambda i: (layer, 0, 0))]
    args = [a1, a2, w_bf16, w_bf16, x, mods4, norm_post]
    out_shape = [jax.ShapeDtypeStruct((st.rows, D_MODEL), F32)]
    out_specs = [row]
    if emit_next:
        in_specs += [pl.BlockSpec((None, 1, D_MODEL), lambda i: (layer + 1, 0, 0)),
                     pl.BlockSpec((None, None, 1, D_MODEL), lambda i: (layer + 1, cond(i), 0, 0)),
                     pl.BlockSpec((None, None, 1, D_MODEL), lambda i: (layer + 1, cond(i), 0, 1))]
        args += [norm_pre, mods4, mods4]
        out_shape.append(jax.ShapeDtypeStruct((st.rows, D_MODEL), BF16))
        out_specs.append(row)
    outs = pl.pallas_call(
        functools.partial(_out_kernel, emit_next=emit_next),
        out_shape=out_shape,
        grid=(st.rows // tm,),
        in_specs=in_specs,
        out_specs=out_specs,
        compiler_params=_cparams(("parallel",)),
        name="out_proj",
    )(*args)
    return (outs[0], outs[1]) if emit_next else (outs[0], None)


def _softmax_pv(scores, values, sink):
    m = sink
    for s in scores:
        m = jnp.maximum(m, jnp.max(s, axis=-1, keepdims=True))
    den = jnp.exp(sink - m)
    acc = None
    for s, v in zip(scores, values):
        p = jnp.exp(s - m)
        den = den + jnp.sum(p, axis=-1, keepdims=True)
        pv = _dot(p.astype(BF16), v)
        acc = pv if acc is None else acc + pv
    return acc / den


def _qk(q, k):
    return lax.dot_general(q, k, (((1,), (1,)), ((), ())), preferred_element_type=F32)


def _upper_half(rows):
    return lax.broadcasted_iota(jnp.int32, (rows, LANES), 1) >= C_HEAD


def _both_halves(x, hh):
    upper = _upper_half(x.shape[0])
    keep = upper if hh == 1 else jnp.logical_not(upper)
    return jnp.where(keep, x, pltpu.roll(x, C_HEAD, axis=1)).astype(BF16)


def _pair_heads(q_chunks, keys, values, sinks, masks):
    rows = q_chunks[0].shape[0]
    upper = _upper_half(rows)
    lower = jnp.logical_not(upper)
    outs = [None] * (2 * C_GROUP)
    for hh in range(2):
        kd = [_both_halves(k, hh) for k in keys]
        vd = [_both_halves(v, hh) for v in values]
        heads = [hh * C_GROUP + g for g in range(C_GROUP)]
        q4 = jnp.concatenate([jnp.where(upper if n % 2 else lower, q_chunks[n // 2], 0.0) for n in heads],
                             axis=0).astype(BF16)
        s4 = [_qk(q4, k) for k in kd]
        p4 = [[] for _ in kd]
        dens = []
        for g, n in enumerate(heads):
            sl = slice(g * rows, (g + 1) * rows)
            scores = [s[sl] if mask is None else jnp.where(mask, s[sl], NEG_INF) for s, mask in zip(s4, masks)]
            sink = sinks(n)
            m = sink
            for s in scores:
                m = jnp.maximum(m, jnp.max(s, axis=-1, keepdims=True))
            den = jnp.exp(sink - m)
            for t, s in enumerate(scores):
                p = jnp.exp(s - m)
                den = den + jnp.sum(p, axis=-1, keepdims=True)
                p4[t].append(p.astype(BF16))
            dens.append(den)
        pv4 = None
        for p_parts, v in zip(p4, vd):
            pv = _dot(jnp.concatenate(p_parts, axis=0), v)
            pv4 = pv if pv4 is None else pv4 + pv
        for g, n in enumerate(heads):
            outs[n] = pv4[g * rows:(g + 1) * rows] / dens[g]
    return jnp.concatenate([jnp.where(lower, outs[2 * m], outs[2 * m + 1]) for m in range(C_GROUP)], axis=-1)


def _attn_ctx_kernel(sink_ref, q_ref, k_ref, v_ref, gate_ref, o_ref):
    kp = pl.program_id(1)
    q = q_ref[...] * (C_HEAD ** -0.5)
    q_chunks = [q[:, m * LANES:(m + 1) * LANES] for m in range(C_GROUP)]
    o = _pair_heads(q_chunks, [k_ref[...]], [v_ref[...]], lambda n: sink_ref[kp * 2 * C_GROUP + n], [None])
    o_ref[...] = (o * _silu(gate_ref[...])).astype(BF16)


def _attn_context(proj, sink):
    qw = 2 * C_GROUP * C_HEAD
    k_cb = C_HEADS * C_HEAD // LANES
    v_cb = (C_HEADS + C_KV_HEADS) * C_HEAD // LANES
    gate_cb = (C_HEADS + 2 * C_KV_HEADS) * C_HEAD // qw
    return pl.pallas_call(
        _attn_ctx_kernel,
        out_shape=jax.ShapeDtypeStruct((PROMPT.rows, D_MODEL), BF16),
        grid=(BATCH, C_KV_HEADS // 2),
        in_specs=[pl.BlockSpec(memory_space=pltpu.SMEM),
                  pl.BlockSpec((SEQ, qw), lambda b, kp: (b, kp)),
                  pl.BlockSpec((SEQ, LANES), lambda b, kp: (b, k_cb + kp)),
                  pl.BlockSpec((SEQ, LANES), lambda b, kp: (b, v_cb + kp)),
                  pl.BlockSpec((SEQ, qw), lambda b, kp: (b, gate_cb + kp))],
        out_specs=pl.BlockSpec((SEQ, qw), lambda b, kp: (b, kp)),
        compiler_params=_cparams(("parallel", "parallel")),
        name="attn_context",
    )(sink, proj, proj, proj, proj)


def _rope(x, cos, sin_signed):
    lane = lax.broadcasted_iota(jnp.int32, x.shape, 1)
    first = (lane & 31) < 16
    partner = jnp.where(first, pltpu.roll(x, LANES - 16, axis=1), pltpu.roll(x, 16, axis=1))
    return x * cos + partner * sin_signed


def _attn_lat_kernel(sink_ref, q_ref, kp_ref, ko_ref, kn_ref, vp_ref, vo_ref, vn_ref, ck_ref, cv_ref,
                     cq_ref, sq_ref, cp_ref, sp_ref, cn_ref, sn_ref, gate_ref, o_ref):
    kpair = pl.program_id(1)
    qb = pl.program_id(2)
    cq = cq_ref[...]
    sq = sq_ref[...]
    q = q_ref[...] * (C_HEAD ** -0.5)
    qr = [_rope(q[:, n * LANES:(n + 1) * LANES], cq, sq) for n in range(4)]
    kband = jnp.concatenate([_rope(kp_ref[...], cp_ref[...], sp_ref[...]),
                             _rope(ko_ref[...], cq, sq),
                             _rope(kn_ref[...], cn_ref[...], sn_ref[...])], axis=0)
    vband = jnp.concatenate([vp_ref[...], vo_ref[...], vn_ref[...]], axis=0)
    qpos = qb * BLOCK + lax.broadcasted_iota(jnp.int32, (BLOCK, 3 * BLOCK), 0)
    kpos = (qb - 1) * BLOCK + lax.broadcasted_iota(jnp.int32, (BLOCK, 3 * BLOCK), 1)
    valid = (jnp.abs(qpos - kpos) <= WINDOW) & (kpos >= 0) & (kpos < DEC_SEQ)
    o = _pair_heads(qr, [kband, ck_ref[...]], [vband, cv_ref[...]],
                    lambda n: sink_ref[kpair * 2 * C_GROUP + n], [valid, None])
    o_ref[...] = (o * _silu(gate_ref[...])).astype(BF16)


def _attn_latent(proj, cache_k4, cache_v4, li, sink, cos_t, sin_t):
    qw = 2 * C_GROUP * C_HEAD
    k_cb = C_HEADS * C_HEAD // LANES
    v_cb = (C_HEADS + C_KV_HEADS) * C_HEAD // LANES
    gate_cb = (C_HEADS + 2 * C_KV_HEADS) * C_HEAD // qw
    nqb = DEC_SEQ // BLOCK

    def rows(delta):
        def idx(b, kp, qb):
            return b * nqb + jnp.clip(qb + delta, 0, nqb - 1)
        return idx

    def kv_spec(cb, delta):
        r = rows(delta)
        return pl.BlockSpec((BLOCK, LANES), lambda b, kp, qb: (r(b, kp, qb), cb + kp))

    def tab_spec(delta):
        return pl.BlockSpec((BLOCK, LANES), lambda b, kp, qb: (jnp.clip(qb + delta, 0, nqb - 1), 0))

    cache_spec = pl.BlockSpec((None, None, PAST_LEN, LANES), lambda b, kp, qb: (b, li, 0, kp))
    own = rows(0)
    return pl.pallas_call(
        _attn_lat_kernel,
        out_shape=jax.ShapeDtypeStruct((SAMPLE.rows, D_MODEL), BF16),
        grid=(DEC_BATCH, C_KV_HEADS // 2, nqb),
        in_specs=[pl.BlockSpec(memory_space=pltpu.SMEM),
                  pl.BlockSpec((BLOCK, qw), lambda b, kp, qb: (own(b, kp, qb), kp)),
                  kv_spec(k_cb, -1), kv_spec(k_cb, 0), kv_spec(k_cb, 1),
                  kv_spec(v_cb, -1), kv_spec(v_cb, 0), kv_spec(v_cb, 1),
                  cache_spec, cache_spec,
                  tab_spec(0), tab_spec(0), tab_spec(-1), tab_spec(-1), tab_spec(1), tab_spec(1),
                  pl.BlockSpec((BLOCK, qw), lambda b, kp, qb: (own(b, kp, qb), gate_cb + kp))],
        out_specs=pl.BlockSpec((BLOCK, qw), lambda b, kp, qb: (own(b, kp, qb), kp)),
        compiler_params=_cparams(("parallel", "parallel", "parallel")),
        name="attn_latent",
    )(sink, proj, proj, proj, proj, proj, proj, proj, cache_k4, cache_v4,
      cos_t, sin_t, cos_t, sin_t, cos_t, sin_t, proj)


def _rope_tables():
    t = jnp.arange(DEC_SEQ, dtype=jnp.int32)
    row = (t // GRID_W).astype(F32)
    col = (t % GRID_W).astype(F32)
    nf = C_HEAD // 4
    inv = 1.0 / (ROPE_BASE ** (jnp.arange(nf, dtype=F32) / nf))
    lane = np.arange(LANES)
    f_of_lane = lane % nf
    use_col = (lane % C_HEAD) >= C_HEAD // 2
    sign = np.where((lane % 32) < 16, -1.0, 1.0).astype(np.float32)
    pos = jnp.where(jnp.asarray(use_col)[None, :], col[:, None], row[:, None])
    ang = pos * inv[jnp.asarray(f_of_lane)][None, :]
    return jnp.cos(ang), jnp.sin(ang) * jnp.asarray(sign)[None, :]


def _even_mixer(st, proj, s0, p, gmat, cs_mat, dft):
    r, kk, v, w2, kka2, kd2, bonus = _wkv_prep(st, proj, p["mu"], p["w0"], p["wup"], p["a0"], p["aup"],
                                                p["k_k"], p["k_a"], p["r_k"], gmat)
    o_f, o_b, s_fin = _wkv(r, kk, v, w2, kka2, kd2, s0)
    ya = _ya(st, o_f, o_b, bonus, proj, p["gn_w"], p["gn_b"], gmat)
    zc, zs = _dft_channels(st, proj, cs_mat)
    yb = _dft_time(st, zc, zs, proj, dft[0], dft[1])
    return ya, yb, s_fin


def kernel(x_prompt, x_sample, c, state_wkv, cache_k, cache_v, c_ctx, mod_w, mod_b, norm_pre, norm_post,
           even_w_in, even_mu, even_w0, even_w_up, even_a0, even_a_up, even_k_k, even_k_a, even_r_k,
           even_gn_w, even_gn_b, even_w_out, odd_w_in, odd_sink, odd_w_out):
    n_odd = odd_w_in.shape[0]
    xs = {PROMPT: x_prompt.reshape(PROMPT.rows, D_MODEL), SAMPLE: x_sample.reshape(SAMPLE.rows, D_MODEL)}
    conds = jnp.concatenate([c_ctx[None, :], c, jnp.zeros((N_COND - 1 - DEC_BATCH, D_MODEL), F32)], axis=0)
    mods4 = _modulation(conds, mod_w, mod_b).reshape(DEPTH, N_COND, 1, 3 * D_MODEL)
    norm_pre3 = norm_pre.reshape(DEPTH, 1, D_MODEL)
    norm_post3 = norm_post.reshape(DEPTH, 1, D_MODEL)

    even_w_in_b = even_w_in.astype(BF16)
    even_w_out_b = even_w_out.astype(BF16)
    odd_w_in_b = odd_w_in.astype(BF16)
    odd_w_out_b = odd_w_out.astype(BF16)

    head_of_lane = np.arange(CW) // A_HEAD
    gmat = jnp.asarray((head_of_lane[:, None] == head_of_lane[None, :]).astype(np.float32)).astype(BF16)
    c_ch, s_ch = _dft_mats(B_GROUP_CH)
    cs_mat = jnp.concatenate([c_ch, s_ch], axis=1).astype(BF16)
    dft = {PROMPT: tuple(m.astype(BF16) for m in _dft_mats(SEQ)),
           SAMPLE: tuple(m.astype(BF16) for m in _dft_mats(DEC_SEQ))}
    cos_t, sin_t = _rope_tables()
    cache_k4 = cache_k.reshape(DEC_BATCH, n_odd, PAST_LEN, C_KV_HEADS * C_HEAD)
    cache_v4 = cache_v.reshape(DEC_BATCH, n_odd, PAST_LEN, C_KV_HEADS * C_HEAD)

    def pad_rows(w, slot):
        return jnp.pad(w, ((0, 0), (slot * LORA, (3 - slot) * LORA), (0, 0)))

    wup_pad = jnp.stack([pad_rows(even_w_up[:, 0], 0), pad_rows(even_w_up[:, 1], 1)], axis=1)
    aup_pad = jnp.stack([pad_rows(even_a_up[:, 0], 2), pad_rows(even_a_up[:, 1], 3)], axis=1)

    new_wkv, new_k, new_v = [], [], []
    hs = {st: _pre_norm(st, xs[st], norm_pre3, mods4, 0) for st in (PROMPT, SAMPLE)}
    for layer in range(DEPTH):
        i = layer // 2
        if layer % 2 == 0:
            p = {
                "mu": even_mu[i][None, :], "w0": even_w0[i], "a0": even_a0[i],
                "wup": wup_pad[i], "aup": aup_pad[i],
                "k_k": even_k_k[i][None, :], "k_a": even_k_a[i][None, :], "r_k": even_r_k[i].reshape(1, A_WIDTH),
                "gn_w": even_gn_w[i][None, :], "gn_b": even_gn_b[i][None, :],
            }
            s0s = state_wkv[:, i].reshape(DEC_BATCH, 2, A_HEADS // 2, 2, A_HEAD, A_HEAD)
            s0 = {PROMPT: jnp.zeros((BATCH // SEQ_PER_GROUP, 2, A_HEAD, A_HEAD, CHAINS), F32),
                  SAMPLE: s0s.transpose(3, 5, 4, 1, 2, 0).reshape(1, 2, A_HEAD, A_HEAD, CHAINS)}
            for st in (PROMPT, SAMPLE):
                proj = _in_proj(st, hs[st], even_w_in_b, i)
                ya, yb, s_fin = _even_mixer(st, proj, s0[st], p, gmat, cs_mat, dft[st])
                xs[st], hs[st] = _out_proj(st, ya, 0, yb, 0, even_w_out_b, i, xs[st], mods4, norm_post3, norm_pre3,
                                           layer)
                if st is PROMPT:
                    s_fin = s_fin.reshape(BATCH // SEQ_PER_GROUP, 2, A_HEAD, A_HEAD, 2, A_HEADS // 2, SEQ_PER_GROUP)
                    new_wkv.append(s_fin.transpose(0, 6, 4, 5, 1, 3, 2).reshape(BATCH, 2, A_HEADS, A_HEAD, A_HEAD))
        else:
            proj_p = _in_proj(PROMPT, hs[PROMPT], odd_w_in_b, i)
            proj_s = _in_proj(SAMPLE, hs[SAMPLE], odd_w_in_b, i)
            kv0 = C_HEADS * C_HEAD
            kvn = C_KV_HEADS * C_HEAD
            new_k.append(proj_p[:, kv0:kv0 + kvn].reshape(BATCH, SEQ, C_KV_HEADS, C_HEAD))
            new_v.append(proj_p[:, kv0 + kvn:kv0 + 2 * kvn].reshape(BATCH, SEQ, C_KV_HEADS, C_HEAD))
            att_p = _attn_context(proj_p, odd_sink[i])
            att_s = _attn_latent(proj_s, cache_k4, cache_v4, i, odd_sink[i], cos_t, sin_t)
            for st, att in ((PROMPT, att_p), (SAMPLE, att_s)):
                xs[st], hs[st] = _out_proj(st, att, 0, att, 1, odd_w_out_b, i, xs[st], mods4, norm_post3, norm_pre3,
                                           layer)

    y_prompt = xs[PROMPT].reshape(BATCH, SEQ, D_MODEL)
    y_sample = xs[SAMPLE].reshape(DEC_BATCH, DEC_SEQ, D_MODEL)
    return (y_prompt, y_sample, jnp.stack(new_wkv, axis=1), jnp.stack(new_k, axis=1), jnp.stack(new_v, axis=1))
```

```python
import collections
import functools

import numpy as np
import jax
import jax.numpy as jnp
from jax import lax
from jax.experimental import pallas as pl
from jax.experimental.pallas import tpu as pltpu

F32 = jnp.float32
BF16 = jnp.bfloat16

D_MODEL = 2048
BATCH = 32
SEQ = 256
DEPTH = 4
DEC_BATCH = 8
DEC_SEQ = 2048
PAST_LEN = 256
GRID_W = 64
RMS_EPS = 1e-6
A_WIDTH = 1024
A_HEAD = 64
A_HEADS = 16
LORA = 64
GN_EPS = 64e-5
DECAY_SCALE = float(np.exp(-0.5))
B_WIDTH = 1024
B_GROUPS = 4
B_GROUP_CH = 256
SHIFT_COLS = 3 * A_WIDTH + 4 * LORA
EVEN_IN = SHIFT_COLS + A_WIDTH + 2 * B_WIDTH
C_HEAD = 64
C_HEADS = 32
C_KV_HEADS = 8
C_GROUP = 4
WINDOW = 128
BLOCK = 128
ROPE_BASE = 10000.0
ODD_IN = (C_HEADS + 2 * C_KV_HEADS) * C_HEAD + D_MODEL
NEG_INF = -1e30

N_COND = 16
SUBLANES = 8
LANES = 128
CHAINS = LANES
SEQ_PER_GROUP = CHAINS // A_HEADS
CW = 256
TM_IN = 512
TN_IN = 1280
TM_ROW = 256
WKV_TT = 32
WKV_VMEM_LIMIT = 56 * 1024 * 1024
WKV_ROWS = 16
WKV_ACCS = 2
VMEM_LIMIT = 48 * 1024 * 1024

Stream = collections.namedtuple("Stream", "rows t_len n_seq cond0 cond_per_seq")
PROMPT = Stream(BATCH * SEQ, SEQ, BATCH, 0, 0)
SAMPLE = Stream(DEC_BATCH * DEC_SEQ, DEC_SEQ, DEC_BATCH, 1, 1)


def _cparams(sem):
    return pltpu.CompilerParams(dimension_semantics=sem, vmem_limit_bytes=VMEM_LIMIT)


def _cond_of_row(st, row0):
    return st.cond0 + st.cond_per_seq * (row0 // st.t_len)


def _sigmoid(x):
    return 1.0 / (1.0 + jnp.exp(-x))


def _silu(x):
    return x * _sigmoid(x)


def _split(a):
    hi = a.astype(BF16)
    lo = (a - hi.astype(F32)).astype(BF16)
    return hi, lo


def _dot(a, b):
    return jnp.dot(a, b, preferred_element_type=F32)


def _dot3_split(a_hl, b_hl):
    (ah, al), (bh, bl) = a_hl, b_hl
    return _dot(ah, bh) + (_dot(ah, bl) + _dot(al, bh))


def _dot3(a, b):
    return _dot3_split(_split(a), _split(b))


def _gsum(x, gmat):
    xh, xl = _split(x)
    return _dot(xh, gmat) + _dot(xl, gmat)


def _mod_kernel(c_ref, w_ref, b_ref, o_ref):
    o_ref[...] = _dot3(_silu(c_ref[...]), w_ref[...]) + b_ref[...]


def _modulation(conds, mod_w, mod_b):
    tn = 512
    n = 3 * D_MODEL
    return pl.pallas_call(
        _mod_kernel,
        out_shape=jax.ShapeDtypeStruct((DEPTH, N_COND, n), F32),
        grid=(DEPTH, n // tn),
        in_specs=[
            pl.BlockSpec((N_COND, D_MODEL), lambda l, j: (0, 0)),
            pl.BlockSpec((None, D_MODEL, tn), lambda l, j: (l, 0, j)),
            pl.BlockSpec((None, 1, tn), lambda l, j: (l, 0, j)),
        ],
        out_specs=pl.BlockSpec((None, N_COND, tn), lambda l, j: (l, 0, j)),
        compiler_params=_cparams(("parallel", "parallel")),
        name="modulation",
    )(conds, mod_w, mod_b.reshape(DEPTH, 1, n))


def _norm_modulate(x, g, sh, sc):
    ms = jnp.mean(x * x, axis=-1, keepdims=True)
    y = x * lax.rsqrt(ms + RMS_EPS) * g
    return (y * (1.0 + sc) + sh).astype(BF16)


def _pre_kernel(x_ref, g_ref, sh_ref, sc_ref, h_ref):
    h_ref[...] = _norm_modulate(x_ref[...], g_ref[...], sh_ref[...], sc_ref[...])


def _pre_norm(st, x, norm_g, mods4, layer):
    tm = TM_ROW
    cond = lambda i: _cond_of_row(st, i * tm)
    return pl.pallas_call(
        _pre_kernel,
        out_shape=jax.ShapeDtypeStruct((st.rows, D_MODEL), BF16),
        grid=(st.rows // tm,),
        in_specs=[pl.BlockSpec((tm, D_MODEL), lambda i: (i, 0)),
                  pl.BlockSpec((None, 1, D_MODEL), lambda i: (layer, 0, 0)),
                  pl.BlockSpec((None, None, 1, D_MODEL), lambda i: (layer, cond(i), 0, 0)),
                  pl.BlockSpec((None, None, 1, D_MODEL), lambda i: (layer, cond(i), 0, 1))],
        out_specs=pl.BlockSpec((tm, D_MODEL), lambda i: (i, 0)),
        compiler_params=_cparams(("parallel",)),
        name="pre_norm",
    )(x, norm_g, mods4, mods4)


def _in_kernel(h_ref, w_ref, o_ref):
    o_ref[...] = _dot(h_ref[...], w_ref[...])


def _in_proj(st, h, w_bf16, li):
    n = w_bf16.shape[-1]
    return pl.pallas_call(
        _in_kernel,
        out_shape=jax.ShapeDtypeStruct((st.rows, n), F32),
        grid=(n // TN_IN, st.rows // TM_IN),
        in_specs=[pl.BlockSpec((TM_IN, D_MODEL), lambda j, i: (i, 0)),
                  pl.BlockSpec((None, D_MODEL, TN_IN), lambda j, i: (li, 0, j))],
        out_specs=pl.BlockSpec((TM_IN, TN_IN), lambda j, i: (i, j)),
        compiler_params=_cparams(("parallel", "parallel")),
        name="in_proj",
    )(h, w_bf16)


def _token_shift(cur, prev8, next8, mu, first, last):
    tm = cur.shape[0]
    rows = lax.broadcasted_iota(jnp.int32, cur.shape, 0)
    prow = jnp.where(first, 0.0, prev8[7:8, :])
    nrow = jnp.where(last, 0.0, next8[0:1, :])
    up = jnp.where(rows == 0, prow, pltpu.roll(cur, 1, axis=0))
    dn = jnp.where(rows == tm - 1, nrow, pltpu.roll(cur, tm - 1, axis=0))
    return cur + mu * (0.5 * (up + dn) - cur)


def _prep_kernel(r_ref, k_ref, v_ref, lo_ref, rp_ref, kp_ref, vp_ref, lp_ref, rn_ref, kn_ref, vn_ref, ln_ref,
                 mur_ref, muk_ref, muv_ref, mul_ref, w0_ref, wuph_ref, wupl_ref, a0_ref, auph_ref, aupl_ref,
                 kkw_ref, ka_ref, rk_ref,
                 gm_ref,
                 r_o, kk_o, v_o, w_o, kka_o, kd_o, bon_o, *, tiles_per_seq):
    i = pl.program_id(0)
    first = lax.rem(i, tiles_per_seq) == 0
    last = lax.rem(i, tiles_per_seq) == tiles_per_seq - 1

    low = _token_shift(lo_ref[...], lp_ref[...], ln_ref[...], mul_ref[...], first, last)
    low_hl = _split(low)
    low_t_hl = _split(jnp.tanh(low))
    gm = gm_ref[...]

    for c in range(A_WIDTH // CW):
        cols = slice(c * CW, (c + 1) * CW)
        r = _token_shift(r_ref[:, cols], rp_ref[:, cols], rn_ref[:, cols], mur_ref[:, cols], first, last)
        k = _token_shift(k_ref[:, cols], kp_ref[:, cols], kn_ref[:, cols], muk_ref[:, cols], first, last)
        v = _token_shift(v_ref[:, cols], vp_ref[:, cols], vn_ref[:, cols], muv_ref[:, cols], first, last)
        kk = k * kkw_ref[:, cols]
        nrm = jnp.sqrt(_gsum(kk * kk, gm))
        kk = kk / jnp.maximum(nrm, 1e-12)
        ka = ka_ref[:, cols]
        kd_sum = None
        for d in range(2):
            w_raw = w0_ref[d:d + 1, cols] + _dot3_split(low_t_hl, (wuph_ref[d, :, cols], wupl_ref[d, :, cols]))
            a = _sigmoid(a0_ref[d:d + 1, cols] + _dot3_split(low_hl, (auph_ref[d, :, cols], aupl_ref[d, :, cols])))
            kd = k * (1.0 + (a - 1.0) * ka)
            w_o[d, :, cols] = jnp.exp(-DECAY_SCALE * _sigmoid(w_raw))
            kka_o[d, :, cols] = kk * a
            kd_o[d, :, cols] = kd
            kd_sum = kd if kd_sum is None else kd_sum + kd
        r_o[:, cols] = r
        kk_o[:, cols] = kk
        v_o[:, cols] = v
        bon_o[:, cols] = _gsum(r * (0.5 * kd_sum) * rk_ref[:, cols], gm) * v


def _wkv_prep(st, proj, mu, w0, wup_pad, a0, aup_pad, k_k, k_a, r_k, gmat):
    tm = TM_ROW
    nb8 = st.rows // SUBLANES
    ncb = A_WIDTH // CW
    low_cb = 3 * A_WIDTH // CW
    assert 4 * LORA == CW and st.t_len % tm == 0

    prev_row = lambda i: jnp.maximum(i * (tm // SUBLANES) - 1, 0)
    next_row = lambda i: jnp.minimum((i + 1) * (tm // SUBLANES), nb8 - 1)

    def main(cb):
        return pl.BlockSpec((tm, A_WIDTH), lambda i: (i, cb))

    def prev(cb):
        return pl.BlockSpec((SUBLANES, A_WIDTH), lambda i: (prev_row(i), cb))

    def nxt(cb):
        return pl.BlockSpec((SUBLANES, A_WIDTH), lambda i: (next_row(i), cb))

    def fixed(spec_fn):
        return [spec_fn(0), spec_fn(1), spec_fn(2)]

    low_main = pl.BlockSpec((tm, CW), lambda i: (i, low_cb))
    low_prev = pl.BlockSpec((SUBLANES, CW), lambda i: (prev_row(i), low_cb))
    low_next = pl.BlockSpec((SUBLANES, CW), lambda i: (next_row(i), low_cb))

    def vec(cb):
        return pl.BlockSpec((1, A_WIDTH), lambda i: (0, cb))

    in_specs = (fixed(main) + [low_main] + fixed(prev) + [low_prev] + fixed(nxt) + [low_next]
                + [vec(0), vec(1), vec(2), pl.BlockSpec((1, CW), lambda i: (0, low_cb))]
                + [pl.BlockSpec((2, A_WIDTH), lambda i: (0, 0)),
                   pl.BlockSpec((2, CW, A_WIDTH), lambda i: (0, 0, 0)),
                   pl.BlockSpec((2, CW, A_WIDTH), lambda i: (0, 0, 0)),
                   pl.BlockSpec((2, A_WIDTH), lambda i: (0, 0)),
                   pl.BlockSpec((2, CW, A_WIDTH), lambda i: (0, 0, 0)),
                   pl.BlockSpec((2, CW, A_WIDTH), lambda i: (0, 0, 0)),
                   vec(0), vec(0), vec(0),
                   pl.BlockSpec((CW, CW), lambda i: (0, 0))])
    tps = st.t_len // tm
    g_n = st.n_seq // SEQ_PER_GROUP

    def til(i):
        s = i // tps
        return (s // SEQ_PER_GROUP, lax.rem(i, tps), lax.rem(s, SEQ_PER_GROUP))

    one = pl.BlockSpec((tm, A_WIDTH), lambda i: (i, 0))
    til1 = pl.BlockSpec((None, tm, A_WIDTH), lambda i: til(i))
    til2 = pl.BlockSpec((2, None, tm, A_WIDTH), lambda i: (0,) + til(i))
    one_sds = jax.ShapeDtypeStruct((st.rows, A_WIDTH), F32)
    til1_sds = jax.ShapeDtypeStruct((g_n, st.t_len, SEQ_PER_GROUP * A_WIDTH), F32)
    til2_sds = jax.ShapeDtypeStruct((2, g_n, st.t_len, SEQ_PER_GROUP * A_WIDTH), F32)
    return pl.pallas_call(
        functools.partial(_prep_kernel, tiles_per_seq=tps),
        out_shape=[til1_sds, til1_sds, til1_sds, til2_sds, til2_sds, til2_sds, one_sds],
        grid=(st.rows // tm,),
        in_specs=in_specs,
        out_specs=[til1, til1, til1, til2, til2, til2, one],
        compiler_params=_cparams(("parallel",)),
        name="wkv_prep",
    )(*([proj] * 12), mu, mu, mu, mu, w0, *_split(wup_pad), a0, *_split(aup_pad), k_k, k_a, r_k, gmat)


def _wkv_kernel(rf_ref, rb_ref, kkf_ref, kkb_ref, vf_ref, vb_ref, wf_ref, wb_ref, kkaf_ref, kkab_ref, kdf_ref, kdb_ref,
                s0_ref, of_ref, ob_ref, sf_ref,
                s2_ref, r2_ref, kk2_ref, v2_ref, kka2_ref, kd2_ref, g_ref, o_ref, *, tt_steps):
    c = pl.program_id(1)
    nblk = WKV_ROWS // SUBLANES
    n_hv = A_HEADS // 2
    pairs = ((rf_ref, rb_ref, r2_ref), (kkf_ref, kkb_ref, kk2_ref), (vf_ref, vb_ref, v2_ref),
             (wf_ref, wb_ref, None), (kkaf_ref, kkab_ref, kka2_ref), (kdf_ref, kdb_ref, kd2_ref))

    @pl.when(c == 0)
    def _():
        s2_ref[...] = s0_ref[...]

    def to_chain_tiles(u, gammas):
        ub = tt_steps - 1 - u
        halves = []
        for xf_ref, xb_ref, _ in pairs:
            x2 = jnp.concatenate([xf_ref[u, :, hv * LANES:(hv + 1) * LANES] for hv in range(n_hv)]
                                 + [xb_ref[ub, :, hv * LANES:(hv + 1) * LANES] for hv in range(n_hv)], axis=0)
            y = x2.T
            halves.append((y[:A_HEAD], y[A_HEAD:]))
        r_h, kk_h, v_h, w_h, kka_h, kd_h = halves
        new_gammas = []
        for hp in range(2):
            gamma = gammas[hp] * w_h[hp]
            inv = 1.0 / gamma
            kk2_ref[hp, u] = gammas[hp] * kk_h[hp]
            kka2_ref[hp, u] = kka_h[hp] * inv
            kd2_ref[hp, u] = kd_h[hp] * inv
            r2_ref[hp, u] = r_h[hp] * gamma
            v2_ref[hp, u] = v_h[hp]
            new_gammas.append(gamma)
        return tuple(new_gammas)

    one = jnp.ones((A_HEAD, CHAINS), F32)
    gammas = lax.fori_loop(0, tt_steps, to_chain_tiles, (one, one))
    for hp in range(2):
        g_ref[hp] = gammas[hp]

    def bcast_row(ref, t, j):
        return ref[t, pl.ds(j, SUBLANES, stride=0), :]

    def time_step(hp, t):
        s_ref = s2_ref.at[hp]
        r_ref, kk_ref, v_ref, kka_ref, kd_ref = (ref.at[hp] for ref in (r2_ref, kk2_ref, v2_ref, kka2_ref, kd2_ref))
        for ib in range(A_HEAD // WKV_ROWS):
            i0 = ib * WKV_ROWS
            rows = [pl.ds(i0 + SUBLANES * b, SUBLANES) for b in range(nblk)]
            sa = [[None] * WKV_ACCS for _ in range(nblk)]
            for j in range(A_HEAD):
                kkj = bcast_row(kk_ref, t, j)
                for b in range(nblk):
                    p = s_ref[j, rows[b], :] * kkj
                    sa[b][j % WKV_ACCS] = p if sa[b][j % WKV_ACCS] is None else sa[b][j % WKV_ACCS] + p
            sa = [functools.reduce(lambda x, y: x + y, parts) for parts in sa]
            v8 = [v_ref[t, rows[b], :] for b in range(nblk)]
            out = [[None] * WKV_ACCS for _ in range(nblk)]
            for j in range(A_HEAD):
                kkaj = bcast_row(kka_ref, t, j)
                kdj = bcast_row(kd_ref, t, j)
                rj = bcast_row(r_ref, t, j)
                for b in range(nblk):
                    sn = s_ref[j, rows[b], :] - sa[b] * kkaj + v8[b] * kdj
                    s_ref[j, rows[b], :] = sn
                    q = sn * rj
                    out[b][j % WKV_ACCS] = q if out[b][j % WKV_ACCS] is None else out[b][j % WKV_ACCS] + q
            for b in range(nblk):
                o_ref[hp, rows[b], :] = functools.reduce(lambda x, y: x + y, out[b])

    def step(u, carry):
        for hp in range(2):
            time_step(hp, u)
        x2 = jnp.concatenate([o_ref[0], o_ref[1]], axis=0).T
        ub = tt_steps - 1 - u
        for hv in range(n_hv):
            of_ref[u, :, hv * LANES:(hv + 1) * LANES] = x2[hv * SUBLANES:(hv + 1) * SUBLANES, :]
            ob_ref[ub, :, hv * LANES:(hv + 1) * LANES] = x2[(n_hv + hv) * SUBLANES:(n_hv + hv + 1) * SUBLANES, :]
        return carry

    lax.fori_loop(0, tt_steps, step, 0)

    def rescale(j, carry):
        for hp in range(2):
            gj = g_ref[hp, pl.ds(j, SUBLANES, stride=0), :]
            for b in range(A_HEAD // SUBLANES):
                rows = pl.ds(b * SUBLANES, SUBLANES)
                s2_ref[hp, j, rows, :] = s2_ref[hp, j, rows, :] * gj
        return carry

    lax.fori_loop(0, A_HEAD, rescale, 0)

    @pl.when(c == pl.num_programs(1) - 1)
    def _():
        sf_ref[...] = s2_ref[...]


def _wkv(r, kk, v, w, kka, kd, s0):
    g_n, t_n = r.shape[0], r.shape[1]
    tt = WKV_TT
    nc = t_n // tt
    split = lambda a: a.reshape(a.shape[:-1] + (SEQ_PER_GROUP, A_WIDTH))
    blk = (tt, SEQ_PER_GROUP, A_WIDTH)

    fwd = pl.BlockSpec((None,) + blk, lambda g, c: (g, c, 0, 0))
    bwd = pl.BlockSpec((None,) + blk, lambda g, c: (g, nc - 1 - c, 0, 0))
    fwd_d = pl.BlockSpec((None, None) + blk, lambda g, c: (0, g, c, 0, 0))
    bwd_d = pl.BlockSpec((None, None) + blk, lambda g, c: (1, g, nc - 1 - c, 0, 0))
    state = pl.BlockSpec((None, 2, A_HEAD, A_HEAD, CHAINS), lambda g, c: (g, 0, 0, 0, 0),
                         pipeline_mode=pl.Buffered(1))
    tiles = pltpu.VMEM((2, tt, A_HEAD, CHAINS), F32)
    o_sds = jax.ShapeDtypeStruct((g_n, t_n, SEQ_PER_GROUP, A_WIDTH), F32)
    r, kk, v, w, kka, kd = (split(a) for a in (r, kk, v, w, kka, kd))
    o_f, o_b, s_fin = pl.pallas_call(
        functools.partial(_wkv_kernel, tt_steps=tt),
        out_shape=[o_sds, o_sds, jax.ShapeDtypeStruct((g_n, 2, A_HEAD, A_HEAD, CHAINS), F32)],
        grid=(g_n, nc),
        in_specs=[fwd, bwd, fwd, bwd, fwd, bwd, fwd_d, bwd_d, fwd_d, bwd_d, fwd_d, bwd_d, state],
        out_specs=[fwd, bwd, state],
        scratch_shapes=([pltpu.VMEM((2, A_HEAD, A_HEAD, CHAINS), F32)] + [tiles] * 5
                        + [pltpu.VMEM((2, A_HEAD, CHAINS), F32)] * 2),
        compiler_params=pltpu.CompilerParams(dimension_semantics=("parallel", "arbitrary"),
                                             vmem_limit_bytes=WKV_VMEM_LIMIT),
        name="wkv",
    )(r, r, kk, kk, v, v, w, w, kka, kka, kd, kd, s0)
    merge = lambda a: a.reshape(g_n, t_n, SEQ_PER_GROUP * A_WIDTH)
    return merge(o_f), merge(o_b), s_fin


def _ya_kernel(of_ref, ob_ref, bon_ref, g0_ref, g1_ref, g2_ref, g3_ref, gw_ref, gb_ref, gm_ref, o_ref):
    gm = gm_ref[...]
    for c, gate_ref in enumerate((g0_ref, g1_ref, g2_ref, g3_ref)):
        cols = slice(c * CW, (c + 1) * CW)
        o = of_ref[:, cols] + ob_ref[:, cols]
        mean = _gsum(o, gm) * (1.0 / A_HEAD)
        dev = o - mean
        var = _gsum(dev * dev, gm) * (1.0 / A_HEAD)
        y = dev * lax.rsqrt(var + GN_EPS) * gw_ref[:, cols] + gb_ref[:, cols]
        o_ref[:, cols] = ((y + bon_ref[:, cols]) * _silu(gate_ref[...])).astype(BF16)


def _ya(st, o_f, o_b, bonus, proj, gn_w, gn_b, gmat):
    tm = TM_ROW
    gate_cb = (SHIFT_COLS) // CW
    blk = pl.BlockSpec((tm, A_WIDTH), lambda i: (i, 0))
    vec = pl.BlockSpec((1, A_WIDTH), lambda i: (0, 0))
    tps = st.t_len // tm

    def til_idx(i):
        s = i // tps
        return (s // SEQ_PER_GROUP, lax.rem(i, tps), lax.rem(s, SEQ_PER_GROUP))

    til = pl.BlockSpec((None, tm, A_WIDTH), til_idx)
    gates = [pl.BlockSpec((tm, CW), lambda i, c=c: (i, gate_cb + c)) for c in range(A_WIDTH // CW)]
    return pl.pallas_call(
        _ya_kernel,
        out_shape=jax.ShapeDtypeStruct((st.rows, A_WIDTH), BF16),
        grid=(st.rows // tm,),
        in_specs=[til, til, blk] + gates + [vec, vec, pl.BlockSpec((CW, CW), lambda i: (0, 0))],
        out_specs=blk,
        compiler_params=_cparams(("parallel",)),
        name="wkv_post",
    )(o_f, o_b, bonus, proj, proj, proj, proj, gn_w, gn_b, gmat)


def _dft_ch_kernel(u_ref, cs_ref, zc_ref, zs_ref):
    z = _dot(u_ref[...].astype(BF16), cs_ref[...])
    zc_ref[...] = z[:, :B_GROUP_CH].astype(BF16)
    zs_ref[...] = z[:, B_GROUP_CH:].astype(BF16)


def _dft_channels(st, proj, cs_mat):
    tm = 2048
    u_cb = (SHIFT_COLS + A_WIDTH) // B_GROUP_CH
    blk = pl.BlockSpec((tm, B_GROUP_CH), lambda i, g: (i, g))
    sds = jax.ShapeDtypeStruct((st.rows, B_WIDTH), BF16)
    return pl.pallas_call(
        _dft_ch_kernel,
        out_shape=[sds, sds],
        grid=(st.rows // tm, B_GROUPS),
        in_specs=[pl.BlockSpec((tm, B_GROUP_CH), lambda i, g: (i, u_cb + g)),
                  pl.BlockSpec((B_GROUP_CH, 2 * B_GROUP_CH), lambda i, g: (0, 0))],
        out_specs=[blk, blk],
        compiler_params=_cparams(("parallel", "parallel")),
        name="dft_channels",
    )(proj, cs_mat)


def _dft_time_kernel(c_ref, s_ref, zc_ref, zs_ref, g0_ref, g1_ref, g2_ref, g3_ref, o_ref, *, scale):
    acc = _dot(c_ref[...], zc_ref[...]) - _dot(s_ref[...], zs_ref[...])
    for g, gate_ref in enumerate((g0_ref, g1_ref, g2_ref, g3_ref)):
        cols = slice(g * B_GROUP_CH, (g + 1) * B_GROUP_CH)
        o_ref[:, cols] = (acc[:, cols] * scale * _silu(gate_ref[...])).astype(BF16)


def _dft_time(st, zc, zs, proj, cmat, smat):
    t_len = st.t_len
    tm = min(t_len, 512)
    mt = t_len // tm
    gate_cb = (SHIFT_COLS + A_WIDTH + B_WIDTH) // B_GROUP_CH
    gate_specs = [pl.BlockSpec((tm, B_GROUP_CH), lambda b, m, g=g: (b * mt + m, gate_cb + g)) for g in range(B_GROUPS)]
    scale = 1.0 / float(np.sqrt(t_len * B_GROUP_CH))
    return pl.pallas_call(
        functools.partial(_dft_time_kernel, scale=scale),
        out_shape=jax.ShapeDtypeStruct((st.rows, B_WIDTH), BF16),
        grid=(st.n_seq, mt),
        in_specs=[pl.BlockSpec((tm, t_len), lambda b, m: (m, 0)),
                  pl.BlockSpec((tm, t_len), lambda b, m: (m, 0)),
                  pl.BlockSpec((t_len, B_WIDTH), lambda b, m: (b, 0)),
                  pl.BlockSpec((t_len, B_WIDTH), lambda b, m: (b, 0))] + gate_specs,
        out_specs=pl.BlockSpec((tm, B_WIDTH), lambda b, m: (b * mt + m, 0)),
        compiler_params=_cparams(("parallel", "parallel")),
        name="dft_time",
    )(cmat, smat, zc, zs, proj, proj, proj, proj)


def _dft_mats(n):
    idx = jnp.arange(n, dtype=jnp.int32)
    prod = (idx[:, None] * idx[None, :]) % n
    ang = prod.astype(F32) * (2.0 * np.pi / n)
    return jnp.cos(ang), jnp.sin(ang)


def _out_kernel(a1_ref, a2_ref, w1_ref, w2_ref, x_ref, gate_ref, g_ref, *rest, emit_next):
    y = _dot(a1_ref[...], w1_ref[...]) + _dot(a2_ref[...], w2_ref[...])
    ms = jnp.mean(y * y, axis=-1, keepdims=True)
    yn = y * lax.rsqrt(ms + RMS_EPS) * g_ref[...]
    x_new = x_ref[...] + gate_ref[...] * yn
    if emit_next:
        ng_ref, nsh_ref, nsc_ref, o_ref, h_ref = rest
        h_ref[...] = _norm_modulate(x_new, ng_ref[...], nsh_ref[...], nsc_ref[...])
    else:
        o_ref, = rest
    o_ref[...] = x_new


def _out_proj(st, a1, a1_cb, a2, a2_cb, w_bf16, li, x, mods4, norm_post, norm_pre, layer):
    tm = TM_ROW
    half = D_MODEL // 2
    cond = lambda i: _cond_of_row(st, i * tm)
    emit_next = layer + 1 < DEPTH
    row = pl.BlockSpec((tm, D_MODEL), lambda i: (i, 0))
    in_specs = [pl.BlockSpec((tm, half), lambda i: (i, a1_cb)),
                pl.BlockSpec((tm, half), lambda i: (i, a2_cb)),
                pl.BlockSpec((None, half, D_MODEL), lambda i: (li, 0, 0)),
                pl.BlockSpec((None, half, D_MODEL), lambda i: (li, 1, 0)),
                row,
                pl.BlockSpec((None, None, 1, D_MODEL), lambda i: (layer, cond(i), 0, 2)),
                pl.BlockSpec((None, 1, D_MODEL), lambda i: (layer, 0, 0))]
    args = [a1, a2, w_bf16, w_bf16, x, mods4, norm_post]
    out_shape = [jax.ShapeDtypeStruct((st.rows, D_MODEL), F32)]
    out_specs = [row]
    if emit_next:
        in_specs += [pl.BlockSpec((None, 1, D_MODEL), lambda i: (layer + 1, 0, 0)),
                     pl.BlockSpec((None, None, 1, D_MODEL), lambda i: (layer + 1, cond(i), 0, 0)),
                     pl.BlockSpec((None, None, 1, D_MODEL), lambda i: (layer + 1, cond(i), 0, 1))]
        args += [norm_pre, mods4, mods4]
        out_shape.append(jax.ShapeDtypeStruct((st.rows, D_MODEL), BF16))
        out_specs.append(row)
    outs = pl.pallas_call(
        functools.partial(_out_kernel, emit_next=emit_next),
        out_shape=out_shape,
        grid=(st.rows // tm,),
        in_specs=in_specs,
        out_specs=out_specs,
        compiler_params=_cparams(("parallel",)),
        name="out_proj",
    )(*args)
    return (outs[0], outs[1]) if emit_next else (outs[0], None)


def _softmax_pv(scores, values, sink):
    m = sink
    for s in scores:
        m = jnp.maximum(m, jnp.max(s, axis=-1, keepdims=True))
    den = jnp.exp(sink - m)
    acc = None
    for s, v in zip(scores, values):
        p = jnp.exp(s - m)
        den = den + jnp.sum(p, axis=-1, keepdims=True)
        pv = _dot(p.astype(BF16), v)
        acc = pv if acc is None else acc + pv
    return acc / den


def _qk(q, k):
    return lax.dot_general(q, k, (((1,), (1,)), ((), ())), preferred_element_type=F32)


def _upper_half(rows):
    return lax.broadcasted_iota(jnp.int32, (rows, LANES), 1) >= C_HEAD


def _both_halves(x, hh):
    upper = _upper_half(x.shape[0])
    keep = upper if hh == 1 else jnp.logical_not(upper)
    return jnp.where(keep, x, pltpu.roll(x, C_HEAD, axis=1)).astype(BF16)


def _pair_heads(q_chunks, keys, values, sinks, masks):
    rows = q_chunks[0].shape[0]
    upper = _upper_half(rows)
    lower = jnp.logical_not(upper)
    outs = [None] * (2 * C_GROUP)
    for hh in range(2):
        kd = [_both_halves(k, hh) for k in keys]
        vd = [_both_halves(v, hh) for v in values]
        heads = [hh * C_GROUP + g for g in range(C_GROUP)]
        q4 = jnp.concatenate([jnp.where(upper if n % 2 else lower, q_chunks[n // 2], 0.0) for n in heads],
                             axis=0).astype(BF16)
        s4 = [_qk(q4, k) for k in kd]
        p4 = [[] for _ in kd]
        dens = []
        for g, n in enumerate(heads):
            sl = slice(g * rows, (g + 1) * rows)
            scores = [s[sl] if mask is None else jnp.where(mask, s[sl], NEG_INF) for s, mask in zip(s4, masks)]
            sink = sinks(n)
            m = sink
            for s in scores:
                m = jnp.maximum(m, jnp.max(s, axis=-1, keepdims=True))
            den = jnp.exp(sink - m)
            for t, s in enumerate(scores):
                p = jnp.exp(s - m)
                den = den + jnp.sum(p, axis=-1, keepdims=True)
                p4[t].append(p.astype(BF16))
            dens.append(den)
        pv4 = None
        for p_parts, v in zip(p4, vd):
            pv = _dot(jnp.concatenate(p_parts, axis=0), v)
            pv4 = pv if pv4 is None else pv4 + pv
        for g, n in enumerate(heads):
            outs[n] = pv4[g * rows:(g + 1) * rows] / dens[g]
    return jnp.concatenate([jnp.where(lower, outs[2 * m], outs[2 * m + 1]) for m in range(C_GROUP)], axis=-1)


def _attn_ctx_kernel(sink_ref, q_ref, k_ref, v_ref, gate_ref, o_ref):
    kp = pl.program_id(1)
    q = q_ref[...] * (C_HEAD ** -0.5)
    q_chunks = [q[:, m * LANES:(m + 1) * LANES] for m in range(C_GROUP)]
    o = _pair_heads(q_chunks, [k_ref[...]], [v_ref[...]], lambda n: sink_ref[kp * 2 * C_GROUP + n], [None])
    o_ref[...] = (o * _silu(gate_ref[...])).astype(BF16)


def _attn_context(proj, sink):
    qw = 2 * C_GROUP * C_HEAD
    k_cb = C_HEADS * C_HEAD // LANES
    v_cb = (C_HEADS + C_KV_HEADS) * C_HEAD // LANES
    gate_cb = (C_HEADS + 2 * C_KV_HEADS) * C_HEAD // qw
    return pl.pallas_call(
        _attn_ctx_kernel,
        out_shape=jax.ShapeDtypeStruct((PROMPT.rows, D_MODEL), BF16),
        grid=(BATCH, C_KV_HEADS // 2),
        in_specs=[pl.BlockSpec(memory_space=pltpu.SMEM),
                  pl.BlockSpec((SEQ, qw), lambda b, kp: (b, kp)),
                  pl.BlockSpec((SEQ, LANES), lambda b, kp: (b, k_cb + kp)),
                  pl.BlockSpec((SEQ, LANES), lambda b, kp: (b, v_cb + kp)),
                  pl.BlockSpec((SEQ, qw), lambda b, kp: (b, gate_cb + kp))],
        out_specs=pl.BlockSpec((SEQ, qw), lambda b, kp: (b, kp)),
        compiler_params=_cparams(("parallel", "parallel")),
        name="attn_context",
    )(sink, proj, proj, proj, proj)


def _rope(x, cos, sin_signed):
    lane = lax.broadcasted_iota(jnp.int32, x.shape, 1)
    first = (lane & 31) < 16
    partner = jnp.where(first, pltpu.roll(x, LANES - 16, axis=1), pltpu.roll(x, 16, axis=1))
    return x * cos + partner * sin_signed


def _attn_lat_kernel(sink_ref, q_ref, kp_ref, ko_ref, kn_ref, vp_ref, vo_ref, vn_ref, ck_ref, cv_ref,
                     cq_ref, sq_ref, cp_ref, sp_ref, cn_ref, sn_ref, gate_ref, o_ref):
    kpair = pl.program_id(1)
    qb = pl.program_id(2)
    cq = cq_ref[...]
    sq = sq_ref[...]
    q = q_ref[...] * (C_HEAD ** -0.5)
    qr = [_rope(q[:, n * LANES:(n + 1) * LANES], cq, sq) for n in range(4)]
    kband = jnp.concatenate([_rope(kp_ref[...], cp_ref[...], sp_ref[...]),
                             _rope(ko_ref[...], cq, sq),
                             _rope(kn_ref[...], cn_ref[...], sn_ref[...])], axis=0)
    vband = jnp.concatenate([vp_ref[...], vo_ref[...], vn_ref[...]], axis=0)
    qpos = qb * BLOCK + lax.broadcasted_iota(jnp.int32, (BLOCK, 3 * BLOCK), 0)
    kpos = (qb - 1) * BLOCK + lax.broadcasted_iota(jnp.int32, (BLOCK, 3 * BLOCK), 1)
    valid = (jnp.abs(qpos - kpos) <= WINDOW) & (kpos >= 0) & (kpos < DEC_SEQ)
    o = _pair_heads(qr, [kband, ck_ref[...]], [vband, cv_ref[...]],
                    lambda n: sink_ref[kpair * 2 * C_GROUP + n], [valid, None])
    o_ref[...] = (o * _silu(gate_ref[...])).astype(BF16)


def _attn_latent(proj, cache_k4, cache_v4, li, sink, cos_t, sin_t):
    qw = 2 * C_GROUP * C_HEAD
    k_cb = C_HEADS * C_HEAD // LANES
    v_cb = (C_HEADS + C_KV_HEADS) * C_HEAD // LANES
    gate_cb = (C_HEADS + 2 * C_KV_HEADS) * C_HEAD // qw
    nqb = DEC_SEQ // BLOCK

    def rows(delta):
        def idx(b, kp, qb):
            return b * nqb + jnp.clip(qb + delta, 0, nqb - 1)
        return idx

    def kv_spec(cb, delta):
        r = rows(delta)
        return pl.BlockSpec((BLOCK, LANES), lambda b, kp, qb: (r(b, kp, qb), cb + kp))

    def tab_spec(delta):
        return pl.BlockSpec((BLOCK, LANES), lambda b, kp, qb: (jnp.clip(qb + delta, 0, nqb - 1), 0))

    cache_spec = pl.BlockSpec((None, None, PAST_LEN, LANES), lambda b, kp, qb: (b, li, 0, kp))
    own = rows(0)
    return pl.pallas_call(
        _attn_lat_kernel,
        out_shape=jax.ShapeDtypeStruct((SAMPLE.rows, D_MODEL), BF16),
        grid=(DEC_BATCH, C_KV_HEADS // 2, nqb),
        in_specs=[pl.BlockSpec(memory_space=pltpu.SMEM),
                  pl.BlockSpec((BLOCK, qw), lambda b, kp, qb: (own(b, kp, qb), kp)),
                  kv_spec(k_cb, -1), kv_spec(k_cb, 0), kv_spec(k_cb, 1),
                  kv_spec(v_cb, -1), kv_spec(v_cb, 0), kv_spec(v_cb, 1),
                  cache_spec, cache_spec,
                  tab_spec(0), tab_spec(0), tab_spec(-1), tab_spec(-1), tab_spec(1), tab_spec(1),
                  pl.BlockSpec((BLOCK, qw), lambda b, kp, qb: (own(b, kp, qb), gate_cb + kp))],
        out_specs=pl.BlockSpec((BLOCK, qw), lambda b, kp, qb: (own(b, kp, qb), kp)),
        compiler_params=_cparams(("parallel", "parallel", "parallel")),
        name="attn_latent",
    )(sink, proj, proj, proj, proj, proj, proj, proj, cache_k4, cache_v4,
      cos_t, sin_t, cos_t, sin_t, cos_t, sin_t, proj)


def _rope_tables():
    t = jnp.arange(DEC_SEQ, dtype=jnp.int32)
    row = (t // GRID_W).astype(F32)
    col = (t % GRID_W).astype(F32)
    nf = C_HEAD // 4
    inv = 1.0 / (ROPE_BASE ** (jnp.arange(nf, dtype=F32) / nf))
    lane = np.arange(LANES)
    f_of_lane = lane % nf
    use_col = (lane % C_HEAD) >= C_HEAD // 2
    sign = np.where((lane % 32) < 16, -1.0, 1.0).astype(np.float32)
    pos = jnp.where(jnp.asarray(use_col)[None, :], col[:, None], row[:, None])
    ang = pos * inv[jnp.asarray(f_of_lane)][None, :]
    return jnp.cos(ang), jnp.sin(ang) * jnp.asarray(sign)[None, :]


def _even_mixer(st, proj, s0, p, gmat, cs_mat, dft):
    r, kk, v, w2, kka2, kd2, bonus = _wkv_prep(st, proj, p["mu"], p["w0"], p["wup"], p["a0"], p["aup"],
                                                p["k_k"], p["k_a"], p["r_k"], gmat)
    o_f, o_b, s_fin = _wkv(r, kk, v, w2, kka2, kd2, s0)
    ya = _ya(st, o_f, o_b, bonus, proj, p["gn_w"], p["gn_b"], gmat)
    zc, zs = _dft_channels(st, proj, cs_mat)
    yb = _dft_time(st, zc, zs, proj, dft[0], dft[1])
    return ya, yb, s_fin


def kernel(x_prompt, x_sample, c, state_wkv, cache_k, cache_v, c_ctx, mod_w, mod_b, norm_pre, norm_post,
           even_w_in, even_mu, even_w0, even_w_up, even_a0, even_a_up, even_k_k, even_k_a, even_r_k,
           even_gn_w, even_gn_b, even_w_out, odd_w_in, odd_sink, odd_w_out):
    n_odd = odd_w_in.shape[0]
    xs = {PROMPT: x_prompt.reshape(PROMPT.rows, D_MODEL), SAMPLE: x_sample.reshape(SAMPLE.rows, D_MODEL)}
    conds = jnp.concatenate([c_ctx[None, :], c, jnp.zeros((N_COND - 1 - DEC_BATCH, D_MODEL), F32)], axis=0)
    mods4 = _modulation(conds, mod_w, mod_b).reshape(DEPTH, N_COND, 1, 3 * D_MODEL)
    norm_pre3 = norm_pre.reshape(DEPTH, 1, D_MODEL)
    norm_post3 = norm_post.reshape(DEPTH, 1, D_MODEL)

    even_w_in_b = even_w_in.astype(BF16)
    even_w_out_b = even_w_out.astype(BF16)
    odd_w_in_b = odd_w_in.astype(BF16)
    odd_w_out_b = odd_w_out.astype(BF16)

    head_of_lane = np.arange(CW) // A_HEAD
    gmat = jnp.asarray((head_of_lane[:, None] == head_of_lane[None, :]).astype(np.float32)).astype(BF16)
    c_ch, s_ch = _dft_mats(B_GROUP_CH)
    cs_mat = jnp.concatenate([c_ch, s_ch], axis=1).astype(BF16)
    dft = {PROMPT: tuple(m.astype(BF16) for m in _dft_mats(SEQ)),
           SAMPLE: tuple(m.astype(BF16) for m in _dft_mats(DEC_SEQ))}
    cos_t, sin_t = _rope_tables()
    cache_k4 = cache_k.reshape(DEC_BATCH, n_odd, PAST_LEN, C_KV_HEADS * C_HEAD)
    cache_v4 = cache_v.reshape(DEC_BATCH, n_odd, PAST_LEN, C_KV_HEADS * C_HEAD)

    def pad_rows(w, slot):
        return jnp.pad(w, ((0, 0), (slot * LORA, (3 - slot) * LORA), (0, 0)))

    wup_pad = jnp.stack([pad_rows(even_w_up[:, 0], 0), pad_rows(even_w_up[:, 1], 1)], axis=1)
    aup_pad = jnp.stack([pad_rows(even_a_up[:, 0], 2), pad_rows(even_a_up[:, 1], 3)], axis=1)

    new_wkv, new_k, new_v = [], [], []
    hs = {st: _pre_norm(st, xs[st], norm_pre3, mods4, 0) for st in (PROMPT, SAMPLE)}
    for layer in range(DEPTH):
        i = layer // 2
        if layer % 2 == 0:
            p = {
                "mu": even_mu[i][None, :], "w0": even_w0[i], "a0": even_a0[i],
                "wup": wup_pad[i], "aup": aup_pad[i],
                "k_k": even_k_k[i][None, :], "k_a": even_k_a[i][None, :], "r_k": even_r_k[i].reshape(1, A_WIDTH),
                "gn_w": even_gn_w[i][None, :], "gn_b": even_gn_b[i][None, :],
            }
            s0s = state_wkv[:, i].reshape(DEC_BATCH, 2, A_HEADS // 2, 2, A_HEAD, A_HEAD)
            s0 = {PROMPT: jnp.zeros((BATCH // SEQ_PER_GROUP, 2, A_HEAD, A_HEAD, CHAINS), F32),
                  SAMPLE: s0s.transpose(3, 5, 4, 1, 2, 0).reshape(1, 2, A_HEAD, A_HEAD, CHAINS)}
            for st in (PROMPT, SAMPLE):
                proj = _in_proj(st, hs[st], even_w_in_b, i)
                ya, yb, s_fin = _even_mixer(st, proj, s0[st], p, gmat, cs_mat, dft[st])
                xs[st], hs[st] = _out_proj(st, ya, 0, yb, 0, even_w_out_b, i, xs[st], mods4, norm_post3, norm_pre3,
                                           layer)
                if st is PROMPT:
                    s_fin = s_fin.reshape(BATCH // SEQ_PER_GROUP, 2, A_HEAD, A_HEAD, 2, A_HEADS // 2, SEQ_PER_GROUP)
                    new_wkv.append(s_fin.transpose(0, 6, 4, 5, 1, 3, 2).reshape(BATCH, 2, A_HEADS, A_HEAD, A_HEAD))
        else:
            proj_p = _in_proj(PROMPT, hs[PROMPT], odd_w_in_b, i)
            proj_s = _in_proj(SAMPLE, hs[SAMPLE], odd_w_in_b, i)
            kv0 = C_HEADS * C_HEAD
            kvn = C_KV_HEADS * C_HEAD
            new_k.append(proj_p[:, kv0:kv0 + kvn].reshape(BATCH, SEQ, C_KV_HEADS, C_HEAD))
            new_v.append(proj_p[:, kv0 + kvn:kv0 + 2 * kvn].reshape(BATCH, SEQ, C_KV_HEADS, C_HEAD))
            att_p = _attn_context(proj_p, odd_sink[i])
            att_s = _attn_latent(proj_s, cache_k4, cache_v4, i, odd_sink[i], cos_t, sin_t)
            for st, att in ((PROMPT, att_p), (SAMPLE, att_s)):
                xs[st], hs[st] = _out_proj(st, att, 0, att, 1, odd_w_out_b, i, xs[st], mods4, norm_post3, norm_pre3,
                                           layer)

    y_prompt = xs[PROMPT].reshape(BATCH, SEQ, D_MODEL)
    y_sample = xs[SAMPLE].reshape(DEC_BATCH, DEC_SEQ, D_MODEL)
    return (y_prompt, y_sample, jnp.stack(new_wkv, axis=1), jnp.stack(new_k, axis=1), jnp.stack(new_v, axis=1))
```

```python
import collections
import functools

import numpy as np
import jax
import jax.numpy as jnp
from jax import lax
from jax.experimental import pallas as pl
from jax.experimental.pallas import tpu as pltpu

F32 = jnp.float32
BF16 = jnp.bfloat16

D_MODEL = 2048
BATCH = 32
SEQ = 256
DEPTH = 4
DEC_BATCH = 8
DEC_SEQ = 2048
PAST_LEN = 256
GRID_W = 64
RMS_EPS = 1e-6
A_WIDTH = 1024
A_HEAD = 64
A_HEADS = 16
LORA = 64
GN_EPS = 64e-5
DECAY_SCALE = float(np.exp(-0.5))
B_WIDTH = 1024
B_GROUPS = 4
B_GROUP_CH = 256
SHIFT_COLS = 3 * A_WIDTH + 4 * LORA
EVEN_IN = SHIFT_COLS + A_WIDTH + 2 * B_WIDTH
C_HEAD = 64
C_HEADS = 32
C_KV_HEADS = 8
C_GROUP = 4
WINDOW = 128
BLOCK = 128
ROPE_BASE = 10000.0
ODD_IN = (C_HEADS + 2 * C_KV_HEADS) * C_HEAD + D_MODEL
NEG_INF = -1e30

N_COND = 16
SUBLANES = 8
LANES = 128
CHAINS = LANES
SEQ_PER_GROUP = CHAINS // A_HEADS
CW = 256
TM_IN = 512
TN_IN = 1280
TM_ROW = 256
WKV_TT = 32
WKV_VMEM_LIMIT = 56 * 1024 * 1024
WKV_ROWS = 16
WKV_ACCS = 2
VMEM_LIMIT = 48 * 1024 * 1024

Stream = collections.namedtuple("Stream", "rows t_len n_seq cond0 cond_per_seq")
PROMPT = Stream(BATCH * SEQ, SEQ, BATCH, 0, 0)
SAMPLE = Stream(DEC_BATCH * DEC_SEQ, DEC_SEQ, DEC_BATCH, 1, 1)


def _cparams(sem):
    return pltpu.CompilerParams(dimension_semantics=sem, vmem_limit_bytes=VMEM_LIMIT)


def _cond_of_row(st, row0):
    return st.cond0 + st.cond_per_seq * (row0 // st.t_len)


def _sigmoid(x):
    return 1.0 / (1.0 + jnp.exp(-x))


def _silu(x):
    return x * _sigmoid(x)


def _split(a):
    hi = a.astype(BF16)
    lo = (a - hi.astype(F32)).astype(BF16)
    return hi, lo


def _dot(a, b):
    return jnp.dot(a, b, preferred_element_type=F32)


def _dot3_split(a_hl, b_hl):
    (ah, al), (bh, bl) = a_hl, b_hl
    return _dot(ah, bh) + (_dot(ah, bl) + _dot(al, bh))


def _dot3(a, b):
    return _dot3_split(_split(a), _split(b))


def _gsum(x, gmat):
    xh, xl = _split(x)
    return _dot(xh, gmat) + _dot(xl, gmat)


def _mod_kernel(c_ref, w_ref, b_ref, o_ref):
    o_ref[...] = _dot3(_silu(c_ref[...]), w_ref[...]) + b_ref[...]


def _modulation(conds, mod_w, mod_b):
    tn = 512
    n = 3 * D_MODEL
    return pl.pallas_call(
        _mod_kernel,
        out_shape=jax.ShapeDtypeStruct((DEPTH, N_COND, n), F32),
        grid=(DEPTH, n // tn),
        in_specs=[
            pl.BlockSpec((N_COND, D_MODEL), lambda l, j: (0, 0)),
            pl.BlockSpec((None, D_MODEL, tn), lambda l, j: (l, 0, j)),
            pl.BlockSpec((None, 1, tn), lambda l, j: (l, 0, j)),
        ],
        out_specs=pl.BlockSpec((None, N_COND, tn), lambda l, j: (l, 0, j)),
        compiler_params=_cparams(("parallel", "parallel")),
        name="modulation",
    )(conds, mod_w, mod_b.reshape(DEPTH, 1, n))


def _norm_modulate(x, g, sh, sc):
    ms = jnp.mean(x * x, axis=-1, keepdims=True)
    y = x * lax.rsqrt(ms + RMS_EPS) * g
    return (y * (1.0 + sc) + sh).astype(BF16)


def _pre_kernel(x_ref, g_ref, sh_ref, sc_ref, h_ref):
    h_ref[...] = _norm_modulate(x_ref[...], g_ref[...], sh_ref[...], sc_ref[...])


def _pre_norm(st, x, norm_g, mods4, layer):
    tm = TM_ROW
    cond = lambda i: _cond_of_row(st, i * tm)
    return pl.pallas_call(
        _pre_kernel,
        out_shape=jax.ShapeDtypeStruct((st.rows, D_MODEL), BF16),
        grid=(st.rows // tm,),
        in_specs=[pl.BlockSpec((tm, D_MODEL), lambda i: (i, 0)),
                  pl.BlockSpec((None, 1, D_MODEL), lambda i: (layer, 0, 0)),
                  pl.BlockSpec((None, None, 1, D_MODEL), lambda i: (layer, cond(i), 0, 0)),
                  pl.BlockSpec((None, None, 1, D_MODEL), lambda i: (layer, cond(i), 0, 1))],
        out_specs=pl.BlockSpec((tm, D_MODEL), lambda i: (i, 0)),
        compiler_params=_cparams(("parallel",)),
        name="pre_norm",
    )(x, norm_g, mods4, mods4)


def _in_kernel(h_ref, w_ref, o_ref):
    o_ref[...] = _dot(h_ref[...], w_ref[...])


def _in_proj(st, h, w_bf16, li):
    n = w_bf16.shape[-1]
    return pl.pallas_call(
        _in_kernel,
        out_shape=jax.ShapeDtypeStruct((st.rows, n), F32),
        grid=(n // TN_IN, st.rows // TM_IN),
        in_specs=[pl.BlockSpec((TM_IN, D_MODEL), lambda j, i: (i, 0)),
                  pl.BlockSpec((None, D_MODEL, TN_IN), lambda j, i: (li, 0, j))],
        out_specs=pl.BlockSpec((TM_IN, TN_IN), lambda j, i: (i, j)),
        compiler_params=_cparams(("parallel", "parallel")),
        name="in_proj",
    )(h, w_bf16)


def _token_shift(cur, prev8, next8, mu, first, last):
    tm = cur.shape[0]
    rows = lax.broadcasted_iota(jnp.int32, cur.shape, 0)
    prow = jnp.where(first, 0.0, prev8[7:8, :])
    nrow = jnp.where(last, 0.0, next8[0:1, :])
    up = jnp.where(rows == 0, prow, pltpu.roll(cur, 1, axis=0))
    dn = jnp.where(rows == tm - 1, nrow, pltpu.roll(cur, tm - 1, axis=0))
    return cur + mu * (0.5 * (up + dn) - cur)


def _prep_kernel(r_ref, k_ref, v_ref, lo_ref, rp_ref, kp_ref, vp_ref, lp_ref, rn_ref, kn_ref, vn_ref, ln_ref,
                 mur_ref, muk_ref, muv_ref, mul_ref, w0_ref, wuph_ref, wupl_ref, a0_ref, auph_ref, aupl_ref,
                 ka_ref, rk_ref, gm_ref,
                 r_o, k_o, v_o, w_o, a_o, bon_o, *, tiles_per_seq):
    i = pl.program_id(0)
    first = lax.rem(i, tiles_per_seq) == 0
    last = lax.rem(i, tiles_per_seq) == tiles_per_seq - 1

    low = _token_shift(lo_ref[...], lp_ref[...], ln_ref[...], mul_ref[...], first, last)
    low_hl = _split(low)
    low_t_hl = _split(jnp.tanh(low))
    gm = gm_ref[...]

    for c in range(A_WIDTH // CW):
        cols = slice(c * CW, (c + 1) * CW)
        r = _token_shift(r_ref[:, cols], rp_ref[:, cols], rn_ref[:, cols], mur_ref[:, cols], first, last)
        k = _token_shift(k_ref[:, cols], kp_ref[:, cols], kn_ref[:, cols], muk_ref[:, cols], first, last)
        v = _token_shift(v_ref[:, cols], vp_ref[:, cols], vn_ref[:, cols], muv_ref[:, cols], first, last)
        ka = ka_ref[:, cols]
        kd_sum = None
        for d in range(2):
            w_raw = w0_ref[d:d + 1, cols] + _dot3_split(low_t_hl, (wuph_ref[d, :, cols], wupl_ref[d, :, cols]))
            a = _sigmoid(a0_ref[d:d + 1, cols] + _dot3_split(low_hl, (auph_ref[d, :, cols], aupl_ref[d, :, cols])))
            kd = k * (1.0 + (a - 1.0) * ka)
            w_o[d, :, cols] = jnp.exp(-DECAY_SCALE * _sigmoid(w_raw))
            a_o[d, :, cols] = a
            kd_sum = kd if kd_sum is None else kd_sum + kd
        r_o[:, cols] = r
        k_o[:, cols] = k
        v_o[:, cols] = v
        bon_o[:, cols] = _gsum(r * (0.5 * kd_sum) * rk_ref[:, cols], gm) * v


def _wkv_prep(st, proj, mu, w0, wup_pad, a0, aup_pad, k_a, r_k, gmat):
    tm = TM_ROW
    nb8 = st.rows // SUBLANES
    ncb = A_WIDTH // CW
    low_cb = 3 * A_WIDTH // CW
    assert 4 * LORA == CW and st.t_len % tm == 0

    prev_row = lambda i: jnp.maximum(i * (tm // SUBLANES) - 1, 0)
    next_row = lambda i: jnp.minimum((i + 1) * (tm // SUBLANES), nb8 - 1)

    def main(cb):
        return pl.BlockSpec((tm, A_WIDTH), lambda i: (i, cb))

    def prev(cb):
        return pl.BlockSpec((SUBLANES, A_WIDTH), lambda i: (prev_row(i), cb))

    def nxt(cb):
        return pl.BlockSpec((SUBLANES, A_WIDTH), lambda i: (next_row(i), cb))

    def fixed(spec_fn):
        return [spec_fn(0), spec_fn(1), spec_fn(2)]

    low_main = pl.BlockSpec((tm, CW), lambda i: (i, low_cb))
    low_prev = pl.BlockSpec((SUBLANES, CW), lambda i: (prev_row(i), low_cb))
    low_next = pl.BlockSpec((SUBLANES, CW), lambda i: (next_row(i), low_cb))

    def vec(cb):
        return pl.BlockSpec((1, A_WIDTH), lambda i: (0, cb))

    in_specs = (fixed(main) + [low_main] + fixed(prev) + [low_prev] + fixed(nxt) + [low_next]
                + [vec(0), vec(1), vec(2), pl.BlockSpec((1, CW), lambda i: (0, low_cb))]
                + [pl.BlockSpec((2, A_WIDTH), lambda i: (0, 0)),
                   pl.BlockSpec((2, CW, A_WIDTH), lambda i: (0, 0, 0)),
                   pl.BlockSpec((2, CW, A_WIDTH), lambda i: (0, 0, 0)),
                   pl.BlockSpec((2, A_WIDTH), lambda i: (0, 0)),
                   pl.BlockSpec((2, CW, A_WIDTH), lambda i: (0, 0, 0)),
                   pl.BlockSpec((2, CW, A_WIDTH), lambda i: (0, 0, 0)),
                   vec(0), vec(0),
                   pl.BlockSpec((CW, CW), lambda i: (0, 0))])
    tps = st.t_len // tm
    g_n = st.n_seq // SEQ_PER_GROUP

    def til(i):
        s = i // tps
        return (s // SEQ_PER_GROUP, lax.rem(i, tps), lax.rem(s, SEQ_PER_GROUP))

    one = pl.BlockSpec((tm, A_WIDTH), lambda i: (i, 0))
    til1 = pl.BlockSpec((None, tm, A_WIDTH), lambda i: til(i))
    til2 = pl.BlockSpec((2, None, tm, A_WIDTH), lambda i: (0,) + til(i))
    one_sds = jax.ShapeDtypeStruct((st.rows, A_WIDTH), F32)
    til1_sds = jax.ShapeDtypeStruct((g_n, st.t_len, SEQ_PER_GROUP * A_WIDTH), F32)
    til2_sds = jax.ShapeDtypeStruct((2, g_n, st.t_len, SEQ_PER_GROUP * A_WIDTH), F32)
    return pl.pallas_call(
        functools.partial(_prep_kernel, tiles_per_seq=tps),
        out_shape=[til1_sds, til1_sds, til1_sds, til2_sds, til2_sds, one_sds],
        grid=(st.rows // tm,),
        in_specs=in_specs,
        out_specs=[til1, til1, til1, til2, til2, one],
        compiler_params=_cparams(("parallel",)),
        name="wkv_prep",
    )(*([proj] * 12), mu, mu, mu, mu, w0, *_split(wup_pad), a0, *_split(aup_pad), k_a, r_k, gmat)


def _wkv_kernel(rf_ref, rb_ref, kf_ref, kb_ref, vf_ref, vb_ref, wf_ref, wb_ref, af_ref, ab_ref, kkw_ref, ka_ref,
                s0_ref, of_ref, ob_ref, sf_ref,
                s2_ref, r2_ref, kk2_ref, v2_ref, kka2_ref, kd2_ref, g_ref, o_ref, *, tt_steps):
    c = pl.program_id(1)
    nblk = WKV_ROWS // SUBLANES
    n_hv = A_HEADS // 2
    pairs = ((rf_ref, rb_ref), (kf_ref, kb_ref), (vf_ref, vb_ref), (wf_ref, wb_ref), (af_ref, ab_ref))

    @pl.when(c == 0)
    def _():
        s2_ref[...] = s0_ref[...]

    def to_chain_tiles(u, gammas):
        ub = tt_steps - 1 - u
        halves = []
        for xf_ref, xb_ref in pairs:
            x2 = jnp.concatenate([xf_ref[u, :, hv * LANES:(hv + 1) * LANES] for hv in range(n_hv)]
                                 + [xb_ref[ub, :, hv * LANES:(hv + 1) * LANES] for hv in range(n_hv)], axis=0)
            y = x2.T
            halves.append((y[:A_HEAD], y[A_HEAD:]))
        r_h, k_h, v_h, w_h, a_h = halves
        new_gammas = []
        for hp in range(2):
            kk = k_h[hp] * kkw_ref[hp]
            norm = jnp.sqrt(jnp.sum(kk * kk, axis=0, keepdims=True))
            kk = kk / jnp.maximum(norm, 1e-12)
            kka = kk * a_h[hp]
            kd = k_h[hp] * (1.0 + (a_h[hp] - 1.0) * ka_ref[hp])
            gamma = gammas[hp] * w_h[hp]
            inv = 1.0 / gamma
            kk2_ref[hp, u] = gammas[hp] * kk
            kka2_ref[hp, u] = kka * inv
            kd2_ref[hp, u] = kd * inv
            r2_ref[hp, u] = r_h[hp] * gamma
            v2_ref[hp, u] = v_h[hp]
            new_gammas.append(gamma)
        return tuple(new_gammas)

    one = jnp.ones((A_HEAD, CHAINS), F32)
    gammas = lax.fori_loop(0, tt_steps, to_chain_tiles, (one, one), unroll=4)
    for hp in range(2):
        g_ref[hp] = gammas[hp]

    def bcast_row(ref, t, j):
        return ref[t, pl.ds(j, SUBLANES, stride=0), :]

    def time_step(hp, t):
        s_ref = s2_ref.at[hp]
        r_ref, kk_ref, v_ref, kka_ref, kd_ref = (ref.at[hp] for ref in (r2_ref, kk2_ref, v2_ref, kka2_ref, kd2_ref))
        for ib in range(A_HEAD // WKV_ROWS):
            i0 = ib * WKV_ROWS
            rows = [pl.ds(i0 + SUBLANES * b, SUBLANES) for b in range(nblk)]
            sa = [[None] * WKV_ACCS for _ in range(nblk)]
            for j in range(A_HEAD):
                kkj = bcast_row(kk_ref, t, j)
                for b in range(nblk):
                    p = s_ref[j, rows[b], :] * kkj
                    sa[b][j % WKV_ACCS] = p if sa[b][j % WKV_ACCS] is None else sa[b][j % WKV_ACCS] + p
            sa = [functools.reduce(lambda x, y: x + y, parts) for parts in sa]
            v8 = [v_ref[t, rows[b], :] for b in range(nblk)]
            out = [[None] * WKV_ACCS for _ in range(nblk)]
            for j in range(A_HEAD):
                kkaj = bcast_row(kka_ref, t, j)
                kdj = bcast_row(kd_ref, t, j)
                rj = bcast_row(r_ref, t, j)
                for b in range(nblk):
                    sn = s_ref[j, rows[b], :] - sa[b] * kkaj + v8[b] * kdj
                    s_ref[j, rows[b], :] = sn
                    q = sn * rj
                    out[b][j % WKV_ACCS] = q if out[b][j % WKV_ACCS] is None else out[b][j % WKV_ACCS] + q
            for b in range(nblk):
                o_ref[hp, rows[b], :] = functools.reduce(lambda x, y: x + y, out[b])

    def step(u, carry):
        for hp in range(2):
            time_step(hp, u)
        x2 = jnp.concatenate([o_ref[0], o_ref[1]], axis=0).T
        ub = tt_steps - 1 - u
        for hv in range(n_hv):
            of_ref[u, :, hv * LANES:(hv + 1) * LANES] = x2[hv * SUBLANES:(hv + 1) * SUBLANES, :]
            ob_ref[ub, :, hv * LANES:(hv + 1) * LANES] = x2[(n_hv + hv) * SUBLANES:(n_hv + hv + 1) * SUBLANES, :]
        return carry

    lax.fori_loop(0, tt_steps, step, 0)

    def rescale(j, carry):
        for hp in range(2):
            gj = g_ref[hp, pl.ds(j, SUBLANES, stride=0), :]
            for b in range(A_HEAD // SUBLANES):
                rows = pl.ds(b * SUBLANES, SUBLANES)
                s2_ref[hp, j, rows, :] = s2_ref[hp, j, rows, :] * gj
        return carry

    lax.fori_loop(0, A_HEAD, rescale, 0)

    @pl.when(c == pl.num_programs(1) - 1)
    def _():
        sf_ref[...] = s2_ref[...]


def _wkv(r, k, v, w, a, kkw_t, ka_t, s0):
    g_n, t_n = r.shape[0], r.shape[1]
    tt = WKV_TT
    nc = t_n // tt
    split = lambda a: a.reshape(a.shape[:-1] + (SEQ_PER_GROUP, A_WIDTH))
    blk = (tt, SEQ_PER_GROUP, A_WIDTH)

    fwd = pl.BlockSpec((None,) + blk, lambda g, c: (g, c, 0, 0))
    bwd = pl.BlockSpec((None,) + blk, lambda g, c: (g, nc - 1 - c, 0, 0))
    fwd_d = pl.BlockSpec((None, None) + blk, lambda g, c: (0, g, c, 0, 0))
    bwd_d = pl.BlockSpec((None, None) + blk, lambda g, c: (1, g, nc - 1 - c, 0, 0))
    state = pl.BlockSpec((None, 2, A_HEAD, A_HEAD, CHAINS), lambda g, c: (g, 0, 0, 0, 0),
                         pipeline_mode=pl.Buffered(1))
    tiles = pltpu.VMEM((2, tt, A_HEAD, CHAINS), F32)
    o_sds = jax.ShapeDtypeStruct((g_n, t_n, SEQ_PER_GROUP, A_WIDTH), F32)
    r, k, v, w, a = (split(x) for x in (r, k, v, w, a))
    const = pl.BlockSpec((2, A_HEAD, CHAINS), lambda g, c: (0, 0, 0))
    o_f, o_b, s_fin = pl.pallas_call(
        functools.partial(_wkv_kernel, tt_steps=tt),
        out_shape=[o_sds, o_sds, jax.ShapeDtypeStruct((g_n, 2, A_HEAD, A_HEAD, CHAINS), F32)],
        grid=(g_n, nc),
        in_specs=[fwd, bwd, fwd, bwd, fwd, bwd, fwd_d, bwd_d, fwd_d, bwd_d, const, const, state],
        out_specs=[fwd, bwd, state],
        scratch_shapes=([pltpu.VMEM((2, A_HEAD, A_HEAD, CHAINS), F32)] + [tiles] * 5
                        + [pltpu.VMEM((2, A_HEAD, CHAINS), F32)] * 2),
        compiler_params=pltpu.CompilerParams(dimension_semantics=("parallel", "arbitrary"),
                                             vmem_limit_bytes=WKV_VMEM_LIMIT),
        name="wkv",
    )(r, r, k, k, v, v, w, w, a, a, kkw_t, ka_t, s0)
    merge = lambda a: a.reshape(g_n, t_n, SEQ_PER_GROUP * A_WIDTH)
    return merge(o_f), merge(o_b), s_fin


def _ya_kernel(of_ref, ob_ref, bon_ref, g0_ref, g1_ref, g2_ref, g3_ref, gw_ref, gb_ref, gm_ref, o_ref):
    gm = gm_ref[...]
    for c, gate_ref in enumerate((g0_ref, g1_ref, g2_ref, g3_ref)):
        cols = slice(c * CW, (c + 1) * CW)
        o = of_ref[:, cols] + ob_ref[:, cols]
        mean = _gsum(o, gm) * (1.0 / A_HEAD)
        dev = o - mean
        var = _gsum(dev * dev, gm) * (1.0 / A_HEAD)
        y = dev * lax.rsqrt(var + GN_EPS) * gw_ref[:, cols] + gb_ref[:, cols]
        o_ref[:, cols] = ((y + bon_ref[:, cols]) * _silu(gate_ref[...])).astype(BF16)


def _ya(st, o_f, o_b, bonus, proj, gn_w, gn_b, gmat):
    tm = TM_ROW
    gate_cb = (SHIFT_COLS) // CW
    blk = pl.BlockSpec((tm, A_WIDTH), lambda i: (i, 0))
    vec = pl.BlockSpec((1, A_WIDTH), lambda i: (0, 0))
    tps = st.t_len // tm

    def til_idx(i):
        s = i // tps
        return (s // SEQ_PER_GROUP, lax.rem(i, tps), lax.rem(s, SEQ_PER_GROUP))

    til = pl.BlockSpec((None, tm, A_WIDTH), til_idx)
    gates = [pl.BlockSpec((tm, CW), lambda i, c=c: (i, gate_cb + c)) for c in range(A_WIDTH // CW)]
    return pl.pallas_call(
        _ya_kernel,
        out_shape=jax.ShapeDtypeStruct((st.rows, A_WIDTH), BF16),
        grid=(st.rows // tm,),
        in_specs=[til, til, blk] + gates + [vec, vec, pl.BlockSpec((CW, CW), lambda i: (0, 0))],
        out_specs=blk,
        compiler_params=_cparams(("parallel",)),
        name="wkv_post",
    )(o_f, o_b, bonus, proj, proj, proj, proj, gn_w, gn_b, gmat)


def _dft_ch_kernel(u_ref, cs_ref, zc_ref, zs_ref):
    z = _dot(u_ref[...].astype(BF16), cs_ref[...])
    zc_ref[...] = z[:, :B_GROUP_CH].astype(BF16)
    zs_ref[...] = z[:, B_GROUP_CH:].astype(BF16)


def _dft_channels(st, proj, cs_mat):
    tm = 2048
    u_cb = (SHIFT_COLS + A_WIDTH) // B_GROUP_CH
    blk = pl.BlockSpec((tm, B_GROUP_CH), lambda i, g: (i, g))
    sds = jax.ShapeDtypeStruct((st.rows, B_WIDTH), BF16)
    return pl.pallas_call(
        _dft_ch_kernel,
        out_shape=[sds, sds],
        grid=(st.rows // tm, B_GROUPS),
        in_specs=[pl.BlockSpec((tm, B_GROUP_CH), lambda i, g: (i, u_cb + g)),
                  pl.BlockSpec((B_GROUP_CH, 2 * B_GROUP_CH), lambda i, g: (0, 0))],
        out_specs=[blk, blk],
        compiler_params=_cparams(("parallel", "parallel")),
        name="dft_channels",
    )(proj, cs_mat)


def _dft_time_kernel(c_ref, s_ref, zc_ref, zs_ref, g0_ref, g1_ref, g2_ref, g3_ref, o_ref, *, scale):
    acc = _dot(c_ref[...], zc_ref[...]) - _dot(s_ref[...], zs_ref[...])
    for g, gate_ref in enumerate((g0_ref, g1_ref, g2_ref, g3_ref)):
        cols = slice(g * B_GROUP_CH, (g + 1) * B_GROUP_CH)
        o_ref[:, cols] = (acc[:, cols] * scale * _silu(gate_ref[...])).astype(BF16)


def _dft_time(st, zc, zs, proj, cmat, smat):
    t_len = st.t_len
    tm = min(t_len, 512)
    mt = t_len // tm
    gate_cb = (SHIFT_COLS + A_WIDTH + B_WIDTH) // B_GROUP_CH
    gate_specs = [pl.BlockSpec((tm, B_GROUP_CH), lambda b, m, g=g: (b * mt + m, gate_cb + g)) for g in range(B_GROUPS)]
    scale = 1.0 / float(np.sqrt(t_len * B_GROUP_CH))
    return pl.pallas_call(
        functools.partial(_dft_time_kernel, scale=scale),
        out_shape=jax.ShapeDtypeStruct((st.rows, B_WIDTH), BF16),
        grid=(st.n_seq, mt),
        in_specs=[pl.BlockSpec((tm, t_len), lambda b, m: (m, 0)),
                  pl.BlockSpec((tm, t_len), lambda b, m: (m, 0)),
                  pl.BlockSpec((t_len, B_WIDTH), lambda b, m: (b, 0)),
                  pl.BlockSpec((t_len, B_WIDTH), lambda b, m: (b, 0))] + gate_specs,
        out_specs=pl.BlockSpec((tm, B_WIDTH), lambda b, m: (b * mt + m, 0)),
        compiler_params=_cparams(("parallel", "parallel")),
        name="dft_time",
    )(cmat, smat, zc, zs, proj, proj, proj, proj)


def _dft_mats(n):
    idx = jnp.arange(n, dtype=jnp.int32)
    prod = (idx[:, None] * idx[None, :]) % n
    ang = prod.astype(F32) * (2.0 * np.pi / n)
    return jnp.cos(ang), jnp.sin(ang)


def _out_kernel(a1_ref, a2_ref, w1_ref, w2_ref, x_ref, gate_ref, g_ref, *rest, emit_next):
    y = _dot(a1_ref[...], w1_ref[...]) + _dot(a2_ref[...], w2_ref[...])
    ms = jnp.mean(y * y, axis=-1, keepdims=True)
    yn = y * lax.rsqrt(ms + RMS_EPS) * g_ref[...]
    x_new = x_ref[...] + gate_ref[...] * yn
    if emit_next:
        ng_ref, nsh_ref, nsc_ref, o_ref, h_ref = rest
        h_ref[...] = _norm_modulate(x_new, ng_ref[...], nsh_ref[...], nsc_ref[...])
    else:
        o_ref, = rest
    o_ref[...] = x_new


def _out_proj(st, a1, a1_cb, a2, a2_cb, w_bf16, li, x, mods4, norm_post, norm_pre, layer):
    tm = TM_ROW
    half = D_MODEL // 2
    cond = lambda i: _cond_of_row(st, i * tm)
    emit_next = layer + 1 < DEPTH
    row = pl.BlockSpec((tm, D_MODEL), lambda i: (i, 0))
    in_specs = [pl.BlockSpec((tm, half), lambda i: (i, a1_cb)),
                pl.BlockSpec((tm, half), lambda i: (i, a2_cb)),
                pl.BlockSpec((None, half, D_MODEL), lambda i: (li, 0, 0)),
                pl.BlockSpec((None, half, D_MODEL), lambda i: (li, 1, 0)),
                row,
                pl.BlockSpec((None, None, 1, D_MODEL), lambda i: (layer, cond(i), 0, 2)),
                pl.BlockSpec((None, 1, D_MODEL), lambda i: (layer, 0, 0))]
    args = [a1, a2, w_bf16, w_bf16, x, mods4, norm_post]
    out_shape = [jax.ShapeDtypeStruct((st.rows, D_MODEL), F32)]
    out_specs = [row]
    if emit_next:
        in_specs += [pl.BlockSpec((None, 1, D_MODEL), lambda i: (layer + 1, 0, 0)),
                     pl.BlockSpec((None, None, 1, D_MODEL), lambda i: (layer + 1, cond(i), 0, 0)),
                     pl.BlockSpec((None, None, 1, D_MODEL), lambda i: (layer + 1, cond(i), 0, 1))]
        args += [norm_pre, mods4, mods4]
        out_shape.append(jax.ShapeDtypeStruct((st.rows, D_MODEL), BF16))
        out_specs.append(row)
    outs = pl.pallas_call(
        functools.partial(_out_kernel, emit_next=emit_next),
        out_shape=out_shape,
        grid=(st.rows // tm,),
        in_specs=in_specs,
        out_specs=out_specs,
        compiler_params=_cparams(("parallel",)),
        name="out_proj",
    )(*args)
    return (outs[0], outs[1]) if emit_next else (outs[0], None)


def _softmax_pv(scores, values, sink):
    m = sink
    for s in scores:
        m = jnp.maximum(m, jnp.max(s, axis=-1, keepdims=True))
    den = jnp.exp(sink - m)
    acc = None
    for s, v in zip(scores, values):
        p = jnp.exp(s - m)
        den = den + jnp.sum(p, axis=-1, keepdims=True)
        pv = _dot(p.astype(BF16), v)
        acc = pv if acc is None else acc + pv
    return acc / den


def _qk(q, k):
    return lax.dot_general(q, k, (((1,), (1,)), ((), ())), preferred_element_type=F32)


def _upper_half(rows):
    return lax.broadcasted_iota(jnp.int32, (rows, LANES), 1) >= C_HEAD


def _both_halves(x, hh):
    upper = _upper_half(x.shape[0])
    keep = upper if hh == 1 else jnp.logical_not(upper)
    return jnp.where(keep, x, pltpu.roll(x, C_HEAD, axis=1)).astype(BF16)


def _pair_heads(q_chunks, keys, values, sinks, masks):
    rows = q_chunks[0].shape[0]
    upper = _upper_half(rows)
    lower = jnp.logical_not(upper)
    outs = [None] * (2 * C_GROUP)
    for hh in range(2):
        kd = [_both_halves(k, hh) for k in keys]
        vd = [_both_halves(v, hh) for v in values]
        heads = [hh * C_GROUP + g for g in range(C_GROUP)]
        q4 = jnp.concatenate([jnp.where(upper if n % 2 else lower, q_chunks[n // 2], 0.0) for n in heads],
                             axis=0).astype(BF16)
        s4 = [_qk(q4, k) for k in kd]
        p4 = [[] for _ in kd]
        dens = []
        for g, n in enumerate(heads):
            sl = slice(g * rows, (g + 1) * rows)
            scores = [s[sl] if mask is None else jnp.where(mask, s[sl], NEG_INF) for s, mask in zip(s4, masks)]
            sink = sinks(n)
            m = sink
            for s in scores:
                m = jnp.maximum(m, jnp.max(s, axis=-1, keepdims=True))
            den = jnp.exp(sink - m)
            for t, s in enumerate(scores):
                p = jnp.exp(s - m)
                den = den + jnp.sum(p, axis=-1, keepdims=True)
                p4[t].append(p.astype(BF16))
            dens.append(den)
        pv4 = None
        for p_parts, v in zip(p4, vd):
            pv = _dot(jnp.concatenate(p_parts, axis=0), v)
            pv4 = pv if pv4 is None else pv4 + pv
        for g, n in enumerate(heads):
            outs[n] = pv4[g * rows:(g + 1) * rows] / dens[g]
    return jnp.concatenate([jnp.where(lower, outs[2 * m], outs[2 * m + 1]) for m in range(C_GROUP)], axis=-1)


def _attn_ctx_kernel(sink_ref, q_ref, k_ref, v_ref, gate_ref, o_ref):
    kp = pl.program_id(1)
    q = q_ref[...] * (C_HEAD ** -0.5)
    q_chunks = [q[:, m * LANES:(m + 1) * LANES] for m in range(C_GROUP)]
    o = _pair_heads(q_chunks, [k_ref[...]], [v_ref[...]], lambda n: sink_ref[kp * 2 * C_GROUP + n], [None])
    o_ref[...] = (o * _silu(gate_ref[...])).astype(BF16)


def _attn_context(proj, sink):
    qw = 2 * C_GROUP * C_HEAD
    k_cb = C_HEADS * C_HEAD // LANES
    v_cb = (C_HEADS + C_KV_HEADS) * C_HEAD // LANES
    gate_cb = (C_HEADS + 2 * C_KV_HEADS) * C_HEAD // qw
    return pl.pallas_call(
        _attn_ctx_kernel,
        out_shape=jax.ShapeDtypeStruct((PROMPT.rows, D_MODEL), BF16),
        grid=(BATCH, C_KV_HEADS // 2),
        in_specs=[pl.BlockSpec(memory_space=pltpu.SMEM),
                  pl.BlockSpec((SEQ, qw), lambda b, kp: (b, kp)),
                  pl.BlockSpec((SEQ, LANES), lambda b, kp: (b, k_cb + kp)),
                  pl.BlockSpec((SEQ, LANES), lambda b, kp: (b, v_cb + kp)),
                  pl.BlockSpec((SEQ, qw), lambda b, kp: (b, gate_cb + kp))],
        out_specs=pl.BlockSpec((SEQ, qw), lambda b, kp: (b, kp)),
        compiler_params=_cparams(("parallel", "parallel")),
        name="attn_context",
    )(sink, proj, proj, proj, proj)


def _rope(x, cos, sin_signed):
    lane = lax.broadcasted_iota(jnp.int32, x.shape, 1)
    first = (lane & 31) < 16
    partner = jnp.where(first, pltpu.roll(x, LANES - 16, axis=1), pltpu.roll(x, 16, axis=1))
    return x * cos + partner * sin_signed


def _attn_lat_kernel(sink_ref, q_ref, kp_ref, ko_ref, kn_ref, vp_ref, vo_ref, vn_ref, ck_ref, cv_ref,
                     cq_ref, sq_ref, cp_ref, sp_ref, cn_ref, sn_ref, gate_ref, o_ref):
    kpair = pl.program_id(1)
    qb = pl.program_id(2)
    cq = cq_ref[...]
    sq = sq_ref[...]
    q = q_ref[...] * (C_HEAD ** -0.5)
    qr = [_rope(q[:, n * LANES:(n + 1) * LANES], cq, sq) for n in range(4)]
    kband = jnp.concatenate([_rope(kp_ref[...], cp_ref[...], sp_ref[...]),
                             _rope(ko_ref[...], cq, sq),
                             _rope(kn_ref[...], cn_ref[...], sn_ref[...])], axis=0)
    vband = jnp.concatenate([vp_ref[...], vo_ref[...], vn_ref[...]], axis=0)
    qpos = qb * BLOCK + lax.broadcasted_iota(jnp.int32, (BLOCK, 3 * BLOCK), 0)
    kpos = (qb - 1) * BLOCK + lax.broadcasted_iota(jnp.int32, (BLOCK, 3 * BLOCK), 1)
    valid = (jnp.abs(qpos - kpos) <= WINDOW) & (kpos >= 0) & (kpos < DEC_SEQ)
    o = _pair_heads(qr, [kband, ck_ref[...]], [vband, cv_ref[...]],
                    lambda n: sink_ref[kpair * 2 * C_GROUP + n], [valid, None])
    o_ref[...] = (o * _silu(gate_ref[...])).astype(BF16)


def _attn_latent(proj, cache_k4, cache_v4, li, sink, cos_t, sin_t):
    qw = 2 * C_GROUP * C_HEAD
    k_cb = C_HEADS * C_HEAD // LANES
    v_cb = (C_HEADS + C_KV_HEADS) * C_HEAD // LANES
    gate_cb = (C_HEADS + 2 * C_KV_HEADS) * C_HEAD // qw
    nqb = DEC_SEQ // BLOCK

    def rows(delta):
        def idx(b, kp, qb):
            return b * nqb + jnp.clip(qb + delta, 0, nqb - 1)
        return idx

    def kv_spec(cb, delta):
        r = rows(delta)
        return pl.BlockSpec((BLOCK, LANES), lambda b, kp, qb: (r(b, kp, qb), cb + kp))

    def tab_spec(delta):
        return pl.BlockSpec((BLOCK, LANES), lambda b, kp, qb: (jnp.clip(qb + delta, 0, nqb - 1), 0))

    cache_spec = pl.BlockSpec((None, None, PAST_LEN, LANES), lambda b, kp, qb: (b, li, 0, kp))
    own = rows(0)
    return pl.pallas_call(
        _attn_lat_kernel,
        out_shape=jax.ShapeDtypeStruct((SAMPLE.rows, D_MODEL), BF16),
        grid=(DEC_BATCH, C_KV_HEADS // 2, nqb),
        in_specs=[pl.BlockSpec(memory_space=pltpu.SMEM),
                  pl.BlockSpec((BLOCK, qw), lambda b, kp, qb: (own(b, kp, qb), kp)),
                  kv_spec(k_cb, -1), kv_spec(k_cb, 0), kv_spec(k_cb, 1),
                  kv_spec(v_cb, -1), kv_spec(v_cb, 0), kv_spec(v_cb, 1),
                  cache_spec, cache_spec,
                  tab_spec(0), tab_spec(0), tab_spec(-1), tab_spec(-1), tab_spec(1), tab_spec(1),
                  pl.BlockSpec((BLOCK, qw), lambda b, kp, qb: (own(b, kp, qb), gate_cb + kp))],
        out_specs=pl.BlockSpec((BLOCK, qw), lambda b, kp, qb: (own(b, kp, qb), kp)),
        compiler_params=_cparams(("parallel", "parallel", "parallel")),
        name="attn_latent",
    )(sink, proj, proj, proj, proj, proj, proj, proj, cache_k4, cache_v4,
      cos_t, sin_t, cos_t, sin_t, cos_t, sin_t, proj)


def _rope_tables():
    t = jnp.arange(DEC_SEQ, dtype=jnp.int32)
    row = (t // GRID_W).astype(F32)
    col = (t % GRID_W).astype(F32)
    nf = C_HEAD // 4
    inv = 1.0 / (ROPE_BASE ** (jnp.arange(nf, dtype=F32) / nf))
    lane = np.arange(LANES)
    f_of_lane = lane % nf
    use_col = (lane % C_HEAD) >= C_HEAD // 2
    sign = np.where((lane % 32) < 16, -1.0, 1.0).astype(np.float32)
    pos = jnp.where(jnp.asarray(use_col)[None, :], col[:, None], row[:, None])
    ang = pos * inv[jnp.asarray(f_of_lane)][None, :]
    return jnp.cos(ang), jnp.sin(ang) * jnp.asarray(sign)[None, :]


def _chain_const_tiles(vec):
    t = vec.reshape(A_HEADS // 2, 2, A_HEAD).transpose(1, 2, 0)
    t = jnp.broadcast_to(t[:, :, None, :, None], (2, A_HEAD, 2, A_HEADS // 2, SEQ_PER_GROUP))
    return t.reshape(2, A_HEAD, CHAINS)


def _even_mixer(st, proj, s0, p, gmat, cs_mat, dft):
    r, k, v, w2, a2, bonus = _wkv_prep(st, proj, p["mu"], p["w0"], p["wup"], p["a0"], p["aup"], p["k_a"], p["r_k"],
                                       gmat)
    o_f, o_b, s_fin = _wkv(r, k, v, w2, a2, p["kkw_t"], p["ka_t"], s0)
    ya = _ya(st, o_f, o_b, bonus, proj, p["gn_w"], p["gn_b"], gmat)
    zc, zs = _dft_channels(st, proj, cs_mat)
    yb = _dft_time(st, zc, zs, proj, dft[0], dft[1])
    return ya, yb, s_fin


def kernel(x_prompt, x_sample, c, state_wkv, cache_k, cache_v, c_ctx, mod_w, mod_b, norm_pre, norm_post,
           even_w_in, even_mu, even_w0, even_w_up, even_a0, even_a_up, even_k_k, even_k_a, even_r_k,
           even_gn_w, even_gn_b, even_w_out, odd_w_in, odd_sink, odd_w_out):
    n_odd = odd_w_in.shape[0]
    xs = {PROMPT: x_prompt.reshape(PROMPT.rows, D_MODEL), SAMPLE: x_sample.reshape(SAMPLE.rows, D_MODEL)}
    conds = jnp.concatenate([c_ctx[None, :], c, jnp.zeros((N_COND - 1 - DEC_BATCH, D_MODEL), F32)], axis=0)
    mods4 = _modulation(conds, mod_w, mod_b).reshape(DEPTH, N_COND, 1, 3 * D_MODEL)
    norm_pre3 = norm_pre.reshape(DEPTH, 1, D_MODEL)
    norm_post3 = norm_post.reshape(DEPTH, 1, D_MODEL)

    even_w_in_b = even_w_in.astype(BF16)
    even_w_out_b = even_w_out.astype(BF16)
    odd_w_in_b = odd_w_in.astype(BF16)
    odd_w_out_b = odd_w_out.astype(BF16)

    head_of_lane = np.arange(CW) // A_HEAD
    gmat = jnp.asarray((head_of_lane[:, None] == head_of_lane[None, :]).astype(np.float32)).astype(BF16)
    c_ch, s_ch = _dft_mats(B_GROUP_CH)
    cs_mat = jnp.concatenate([c_ch, s_ch], axis=1).astype(BF16)
    dft = {PROMPT: tuple(m.astype(BF16) for m in _dft_mats(SEQ)),
           SAMPLE: tuple(m.astype(BF16) for m in _dft_mats(DEC_SEQ))}
    cos_t, sin_t = _rope_tables()
    cache_k4 = cache_k.reshape(DEC_BATCH, n_odd, PAST_LEN, C_KV_HEADS * C_HEAD)
    cache_v4 = cache_v.reshape(DEC_BATCH, n_odd, PAST_LEN, C_KV_HEADS * C_HEAD)

    def pad_rows(w, slot):
        return jnp.pad(w, ((0, 0), (slot * LORA, (3 - slot) * LORA), (0, 0)))

    wup_pad = jnp.stack([pad_rows(even_w_up[:, 0], 0), pad_rows(even_w_up[:, 1], 1)], axis=1)
    aup_pad = jnp.stack([pad_rows(even_a_up[:, 0], 2), pad_rows(even_a_up[:, 1], 3)], axis=1)

    new_wkv, new_k, new_v = [], [], []
    hs = {st: _pre_norm(st, xs[st], norm_pre3, mods4, 0) for st in (PROMPT, SAMPLE)}
    for layer in range(DEPTH):
        i = layer // 2
        if layer % 2 == 0:
            p = {
                "mu": even_mu[i][None, :], "w0": even_w0[i], "a0": even_a0[i],
                "wup": wup_pad[i], "aup": aup_pad[i],
                "k_a": even_k_a[i][None, :], "r_k": even_r_k[i].reshape(1, A_WIDTH),
                "kkw_t": _chain_const_tiles(even_k_k[i]), "ka_t": _chain_const_tiles(even_k_a[i]),
                "gn_w": even_gn_w[i][None, :], "gn_b": even_gn_b[i][None, :],
            }
            s0s = state_wkv[:, i].reshape(DEC_BATCH, 2, A_HEADS // 2, 2, A_HEAD, A_HEAD)
            s0 = {PROMPT: jnp.zeros((BATCH // SEQ_PER_GROUP, 2, A_HEAD, A_HEAD, CHAINS), F32),
                  SAMPLE: s0s.transpose(3, 5, 4, 1, 2, 0).reshape(1, 2, A_HEAD, A_HEAD, CHAINS)}
            for st in (PROMPT, SAMPLE):
                proj = _in_proj(st, hs[st], even_w_in_b, i)
                ya, yb, s_fin = _even_mixer(st, proj, s0[st], p, gmat, cs_mat, dft[st])
                xs[st], hs[st] = _out_proj(st, ya, 0, yb, 0, even_w_out_b, i, xs[st], mods4, norm_post3, norm_pre3,
                                           layer)
                if st is PROMPT:
                    s_fin = s_fin.reshape(BATCH // SEQ_PER_GROUP, 2, A_HEAD, A_HEAD, 2, A_HEADS // 2, SEQ_PER_GROUP)
                    new_wkv.append(s_fin.transpose(0, 6, 4, 5, 1, 3, 2).reshape(BATCH, 2, A_HEADS, A_HEAD, A_HEAD))
        else:
            proj_p = _in_proj(PROMPT, hs[PROMPT], odd_w_in_b, i)
            proj_s = _in_proj(SAMPLE, hs[SAMPLE], odd_w_in_b, i)
            kv0 = C_HEADS * C_HEAD
            kvn = C_KV_HEADS * C_HEAD
            new_k.append(proj_p[:, kv0:kv0 + kvn].reshape(BATCH, SEQ, C_KV_HEADS, C_HEAD))
            new_v.append(proj_p[:, kv0 + kvn:kv0 + 2 * kvn].reshape(BATCH, SEQ, C_KV_HEADS, C_HEAD))
            att_p = _attn_context(proj_p, odd_sink[i])
            att_s = _attn_latent(proj_s, cache_k4, cache_v4, i, odd_sink[i], cos_t, sin_t)
            for st, att in ((PROMPT, att_p), (SAMPLE, att_s)):
                xs[st], hs[st] = _out_proj(st, att, 0, att, 1, odd_w_out_b, i, xs[st], mods4, norm_post3, norm_pre3,
                                           layer)

    y_prompt = xs[PROMPT].reshape(BATCH, SEQ, D_MODEL)
    y_sample = xs[SAMPLE].reshape(DEC_BATCH, DEC_SEQ, D_MODEL)
    return (y_prompt, y_sample, jnp.stack(new_wkv, axis=1), jnp.stack(new_k, axis=1), jnp.stack(new_v, axis=1))
```

```python
import collections
import functools

import numpy as np
import jax
import jax.numpy as jnp
from jax import lax
from jax.experimental import pallas as pl
from jax.experimental.pallas import tpu as pltpu

F32 = jnp.float32
BF16 = jnp.bfloat16

D_MODEL = 2048
BATCH = 32
SEQ = 256
DEPTH = 4
DEC_BATCH = 8
DEC_SEQ = 2048
PAST_LEN = 256
GRID_W = 64
RMS_EPS = 1e-6
A_WIDTH = 1024
A_HEAD = 64
A_HEADS = 16
LORA = 64
GN_EPS = 64e-5
DECAY_SCALE = float(np.exp(-0.5))
B_WIDTH = 1024
B_GROUPS = 4
B_GROUP_CH = 256
SHIFT_COLS = 3 * A_WIDTH + 4 * LORA
EVEN_IN = SHIFT_COLS + A_WIDTH + 2 * B_WIDTH
C_HEAD = 64
C_HEADS = 32
C_KV_HEADS = 8
C_GROUP = 4
WINDOW = 128
BLOCK = 128
ROPE_BASE = 10000.0
ODD_IN = (C_HEADS + 2 * C_KV_HEADS) * C_HEAD + D_MODEL
NEG_INF = -1e30

N_COND = 16
SUBLANES = 8
LANES = 128
CHAINS = LANES
SEQ_PER_GROUP = CHAINS // A_HEADS
CW = 256
TM_IN = 512
TN_IN = 1280
TM_ROW = 256
TM_OUT = 512
OUT_SUBTILES = 2
WKV_TT = 32
WKV_VMEM_LIMIT = 56 * 1024 * 1024
WKV_ROWS = 16
WKV_ACCS = 2
VMEM_LIMIT = 48 * 1024 * 1024

Stream = collections.namedtuple("Stream", "rows t_len n_seq cond0 cond_per_seq")
PROMPT = Stream(BATCH * SEQ, SEQ, BATCH, 0, 0)
SAMPLE = Stream(DEC_BATCH * DEC_SEQ, DEC_SEQ, DEC_BATCH, 1, 1)


def _cparams(sem):
    return pltpu.CompilerParams(dimension_semantics=sem, vmem_limit_bytes=VMEM_LIMIT)


def _cond_of_row(st, row0):
    return st.cond0 + st.cond_per_seq * (row0 // st.t_len)


def _sigmoid(x):
    return 1.0 / (1.0 + jnp.exp(-x))


def _silu(x):
    return x * _sigmoid(x)


def _split(a):
    hi = a.astype(BF16)
    lo = (a - hi.astype(F32)).astype(BF16)
    return hi, lo


def _dot(a, b):
    return jnp.dot(a, b, preferred_element_type=F32)


def _dot3_split(a_hl, b_hl):
    (ah, al), (bh, bl) = a_hl, b_hl
    return _dot(ah, bh) + (_dot(ah, bl) + _dot(al, bh))


def _dot3(a, b):
    return _dot3_split(_split(a), _split(b))


def _gsum(x, gmat):
    xh, xl = _split(x)
    return _dot(xh, gmat) + _dot(xl, gmat)


def _mod_kernel(c_ref, w_ref, b_ref, o_ref):
    o_ref[...] = _dot3(_silu(c_ref[...]), w_ref[...]) + b_ref[...]


def _modulation(conds, mod_w, mod_b):
    tn = 512
    n = 3 * D_MODEL
    return pl.pallas_call(
        _mod_kernel,
        out_shape=jax.ShapeDtypeStruct((DEPTH, N_COND, n), F32),
        grid=(DEPTH, n // tn),
        in_specs=[
            pl.BlockSpec((N_COND, D_MODEL), lambda l, j: (0, 0)),
            pl.BlockSpec((None, D_MODEL, tn), lambda l, j: (l, 0, j)),
            pl.BlockSpec((None, 1, tn), lambda l, j: (l, 0, j)),
        ],
        out_specs=pl.BlockSpec((None, N_COND, tn), lambda l, j: (l, 0, j)),
        compiler_params=_cparams(("parallel", "parallel")),
        name="modulation",
    )(conds, mod_w, mod_b.reshape(DEPTH, 1, n))


def _norm_modulate(x, g, sh, sc):
    ms = jnp.mean(x * x, axis=-1, keepdims=True)
    y = x * lax.rsqrt(ms + RMS_EPS) * g
    return (y * (1.0 + sc) + sh).astype(BF16)


def _pre_kernel(x_ref, g_ref, sh_ref, sc_ref, h_ref):
    h_ref[...] = _norm_modulate(x_ref[...], g_ref[...], sh_ref[...], sc_ref[...])


def _pre_norm(st, x, norm_g, mods4, layer):
    tm = TM_ROW
    cond = lambda i: _cond_of_row(st, i * tm)
    return pl.pallas_call(
        _pre_kernel,
        out_shape=jax.ShapeDtypeStruct((st.rows, D_MODEL), BF16),
        grid=(st.rows // tm,),
        in_specs=[pl.BlockSpec((tm, D_MODEL), lambda i: (i, 0)),
                  pl.BlockSpec((None, 1, D_MODEL), lambda i: (layer, 0, 0)),
                  pl.BlockSpec((None, None, 1, D_MODEL), lambda i: (layer, cond(i), 0, 0)),
                  pl.BlockSpec((None, None, 1, D_MODEL), lambda i: (layer, cond(i), 0, 1))],
        out_specs=pl.BlockSpec((tm, D_MODEL), lambda i: (i, 0)),
        compiler_params=_cparams(("parallel",)),
        name="pre_norm",
    )(x, norm_g, mods4, mods4)


def _in_kernel(h_ref, w_ref, o_ref):
    o_ref[...] = _dot(h_ref[...], w_ref[...])


def _in_proj(st, h, w_bf16, li):
    n = w_bf16.shape[-1]
    return pl.pallas_call(
        _in_kernel,
        out_shape=jax.ShapeDtypeStruct((st.rows, n), F32),
        grid=(n // TN_IN, st.rows // TM_IN),
        in_specs=[pl.BlockSpec((TM_IN, D_MODEL), lambda j, i: (i, 0)),
                  pl.BlockSpec((None, D_MODEL, TN_IN), lambda j, i: (li, 0, j))],
        out_specs=pl.BlockSpec((TM_IN, TN_IN), lambda j, i: (i, j)),
        compiler_params=_cparams(("parallel", "parallel")),
        name="in_proj",
    )(h, w_bf16)


def _token_shift(cur, prev8, next8, mu, first, last):
    tm = cur.shape[0]
    rows = lax.broadcasted_iota(jnp.int32, cur.shape, 0)
    prow = jnp.where(first, 0.0, prev8[7:8, :])
    nrow = jnp.where(last, 0.0, next8[0:1, :])
    up = jnp.where(rows == 0, prow, pltpu.roll(cur, 1, axis=0))
    dn = jnp.where(rows == tm - 1, nrow, pltpu.roll(cur, tm - 1, axis=0))
    return cur + mu * (0.5 * (up + dn) - cur)


def _prep_kernel(r_ref, k_ref, v_ref, lo_ref, rp_ref, kp_ref, vp_ref, lp_ref, rn_ref, kn_ref, vn_ref, ln_ref,
                 mur_ref, muk_ref, muv_ref, mul_ref, w0_ref, wuph_ref, wupl_ref, a0_ref, auph_ref, aupl_ref,
                 ka_ref, rk_ref, gm_ref,
                 r_o, k_o, v_o, w_o, a_o, bon_o, *, tiles_per_seq):
    i = pl.program_id(0)
    first = lax.rem(i, tiles_per_seq) == 0
    last = lax.rem(i, tiles_per_seq) == tiles_per_seq - 1

    low = _token_shift(lo_ref[...], lp_ref[...], ln_ref[...], mul_ref[...], first, last)
    low_hl = _split(low)
    low_t_hl = _split(jnp.tanh(low))
    gm = gm_ref[...]

    for c in range(A_WIDTH // CW):
        cols = slice(c * CW, (c + 1) * CW)
        r = _token_shift(r_ref[:, cols], rp_ref[:, cols], rn_ref[:, cols], mur_ref[:, cols], first, last)
        k = _token_shift(k_ref[:, cols], kp_ref[:, cols], kn_ref[:, cols], muk_ref[:, cols], first, last)
        v = _token_shift(v_ref[:, cols], vp_ref[:, cols], vn_ref[:, cols], muv_ref[:, cols], first, last)
        ka = ka_ref[:, cols]
        kd_sum = None
        for d in range(2):
            w_raw = w0_ref[d:d + 1, cols] + _dot3_split(low_t_hl, (wuph_ref[d, :, cols], wupl_ref[d, :, cols]))
            a = _sigmoid(a0_ref[d:d + 1, cols] + _dot3_split(low_hl, (auph_ref[d, :, cols], aupl_ref[d, :, cols])))
            kd = k * (1.0 + (a - 1.0) * ka)
            w_o[d, :, cols] = jnp.exp(-DECAY_SCALE * _sigmoid(w_raw))
            a_o[d, :, cols] = a
            kd_sum = kd if kd_sum is None else kd_sum + kd
        r_o[:, cols] = r
        k_o[:, cols] = k
        v_o[:, cols] = v
        bon_o[:, cols] = _gsum(r * (0.5 * kd_sum) * rk_ref[:, cols], gm) * v


def _wkv_prep(st, proj, mu, w0, wup_pad, a0, aup_pad, k_a, r_k, gmat):
    tm = TM_ROW
    nb8 = st.rows // SUBLANES
    ncb = A_WIDTH // CW
    low_cb = 3 * A_WIDTH // CW
    assert 4 * LORA == CW and st.t_len % tm == 0

    prev_row = lambda i: jnp.maximum(i * (tm // SUBLANES) - 1, 0)
    next_row = lambda i: jnp.minimum((i + 1) * (tm // SUBLANES), nb8 - 1)

    def main(cb):
        return pl.BlockSpec((tm, A_WIDTH), lambda i: (i, cb))

    def prev(cb):
        return pl.BlockSpec((SUBLANES, A_WIDTH), lambda i: (prev_row(i), cb))

    def nxt(cb):
        return pl.BlockSpec((SUBLANES, A_WIDTH), lambda i: (next_row(i), cb))

    def fixed(spec_fn):
        return [spec_fn(0), spec_fn(1), spec_fn(2)]

    low_main = pl.BlockSpec((tm, CW), lambda i: (i, low_cb))
    low_prev = pl.BlockSpec((SUBLANES, CW), lambda i: (prev_row(i), low_cb))
    low_next = pl.BlockSpec((SUBLANES, CW), lambda i: (next_row(i), low_cb))

    def vec(cb):
        return pl.BlockSpec((1, A_WIDTH), lambda i: (0, cb))

    in_specs = (fixed(main) + [low_main] + fixed(prev) + [low_prev] + fixed(nxt) + [low_next]
                + [vec(0), vec(1), vec(2), pl.BlockSpec((1, CW), lambda i: (0, low_cb))]
                + [pl.BlockSpec((2, A_WIDTH), lambda i: (0, 0)),
                   pl.BlockSpec((2, CW, A_WIDTH), lambda i: (0, 0, 0)),
                   pl.BlockSpec((2, CW, A_WIDTH), lambda i: (0, 0, 0)),
                   pl.BlockSpec((2, A_WIDTH), lambda i: (0, 0)),
                   pl.BlockSpec((2, CW, A_WIDTH), lambda i: (0, 0, 0)),
                   pl.BlockSpec((2, CW, A_WIDTH), lambda i: (0, 0, 0)),
                   vec(0), vec(0),
                   pl.BlockSpec((CW, CW), lambda i: (0, 0))])
    tps = st.t_len // tm
    g_n = st.n_seq // SEQ_PER_GROUP

    def til(i):
        s = i // tps
        return (s // SEQ_PER_GROUP, lax.rem(i, tps), lax.rem(s, SEQ_PER_GROUP))

    one = pl.BlockSpec((tm, A_WIDTH), lambda i: (i, 0))
    til1 = pl.BlockSpec((None, tm, A_WIDTH), lambda i: til(i))
    til2 = pl.BlockSpec((2, None, tm, A_WIDTH), lambda i: (0,) + til(i))
    one_sds = jax.ShapeDtypeStruct((st.rows, A_WIDTH), F32)
    til1_sds = jax.ShapeDtypeStruct((g_n, st.t_len, SEQ_PER_GROUP * A_WIDTH), F32)
    til2_sds = jax.ShapeDtypeStruct((2, g_n, st.t_len, SEQ_PER_GROUP * A_WIDTH), F32)
    return pl.pallas_call(
        functools.partial(_prep_kernel, tiles_per_seq=tps),
        out_shape=[til1_sds, til1_sds, til1_sds, til2_sds, til2_sds, one_sds],
        grid=(st.rows // tm,),
        in_specs=in_specs,
        out_specs=[til1, til1, til1, til2, til2, one],
        compiler_params=_cparams(("parallel",)),
        name="wkv_prep",
    )(*([proj] * 12), mu, mu, mu, mu, w0, *_split(wup_pad), a0, *_split(aup_pad), k_a, r_k, gmat)


def _wkv_kernel(rf_ref, rb_ref, kf_ref, kb_ref, vf_ref, vb_ref, wf_ref, wb_ref, af_ref, ab_ref, kkw_ref, ka_ref,
                s0_ref, of_ref, ob_ref, sf_ref,
                s2_ref, r2_ref, kk2_ref, v2_ref, kka2_ref, kd2_ref, g_ref, o_ref, *, tt_steps):
    c = pl.program_id(1)
    nblk = WKV_ROWS // SUBLANES
    n_hv = A_HEADS // 2
    pairs = ((rf_ref, rb_ref), (kf_ref, kb_ref), (vf_ref, vb_ref), (wf_ref, wb_ref), (af_ref, ab_ref))

    @pl.when(c == 0)
    def _():
        s2_ref[...] = s0_ref[...]

    def to_chain_tiles(u, gammas):
        ub = tt_steps - 1 - u
        halves = []
        for xf_ref, xb_ref in pairs:
            x2 = jnp.concatenate([xf_ref[u, :, hv * LANES:(hv + 1) * LANES] for hv in range(n_hv)]
                                 + [xb_ref[ub, :, hv * LANES:(hv + 1) * LANES] for hv in range(n_hv)], axis=0)
            y = x2.T
            halves.append((y[:A_HEAD], y[A_HEAD:]))
        r_h, k_h, v_h, w_h, a_h = halves
        new_gammas = []
        for hp in range(2):
            kk = k_h[hp] * kkw_ref[hp]
            norm = jnp.sqrt(jnp.sum(kk * kk, axis=0, keepdims=True))
            kk = kk / jnp.maximum(norm, 1e-12)
            kka = kk * a_h[hp]
            kd = k_h[hp] * (1.0 + (a_h[hp] - 1.0) * ka_ref[hp])
            gamma = gammas[hp] * w_h[hp]
            inv = 1.0 / gamma
            kk2_ref[hp, u] = gammas[hp] * kk
            kka2_ref[hp, u] = kka * inv
            kd2_ref[hp, u] = kd * inv
            r2_ref[hp, u] = r_h[hp] * gamma
            v2_ref[hp, u] = v_h[hp]
            new_gammas.append(gamma)
        return tuple(new_gammas)

    one = jnp.ones((A_HEAD, CHAINS), F32)
    gammas = lax.fori_loop(0, tt_steps, to_chain_tiles, (one, one), unroll=4)
    for hp in range(2):
        g_ref[hp] = gammas[hp]

    def bcast_row(ref, t, j):
        return ref[t, pl.ds(j, SUBLANES, stride=0), :]

    def time_step(hp, t):
        s_ref = s2_ref.at[hp]
        r_ref, kk_ref, v_ref, kka_ref, kd_ref = (ref.at[hp] for ref in (r2_ref, kk2_ref, v2_ref, kka2_ref, kd2_ref))
        for ib in range(A_HEAD // WKV_ROWS):
            i0 = ib * WKV_ROWS
            rows = [pl.ds(i0 + SUBLANES * b, SUBLANES) for b in range(nblk)]
            sa = [[None] * WKV_ACCS for _ in range(nblk)]
            for j in range(A_HEAD):
                kkj = bcast_row(kk_ref, t, j)
                for b in range(nblk):
                    p = s_ref[j, rows[b], :] * kkj
                    sa[b][j % WKV_ACCS] = p if sa[b][j % WKV_ACCS] is None else sa[b][j % WKV_ACCS] + p
            sa = [functools.reduce(lambda x, y: x + y, parts) for parts in sa]
            v8 = [v_ref[t, rows[b], :] for b in range(nblk)]
            out = [[None] * WKV_ACCS for _ in range(nblk)]
            for j in range(A_HEAD):
                kkaj = bcast_row(kka_ref, t, j)
                kdj = bcast_row(kd_ref, t, j)
                rj = bcast_row(r_ref, t, j)
                for b in range(nblk):
                    sn = s_ref[j, rows[b], :] - sa[b] * kkaj + v8[b] * kdj
                    s_ref[j, rows[b], :] = sn
                    q = sn * rj
                    out[b][j % WKV_ACCS] = q if out[b][j % WKV_ACCS] is None else out[b][j % WKV_ACCS] + q
            for b in range(nblk):
                o_ref[hp, rows[b], :] = functools.reduce(lambda x, y: x + y, out[b])

    def step(u, carry):
        for hp in range(2):
            time_step(hp, u)
        x2 = jnp.concatenate([o_ref[0], o_ref[1]], axis=0).T
        ub = tt_steps - 1 - u
        for hv in range(n_hv):
            of_ref[u, :, hv * LANES:(hv + 1) * LANES] = x2[hv * SUBLANES:(hv + 1) * SUBLANES, :]
            ob_ref[ub, :, hv * LANES:(hv + 1) * LANES] = x2[(n_hv + hv) * SUBLANES:(n_hv + hv + 1) * SUBLANES, :]
        return carry

    lax.fori_loop(0, tt_steps, step, 0)

    def rescale(j, carry):
        for hp in range(2):
            gj = g_ref[hp, pl.ds(j, SUBLANES, stride=0), :]
            for b in range(A_HEAD // SUBLANES):
                rows = pl.ds(b * SUBLANES, SUBLANES)
                s2_ref[hp, j, rows, :] = s2_ref[hp, j, rows, :] * gj
        return carry

    lax.fori_loop(0, A_HEAD, rescale, 0)

    @pl.when(c == pl.num_programs(1) - 1)
    def _():
        sf_ref[...] = s2_ref[...]


def _wkv(r, k, v, w, a, kkw_t, ka_t, s0):
    g_n, t_n = r.shape[0], r.shape[1]
    tt = WKV_TT
    nc = t_n // tt
    split = lambda a: a.reshape(a.shape[:-1] + (SEQ_PER_GROUP, A_WIDTH))
    blk = (tt, SEQ_PER_GROUP, A_WIDTH)

    fwd = pl.BlockSpec((None,) + blk, lambda g, c: (g, c, 0, 0))
    bwd = pl.BlockSpec((None,) + blk, lambda g, c: (g, nc - 1 - c, 0, 0))
    fwd_d = pl.BlockSpec((None, None) + blk, lambda g, c: (0, g, c, 0, 0))
    bwd_d = pl.BlockSpec((None, None) + blk, lambda g, c: (1, g, nc - 1 - c, 0, 0))
    state = pl.BlockSpec((None, 2, A_HEAD, A_HEAD, CHAINS), lambda g, c: (g, 0, 0, 0, 0),
                         pipeline_mode=pl.Buffered(1))
    tiles = pltpu.VMEM((2, tt, A_HEAD, CHAINS), F32)
    o_sds = jax.ShapeDtypeStruct((g_n, t_n, SEQ_PER_GROUP, A_WIDTH), F32)
    r, k, v, w, a = (split(x) for x in (r, k, v, w, a))
    const = pl.BlockSpec((2, A_HEAD, CHAINS), lambda g, c: (0, 0, 0))
    o_f, o_b, s_fin = pl.pallas_call(
        functools.partial(_wkv_kernel, tt_steps=tt),
        out_shape=[o_sds, o_sds, jax.ShapeDtypeStruct((g_n, 2, A_HEAD, A_HEAD, CHAINS), F32)],
        grid=(g_n, nc),
        in_specs=[fwd, bwd, fwd, bwd, fwd, bwd, fwd_d, bwd_d, fwd_d, bwd_d, const, const, state],
        out_specs=[fwd, bwd, state],
        scratch_shapes=([pltpu.VMEM((2, A_HEAD, A_HEAD, CHAINS), F32)] + [tiles] * 5
                        + [pltpu.VMEM((2, A_HEAD, CHAINS), F32)] * 2),
        compiler_params=pltpu.CompilerParams(dimension_semantics=("parallel", "arbitrary"),
                                             vmem_limit_bytes=WKV_VMEM_LIMIT),
        name="wkv",
    )(r, r, k, k, v, v, w, w, a, a, kkw_t, ka_t, s0)
    merge = lambda a: a.reshape(g_n, t_n, SEQ_PER_GROUP * A_WIDTH)
    return merge(o_f), merge(o_b), s_fin


def _ya_kernel(of_ref, ob_ref, bon_ref, g0_ref, g1_ref, g2_ref, g3_ref, gw_ref, gb_ref, gm_ref, o_ref):
    gm = gm_ref[...]
    for c, gate_ref in enumerate((g0_ref, g1_ref, g2_ref, g3_ref)):
        cols = slice(c * CW, (c + 1) * CW)
        o = of_ref[:, cols] + ob_ref[:, cols]
        mean = _gsum(o, gm) * (1.0 / A_HEAD)
        dev = o - mean
        var = _gsum(dev * dev, gm) * (1.0 / A_HEAD)
        y = dev * lax.rsqrt(var + GN_EPS) * gw_ref[:, cols] + gb_ref[:, cols]
        o_ref[:, cols] = ((y + bon_ref[:, cols]) * _silu(gate_ref[...])).astype(BF16)


def _ya(st, o_f, o_b, bonus, proj, gn_w, gn_b, gmat):
    tm = TM_ROW
    gate_cb = (SHIFT_COLS) // CW
    blk = pl.BlockSpec((tm, A_WIDTH), lambda i: (i, 0))
    vec = pl.BlockSpec((1, A_WIDTH), lambda i: (0, 0))
    tps = st.t_len // tm

    def til_idx(i):
        s = i // tps
        return (s // SEQ_PER_GROUP, lax.rem(i, tps), lax.rem(s, SEQ_PER_GROUP))

    til = pl.BlockSpec((None, tm, A_WIDTH), til_idx)
    gates = [pl.BlockSpec((tm, CW), lambda i, c=c: (i, gate_cb + c)) for c in range(A_WIDTH // CW)]
    return pl.pallas_call(
        _ya_kernel,
        out_shape=jax.ShapeDtypeStruct((st.rows, A_WIDTH), BF16),
        grid=(st.rows // tm,),
        in_specs=[til, til, blk] + gates + [vec, vec, pl.BlockSpec((CW, CW), lambda i: (0, 0))],
        out_specs=blk,
        compiler_params=_cparams(("parallel",)),
        name="wkv_post",
    )(o_f, o_b, bonus, proj, proj, proj, proj, gn_w, gn_b, gmat)


def _dft_ch_kernel(u_ref, cs_ref, zc_ref, zs_ref):
    z = _dot(u_ref[...].astype(BF16), cs_ref[...])
    zc_ref[...] = z[:, :B_GROUP_CH].astype(BF16)
    zs_ref[...] = z[:, B_GROUP_CH:].astype(BF16)


def _dft_channels(st, proj, cs_mat):
    tm = 2048
    u_cb = (SHIFT_COLS + A_WIDTH) // B_GROUP_CH
    blk = pl.BlockSpec((tm, B_GROUP_CH), lambda i, g: (i, g))
    sds = jax.ShapeDtypeStruct((st.rows, B_WIDTH), BF16)
    return pl.pallas_call(
        _dft_ch_kernel,
        out_shape=[sds, sds],
        grid=(st.rows // tm, B_GROUPS),
        in_specs=[pl.BlockSpec((tm, B_GROUP_CH), lambda i, g: (i, u_cb + g)),
                  pl.BlockSpec((B_GROUP_CH, 2 * B_GROUP_CH), lambda i, g: (0, 0))],
        out_specs=[blk, blk],
        compiler_params=_cparams(("parallel", "parallel")),
        name="dft_channels",
    )(proj, cs_mat)


def _dft_time_kernel(c_ref, s_ref, zc_ref, zs_ref, g0_ref, g1_ref, g2_ref, g3_ref, o_ref, *, scale):
    tm = o_ref.shape[0]
    rows = pl.ds(pl.multiple_of(pl.program_id(1) * tm, tm), tm)
    acc = _dot(c_ref[rows, :], zc_ref[...]) - _dot(s_ref[rows, :], zs_ref[...])
    for g, gate_ref in enumerate((g0_ref, g1_ref, g2_ref, g3_ref)):
        cols = slice(g * B_GROUP_CH, (g + 1) * B_GROUP_CH)
        o_ref[:, cols] = (acc[:, cols] * scale * _silu(gate_ref[...])).astype(BF16)


def _dft_time(st, zc, zs, proj, cmat, smat):
    t_len = st.t_len
    tm = min(t_len, 512)
    mt = t_len // tm
    whole = pl.BlockSpec((t_len, t_len), lambda b, m: (0, 0), pipeline_mode=pl.Buffered(1))
    gate_cb = (SHIFT_COLS + A_WIDTH + B_WIDTH) // B_GROUP_CH
    gate_specs = [pl.BlockSpec((tm, B_GROUP_CH), lambda b, m, g=g: (b * mt + m, gate_cb + g)) for g in range(B_GROUPS)]
    scale = 1.0 / float(np.sqrt(t_len * B_GROUP_CH))
    return pl.pallas_call(
        functools.partial(_dft_time_kernel, scale=scale),
        out_shape=jax.ShapeDtypeStruct((st.rows, B_WIDTH), BF16),
        grid=(st.n_seq, mt),
        in_specs=[whole, whole,
                  pl.BlockSpec((t_len, B_WIDTH), lambda b, m: (b, 0)),
                  pl.BlockSpec((t_len, B_WIDTH), lambda b, m: (b, 0))] + gate_specs,
        out_specs=pl.BlockSpec((tm, B_WIDTH), lambda b, m: (b * mt + m, 0)),
        compiler_params=_cparams(("parallel", "parallel")),
        name="dft_time",
    )(cmat, smat, zc, zs, proj, proj, proj, proj)


def _dft_mats(n):
    idx = jnp.arange(n, dtype=jnp.int32)
    prod = (idx[:, None] * idx[None, :]) % n
    ang = prod.astype(F32) * (2.0 * np.pi / n)
    return jnp.cos(ang), jnp.sin(ang)


def _out_kernel(a1_ref, a2_ref, w1_ref, w2_ref, x_ref, gate_ref, g_ref, *rest, emit_next):
    if emit_next:
        ng_ref, nsh_ref, nsc_ref, o_ref, h_ref = rest
    else:
        o_ref, = rest
    sub = x_ref.shape[0] // OUT_SUBTILES
    for s in range(OUT_SUBTILES):
        rows = slice(s * sub, (s + 1) * sub)
        y = _dot(a1_ref[rows, :], w1_ref[...]) + _dot(a2_ref[rows, :], w2_ref[...])
        ms = jnp.mean(y * y, axis=-1, keepdims=True)
        yn = y * lax.rsqrt(ms + RMS_EPS) * g_ref[...]
        x_new = x_ref[rows, :] + gate_ref[...] * yn
        if emit_next:
            h_ref[rows, :] = _norm_modulate(x_new, ng_ref[...], nsh_ref[...], nsc_ref[...])
        o_ref[rows, :] = x_new


def _out_proj(st, a1, a1_cb, a2, a2_cb, w_bf16, li, x, mods4, norm_post, norm_pre, layer):
    tm = TM_OUT
    half = D_MODEL // 2
    cond = lambda i: _cond_of_row(st, i * tm)
    emit_next = layer + 1 < DEPTH
    row = pl.BlockSpec((tm, D_MODEL), lambda i: (i, 0))
    in_specs = [pl.BlockSpec((tm, half), lambda i: (i, a1_cb)),
                pl.BlockSpec((tm, half), lambda i: (i, a2_cb)),
                pl.BlockSpec((None, half, D_MODEL), lambda i: (li, 0, 0)),
                pl.BlockSpec((None, half, D_MODEL), lambda i: (li, 1, 0)),
                row,
                pl.BlockSpec((None, None, 1, D_MODEL), lambda i: (layer, cond(i), 0, 2)),
                pl.BlockSpec((None, 1, D_MODEL), lambda i: (layer, 0, 0))]
    args = [a1, a2, w_bf16, w_bf16, x, mods4, norm_post]
    out_shape = [jax.ShapeDtypeStruct((st.rows, D_MODEL), F32)]
    out_specs = [row]
    if emit_next:
        in_specs += [pl.BlockSpec((None, 1, D_MODEL), lambda i: (layer + 1, 0, 0)),
                     pl.BlockSpec((None, None, 1, D_MODEL), lambda i: (layer + 1, cond(i), 0, 0)),
                     pl.BlockSpec((None, None, 1, D_MODEL), lambda i: (layer + 1, cond(i), 0, 1))]
        args += [norm_pre, mods4, mods4]
        out_shape.append(jax.ShapeDtypeStruct((st.rows, D_MODEL), BF16))
        out_specs.append(row)
    outs = pl.pallas_call(
        functools.partial(_out_kernel, emit_next=emit_next),
        out_shape=out_shape,
        grid=(st.rows // tm,),
        in_specs=in_specs,
        out_specs=out_specs,
        compiler_params=pltpu.CompilerParams(dimension_semantics=("parallel",), vmem_limit_bytes=WKV_VMEM_LIMIT),
        name="out_proj",
    )(*args)
    return (outs[0], outs[1]) if emit_next else (outs[0], None)


def _softmax_pv(scores, values, sink):
    m = sink
    for s in scores:
        m = jnp.maximum(m, jnp.max(s, axis=-1, keepdims=True))
    den = jnp.exp(sink - m)
    acc = None
    for s, v in zip(scores, values):
        p = jnp.exp(s - m)
        den = den + jnp.sum(p, axis=-1, keepdims=True)
        pv = _dot(p.astype(BF16), v)
        acc = pv if acc is None else acc + pv
    return acc / den


def _qk(q, k):
    return lax.dot_general(q, k, (((1,), (1,)), ((), ())), preferred_element_type=F32)


def _upper_half(rows):
    return lax.broadcasted_iota(jnp.int32, (rows, LANES), 1) >= C_HEAD


def _both_halves(x, hh):
    upper = _upper_half(x.shape[0])
    keep = upper if hh == 1 else jnp.logical_not(upper)
    return jnp.where(keep, x, pltpu.roll(x, C_HEAD, axis=1)).astype(BF16)


def _pair_heads(q_chunks, keys, values, sinks, masks):
    rows = q_chunks[0].shape[0]
    upper = _upper_half(rows)
    lower = jnp.logical_not(upper)
    outs = [None] * (2 * C_GROUP)
    for hh in range(2):
        kd = [_both_halves(k, hh) for k in keys]
        vd = [_both_halves(v, hh) for v in values]
        heads = [hh * C_GROUP + g for g in range(C_GROUP)]
        q4 = jnp.concatenate([jnp.where(upper if n % 2 else lower, q_chunks[n // 2], 0.0) for n in heads],
                             axis=0).astype(BF16)
        s4 = [_qk(q4, k) for k in kd]
        p4 = [[] for _ in kd]
        dens = []
        for g, n in enumerate(heads):
            sl = slice(g * rows, (g + 1) * rows)
            scores = [s[sl] if mask is None else jnp.where(mask, s[sl], NEG_INF) for s, mask in zip(s4, masks)]
            sink = sinks(n)
            m = sink
            for s in scores:
                m = jnp.maximum(m, jnp.max(s, axis=-1, keepdims=True))
            den = jnp.exp(sink - m)
            for t, s in enumerate(scores):
                p = jnp.exp(s - m)
                den = den + jnp.sum(p, axis=-1, keepdims=True)
                p4[t].append(p.astype(BF16))
            dens.append(den)
        pv4 = None
        for p_parts, v in zip(p4, vd):
            pv = _dot(jnp.concatenate(p_parts, axis=0), v)
            pv4 = pv if pv4 is None else pv4 + pv
        for g, n in enumerate(heads):
            outs[n] = pv4[g * rows:(g + 1) * rows] / dens[g]
    return jnp.concatenate([jnp.where(lower, outs[2 * m], outs[2 * m + 1]) for m in range(C_GROUP)], axis=-1)


def _attn_ctx_kernel(sink_ref, q_ref, k_ref, v_ref, gate_ref, o_ref):
    kp = pl.program_id(1)
    q = q_ref[...] * (C_HEAD ** -0.5)
    q_chunks = [q[:, m * LANES:(m + 1) * LANES] for m in range(C_GROUP)]
    o = _pair_heads(q_chunks, [k_ref[...]], [v_ref[...]], lambda n: sink_ref[kp * 2 * C_GROUP + n], [None])
    o_ref[...] = (o * _silu(gate_ref[...])).astype(BF16)


def _attn_context(proj, sink):
    qw = 2 * C_GROUP * C_HEAD
    k_cb = C_HEADS * C_HEAD // LANES
    v_cb = (C_HEADS + C_KV_HEADS) * C_HEAD // LANES
    gate_cb = (C_HEADS + 2 * C_KV_HEADS) * C_HEAD // qw
    return pl.pallas_call(
        _attn_ctx_kernel,
        out_shape=jax.ShapeDtypeStruct((PROMPT.rows, D_MODEL), BF16),
        grid=(BATCH, C_KV_HEADS // 2),
        in_specs=[pl.BlockSpec(memory_space=pltpu.SMEM),
                  pl.BlockSpec((SEQ, qw), lambda b, kp: (b, kp)),
                  pl.BlockSpec((SEQ, LANES), lambda b, kp: (b, k_cb + kp)),
                  pl.BlockSpec((SEQ, LANES), lambda b, kp: (b, v_cb + kp)),
                  pl.BlockSpec((SEQ, qw), lambda b, kp: (b, gate_cb + kp))],
        out_specs=pl.BlockSpec((SEQ, qw), lambda b, kp: (b, kp)),
        compiler_params=_cparams(("parallel", "parallel")),
        name="attn_context",
    )(sink, proj, proj, proj, proj)


def _rope(x, cos, sin_signed):
    lane = lax.broadcasted_iota(jnp.int32, x.shape, 1)
    first = (lane & 31) < 16
    partner = jnp.where(first, pltpu.roll(x, LANES - 16, axis=1), pltpu.roll(x, 16, axis=1))
    return x * cos + partner * sin_signed


def _attn_lat_kernel(sink_ref, q_ref, kp_ref, ko_ref, kn_ref, vp_ref, vo_ref, vn_ref, ck_ref, cv_ref,
                     cq_ref, sq_ref, cp_ref, sp_ref, cn_ref, sn_ref, gate_ref, o_ref):
    kpair = pl.program_id(1)
    qb = pl.program_id(2)
    cq = cq_ref[...]
    sq = sq_ref[...]
    q = q_ref[...] * (C_HEAD ** -0.5)
    qr = [_rope(q[:, n * LANES:(n + 1) * LANES], cq, sq) for n in range(4)]
    kband = jnp.concatenate([_rope(kp_ref[...], cp_ref[...], sp_ref[...]),
                             _rope(ko_ref[...], cq, sq),
                             _rope(kn_ref[...], cn_ref[...], sn_ref[...])], axis=0)
    vband = jnp.concatenate([vp_ref[...], vo_ref[...], vn_ref[...]], axis=0)
    qpos = qb * BLOCK + lax.broadcasted_iota(jnp.int32, (BLOCK, 3 * BLOCK), 0)
    kpos = (qb - 1) * BLOCK + lax.broadcasted_iota(jnp.int32, (BLOCK, 3 * BLOCK), 1)
    valid = (jnp.abs(qpos - kpos) <= WINDOW) & (kpos >= 0) & (kpos < DEC_SEQ)
    o = _pair_heads(qr, [kband, ck_ref[...]], [vband, cv_ref[...]],
                    lambda n: sink_ref[kpair * 2 * C_GROUP + n], [valid, None])
    o_ref[...] = (o * _silu(gate_ref[...])).astype(BF16)


def _attn_latent(proj, cache_k4, cache_v4, li, sink, cos_t, sin_t):
    qw = 2 * C_GROUP * C_HEAD
    k_cb = C_HEADS * C_HEAD // LANES
    v_cb = (C_HEADS + C_KV_HEADS) * C_HEAD // LANES
    gate_cb = (C_HEADS + 2 * C_KV_HEADS) * C_HEAD // qw
    nqb = DEC_SEQ // BLOCK

    def rows(delta):
        def idx(b, kp, qb):
            return b * nqb + jnp.clip(qb + delta, 0, nqb - 1)
        return idx

    def kv_spec(cb, delta):
        r = rows(delta)
        return pl.BlockSpec((BLOCK, LANES), lambda b, kp, qb: (r(b, kp, qb), cb + kp))

    def tab_spec(delta):
        return pl.BlockSpec((BLOCK, LANES), lambda b, kp, qb: (jnp.clip(qb + delta, 0, nqb - 1), 0))

    cache_spec = pl.BlockSpec((None, None, PAST_LEN, LANES), lambda b, kp, qb: (b, li, 0, kp))
    own = rows(0)
    return pl.pallas_call(
        _attn_lat_kernel,
        out_shape=jax.ShapeDtypeStruct((SAMPLE.rows, D_MODEL), BF16),
        grid=(DEC_BATCH, C_KV_HEADS // 2, nqb),
        in_specs=[pl.BlockSpec(memory_space=pltpu.SMEM),
                  pl.BlockSpec((BLOCK, qw), lambda b, kp, qb: (own(b, kp, qb), kp)),
                  kv_spec(k_cb, -1), kv_spec(k_cb, 0), kv_spec(k_cb, 1),
                  kv_spec(v_cb, -1), kv_spec(v_cb, 0), kv_spec(v_cb, 1),
                  cache_spec, cache_spec,
                  tab_spec(0), tab_spec(0), tab_spec(-1), tab_spec(-1), tab_spec(1), tab_spec(1),
                  pl.BlockSpec((BLOCK, qw), lambda b, kp, qb: (own(b, kp, qb), gate_cb + kp))],
        out_specs=pl.BlockSpec((BLOCK, qw), lambda b, kp, qb: (own(b, kp, qb), kp)),
        compiler_params=_cparams(("parallel", "parallel", "parallel")),
        name="attn_latent",
    )(sink, proj, proj, proj, proj, proj, proj, proj, cache_k4, cache_v4,
      cos_t, sin_t, cos_t, sin_t, cos_t, sin_t, proj)


def _rope_tables():
    t = jnp.arange(DEC_SEQ, dtype=jnp.int32)
    row = (t // GRID_W).astype(F32)
    col = (t % GRID_W).astype(F32)
    nf = C_HEAD // 4
    inv = 1.0 / (ROPE_BASE ** (jnp.arange(nf, dtype=F32) / nf))
    lane = np.arange(LANES)
    f_of_lane = lane % nf
    use_col = (lane % C_HEAD) >= C_HEAD // 2
    sign = np.where((lane % 32) < 16, -1.0, 1.0).astype(np.float32)
    pos = jnp.where(jnp.asarray(use_col)[None, :], col[:, None], row[:, None])
    ang = pos * inv[jnp.asarray(f_of_lane)][None, :]
    return jnp.cos(ang), jnp.sin(ang) * jnp.asarray(sign)[None, :]


def _chain_const_tiles(vec):
    t = vec.reshape(A_HEADS // 2, 2, A_HEAD).transpose(1, 2, 0)
    t = jnp.broadcast_to(t[:, :, None, :, None], (2, A_HEAD, 2, A_HEADS // 2, SEQ_PER_GROUP))
    return t.reshape(2, A_HEAD, CHAINS)


def _even_mixer(st, proj, s0, p, gmat, cs_mat, dft):
    r, k, v, w2, a2, bonus = _wkv_prep(st, proj, p["mu"], p["w0"], p["wup"], p["a0"], p["aup"], p["k_a"], p["r_k"],
                                       gmat)
    o_f, o_b, s_fin = _wkv(r, k, v, w2, a2, p["kkw_t"], p["ka_t"], s0)
    ya = _ya(st, o_f, o_b, bonus, proj, p["gn_w"], p["gn_b"], gmat)
    zc, zs = _dft_channels(st, proj, cs_mat)
    yb = _dft_time(st, zc, zs, proj, dft[0], dft[1])
    return ya, yb, s_fin


def kernel(x_prompt, x_sample, c, state_wkv, cache_k, cache_v, c_ctx, mod_w, mod_b, norm_pre, norm_post,
           even_w_in, even_mu, even_w0, even_w_up, even_a0, even_a_up, even_k_k, even_k_a, even_r_k,
           even_gn_w, even_gn_b, even_w_out, odd_w_in, odd_sink, odd_w_out):
    n_odd = odd_w_in.shape[0]
    xs = {PROMPT: x_prompt.reshape(PROMPT.rows, D_MODEL), SAMPLE: x_sample.reshape(SAMPLE.rows, D_MODEL)}
    conds = jnp.concatenate([c_ctx[None, :], c, jnp.zeros((N_COND - 1 - DEC_BATCH, D_MODEL), F32)], axis=0)
    mods4 = _modulation(conds, mod_w, mod_b).reshape(DEPTH, N_COND, 1, 3 * D_MODEL)
    norm_pre3 = norm_pre.reshape(DEPTH, 1, D_MODEL)
    norm_post3 = norm_post.reshape(DEPTH, 1, D_MODEL)

    even_w_in_b = even_w_in.astype(BF16)
    even_w_out_b = even_w_out.astype(BF16)
    odd_w_in_b = odd_w_in.astype(BF16)
    odd_w_out_b = odd_w_out.astype(BF16)

    head_of_lane = np.arange(CW) // A_HEAD
    gmat = jnp.asarray((head_of_lane[:, None] == head_of_lane[None, :]).astype(np.float32)).astype(BF16)
    c_ch, s_ch = _dft_mats(B_GROUP_CH)
    cs_mat = jnp.concatenate([c_ch, s_ch], axis=1).astype(BF16)
    dft = {PROMPT: tuple(m.astype(BF16) for m in _dft_mats(SEQ)),
           SAMPLE: tuple(m.astype(BF16) for m in _dft_mats(DEC_SEQ))}
    cos_t, sin_t = _rope_tables()
    cache_k4 = cache_k.reshape(DEC_BATCH, n_odd, PAST_LEN, C_KV_HEADS * C_HEAD)
    cache_v4 = cache_v.reshape(DEC_BATCH, n_odd, PAST_LEN, C_KV_HEADS * C_HEAD)

    def pad_rows(w, slot):
        return jnp.pad(w, ((0, 0), (slot * LORA, (3 - slot) * LORA), (0, 0)))

    wup_pad = jnp.stack([pad_rows(even_w_up[:, 0], 0), pad_rows(even_w_up[:, 1], 1)], axis=1)
    aup_pad = jnp.stack([pad_rows(even_a_up[:, 0], 2), pad_rows(even_a_up[:, 1], 3)], axis=1)

    new_wkv, new_k, new_v = [], [], []
    hs = {st: _pre_norm(st, xs[st], norm_pre3, mods4, 0) for st in (PROMPT, SAMPLE)}
    for layer in range(DEPTH):
        i = layer // 2
        if layer % 2 == 0:
            p = {
                "mu": even_mu[i][None, :], "w0": even_w0[i], "a0": even_a0[i],
                "wup": wup_pad[i], "aup": aup_pad[i],
                "k_a": even_k_a[i][None, :], "r_k": even_r_k[i].reshape(1, A_WIDTH),
                "kkw_t": _chain_const_tiles(even_k_k[i]), "ka_t": _chain_const_tiles(even_k_a[i]),
                "gn_w": even_gn_w[i][None, :], "gn_b": even_gn_b[i][None, :],
            }
            s0s = state_wkv[:, i].reshape(DEC_BATCH, 2, A_HEADS // 2, 2, A_HEAD, A_HEAD)
            s0 = {PROMPT: jnp.zeros((BATCH // SEQ_PER_GROUP, 2, A_HEAD, A_HEAD, CHAINS), F32),
                  SAMPLE: s0s.transpose(3, 5, 4, 1, 2, 0).reshape(1, 2, A_HEAD, A_HEAD, CHAINS)}
            for st in (PROMPT, SAMPLE):
                proj = _in_proj(st, hs[st], even_w_in_b, i)
                ya, yb, s_fin = _even_mixer(st, proj, s0[st], p, gmat, cs_mat, dft[st])
                xs[st], hs[st] = _out_proj(st, ya, 0, yb, 0, even_w_out_b, i, xs[st], mods4, norm_post3, norm_pre3,
                                           layer)
                if st is PROMPT:
                    s_fin = s_fin.reshape(BATCH // SEQ_PER_GROUP, 2, A_HEAD, A_HEAD, 2, A_HEADS // 2, SEQ_PER_GROUP)
                    new_wkv.append(s_fin.transpose(0, 6, 4, 5, 1, 3, 2).reshape(BATCH, 2, A_HEADS, A_HEAD, A_HEAD))
        else:
            proj_p = _in_proj(PROMPT, hs[PROMPT], odd_w_in_b, i)
            proj_s = _in_proj(SAMPLE, hs[SAMPLE], odd_w_in_b, i)
            kv0 = C_HEADS * C_HEAD
            kvn = C_KV_HEADS * C_HEAD
            new_k.append(proj_p[:, kv0:kv0 + kvn].reshape(BATCH, SEQ, C_KV_HEADS, C_HEAD))
            new_v.append(proj_p[:, kv0 + kvn:kv0 + 2 * kvn].reshape(BATCH, SEQ, C_KV_HEADS, C_HEAD))
            att_p = _attn_context(proj_p, odd_sink[i])
            att_s = _attn_latent(proj_s, cache_k4, cache_v4, i, odd_sink[i], cos_t, sin_t)
            for st, att in ((PROMPT, att_p), (SAMPLE, att_s)):
                xs[st], hs[st] = _out_proj(st, att, 0, att, 1, odd_w_out_b, i, xs[st], mods4, norm_post3, norm_pre3,
                                           layer)

    y_prompt = xs[PROMPT].reshape(BATCH, SEQ, D_MODEL)
    y_sample = xs[SAMPLE].reshape(DEC_BATCH, DEC_SEQ, D_MODEL)
    return (y_prompt, y_sample, jnp.stack(new_wkv, axis=1), jnp.stack(new_k, axis=1), jnp.stack(new_v, axis=1))
```

```python
import collections
import functools

import numpy as np
import jax
import jax.numpy as jnp
from jax import lax
from jax.experimental import pallas as pl
from jax.experimental.pallas import tpu as pltpu

F32 = jnp.float32
BF16 = jnp.bfloat16

D_MODEL = 2048
BATCH = 32
SEQ = 256
DEPTH = 4
DEC_BATCH = 8
DEC_SEQ = 2048
PAST_LEN = 256
GRID_W = 64
RMS_EPS = 1e-6
A_WIDTH = 1024
A_HEAD = 64
A_HEADS = 16
LORA = 64
GN_EPS = 64e-5
DECAY_SCALE = float(np.exp(-0.5))
B_WIDTH = 1024
B_GROUPS = 4
B_GROUP_CH = 256
SHIFT_COLS = 3 * A_WIDTH + 4 * LORA
EVEN_IN = SHIFT_COLS + A_WIDTH + 2 * B_WIDTH
C_HEAD = 64
C_HEADS = 32
C_KV_HEADS = 8
C_GROUP = 4
WINDOW = 128
BLOCK = 128
ROPE_BASE = 10000.0
ROPE_PAIR = C_HEAD // 4
KK_NORM_FLOOR = 1e-12
ODD_IN = (C_HEADS + 2 * C_KV_HEADS) * C_HEAD + D_MODEL
NEG_INF = -1e30

N_COND = 16
SUBLANES = 8
LANES = 128
CHAINS = LANES
SEQ_PER_GROUP = CHAINS // A_HEADS
CW = 256
TM_IN = 512
TN_IN = 1280
TM_ROW = 256
TM_OUT = 512
OUT_SUBTILES = 2
PRE_SUBTILES = 8
WKV_TT = 32
BIG_VMEM_LIMIT = 56 * 1024 * 1024
WKV_ROWS = 16
WKV_ACCS = 2
VMEM_LIMIT = 48 * 1024 * 1024

Stream = collections.namedtuple("Stream", "rows t_len n_seq cond0 cond_per_seq")
PROMPT = Stream(BATCH * SEQ, SEQ, BATCH, 0, 0)
SAMPLE = Stream(DEC_BATCH * DEC_SEQ, DEC_SEQ, DEC_BATCH, 1, 1)


def _cparams(sem):
    return pltpu.CompilerParams(dimension_semantics=sem, vmem_limit_bytes=VMEM_LIMIT)


def _cond_of_row(st, row0):
    return st.cond0 + st.cond_per_seq * (row0 // st.t_len)


def _sigmoid(x):
    return 1.0 / (1.0 + jnp.exp(-x))


def _silu(x):
    return x * _sigmoid(x)


def _split(a):
    hi = a.astype(BF16)
    lo = (a - hi.astype(F32)).astype(BF16)
    return hi, lo


def _dot(a, b):
    return jnp.dot(a, b, preferred_element_type=F32)


def _dot3_split(a_hl, b_hl):
    (ah, al), (bh, bl) = a_hl, b_hl
    return _dot(ah, bh) + (_dot(ah, bl) + _dot(al, bh))


def _dot3(a, b):
    return _dot3_split(_split(a), _split(b))


def _gsum(x, gmat):
    xh, xl = _split(x)
    return _dot(xh, gmat) + _dot(xl, gmat)


def _mod_kernel(c_ref, w_ref, b_ref, o_ref):
    o_ref[...] = _dot3(_silu(c_ref[...]), w_ref[...]) + b_ref[...]


def _modulation(conds, mod_w, mod_b):
    tn = 512
    n = 3 * D_MODEL
    return pl.pallas_call(
        _mod_kernel,
        out_shape=jax.ShapeDtypeStruct((DEPTH, N_COND, n), F32),
        grid=(DEPTH, n // tn),
        in_specs=[
            pl.BlockSpec((N_COND, D_MODEL), lambda l, j: (0, 0)),
            pl.BlockSpec((None, D_MODEL, tn), lambda l, j: (l, 0, j)),
            pl.BlockSpec((None, 1, tn), lambda l, j: (l, 0, j)),
        ],
        out_specs=pl.BlockSpec((None, N_COND, tn), lambda l, j: (l, 0, j)),
        compiler_params=_cparams(("parallel", "parallel")),
        name="modulation",
    )(conds, mod_w, mod_b.reshape(DEPTH, 1, n))


def _norm_modulate(x, g, sh, sc):
    ms = jnp.mean(x * x, axis=-1, keepdims=True)
    y = x * lax.rsqrt(ms + RMS_EPS) * g
    return (y * (1.0 + sc) + sh).astype(BF16)


def _pre_kernel(x_ref, g_ref, sh_ref, sc_ref, h_ref):
    sub = x_ref.shape[0] // PRE_SUBTILES
    for s in range(PRE_SUBTILES):
        rows = slice(s * sub, (s + 1) * sub)
        h_ref[rows, :] = _norm_modulate(x_ref[rows, :], g_ref[...], sh_ref[...], sc_ref[...])


def _pre_norm(st, x, norm_g, mods4, layer):
    tm = TM_ROW
    cond = lambda i: _cond_of_row(st, i * tm)
    return pl.pallas_call(
        _pre_kernel,
        out_shape=jax.ShapeDtypeStruct((st.rows, D_MODEL), BF16),
        grid=(st.rows // tm,),
        in_specs=[pl.BlockSpec((tm, D_MODEL), lambda i: (i, 0)),
                  pl.BlockSpec((None, 1, D_MODEL), lambda i: (layer, 0, 0)),
                  pl.BlockSpec((None, None, 1, D_MODEL), lambda i: (layer, cond(i), 0, 0)),
                  pl.BlockSpec((None, None, 1, D_MODEL), lambda i: (layer, cond(i), 0, 1))],
        out_specs=pl.BlockSpec((tm, D_MODEL), lambda i: (i, 0)),
        compiler_params=_cparams(("parallel",)),
        name="pre_norm",
    )(x, norm_g, mods4, mods4)


def _in_kernel(h_ref, w_ref, o_ref):
    o_ref[...] = _dot(h_ref[...], w_ref[...])


def _in_proj(st, h, w_bf16, li):
    n = w_bf16.shape[-1]
    return pl.pallas_call(
        _in_kernel,
        out_shape=jax.ShapeDtypeStruct((st.rows, n), F32),
        grid=(n // TN_IN, st.rows // TM_IN),
        in_specs=[pl.BlockSpec((TM_IN, D_MODEL), lambda j, i: (i, 0)),
                  pl.BlockSpec((None, D_MODEL, TN_IN), lambda j, i: (li, 0, j))],
        out_specs=pl.BlockSpec((TM_IN, TN_IN), lambda j, i: (i, j)),
        compiler_params=_cparams(("parallel", "parallel")),
        name="in_proj",
    )(h, w_bf16)


def _token_shift(cur, prev8, next8, mu, first, last):
    tm = cur.shape[0]
    rows = lax.broadcasted_iota(jnp.int32, cur.shape, 0)
    prow = jnp.where(first, 0.0, prev8[7:8, :])
    nrow = jnp.where(last, 0.0, next8[0:1, :])
    up = jnp.where(rows == 0, prow, pltpu.roll(cur, 1, axis=0))
    dn = jnp.where(rows == tm - 1, nrow, pltpu.roll(cur, tm - 1, axis=0))
    return cur + mu * (0.5 * (up + dn) - cur)


def _prep_kernel(r_ref, k_ref, v_ref, lo_ref, rp_ref, kp_ref, vp_ref, lp_ref, rn_ref, kn_ref, vn_ref, ln_ref,
                 mur_ref, muk_ref, muv_ref, mul_ref, w0_ref, wuph_ref, wupl_ref, a0_ref, auph_ref, aupl_ref,
                 ka_ref, rk_ref, gm_ref,
                 r_o, k_o, v_o, w_o, a_o, bon_o, *, tiles_per_seq):
    i = pl.program_id(0)
    first = lax.rem(i, tiles_per_seq) == 0
    last = lax.rem(i, tiles_per_seq) == tiles_per_seq - 1

    low = _token_shift(lo_ref[...], lp_ref[...], ln_ref[...], mul_ref[...], first, last)
    low_hl = _split(low)
    low_t_hl = _split(jnp.tanh(low))
    gm = gm_ref[...]

    for c in range(A_WIDTH // CW):
        cols = slice(c * CW, (c + 1) * CW)
        r = _token_shift(r_ref[:, cols], rp_ref[:, cols], rn_ref[:, cols], mur_ref[:, cols], first, last)
        k = _token_shift(k_ref[:, cols], kp_ref[:, cols], kn_ref[:, cols], muk_ref[:, cols], first, last)
        v = _token_shift(v_ref[:, cols], vp_ref[:, cols], vn_ref[:, cols], muv_ref[:, cols], first, last)
        ka = ka_ref[:, cols]
        kd_sum = None
        for d in range(2):
            w_raw = w0_ref[d:d + 1, cols] + _dot3_split(low_t_hl, (wuph_ref[d, :, cols], wupl_ref[d, :, cols]))
            a = _sigmoid(a0_ref[d:d + 1, cols] + _dot3_split(low_hl, (auph_ref[d, :, cols], aupl_ref[d, :, cols])))
            kd = k * (1.0 + (a - 1.0) * ka)
            w_o[d, :, cols] = jnp.exp(-DECAY_SCALE * _sigmoid(w_raw))
            a_o[d, :, cols] = a
            kd_sum = kd if kd_sum is None else kd_sum + kd
        r_o[:, cols] = r
        k_o[:, cols] = k
        v_o[:, cols] = v
        bon_o[:, cols] = _gsum(r * (0.5 * kd_sum) * rk_ref[:, cols], gm) * v


def _wkv_prep(st, proj, mu, w0, wup_pad, a0, aup_pad, k_a, r_k, gmat):
    tm = TM_ROW
    nb8 = st.rows // SUBLANES
    low_cb = 3 * A_WIDTH // CW
    assert 4 * LORA == CW and st.t_len % tm == 0

    prev_row = lambda i: jnp.maximum(i * (tm // SUBLANES) - 1, 0)
    next_row = lambda i: jnp.minimum((i + 1) * (tm // SUBLANES), nb8 - 1)

    def main(cb):
        return pl.BlockSpec((tm, A_WIDTH), lambda i: (i, cb))

    def prev(cb):
        return pl.BlockSpec((SUBLANES, A_WIDTH), lambda i: (prev_row(i), cb))

    def nxt(cb):
        return pl.BlockSpec((SUBLANES, A_WIDTH), lambda i: (next_row(i), cb))

    def fixed(spec_fn):
        return [spec_fn(0), spec_fn(1), spec_fn(2)]

    low_main = pl.BlockSpec((tm, CW), lambda i: (i, low_cb))
    low_prev = pl.BlockSpec((SUBLANES, CW), lambda i: (prev_row(i), low_cb))
    low_next = pl.BlockSpec((SUBLANES, CW), lambda i: (next_row(i), low_cb))

    def vec(cb):
        return pl.BlockSpec((1, A_WIDTH), lambda i: (0, cb))

    in_specs = (fixed(main) + [low_main] + fixed(prev) + [low_prev] + fixed(nxt) + [low_next]
                + [vec(0), vec(1), vec(2), pl.BlockSpec((1, CW), lambda i: (0, low_cb))]
                + [pl.BlockSpec((2, A_WIDTH), lambda i: (0, 0)),
                   pl.BlockSpec((2, CW, A_WIDTH), lambda i: (0, 0, 0)),
                   pl.BlockSpec((2, CW, A_WIDTH), lambda i: (0, 0, 0)),
                   pl.BlockSpec((2, A_WIDTH), lambda i: (0, 0)),
                   pl.BlockSpec((2, CW, A_WIDTH), lambda i: (0, 0, 0)),
                   pl.BlockSpec((2, CW, A_WIDTH), lambda i: (0, 0, 0)),
                   vec(0), vec(0),
                   pl.BlockSpec((CW, CW), lambda i: (0, 0))])
    tps = st.t_len // tm
    g_n = st.n_seq // SEQ_PER_GROUP

    def til(i):
        s = i // tps
        return (s // SEQ_PER_GROUP, lax.rem(i, tps), lax.rem(s, SEQ_PER_GROUP))

    one = pl.BlockSpec((tm, A_WIDTH), lambda i: (i, 0))
    til1 = pl.BlockSpec((None, tm, A_WIDTH), lambda i: til(i))
    til2 = pl.BlockSpec((2, None, tm, A_WIDTH), lambda i: (0,) + til(i))
    one_sds = jax.ShapeDtypeStruct((st.rows, A_WIDTH), F32)
    til1_sds = jax.ShapeDtypeStruct((g_n, st.t_len, SEQ_PER_GROUP * A_WIDTH), F32)
    til2_sds = jax.ShapeDtypeStruct((2, g_n, st.t_len, SEQ_PER_GROUP * A_WIDTH), F32)
    return pl.pallas_call(
        functools.partial(_prep_kernel, tiles_per_seq=tps),
        out_shape=[til1_sds, til1_sds, til1_sds, til2_sds, til2_sds, one_sds],
        grid=(st.rows // tm,),
        in_specs=in_specs,
        out_specs=[til1, til1, til1, til2, til2, one],
        compiler_params=_cparams(("parallel",)),
        name="wkv_prep",
    )(*([proj] * 12), mu, mu, mu, mu, w0, *_split(wup_pad), a0, *_split(aup_pad), k_a, r_k, gmat)


def _wkv_kernel(rf_ref, rb_ref, kf_ref, kb_ref, vf_ref, vb_ref, wf_ref, wb_ref, af_ref, ab_ref, kkw_ref, ka_ref,
                s0_ref, of_ref, ob_ref, sf_ref,
                s2_ref, r2_ref, kk2_ref, v2_ref, kka2_ref, kd2_ref, g_ref, o_ref, *, tt_steps):
    c = pl.program_id(1)
    nblk = WKV_ROWS // SUBLANES
    n_hv = A_HEADS // 2
    pairs = ((rf_ref, rb_ref), (kf_ref, kb_ref), (vf_ref, vb_ref), (wf_ref, wb_ref), (af_ref, ab_ref))

    @pl.when(c == 0)
    def _():
        s2_ref[...] = s0_ref[...]

    def to_chain_tiles(u, gammas):
        ub = tt_steps - 1 - u
        halves = []
        for xf_ref, xb_ref in pairs:
            x2 = jnp.concatenate([xf_ref[u, :, hv * LANES:(hv + 1) * LANES] for hv in range(n_hv)]
                                 + [xb_ref[ub, :, hv * LANES:(hv + 1) * LANES] for hv in range(n_hv)], axis=0)
            y = x2.T
            halves.append((y[:A_HEAD], y[A_HEAD:]))
        r_h, k_h, v_h, w_h, a_h = halves
        new_gammas = []
        for hp in range(2):
            kk = k_h[hp] * kkw_ref[hp]
            norm = jnp.sqrt(jnp.sum(kk * kk, axis=0, keepdims=True))
            kk = kk / jnp.maximum(norm, KK_NORM_FLOOR)
            kka = kk * a_h[hp]
            kd = k_h[hp] * (1.0 + (a_h[hp] - 1.0) * ka_ref[hp])
            gamma = gammas[hp] * w_h[hp]
            inv = 1.0 / gamma
            kk2_ref[hp, u] = gammas[hp] * kk
            kka2_ref[hp, u] = kka * inv
            kd2_ref[hp, u] = kd * inv
            r2_ref[hp, u] = r_h[hp] * gamma
            v2_ref[hp, u] = v_h[hp]
            new_gammas.append(gamma)
        return tuple(new_gammas)

    one = jnp.ones((A_HEAD, CHAINS), F32)
    gammas = lax.fori_loop(0, tt_steps, to_chain_tiles, (one, one), unroll=4)
    for hp in range(2):
        g_ref[hp] = gammas[hp]

    def bcast_row(ref, t, j):
        return ref[t, pl.ds(j, SUBLANES, stride=0), :]

    def time_step(hp, t):
        s_ref = s2_ref.at[hp]
        r_ref, kk_ref, v_ref, kka_ref, kd_ref = (ref.at[hp] for ref in (r2_ref, kk2_ref, v2_ref, kka2_ref, kd2_ref))
        for ib in range(A_HEAD // WKV_ROWS):
            i0 = ib * WKV_ROWS
            rows = [pl.ds(i0 + SUBLANES * b, SUBLANES) for b in range(nblk)]
            sa = [[None] * WKV_ACCS for _ in range(nblk)]
            for j in range(A_HEAD):
                kkj = bcast_row(kk_ref, t, j)
                for b in range(nblk):
                    p = s_ref[j, rows[b], :] * kkj
                    sa[b][j % WKV_ACCS] = p if sa[b][j % WKV_ACCS] is None else sa[b][j % WKV_ACCS] + p
            sa = [functools.reduce(lambda x, y: x + y, parts) for parts in sa]
            v8 = [v_ref[t, rows[b], :] for b in range(nblk)]
            out = [[None] * WKV_ACCS for _ in range(nblk)]
            for j in range(A_HEAD):
                kkaj = bcast_row(kka_ref, t, j)
                kdj = bcast_row(kd_ref, t, j)
                rj = bcast_row(r_ref, t, j)
                for b in range(nblk):
                    sn = s_ref[j, rows[b], :] - sa[b] * kkaj + v8[b] * kdj
                    s_ref[j, rows[b], :] = sn
                    q = sn * rj
                    out[b][j % WKV_ACCS] = q if out[b][j % WKV_ACCS] is None else out[b][j % WKV_ACCS] + q
            for b in range(nblk):
                o_ref[hp, rows[b], :] = functools.reduce(lambda x, y: x + y, out[b])

    def step(u, carry):
        for hp in range(2):
            time_step(hp, u)
        x2 = jnp.concatenate([o_ref[0], o_ref[1]], axis=0).T
        ub = tt_steps - 1 - u
        for hv in range(n_hv):
            of_ref[u, :, hv * LANES:(hv + 1) * LANES] = x2[hv * SUBLANES:(hv + 1) * SUBLANES, :]
            ob_ref[ub, :, hv * LANES:(hv + 1) * LANES] = x2[(n_hv + hv) * SUBLANES:(n_hv + hv + 1) * SUBLANES, :]
        return carry

    lax.fori_loop(0, tt_steps, step, 0)

    def rescale(j, carry):
        for hp in range(2):
            gj = g_ref[hp, pl.ds(j, SUBLANES, stride=0), :]
            for b in range(A_HEAD // SUBLANES):
                rows = pl.ds(b * SUBLANES, SUBLANES)
                s2_ref[hp, j, rows, :] = s2_ref[hp, j, rows, :] * gj
        return carry

    lax.fori_loop(0, A_HEAD, rescale, 0)

    @pl.when(c == pl.num_programs(1) - 1)
    def _():
        sf_ref[...] = s2_ref[...]


def _wkv(r, k, v, w, a, kkw_t, ka_t, s0):
    g_n, t_n = r.shape[0], r.shape[1]
    tt = WKV_TT
    nc = t_n // tt
    split = lambda a: a.reshape(a.shape[:-1] + (SEQ_PER_GROUP, A_WIDTH))
    blk = (tt, SEQ_PER_GROUP, A_WIDTH)

    fwd = pl.BlockSpec((None,) + blk, lambda g, c: (g, c, 0, 0))
    bwd = pl.BlockSpec((None,) + blk, lambda g, c: (g, nc - 1 - c, 0, 0))
    fwd_d = pl.BlockSpec((None, None) + blk, lambda g, c: (0, g, c, 0, 0))
    bwd_d = pl.BlockSpec((None, None) + blk, lambda g, c: (1, g, nc - 1 - c, 0, 0))
    state = pl.BlockSpec((None, 2, A_HEAD, A_HEAD, CHAINS), lambda g, c: (g, 0, 0, 0, 0),
                         pipeline_mode=pl.Buffered(1))
    tiles = pltpu.VMEM((2, tt, A_HEAD, CHAINS), F32)
    o_sds = jax.ShapeDtypeStruct((g_n, t_n, SEQ_PER_GROUP, A_WIDTH), F32)
    r, k, v, w, a = (split(x) for x in (r, k, v, w, a))
    const = pl.BlockSpec((2, A_HEAD, CHAINS), lambda g, c: (0, 0, 0))
    o_f, o_b, s_fin = pl.pallas_call(
        functools.partial(_wkv_kernel, tt_steps=tt),
        out_shape=[o_sds, o_sds, jax.ShapeDtypeStruct((g_n, 2, A_HEAD, A_HEAD, CHAINS), F32)],
        grid=(g_n, nc),
        in_specs=[fwd, bwd, fwd, bwd, fwd, bwd, fwd_d, bwd_d, fwd_d, bwd_d, const, const, state],
        out_specs=[fwd, bwd, state],
        scratch_shapes=([pltpu.VMEM((2, A_HEAD, A_HEAD, CHAINS), F32)] + [tiles] * 5
                        + [pltpu.VMEM((2, A_HEAD, CHAINS), F32)] * 2),
        compiler_params=pltpu.CompilerParams(dimension_semantics=("parallel", "arbitrary"),
                                             vmem_limit_bytes=BIG_VMEM_LIMIT),
        name="wkv",
    )(r, r, k, k, v, v, w, w, a, a, kkw_t, ka_t, s0)
    merge = lambda a: a.reshape(g_n, t_n, SEQ_PER_GROUP * A_WIDTH)
    return merge(o_f), merge(o_b), s_fin


def _ya_kernel(of_ref, ob_ref, bon_ref, g0_ref, g1_ref, g2_ref, g3_ref, gw_ref, gb_ref, gm_ref, o_ref):
    gm = gm_ref[...]
    for c, gate_ref in enumerate((g0_ref, g1_ref, g2_ref, g3_ref)):
        cols = slice(c * CW, (c + 1) * CW)
        o = of_ref[:, cols] + ob_ref[:, cols]
        mean = _gsum(o, gm) * (1.0 / A_HEAD)
        dev = o - mean
        var = _gsum(dev * dev, gm) * (1.0 / A_HEAD)
        y = dev * lax.rsqrt(var + GN_EPS) * gw_ref[:, cols] + gb_ref[:, cols]
        o_ref[:, cols] = ((y + bon_ref[:, cols]) * _silu(gate_ref[...])).astype(BF16)


def _ya(st, o_f, o_b, bonus, proj, gn_w, gn_b, gmat):
    tm = TM_ROW
    gate_cb = (SHIFT_COLS) // CW
    blk = pl.BlockSpec((tm, A_WIDTH), lambda i: (i, 0))
    vec = pl.BlockSpec((1, A_WIDTH), lambda i: (0, 0))
    tps = st.t_len // tm

    def til_idx(i):
        s = i // tps
        return (s // SEQ_PER_GROUP, lax.rem(i, tps), lax.rem(s, SEQ_PER_GROUP))

    til = pl.BlockSpec((None, tm, A_WIDTH), til_idx)
    gates = [pl.BlockSpec((tm, CW), lambda i, c=c: (i, gate_cb + c)) for c in range(A_WIDTH // CW)]
    return pl.pallas_call(
        _ya_kernel,
        out_shape=jax.ShapeDtypeStruct((st.rows, A_WIDTH), BF16),
        grid=(st.rows // tm,),
        in_specs=[til, til, blk] + gates + [vec, vec, pl.BlockSpec((CW, CW), lambda i: (0, 0))],
        out_specs=blk,
        compiler_params=_cparams(("parallel",)),
        name="wkv_post",
    )(o_f, o_b, bonus, proj, proj, proj, proj, gn_w, gn_b, gmat)


def _dft_ch_kernel(u_ref, cs_ref, zc_ref, zs_ref):
    z = _dot(u_ref[...].astype(BF16), cs_ref[...])
    zc_ref[...] = z[:, :B_GROUP_CH].astype(BF16)
    zs_ref[...] = z[:, B_GROUP_CH:].astype(BF16)


def _dft_channels(st, proj, cs_mat):
    tm = 2048
    u_cb = (SHIFT_COLS + A_WIDTH) // B_GROUP_CH
    blk = pl.BlockSpec((tm, B_GROUP_CH), lambda i, g: (i, g))
    sds = jax.ShapeDtypeStruct((st.rows, B_WIDTH), BF16)
    return pl.pallas_call(
        _dft_ch_kernel,
        out_shape=[sds, sds],
        grid=(st.rows // tm, B_GROUPS),
        in_specs=[pl.BlockSpec((tm, B_GROUP_CH), lambda i, g: (i, u_cb + g)),
                  pl.BlockSpec((B_GROUP_CH, 2 * B_GROUP_CH), lambda i, g: (0, 0))],
        out_specs=[blk, blk],
        compiler_params=_cparams(("parallel", "parallel")),
        name="dft_channels",
    )(proj, cs_mat)


def _dft_time_kernel(c_ref, s_ref, zc_ref, zs_ref, g0_ref, g1_ref, g2_ref, g3_ref, o_ref, *, scale):
    tm = o_ref.shape[0]
    rows = pl.ds(pl.multiple_of(pl.program_id(1) * tm, tm), tm)
    acc = _dot(c_ref[rows, :], zc_ref[...]) - _dot(s_ref[rows, :], zs_ref[...])
    for g, gate_ref in enumerate((g0_ref, g1_ref, g2_ref, g3_ref)):
        cols = slice(g * B_GROUP_CH, (g + 1) * B_GROUP_CH)
        o_ref[:, cols] = (acc[:, cols] * scale * _silu(gate_ref[...])).astype(BF16)


def _dft_time(st, zc, zs, proj, cmat, smat):
    t_len = st.t_len
    tm = min(t_len, 512)
    mt = t_len // tm
    whole = pl.BlockSpec((t_len, t_len), lambda b, m: (0, 0), pipeline_mode=pl.Buffered(1))
    gate_cb = (SHIFT_COLS + A_WIDTH + B_WIDTH) // B_GROUP_CH
    gate_specs = [pl.BlockSpec((tm, B_GROUP_CH), lambda b, m, g=g: (b * mt + m, gate_cb + g)) for g in range(B_GROUPS)]
    scale = 1.0 / float(np.sqrt(t_len * B_GROUP_CH))
    return pl.pallas_call(
        functools.partial(_dft_time_kernel, scale=scale),
        out_shape=jax.ShapeDtypeStruct((st.rows, B_WIDTH), BF16),
        grid=(st.n_seq, mt),
        in_specs=[whole, whole,
                  pl.BlockSpec((t_len, B_WIDTH), lambda b, m: (b, 0)),
                  pl.BlockSpec((t_len, B_WIDTH), lambda b, m: (b, 0))] + gate_specs,
        out_specs=pl.BlockSpec((tm, B_WIDTH), lambda b, m: (b * mt + m, 0)),
        compiler_params=_cparams(("parallel", "parallel")),
        name="dft_time",
    )(cmat, smat, zc, zs, proj, proj, proj, proj)


def _dft_mats(n):
    idx = jnp.arange(n, dtype=jnp.int32)
    prod = (idx[:, None] * idx[None, :]) % n
    ang = prod.astype(F32) * (2.0 * np.pi / n)
    return jnp.cos(ang), jnp.sin(ang)


def _out_kernel(a1_ref, a2_ref, w1_ref, w2_ref, x_ref, gate_ref, g_ref, *rest, emit_next):
    if emit_next:
        ng_ref, nsh_ref, nsc_ref, o_ref, h_ref = rest
    else:
        o_ref, = rest
    sub = x_ref.shape[0] // OUT_SUBTILES
    for s in range(OUT_SUBTILES):
        rows = slice(s * sub, (s + 1) * sub)
        y = _dot(a1_ref[rows, :], w1_ref[...]) + _dot(a2_ref[rows, :], w2_ref[...])
        ms = jnp.mean(y * y, axis=-1, keepdims=True)
        yn = y * lax.rsqrt(ms + RMS_EPS) * g_ref[...]
        x_new = x_ref[rows, :] + gate_ref[...] * yn
        if emit_next:
            h_ref[rows, :] = _norm_modulate(x_new, ng_ref[...], nsh_ref[...], nsc_ref[...])
        o_ref[rows, :] = x_new


def _out_proj(st, a1, a1_cb, a2, a2_cb, w_bf16, li, x, mods4, norm_post, norm_pre, layer):
    tm = TM_OUT
    half = D_MODEL // 2
    cond = lambda i: _cond_of_row(st, i * tm)
    emit_next = layer + 1 < DEPTH
    row = pl.BlockSpec((tm, D_MODEL), lambda i: (i, 0))
    in_specs = [pl.BlockSpec((tm, half), lambda i: (i, a1_cb)),
                pl.BlockSpec((tm, half), lambda i: (i, a2_cb)),
                pl.BlockSpec((None, half, D_MODEL), lambda i: (li, 0, 0)),
                pl.BlockSpec((None, half, D_MODEL), lambda i: (li, 1, 0)),
                row,
                pl.BlockSpec((None, None, 1, D_MODEL), lambda i: (layer, cond(i), 0, 2)),
                pl.BlockSpec((None, 1, D_MODEL), lambda i: (layer, 0, 0))]
    args = [a1, a2, w_bf16, w_bf16, x, mods4, norm_post]
    out_shape = [jax.ShapeDtypeStruct((st.rows, D_MODEL), F32)]
    out_specs = [row]
    if emit_next:
        in_specs += [pl.BlockSpec((None, 1, D_MODEL), lambda i: (layer + 1, 0, 0)),
                     pl.BlockSpec((None, None, 1, D_MODEL), lambda i: (layer + 1, cond(i), 0, 0)),
                     pl.BlockSpec((None, None, 1, D_MODEL), lambda i: (layer + 1, cond(i), 0, 1))]
        args += [norm_pre, mods4, mods4]
        out_shape.append(jax.ShapeDtypeStruct((st.rows, D_MODEL), BF16))
        out_specs.append(row)
    outs = pl.pallas_call(
        functools.partial(_out_kernel, emit_next=emit_next),
        out_shape=out_shape,
        grid=(st.rows // tm,),
        in_specs=in_specs,
        out_specs=out_specs,
        compiler_params=pltpu.CompilerParams(dimension_semantics=("parallel",), vmem_limit_bytes=BIG_VMEM_LIMIT),
        name="out_proj",
    )(*args)
    return (outs[0], outs[1]) if emit_next else (outs[0], None)


def _softmax_pv(scores, values, sink):
    m = sink
    for s in scores:
        m = jnp.maximum(m, jnp.max(s, axis=-1, keepdims=True))
    den = jnp.exp(sink - m)
    acc = None
    for s, v in zip(scores, values):
        p = jnp.exp(s - m)
        den = den + jnp.sum(p, axis=-1, keepdims=True)
        pv = _dot(p.astype(BF16), v)
        acc = pv if acc is None else acc + pv
    return acc / den


def _qk(q, k):
    return lax.dot_general(q, k, (((1,), (1,)), ((), ())), preferred_element_type=F32)


def _upper_half(rows):
    return lax.broadcasted_iota(jnp.int32, (rows, LANES), 1) >= C_HEAD


def _both_halves(x, hh):
    upper = _upper_half(x.shape[0])
    keep = upper if hh == 1 else jnp.logical_not(upper)
    return jnp.where(keep, x, pltpu.roll(x, C_HEAD, axis=1)).astype(BF16)


def _pair_heads(q_chunks, keys, values, sinks, masks):
    rows = q_chunks[0].shape[0]
    upper = _upper_half(rows)
    lower = jnp.logical_not(upper)
    outs = [None] * (2 * C_GROUP)
    for hh in range(2):
        kd = [_both_halves(k, hh) for k in keys]
        vd = [_both_halves(v, hh) for v in values]
        heads = [hh * C_GROUP + g for g in range(C_GROUP)]
        q4 = jnp.concatenate([jnp.where(upper if n % 2 else lower, q_chunks[n // 2], 0.0) for n in heads],
                             axis=0).astype(BF16)
        s4 = [_qk(q4, k) for k in kd]
        p4 = [[] for _ in kd]
        dens = []
        for g, n in enumerate(heads):
            sl = slice(g * rows, (g + 1) * rows)
            scores = [s[sl] if mask is None else jnp.where(mask, s[sl], NEG_INF) for s, mask in zip(s4, masks)]
            sink = sinks(n)
            m = sink
            for s in scores:
                m = jnp.maximum(m, jnp.max(s, axis=-1, keepdims=True))
            den = jnp.exp(sink - m)
            for t, s in enumerate(scores):
                p = jnp.exp(s - m)
                den = den + jnp.sum(p, axis=-1, keepdims=True)
                p4[t].append(p.astype(BF16))
            dens.append(den)
        pv4 = None
        for p_parts, v in zip(p4, vd):
            pv = _dot(jnp.concatenate(p_parts, axis=0), v)
            pv4 = pv if pv4 is None else pv4 + pv
        for g, n in enumerate(heads):
            outs[n] = pv4[g * rows:(g + 1) * rows] / dens[g]
    return jnp.concatenate([jnp.where(lower, outs[2 * m], outs[2 * m + 1]) for m in range(C_GROUP)], axis=-1)


def _attn_ctx_kernel(sink_ref, q_ref, k_ref, v_ref, gate_ref, o_ref):
    kp = pl.program_id(1)
    q = q_ref[...] * (C_HEAD ** -0.5)
    q_chunks = [q[:, m * LANES:(m + 1) * LANES] for m in range(C_GROUP)]
    o = _pair_heads(q_chunks, [k_ref[...]], [v_ref[...]], lambda n: sink_ref[kp * 2 * C_GROUP + n], [None])
    o_ref[...] = (o * _silu(gate_ref[...])).astype(BF16)


def _attn_context(proj, sink):
    qw = 2 * C_GROUP * C_HEAD
    k_cb = C_HEADS * C_HEAD // LANES
    v_cb = (C_HEADS + C_KV_HEADS) * C_HEAD // LANES
    gate_cb = (C_HEADS + 2 * C_KV_HEADS) * C_HEAD // qw
    return pl.pallas_call(
        _attn_ctx_kernel,
        out_shape=jax.ShapeDtypeStruct((PROMPT.rows, D_MODEL), BF16),
        grid=(BATCH, C_KV_HEADS // 2),
        in_specs=[pl.BlockSpec(memory_space=pltpu.SMEM),
                  pl.BlockSpec((SEQ, qw), lambda b, kp: (b, kp)),
                  pl.BlockSpec((SEQ, LANES), lambda b, kp: (b, k_cb + kp)),
                  pl.BlockSpec((SEQ, LANES), lambda b, kp: (b, v_cb + kp)),
                  pl.BlockSpec((SEQ, qw), lambda b, kp: (b, gate_cb + kp))],
        out_specs=pl.BlockSpec((SEQ, qw), lambda b, kp: (b, kp)),
        compiler_params=_cparams(("parallel", "parallel")),
        name="attn_context",
    )(sink, proj, proj, proj, proj)


def _rope(x, cos, sin_signed):
    lane = lax.broadcasted_iota(jnp.int32, x.shape, 1)
    first = (lane & (2 * ROPE_PAIR - 1)) < ROPE_PAIR
    partner = jnp.where(first, pltpu.roll(x, LANES - ROPE_PAIR, axis=1), pltpu.roll(x, ROPE_PAIR, axis=1))
    return x * cos + partner * sin_signed


def _attn_lat_kernel(sink_ref, q_ref, kp_ref, ko_ref, kn_ref, vp_ref, vo_ref, vn_ref, ck_ref, cv_ref,
                     cq_ref, sq_ref, cp_ref, sp_ref, cn_ref, sn_ref, gate_ref, o_ref):
    kpair = pl.program_id(1)
    qb = pl.program_id(2)
    cq = cq_ref[...]
    sq = sq_ref[...]
    q = q_ref[...] * (C_HEAD ** -0.5)
    qr = [_rope(q[:, n * LANES:(n + 1) * LANES], cq, sq) for n in range(4)]
    kband = jnp.concatenate([_rope(kp_ref[...], cp_ref[...], sp_ref[...]),
                             _rope(ko_ref[...], cq, sq),
                             _rope(kn_ref[...], cn_ref[...], sn_ref[...])], axis=0)
    vband = jnp.concatenate([vp_ref[...], vo_ref[...], vn_ref[...]], axis=0)
    qpos = qb * BLOCK + lax.broadcasted_iota(jnp.int32, (BLOCK, 3 * BLOCK), 0)
    kpos = (qb - 1) * BLOCK + lax.broadcasted_iota(jnp.int32, (BLOCK, 3 * BLOCK), 1)
    valid = (jnp.abs(qpos - kpos) <= WINDOW) & (kpos >= 0) & (kpos < DEC_SEQ)
    o = _pair_heads(qr, [kband, ck_ref[...]], [vband, cv_ref[...]],
                    lambda n: sink_ref[kpair * 2 * C_GROUP + n], [valid, None])
    o_ref[...] = (o * _silu(gate_ref[...])).astype(BF16)


def _attn_latent(proj, cache_k4, cache_v4, li, sink, cos_t, sin_t):
    qw = 2 * C_GROUP * C_HEAD
    k_cb = C_HEADS * C_HEAD // LANES
    v_cb = (C_HEADS + C_KV_HEADS) * C_HEAD // LANES
    gate_cb = (C_HEADS + 2 * C_KV_HEADS) * C_HEAD // qw
    nqb = DEC_SEQ // BLOCK

    def rows(delta):
        def idx(b, kp, qb):
            return b * nqb + jnp.clip(qb + delta, 0, nqb - 1)
        return idx

    def kv_spec(cb, delta):
        r = rows(delta)
        return pl.BlockSpec((BLOCK, LANES), lambda b, kp, qb: (r(b, kp, qb), cb + kp))

    def tab_spec(delta):
        return pl.BlockSpec((BLOCK, LANES), lambda b, kp, qb: (jnp.clip(qb + delta, 0, nqb - 1), 0))

    cache_spec = pl.BlockSpec((None, None, PAST_LEN, LANES), lambda b, kp, qb: (b, li, 0, kp))
    own = rows(0)
    return pl.pallas_call(
        _attn_lat_kernel,
        out_shape=jax.ShapeDtypeStruct((SAMPLE.rows, D_MODEL), BF16),
        grid=(DEC_BATCH, C_KV_HEADS // 2, nqb),
        in_specs=[pl.BlockSpec(memory_space=pltpu.SMEM),
                  pl.BlockSpec((BLOCK, qw), lambda b, kp, qb: (own(b, kp, qb), kp)),
                  kv_spec(k_cb, -1), kv_spec(k_cb, 0), kv_spec(k_cb, 1),
                  kv_spec(v_cb, -1), kv_spec(v_cb, 0), kv_spec(v_cb, 1),
                  cache_spec, cache_spec,
                  tab_spec(0), tab_spec(0), tab_spec(-1), tab_spec(-1), tab_spec(1), tab_spec(1),
                  pl.BlockSpec((BLOCK, qw), lambda b, kp, qb: (own(b, kp, qb), gate_cb + kp))],
        out_specs=pl.BlockSpec((BLOCK, qw), lambda b, kp, qb: (own(b, kp, qb), kp)),
        compiler_params=_cparams(("parallel", "parallel", "parallel")),
        name="attn_latent",
    )(sink, proj, proj, proj, proj, proj, proj, proj, cache_k4, cache_v4,
      cos_t, sin_t, cos_t, sin_t, cos_t, sin_t, proj)


def _rope_tables():
    t = jnp.arange(DEC_SEQ, dtype=jnp.int32)
    row = (t // GRID_W).astype(F32)
    col = (t % GRID_W).astype(F32)
    nf = ROPE_PAIR
    inv = 1.0 / (ROPE_BASE ** (jnp.arange(nf, dtype=F32) / nf))
    lane = np.arange(LANES)
    f_of_lane = lane % nf
    use_col = (lane % C_HEAD) >= C_HEAD // 2
    sign = np.where((lane % (2 * ROPE_PAIR)) < ROPE_PAIR, -1.0, 1.0).astype(np.float32)
    pos = jnp.where(jnp.asarray(use_col)[None, :], col[:, None], row[:, None])
    ang = pos * inv[jnp.asarray(f_of_lane)][None, :]
    return jnp.cos(ang), jnp.sin(ang) * jnp.asarray(sign)[None, :]


def _chain_const_tiles(vec):
    t = vec.reshape(A_HEADS // 2, 2, A_HEAD).transpose(1, 2, 0)
    t = jnp.broadcast_to(t[:, :, None, :, None], (2, A_HEAD, 2, A_HEADS // 2, SEQ_PER_GROUP))
    return t.reshape(2, A_HEAD, CHAINS)


def _even_mixer(st, proj, s0, p, gmat, cs_mat, dft):
    r, k, v, w2, a2, bonus = _wkv_prep(st, proj, p["mu"], p["w0"], p["wup"], p["a0"], p["aup"], p["k_a"], p["r_k"],
                                       gmat)
    o_f, o_b, s_fin = _wkv(r, k, v, w2, a2, p["kkw_t"], p["ka_t"], s0)
    ya = _ya(st, o_f, o_b, bonus, proj, p["gn_w"], p["gn_b"], gmat)
    zc, zs = _dft_channels(st, proj, cs_mat)
    yb = _dft_time(st, zc, zs, proj, dft[0], dft[1])
    return ya, yb, s_fin


def kernel(x_prompt, x_sample, c, state_wkv, cache_k, cache_v, c_ctx, mod_w, mod_b, norm_pre, norm_post,
           even_w_in, even_mu, even_w0, even_w_up, even_a0, even_a_up, even_k_k, even_k_a, even_r_k,
           even_gn_w, even_gn_b, even_w_out, odd_w_in, odd_sink, odd_w_out):
    n_odd = odd_w_in.shape[0]
    xs = {PROMPT: x_prompt.reshape(PROMPT.rows, D_MODEL), SAMPLE: x_sample.reshape(SAMPLE.rows, D_MODEL)}
    conds = jnp.concatenate([c_ctx[None, :], c, jnp.zeros((N_COND - 1 - DEC_BATCH, D_MODEL), F32)], axis=0)
    mods4 = _modulation(conds, mod_w, mod_b).reshape(DEPTH, N_COND, 1, 3 * D_MODEL)
    norm_pre3 = norm_pre.reshape(DEPTH, 1, D_MODEL)
    norm_post3 = norm_post.reshape(DEPTH, 1, D_MODEL)

    even_w_in_b = even_w_in.astype(BF16)
    even_w_out_b = even_w_out.astype(BF16)
    odd_w_in_b = odd_w_in.astype(BF16)
    odd_w_out_b = odd_w_out.astype(BF16)

    head_of_lane = np.arange(CW) // A_HEAD
    gmat = jnp.asarray((head_of_lane[:, None] == head_of_lane[None, :]).astype(np.float32)).astype(BF16)
    c_ch, s_ch = _dft_mats(B_GROUP_CH)
    cs_mat = jnp.concatenate([c_ch, s_ch], axis=1).astype(BF16)
    dft = {PROMPT: tuple(m.astype(BF16) for m in _dft_mats(SEQ)),
           SAMPLE: tuple(m.astype(BF16) for m in _dft_mats(DEC_SEQ))}
    cos_t, sin_t = _rope_tables()
    cache_k4 = cache_k.reshape(DEC_BATCH, n_odd, PAST_LEN, C_KV_HEADS * C_HEAD)
    cache_v4 = cache_v.reshape(DEC_BATCH, n_odd, PAST_LEN, C_KV_HEADS * C_HEAD)

    def pad_rows(w, slot):
        return jnp.pad(w, ((0, 0), (slot * LORA, (3 - slot) * LORA), (0, 0)))

    wup_pad = jnp.stack([pad_rows(even_w_up[:, 0], 0), pad_rows(even_w_up[:, 1], 1)], axis=1)
    aup_pad = jnp.stack([pad_rows(even_a_up[:, 0], 2), pad_rows(even_a_up[:, 1], 3)], axis=1)

    new_wkv, new_k, new_v = [], [], []
    hs = {st: _pre_norm(st, xs[st], norm_pre3, mods4, 0) for st in (PROMPT, SAMPLE)}
    for layer in range(DEPTH):
        i = layer // 2
        if layer % 2 == 0:
            p = {
                "mu": even_mu[i][None, :], "w0": even_w0[i], "a0": even_a0[i],
                "wup": wup_pad[i], "aup": aup_pad[i],
                "k_a": even_k_a[i][None, :], "r_k": even_r_k[i].reshape(1, A_WIDTH),
                "kkw_t": _chain_const_tiles(even_k_k[i]), "ka_t": _chain_const_tiles(even_k_a[i]),
                "gn_w": even_gn_w[i][None, :], "gn_b": even_gn_b[i][None, :],
            }
            s0s = state_wkv[:, i].reshape(DEC_BATCH, 2, A_HEADS // 2, 2, A_HEAD, A_HEAD)
            s0 = {PROMPT: jnp.zeros((BATCH // SEQ_PER_GROUP, 2, A_HEAD, A_HEAD, CHAINS), F32),
                  SAMPLE: s0s.transpose(3, 5, 4, 1, 2, 0).reshape(1, 2, A_HEAD, A_HEAD, CHAINS)}
            for st in (PROMPT, SAMPLE):
                proj = _in_proj(st, hs[st], even_w_in_b, i)
                ya, yb, s_fin = _even_mixer(st, proj, s0[st], p, gmat, cs_mat, dft[st])
                xs[st], hs[st] = _out_proj(st, ya, 0, yb, 0, even_w_out_b, i, xs[st], mods4, norm_post3, norm_pre3,
                                           layer)
                if st is PROMPT:
                    s_fin = s_fin.reshape(BATCH // SEQ_PER_GROUP, 2, A_HEAD, A_HEAD, 2, A_HEADS // 2, SEQ_PER_GROUP)
                    new_wkv.append(s_fin.transpose(0, 6, 4, 5, 1, 3, 2).reshape(BATCH, 2, A_HEADS, A_HEAD, A_HEAD))
        else:
            proj_p = _in_proj(PROMPT, hs[PROMPT], odd_w_in_b, i)
            proj_s = _in_proj(SAMPLE, hs[SAMPLE], odd_w_in_b, i)
            kv0 = C_HEADS * C_HEAD
            kvn = C_KV_HEADS * C_HEAD
            new_k.append(proj_p[:, kv0:kv0 + kvn].reshape(BATCH, SEQ, C_KV_HEADS, C_HEAD))
            new_v.append(proj_p[:, kv0 + kvn:kv0 + 2 * kvn].reshape(BATCH, SEQ, C_KV_HEADS, C_HEAD))
            att_p = _attn_context(proj_p, odd_sink[i])
            att_s = _attn_latent(proj_s, cache_k4, cache_v4, i, odd_sink[i], cos_t, sin_t)
            for st, att in ((PROMPT, att_p), (SAMPLE, att_s)):
                xs[st], hs[st] = _out_proj(st, att, 0, att, 1, odd_w_out_b, i, xs[st], mods4, norm_post3, norm_pre3,
                                           layer)

    y_prompt = xs[PROMPT].reshape(BATCH, SEQ, D_MODEL)
    y_sample = xs[SAMPLE].reshape(DEC_BATCH, DEC_SEQ, D_MODEL)
    return (y_prompt, y_sample, jnp.stack(new_wkv, axis=1), jnp.stack(new_k, axis=1), jnp.stack(new_v, axis=1))
```

```python
import collections
import functools

import numpy as np
import jax
import jax.numpy as jnp
from jax import lax
from jax.experimental import pallas as pl
from jax.experimental.pallas import tpu as pltpu

F32 = jnp.float32
BF16 = jnp.bfloat16

D_MODEL = 2048
BATCH = 32
SEQ = 256
DEPTH = 4
DEC_BATCH = 8
DEC_SEQ = 2048
PAST_LEN = 256
GRID_W = 64
RMS_EPS = 1e-6
A_WIDTH = 1024
A_HEAD = 64
A_HEADS = 16
LORA = 64
GN_EPS = 64e-5
DECAY_SCALE = float(np.exp(-0.5))
B_WIDTH = 1024
B_GROUPS = 4
B_GROUP_CH = 256
SHIFT_COLS = 3 * A_WIDTH + 4 * LORA
EVEN_IN = SHIFT_COLS + A_WIDTH + 2 * B_WIDTH
C_HEAD = 64
C_HEADS = 32
C_KV_HEADS = 8
C_GROUP = 4
WINDOW = 128
BLOCK = 128
ROPE_BASE = 10000.0
ROPE_PAIR = C_HEAD // 4
KK_NORM_FLOOR = 1e-12
ODD_IN = (C_HEADS + 2 * C_KV_HEADS) * C_HEAD + D_MODEL
NEG_INF = -1e30

N_COND = 16
SUBLANES = 8
LANES = 128
CHAINS = LANES
SEQ_PER_GROUP = CHAINS // A_HEADS
CW = 256
TM_IN = 1024
TN_IN = 1280
TM_ROW = 256
TM_OUT = 512
OUT_SUBTILES = 2
PRE_SUBTILES = 8
WKV_TT = 32
BIG_VMEM_LIMIT = 56 * 1024 * 1024
WKV_ROWS = 16
WKV_ACCS = 2
VMEM_LIMIT = 48 * 1024 * 1024

Stream = collections.namedtuple("Stream", "rows t_len n_seq cond0 cond_per_seq")
PROMPT = Stream(BATCH * SEQ, SEQ, BATCH, 0, 0)
SAMPLE = Stream(DEC_BATCH * DEC_SEQ, DEC_SEQ, DEC_BATCH, 1, 1)


def _cparams(sem):
    return pltpu.CompilerParams(dimension_semantics=sem, vmem_limit_bytes=VMEM_LIMIT)


def _cond_of_row(st, row0):
    return st.cond0 + st.cond_per_seq * (row0 // st.t_len)


def _sigmoid(x):
    return 1.0 / (1.0 + jnp.exp(-x))


def _silu(x):
    return x * _sigmoid(x)


def _split(a):
    hi = a.astype(BF16)
    lo = (a - hi.astype(F32)).astype(BF16)
    return hi, lo


def _dot(a, b):
    return jnp.dot(a, b, preferred_element_type=F32)


def _dot3_split(a_hl, b_hl):
    (ah, al), (bh, bl) = a_hl, b_hl
    return _dot(ah, bh) + (_dot(ah, bl) + _dot(al, bh))


def _dot3(a, b):
    return _dot3_split(_split(a), _split(b))


def _gsum(x, gmat):
    xh, xl = _split(x)
    return _dot(xh, gmat) + _dot(xl, gmat)


def _mod_kernel(c_ref, w_ref, b_ref, o_ref):
    o_ref[...] = _dot3(_silu(c_ref[...]), w_ref[...]) + b_ref[...]


def _modulation(conds, mod_w, mod_b):
    tn = 512
    n = 3 * D_MODEL
    return pl.pallas_call(
        _mod_kernel,
        out_shape=jax.ShapeDtypeStruct((DEPTH, N_COND, n), F32),
        grid=(DEPTH, n // tn),
        in_specs=[
            pl.BlockSpec((N_COND, D_MODEL), lambda l, j: (0, 0)),
            pl.BlockSpec((None, D_MODEL, tn), lambda l, j: (l, 0, j)),
            pl.BlockSpec((None, 1, tn), lambda l, j: (l, 0, j)),
        ],
        out_specs=pl.BlockSpec((None, N_COND, tn), lambda l, j: (l, 0, j)),
        compiler_params=_cparams(("parallel", "parallel")),
        name="modulation",
    )(conds, mod_w, mod_b.reshape(DEPTH, 1, n))


def _norm_modulate(x, g, sh, sc):
    ms = jnp.mean(x * x, axis=-1, keepdims=True)
    y = x * lax.rsqrt(ms + RMS_EPS) * g
    return (y * (1.0 + sc) + sh).astype(BF16)


def _pre_kernel(x_ref, g_ref, sh_ref, sc_ref, h_ref):
    sub = x_ref.shape[0] // PRE_SUBTILES
    for s in range(PRE_SUBTILES):
        rows = slice(s * sub, (s + 1) * sub)
        h_ref[rows, :] = _norm_modulate(x_ref[rows, :], g_ref[...], sh_ref[...], sc_ref[...])


def _pre_norm(st, x, norm_g, mods4, layer):
    tm = TM_ROW
    cond = lambda i: _cond_of_row(st, i * tm)
    return pl.pallas_call(
        _pre_kernel,
        out_shape=jax.ShapeDtypeStruct((st.rows, D_MODEL), BF16),
        grid=(st.rows // tm,),
        in_specs=[pl.BlockSpec((tm, D_MODEL), lambda i: (i, 0)),
                  pl.BlockSpec((None, 1, D_MODEL), lambda i: (layer, 0, 0)),
                  pl.BlockSpec((None, None, 1, D_MODEL), lambda i: (layer, cond(i), 0, 0)),
                  pl.BlockSpec((None, None, 1, D_MODEL), lambda i: (layer, cond(i), 0, 1))],
        out_specs=pl.BlockSpec((tm, D_MODEL), lambda i: (i, 0)),
        compiler_params=_cparams(("parallel",)),
        name="pre_norm",
    )(x, norm_g, mods4, mods4)


def _in_kernel(h_ref, w_ref, o_ref, wb_ref):
    @pl.when(pl.program_id(1) == 0)
    def _():
        wb_ref[...] = w_ref[...].astype(BF16)

    o_ref[...] = _dot(h_ref[...], wb_ref[...])


def _in_proj(st, h, w, li):
    n = w.shape[-1]
    return pl.pallas_call(
        _in_kernel,
        out_shape=jax.ShapeDtypeStruct((st.rows, n), F32),
        grid=(n // TN_IN, st.rows // TM_IN),
        in_specs=[pl.BlockSpec((TM_IN, D_MODEL), lambda j, i: (i, 0)),
                  pl.BlockSpec((None, D_MODEL, TN_IN), lambda j, i: (li, 0, j))],
        out_specs=pl.BlockSpec((TM_IN, TN_IN), lambda j, i: (i, j)),
        scratch_shapes=[pltpu.VMEM((D_MODEL, TN_IN), BF16)],
        compiler_params=pltpu.CompilerParams(dimension_semantics=("parallel", "arbitrary"),
                                             vmem_limit_bytes=BIG_VMEM_LIMIT),
        name="in_proj",
    )(h, w)


def _token_shift(cur, prev8, next8, mu, first, last):
    tm = cur.shape[0]
    rows = lax.broadcasted_iota(jnp.int32, cur.shape, 0)
    prow = jnp.where(first, 0.0, prev8[7:8, :])
    nrow = jnp.where(last, 0.0, next8[0:1, :])
    up = jnp.where(rows == 0, prow, pltpu.roll(cur, 1, axis=0))
    dn = jnp.where(rows == tm - 1, nrow, pltpu.roll(cur, tm - 1, axis=0))
    return cur + mu * (0.5 * (up + dn) - cur)


def _prep_kernel(r_ref, k_ref, v_ref, lo_ref, rp_ref, kp_ref, vp_ref, lp_ref, rn_ref, kn_ref, vn_ref, ln_ref,
                 mur_ref, muk_ref, muv_ref, mul_ref, w0_ref, wuph_ref, wupl_ref, a0_ref, auph_ref, aupl_ref,
                 ka_ref, rk_ref, gm_ref,
                 r_o, k_o, v_o, w_o, a_o, bon_o, *, tiles_per_seq):
    i = pl.program_id(0)
    first = lax.rem(i, tiles_per_seq) == 0
    last = lax.rem(i, tiles_per_seq) == tiles_per_seq - 1

    low = _token_shift(lo_ref[...], lp_ref[...], ln_ref[...], mul_ref[...], first, last)
    low_hl = _split(low)
    low_t_hl = _split(jnp.tanh(low))
    gm = gm_ref[...]

    for c in range(A_WIDTH // CW):
        cols = slice(c * CW, (c + 1) * CW)
        r = _token_shift(r_ref[:, cols], rp_ref[:, cols], rn_ref[:, cols], mur_ref[:, cols], first, last)
        k = _token_shift(k_ref[:, cols], kp_ref[:, cols], kn_ref[:, cols], muk_ref[:, cols], first, last)
        v = _token_shift(v_ref[:, cols], vp_ref[:, cols], vn_ref[:, cols], muv_ref[:, cols], first, last)
        ka = ka_ref[:, cols]
        kd_sum = None
        for d in range(2):
            w_raw = w0_ref[d:d + 1, cols] + _dot3_split(low_t_hl, (wuph_ref[d, :, cols], wupl_ref[d, :, cols]))
            a = _sigmoid(a0_ref[d:d + 1, cols] + _dot3_split(low_hl, (auph_ref[d, :, cols], aupl_ref[d, :, cols])))
            kd = k * (1.0 + (a - 1.0) * ka)
            w_o[d, :, cols] = jnp.exp(-DECAY_SCALE * _sigmoid(w_raw))
            a_o[d, :, cols] = a
            kd_sum = kd if kd_sum is None else kd_sum + kd
        r_o[:, cols] = r
        k_o[:, cols] = k
        v_o[:, cols] = v
        bon_o[:, cols] = _gsum(r * (0.5 * kd_sum) * rk_ref[:, cols], gm) * v


def _wkv_prep(st, proj, mu, w0, wup_pad, a0, aup_pad, k_a, r_k, gmat):
    tm = TM_ROW
    nb8 = st.rows // SUBLANES
    low_cb = 3 * A_WIDTH // CW
    assert 4 * LORA == CW and st.t_len % tm == 0

    prev_row = lambda i: jnp.maximum(i * (tm // SUBLANES) - 1, 0)
    next_row = lambda i: jnp.minimum((i + 1) * (tm // SUBLANES), nb8 - 1)

    def main(cb):
        return pl.BlockSpec((tm, A_WIDTH), lambda i: (i, cb))

    def prev(cb):
        return pl.BlockSpec((SUBLANES, A_WIDTH), lambda i: (prev_row(i), cb))

    def nxt(cb):
        return pl.BlockSpec((SUBLANES, A_WIDTH), lambda i: (next_row(i), cb))

    def fixed(spec_fn):
        return [spec_fn(0), spec_fn(1), spec_fn(2)]

    low_main = pl.BlockSpec((tm, CW), lambda i: (i, low_cb))
    low_prev = pl.BlockSpec((SUBLANES, CW), lambda i: (prev_row(i), low_cb))
    low_next = pl.BlockSpec((SUBLANES, CW), lambda i: (next_row(i), low_cb))

    def vec(cb):
        return pl.BlockSpec((1, A_WIDTH), lambda i: (0, cb))

    in_specs = (fixed(main) + [low_main] + fixed(prev) + [low_prev] + fixed(nxt) + [low_next]
                + [vec(0), vec(1), vec(2), pl.BlockSpec((1, CW), lambda i: (0, low_cb))]
                + [pl.BlockSpec((2, A_WIDTH), lambda i: (0, 0)),
                   pl.BlockSpec((2, CW, A_WIDTH), lambda i: (0, 0, 0)),
                   pl.BlockSpec((2, CW, A_WIDTH), lambda i: (0, 0, 0)),
                   pl.BlockSpec((2, A_WIDTH), lambda i: (0, 0)),
                   pl.BlockSpec((2, CW, A_WIDTH), lambda i: (0, 0, 0)),
                   pl.BlockSpec((2, CW, A_WIDTH), lambda i: (0, 0, 0)),
                   vec(0), vec(0),
                   pl.BlockSpec((CW, CW), lambda i: (0, 0))])
    tps = st.t_len // tm
    g_n = st.n_seq // SEQ_PER_GROUP

    def til(i):
        s = i // tps
        return (s // SEQ_PER_GROUP, lax.rem(i, tps), lax.rem(s, SEQ_PER_GROUP))

    one = pl.BlockSpec((tm, A_WIDTH), lambda i: (i, 0))
    til1 = pl.BlockSpec((None, tm, A_WIDTH), lambda i: til(i))
    til2 = pl.BlockSpec((2, None, tm, A_WIDTH), lambda i: (0,) + til(i))
    one_sds = jax.ShapeDtypeStruct((st.rows, A_WIDTH), F32)
    til1_sds = jax.ShapeDtypeStruct((g_n, st.t_len, SEQ_PER_GROUP * A_WIDTH), F32)
    til2_sds = jax.ShapeDtypeStruct((2, g_n, st.t_len, SEQ_PER_GROUP * A_WIDTH), F32)
    return pl.pallas_call(
        functools.partial(_prep_kernel, tiles_per_seq=tps),
        out_shape=[til1_sds, til1_sds, til1_sds, til2_sds, til2_sds, one_sds],
        grid=(st.rows // tm,),
        in_specs=in_specs,
        out_specs=[til1, til1, til1, til2, til2, one],
        compiler_params=_cparams(("parallel",)),
        name="wkv_prep",
    )(*([proj] * 12), mu, mu, mu, mu, w0, *_split(wup_pad), a0, *_split(aup_pad), k_a, r_k, gmat)


def _wkv_kernel(rf_ref, rb_ref, kf_ref, kb_ref, vf_ref, vb_ref, wf_ref, wb_ref, af_ref, ab_ref, kkw_ref, ka_ref,
                s0_ref, of_ref, ob_ref, sf_ref,
                s2_ref, r2_ref, kk2_ref, v2_ref, kka2_ref, kd2_ref, g_ref, o_ref, *, tt_steps):
    c = pl.program_id(1)
    nblk = WKV_ROWS // SUBLANES
    n_hv = A_HEADS // 2
    pairs = ((rf_ref, rb_ref), (kf_ref, kb_ref), (vf_ref, vb_ref), (wf_ref, wb_ref), (af_ref, ab_ref))

    @pl.when(c == 0)
    def _():
        s2_ref[...] = s0_ref[...]

    def to_chain_tiles(u, gammas):
        ub = tt_steps - 1 - u
        halves = []
        for xf_ref, xb_ref in pairs:
            x2 = jnp.concatenate([xf_ref[u, :, hv * LANES:(hv + 1) * LANES] for hv in range(n_hv)]
                                 + [xb_ref[ub, :, hv * LANES:(hv + 1) * LANES] for hv in range(n_hv)], axis=0)
            y = x2.T
            halves.append((y[:A_HEAD], y[A_HEAD:]))
        r_h, k_h, v_h, w_h, a_h = halves
        new_gammas = []
        for hp in range(2):
            kk = k_h[hp] * kkw_ref[hp]
            norm = jnp.sqrt(jnp.sum(kk * kk, axis=0, keepdims=True))
            kk = kk / jnp.maximum(norm, KK_NORM_FLOOR)
            kka = kk * a_h[hp]
            kd = k_h[hp] * (1.0 + (a_h[hp] - 1.0) * ka_ref[hp])
            gamma = gammas[hp] * w_h[hp]
            inv = 1.0 / gamma
            kk2_ref[hp, u] = gammas[hp] * kk
            kka2_ref[hp, u] = kka * inv
            kd2_ref[hp, u] = kd * inv
            r2_ref[hp, u] = r_h[hp] * gamma
            v2_ref[hp, u] = v_h[hp]
            new_gammas.append(gamma)
        return tuple(new_gammas)

    one = jnp.ones((A_HEAD, CHAINS), F32)
    gammas = lax.fori_loop(0, tt_steps, to_chain_tiles, (one, one), unroll=4)
    for hp in range(2):
        g_ref[hp] = gammas[hp]

    def bcast_row(ref, t, j):
        return ref[t, pl.ds(j, SUBLANES, stride=0), :]

    def time_step(hp, t):
        s_ref = s2_ref.at[hp]
        r_ref, kk_ref, v_ref, kka_ref, kd_ref = (ref.at[hp] for ref in (r2_ref, kk2_ref, v2_ref, kka2_ref, kd2_ref))
        for ib in range(A_HEAD // WKV_ROWS):
            i0 = ib * WKV_ROWS
            rows = [pl.ds(i0 + SUBLANES * b, SUBLANES) for b in range(nblk)]
            sa = [[None] * WKV_ACCS for _ in range(nblk)]
            for j in range(A_HEAD):
                kkj = bcast_row(kk_ref, t, j)
                for b in range(nblk):
                    p = s_ref[j, rows[b], :] * kkj
                    sa[b][j % WKV_ACCS] = p if sa[b][j % WKV_ACCS] is None else sa[b][j % WKV_ACCS] + p
            sa = [functools.reduce(lambda x, y: x + y, parts) for parts in sa]
            v8 = [v_ref[t, rows[b], :] for b in range(nblk)]
            out = [[None] * WKV_ACCS for _ in range(nblk)]
            for j in range(A_HEAD):
                kkaj = bcast_row(kka_ref, t, j)
                kdj = bcast_row(kd_ref, t, j)
                rj = bcast_row(r_ref, t, j)
                for b in range(nblk):
                    sn = s_ref[j, rows[b], :] - sa[b] * kkaj + v8[b] * kdj
                    s_ref[j, rows[b], :] = sn
                    q = sn * rj
                    out[b][j % WKV_ACCS] = q if out[b][j % WKV_ACCS] is None else out[b][j % WKV_ACCS] + q
            for b in range(nblk):
                o_ref[hp, rows[b], :] = functools.reduce(lambda x, y: x + y, out[b])

    def step(u, carry):
        for hp in range(2):
            time_step(hp, u)
        x2 = jnp.concatenate([o_ref[0], o_ref[1]], axis=0).T
        ub = tt_steps - 1 - u
        for hv in range(n_hv):
            of_ref[u, :, hv * LANES:(hv + 1) * LANES] = x2[hv * SUBLANES:(hv + 1) * SUBLANES, :]
            ob_ref[ub, :, hv * LANES:(hv + 1) * LANES] = x2[(n_hv + hv) * SUBLANES:(n_hv + hv + 1) * SUBLANES, :]
        return carry

    lax.fori_loop(0, tt_steps, step, 0)

    def rescale(j, carry):
        for hp in range(2):
            gj = g_ref[hp, pl.ds(j, SUBLANES, stride=0), :]
            for b in range(A_HEAD // SUBLANES):
                rows = pl.ds(b * SUBLANES, SUBLANES)
                s2_ref[hp, j, rows, :] = s2_ref[hp, j, rows, :] * gj
        return carry

    lax.fori_loop(0, A_HEAD, rescale, 0)

    @pl.when(c == pl.num_programs(1) - 1)
    def _():
        sf_ref[...] = s2_ref[...]


def _wkv(r, k, v, w, a, kkw_t, ka_t, s0):
    g_n, t_n = r.shape[0], r.shape[1]
    tt = WKV_TT
    nc = t_n // tt
    split = lambda a: a.reshape(a.shape[:-1] + (SEQ_PER_GROUP, A_WIDTH))
    blk = (tt, SEQ_PER_GROUP, A_WIDTH)

    fwd = pl.BlockSpec((None,) + blk, lambda g, c: (g, c, 0, 0))
    bwd = pl.BlockSpec((None,) + blk, lambda g, c: (g, nc - 1 - c, 0, 0))
    fwd_d = pl.BlockSpec((None, None) + blk, lambda g, c: (0, g, c, 0, 0))
    bwd_d = pl.BlockSpec((None, None) + blk, lambda g, c: (1, g, nc - 1 - c, 0, 0))
    state = pl.BlockSpec((None, 2, A_HEAD, A_HEAD, CHAINS), lambda g, c: (g, 0, 0, 0, 0),
                         pipeline_mode=pl.Buffered(1))
    tiles = pltpu.VMEM((2, tt, A_HEAD, CHAINS), F32)
    o_sds = jax.ShapeDtypeStruct((g_n, t_n, SEQ_PER_GROUP, A_WIDTH), F32)
    r, k, v, w, a = (split(x) for x in (r, k, v, w, a))
    const = pl.BlockSpec((2, A_HEAD, CHAINS), lambda g, c: (0, 0, 0))
    o_f, o_b, s_fin = pl.pallas_call(
        functools.partial(_wkv_kernel, tt_steps=tt),
        out_shape=[o_sds, o_sds, jax.ShapeDtypeStruct((g_n, 2, A_HEAD, A_HEAD, CHAINS), F32)],
        grid=(g_n, nc),
        in_specs=[fwd, bwd, fwd, bwd, fwd, bwd, fwd_d, bwd_d, fwd_d, bwd_d, const, const, state],
        out_specs=[fwd, bwd, state],
        scratch_shapes=([pltpu.VMEM((2, A_HEAD, A_HEAD, CHAINS), F32)] + [tiles] * 5
                        + [pltpu.VMEM((2, A_HEAD, CHAINS), F32)] * 2),
        compiler_params=pltpu.CompilerParams(dimension_semantics=("parallel", "arbitrary"),
                                             vmem_limit_bytes=BIG_VMEM_LIMIT),
        name="wkv",
    )(r, r, k, k, v, v, w, w, a, a, kkw_t, ka_t, s0)
    merge = lambda a: a.reshape(g_n, t_n, SEQ_PER_GROUP * A_WIDTH)
    return merge(o_f), merge(o_b), s_fin


def _ya_kernel(of_ref, ob_ref, bon_ref, g0_ref, g1_ref, g2_ref, g3_ref, gw_ref, gb_ref, gm_ref, o_ref):
    gm = gm_ref[...]
    for c, gate_ref in enumerate((g0_ref, g1_ref, g2_ref, g3_ref)):
        cols = slice(c * CW, (c + 1) * CW)
        o = of_ref[:, cols] + ob_ref[:, cols]
        mean = _gsum(o, gm) * (1.0 / A_HEAD)
        dev = o - mean
        var = _gsum(dev * dev, gm) * (1.0 / A_HEAD)
        y = dev * lax.rsqrt(var + GN_EPS) * gw_ref[:, cols] + gb_ref[:, cols]
        o_ref[:, cols] = ((y + bon_ref[:, cols]) * _silu(gate_ref[...])).astype(BF16)


def _ya(st, o_f, o_b, bonus, proj, gn_w, gn_b, gmat):
    tm = TM_ROW
    gate_cb = (SHIFT_COLS) // CW
    blk = pl.BlockSpec((tm, A_WIDTH), lambda i: (i, 0))
    vec = pl.BlockSpec((1, A_WIDTH), lambda i: (0, 0))
    tps = st.t_len // tm

    def til_idx(i):
        s = i // tps
        return (s // SEQ_PER_GROUP, lax.rem(i, tps), lax.rem(s, SEQ_PER_GROUP))

    til = pl.BlockSpec((None, tm, A_WIDTH), til_idx)
    gates = [pl.BlockSpec((tm, CW), lambda i, c=c: (i, gate_cb + c)) for c in range(A_WIDTH // CW)]
    return pl.pallas_call(
        _ya_kernel,
        out_shape=jax.ShapeDtypeStruct((st.rows, A_WIDTH), BF16),
        grid=(st.rows // tm,),
        in_specs=[til, til, blk] + gates + [vec, vec, pl.BlockSpec((CW, CW), lambda i: (0, 0))],
        out_specs=blk,
        compiler_params=_cparams(("parallel",)),
        name="wkv_post",
    )(o_f, o_b, bonus, proj, proj, proj, proj, gn_w, gn_b, gmat)


def _dft_ch_kernel(u_ref, cs_ref, zc_ref, zs_ref):
    z = _dot(u_ref[...].astype(BF16), cs_ref[...])
    zc_ref[...] = z[:, :B_GROUP_CH].astype(BF16)
    zs_ref[...] = z[:, B_GROUP_CH:].astype(BF16)


def _dft_channels(st, proj, cs_mat):
    tm = 2048
    u_cb = (SHIFT_COLS + A_WIDTH) // B_GROUP_CH
    blk = pl.BlockSpec((tm, B_GROUP_CH), lambda i, g: (i, g))
    sds = jax.ShapeDtypeStruct((st.rows, B_WIDTH), BF16)
    return pl.pallas_call(
        _dft_ch_kernel,
        out_shape=[sds, sds],
        grid=(st.rows // tm, B_GROUPS),
        in_specs=[pl.BlockSpec((tm, B_GROUP_CH), lambda i, g: (i, u_cb + g)),
                  pl.BlockSpec((B_GROUP_CH, 2 * B_GROUP_CH), lambda i, g: (0, 0))],
        out_specs=[blk, blk],
        compiler_params=_cparams(("parallel", "parallel")),
        name="dft_channels",
    )(proj, cs_mat)


def _dft_time_kernel(c_ref, s_ref, zc_ref, zs_ref, g0_ref, g1_ref, g2_ref, g3_ref, o_ref, *, scale):
    tm = o_ref.shape[0]
    rows = pl.ds(pl.multiple_of(pl.program_id(1) * tm, tm), tm)
    acc = _dot(c_ref[rows, :], zc_ref[...]) - _dot(s_ref[rows, :], zs_ref[...])
    for g, gate_ref in enumerate((g0_ref, g1_ref, g2_ref, g3_ref)):
        cols = slice(g * B_GROUP_CH, (g + 1) * B_GROUP_CH)
        o_ref[:, cols] = (acc[:, cols] * scale * _silu(gate_ref[...])).astype(BF16)


def _dft_time(st, zc, zs, proj, cmat, smat):
    t_len = st.t_len
    tm = min(t_len, 512)
    mt = t_len // tm
    whole = pl.BlockSpec((t_len, t_len), lambda b, m: (0, 0), pipeline_mode=pl.Buffered(1))
    gate_cb = (SHIFT_COLS + A_WIDTH + B_WIDTH) // B_GROUP_CH
    gate_specs = [pl.BlockSpec((tm, B_GROUP_CH), lambda b, m, g=g: (b * mt + m, gate_cb + g)) for g in range(B_GROUPS)]
    scale = 1.0 / float(np.sqrt(t_len * B_GROUP_CH))
    return pl.pallas_call(
        functools.partial(_dft_time_kernel, scale=scale),
        out_shape=jax.ShapeDtypeStruct((st.rows, B_WIDTH), BF16),
        grid=(st.n_seq, mt),
        in_specs=[whole, whole,
                  pl.BlockSpec((t_len, B_WIDTH), lambda b, m: (b, 0)),
                  pl.BlockSpec((t_len, B_WIDTH), lambda b, m: (b, 0))] + gate_specs,
        out_specs=pl.BlockSpec((tm, B_WIDTH), lambda b, m: (b * mt + m, 0)),
        compiler_params=_cparams(("parallel", "parallel")),
        name="dft_time",
    )(cmat, smat, zc, zs, proj, proj, proj, proj)


def _dft_mats(n):
    idx = jnp.arange(n, dtype=jnp.int32)
    prod = (idx[:, None] * idx[None, :]) % n
    ang = prod.astype(F32) * (2.0 * np.pi / n)
    return jnp.cos(ang), jnp.sin(ang)


def _out_kernel(a1_ref, a2_ref, w1_ref, w2_ref, x_ref, gate_ref, g_ref, *rest, emit_next):
    if emit_next:
        ng_ref, nsh_ref, nsc_ref, o_ref, h_ref = rest
    else:
        o_ref, = rest
    sub = x_ref.shape[0] // OUT_SUBTILES
    for s in range(OUT_SUBTILES):
        rows = slice(s * sub, (s + 1) * sub)
        y = _dot(a1_ref[rows, :], w1_ref[...]) + _dot(a2_ref[rows, :], w2_ref[...])
        ms = jnp.mean(y * y, axis=-1, keepdims=True)
        yn = y * lax.rsqrt(ms + RMS_EPS) * g_ref[...]
        x_new = x_ref[rows, :] + gate_ref[...] * yn
        if emit_next:
            h_ref[rows, :] = _norm_modulate(x_new, ng_ref[...], nsh_ref[...], nsc_ref[...])
        o_ref[rows, :] = x_new


def _out_proj(st, a1, a1_cb, a2, a2_cb, w_bf16, li, x, mods4, norm_post, norm_pre, layer):
    tm = TM_OUT
    half = D_MODEL // 2
    cond = lambda i: _cond_of_row(st, i * tm)
    emit_next = layer + 1 < DEPTH
    row = pl.BlockSpec((tm, D_MODEL), lambda i: (i, 0))
    in_specs = [pl.BlockSpec((tm, half), lambda i: (i, a1_cb)),
                pl.BlockSpec((tm, half), lambda i: (i, a2_cb)),
                pl.BlockSpec((None, half, D_MODEL), lambda i: (li, 0, 0)),
                pl.BlockSpec((None, half, D_MODEL), lambda i: (li, 1, 0)),
                row,
                pl.BlockSpec((None, None, 1, D_MODEL), lambda i: (layer, cond(i), 0, 2)),
                pl.BlockSpec((None, 1, D_MODEL), lambda i: (layer, 0, 0))]
    args = [a1, a2, w_bf16, w_bf16, x, mods4, norm_post]
    out_shape = [jax.ShapeDtypeStruct((st.rows, D_MODEL), F32)]
    out_specs = [row]
    if emit_next:
        in_specs += [pl.BlockSpec((None, 1, D_MODEL), lambda i: (layer + 1, 0, 0)),
                     pl.BlockSpec((None, None, 1, D_MODEL), lambda i: (layer + 1, cond(i), 0, 0)),
                     pl.BlockSpec((None, None, 1, D_MODEL), lambda i: (layer + 1, cond(i), 0, 1))]
        args += [norm_pre, mods4, mods4]
        out_shape.append(jax.ShapeDtypeStruct((st.rows, D_MODEL), BF16))
        out_specs.append(row)
    outs = pl.pallas_call(
        functools.partial(_out_kernel, emit_next=emit_next),
        out_shape=out_shape,
        grid=(st.rows // tm,),
        in_specs=in_specs,
        out_specs=out_specs,
        compiler_params=pltpu.CompilerParams(dimension_semantics=("parallel",), vmem_limit_bytes=BIG_VMEM_LIMIT),
        name="out_proj",
    )(*args)
    return (outs[0], outs[1]) if emit_next else (outs[0], None)


def _softmax_pv(scores, values, sink):
    m = sink
    for s in scores:
        m = jnp.maximum(m, jnp.max(s, axis=-1, keepdims=True))
    den = jnp.exp(sink - m)
    acc = None
    for s, v in zip(scores, values):
        p = jnp.exp(s - m)
        den = den + jnp.sum(p, axis=-1, keepdims=True)
        pv = _dot(p.astype(BF16), v)
        acc = pv if acc is None else acc + pv
    return acc / den


def _qk(q, k):
    return lax.dot_general(q, k, (((1,), (1,)), ((), ())), preferred_element_type=F32)


def _upper_half(rows):
    return lax.broadcasted_iota(jnp.int32, (rows, LANES), 1) >= C_HEAD


def _both_halves(x, hh):
    upper = _upper_half(x.shape[0])
    keep = upper if hh == 1 else jnp.logical_not(upper)
    return jnp.where(keep, x, pltpu.roll(x, C_HEAD, axis=1)).astype(BF16)


def _pair_heads(q_chunks, keys, values, sinks, masks):
    rows = q_chunks[0].shape[0]
    upper = _upper_half(rows)
    lower = jnp.logical_not(upper)
    outs = [None] * (2 * C_GROUP)
    for hh in range(2):
        kd = [_both_halves(k, hh) for k in keys]
        vd = [_both_halves(v, hh) for v in values]
        heads = [hh * C_GROUP + g for g in range(C_GROUP)]
        q4 = jnp.concatenate([jnp.where(upper if n % 2 else lower, q_chunks[n // 2], 0.0) for n in heads],
                             axis=0).astype(BF16)
        s4 = [_qk(q4, k) for k in kd]
        p4 = [[] for _ in kd]
        dens = []
        for g, n in enumerate(heads):
            sl = slice(g * rows, (g + 1) * rows)
            scores = [s[sl] if mask is None else jnp.where(mask, s[sl], NEG_INF) for s, mask in zip(s4, masks)]
            sink = sinks(n)
            m = sink
            for s in scores:
                m = jnp.maximum(m, jnp.max(s, axis=-1, keepdims=True))
            den = jnp.exp(sink - m)
            for t, s in enumerate(scores):
                p = jnp.exp(s - m)
                den = den + jnp.sum(p, axis=-1, keepdims=True)
                p4[t].append(p.astype(BF16))
            dens.append(den)
        pv4 = None
        for p_parts, v in zip(p4, vd):
            pv = _dot(jnp.concatenate(p_parts, axis=0), v)
            pv4 = pv if pv4 is None else pv4 + pv
        for g, n in enumerate(heads):
            outs[n] = pv4[g * rows:(g + 1) * rows] / dens[g]
    return jnp.concatenate([jnp.where(lower, outs[2 * m], outs[2 * m + 1]) for m in range(C_GROUP)], axis=-1)


def _attn_ctx_kernel(sink_ref, q_ref, k_ref, v_ref, gate_ref, o_ref):
    kp = pl.program_id(1)
    q = q_ref[...] * (C_HEAD ** -0.5)
    q_chunks = [q[:, m * LANES:(m + 1) * LANES] for m in range(C_GROUP)]
    o = _pair_heads(q_chunks, [k_ref[...]], [v_ref[...]], lambda n: sink_ref[kp * 2 * C_GROUP + n], [None])
    o_ref[...] = (o * _silu(gate_ref[...])).astype(BF16)


def _attn_context(proj, sink):
    qw = 2 * C_GROUP * C_HEAD
    k_cb = C_HEADS * C_HEAD // LANES
    v_cb = (C_HEADS + C_KV_HEADS) * C_HEAD // LANES
    gate_cb = (C_HEADS + 2 * C_KV_HEADS) * C_HEAD // qw
    return pl.pallas_call(
        _attn_ctx_kernel,
        out_shape=jax.ShapeDtypeStruct((PROMPT.rows, D_MODEL), BF16),
        grid=(BATCH, C_KV_HEADS // 2),
        in_specs=[pl.BlockSpec(memory_space=pltpu.SMEM),
                  pl.BlockSpec((SEQ, qw), lambda b, kp: (b, kp)),
                  pl.BlockSpec((SEQ, LANES), lambda b, kp: (b, k_cb + kp)),
                  pl.BlockSpec((SEQ, LANES), lambda b, kp: (b, v_cb + kp)),
                  pl.BlockSpec((SEQ, qw), lambda b, kp: (b, gate_cb + kp))],
        out_specs=pl.BlockSpec((SEQ, qw), lambda b, kp: (b, kp)),
        compiler_params=_cparams(("parallel", "parallel")),
        name="attn_context",
    )(sink, proj, proj, proj, proj)


def _rope(x, cos, sin_signed):
    lane = lax.broadcasted_iota(jnp.int32, x.shape, 1)
    first = (lane & (2 * ROPE_PAIR - 1)) < ROPE_PAIR
    partner = jnp.where(first, pltpu.roll(x, LANES - ROPE_PAIR, axis=1), pltpu.roll(x, ROPE_PAIR, axis=1))
    return x * cos + partner * sin_signed


def _attn_lat_kernel(sink_ref, q_ref, kp_ref, ko_ref, kn_ref, vp_ref, vo_ref, vn_ref, ck_ref, cv_ref,
                     cq_ref, sq_ref, cp_ref, sp_ref, cn_ref, sn_ref, gate_ref, o_ref):
    kpair = pl.program_id(1)
    qb = pl.program_id(2)
    cq = cq_ref[...]
    sq = sq_ref[...]
    q = q_ref[...] * (C_HEAD ** -0.5)
    qr = [_rope(q[:, n * LANES:(n + 1) * LANES], cq, sq) for n in range(4)]
    kband = jnp.concatenate([_rope(kp_ref[...], cp_ref[...], sp_ref[...]),
                             _rope(ko_ref[...], cq, sq),
                             _rope(kn_ref[...], cn_ref[...], sn_ref[...])], axis=0)
    vband = jnp.concatenate([vp_ref[...], vo_ref[...], vn_ref[...]], axis=0)
    qpos = qb * BLOCK + lax.broadcasted_iota(jnp.int32, (BLOCK, 3 * BLOCK), 0)
    kpos = (qb - 1) * BLOCK + lax.broadcasted_iota(jnp.int32, (BLOCK, 3 * BLOCK), 1)
    valid = (jnp.abs(qpos - kpos) <= WINDOW) & (kpos >= 0) & (kpos < DEC_SEQ)
    o = _pair_heads(qr, [kband, ck_ref[...]], [vband, cv_ref[...]],
                    lambda n: sink_ref[kpair * 2 * C_GROUP + n], [valid, None])
    o_ref[...] = (o * _silu(gate_ref[...])).astype(BF16)


def _attn_latent(proj, cache_k4, cache_v4, li, sink, cos_t, sin_t):
    qw = 2 * C_GROUP * C_HEAD
    k_cb = C_HEADS * C_HEAD // LANES
    v_cb = (C_HEADS + C_KV_HEADS) * C_HEAD // LANES
    gate_cb = (C_HEADS + 2 * C_KV_HEADS) * C_HEAD // qw
    nqb = DEC_SEQ // BLOCK

    def rows(delta):
        def idx(b, kp, qb):
            return b * nqb + jnp.clip(qb + delta, 0, nqb - 1)
        return idx

    def kv_spec(cb, delta):
        r = rows(delta)
        return pl.BlockSpec((BLOCK, LANES), lambda b, kp, qb: (r(b, kp, qb), cb + kp))

    def tab_spec(delta):
        return pl.BlockSpec((BLOCK, LANES), lambda b, kp, qb: (jnp.clip(qb + delta, 0, nqb - 1), 0))

    cache_spec = pl.BlockSpec((None, None, PAST_LEN, LANES), lambda b, kp, qb: (b, li, 0, kp))
    own = rows(0)
    return pl.pallas_call(
        _attn_lat_kernel,
        out_shape=jax.ShapeDtypeStruct((SAMPLE.rows, D_MODEL), BF16),
        grid=(DEC_BATCH, C_KV_HEADS // 2, nqb),
        in_specs=[pl.BlockSpec(memory_space=pltpu.SMEM),
                  pl.BlockSpec((BLOCK, qw), lambda b, kp, qb: (own(b, kp, qb), kp)),
                  kv_spec(k_cb, -1), kv_spec(k_cb, 0), kv_spec(k_cb, 1),
                  kv_spec(v_cb, -1), kv_spec(v_cb, 0), kv_spec(v_cb, 1),
                  cache_spec, cache_spec,
                  tab_spec(0), tab_spec(0), tab_spec(-1), tab_spec(-1), tab_spec(1), tab_spec(1),
                  pl.BlockSpec((BLOCK, qw), lambda b, kp, qb: (own(b, kp, qb), gate_cb + kp))],
        out_specs=pl.BlockSpec((BLOCK, qw), lambda b, kp, qb: (own(b, kp, qb), kp)),
        compiler_params=_cparams(("parallel", "parallel", "parallel")),
        name="attn_latent",
    )(sink, proj, proj, proj, proj, proj, proj, proj, cache_k4, cache_v4,
      cos_t, sin_t, cos_t, sin_t, cos_t, sin_t, proj)


def _rope_tables():
    t = jnp.arange(DEC_SEQ, dtype=jnp.int32)
    row = (t // GRID_W).astype(F32)
    col = (t % GRID_W).astype(F32)
    nf = ROPE_PAIR
    inv = 1.0 / (ROPE_BASE ** (jnp.arange(nf, dtype=F32) / nf))
    lane = np.arange(LANES)
    f_of_lane = lane % nf
    use_col = (lane % C_HEAD) >= C_HEAD // 2
    sign = np.where((lane % (2 * ROPE_PAIR)) < ROPE_PAIR, -1.0, 1.0).astype(np.float32)
    pos = jnp.where(jnp.asarray(use_col)[None, :], col[:, None], row[:, None])
    ang = pos * inv[jnp.asarray(f_of_lane)][None, :]
    return jnp.cos(ang), jnp.sin(ang) * jnp.asarray(sign)[None, :]


def _chain_const_tiles(vec):
    t = vec.reshape(A_HEADS // 2, 2, A_HEAD).transpose(1, 2, 0)
    t = jnp.broadcast_to(t[:, :, None, :, None], (2, A_HEAD, 2, A_HEADS // 2, SEQ_PER_GROUP))
    return t.reshape(2, A_HEAD, CHAINS)


def _even_mixer(st, proj, s0, p, gmat, cs_mat, dft):
    r, k, v, w2, a2, bonus = _wkv_prep(st, proj, p["mu"], p["w0"], p["wup"], p["a0"], p["aup"], p["k_a"], p["r_k"],
                                       gmat)
    o_f, o_b, s_fin = _wkv(r, k, v, w2, a2, p["kkw_t"], p["ka_t"], s0)
    ya = _ya(st, o_f, o_b, bonus, proj, p["gn_w"], p["gn_b"], gmat)
    zc, zs = _dft_channels(st, proj, cs_mat)
    yb = _dft_time(st, zc, zs, proj, dft[0], dft[1])
    return ya, yb, s_fin


def kernel(x_prompt, x_sample, c, state_wkv, cache_k, cache_v, c_ctx, mod_w, mod_b, norm_pre, norm_post,
           even_w_in, even_mu, even_w0, even_w_up, even_a0, even_a_up, even_k_k, even_k_a, even_r_k,
           even_gn_w, even_gn_b, even_w_out, odd_w_in, odd_sink, odd_w_out):
    n_odd = odd_w_in.shape[0]
    xs = {PROMPT: x_prompt.reshape(PROMPT.rows, D_MODEL), SAMPLE: x_sample.reshape(SAMPLE.rows, D_MODEL)}
    conds = jnp.concatenate([c_ctx[None, :], c, jnp.zeros((N_COND - 1 - DEC_BATCH, D_MODEL), F32)], axis=0)
    mods4 = _modulation(conds, mod_w, mod_b).reshape(DEPTH, N_COND, 1, 3 * D_MODEL)
    norm_pre3 = norm_pre.reshape(DEPTH, 1, D_MODEL)
    norm_post3 = norm_post.reshape(DEPTH, 1, D_MODEL)

    even_w_out_b = even_w_out.astype(BF16)
    odd_w_out_b = odd_w_out.astype(BF16)

    head_of_lane = np.arange(CW) // A_HEAD
    gmat = jnp.asarray((head_of_lane[:, None] == head_of_lane[None, :]).astype(np.float32)).astype(BF16)
    c_ch, s_ch = _dft_mats(B_GROUP_CH)
    cs_mat = jnp.concatenate([c_ch, s_ch], axis=1).astype(BF16)
    dft = {PROMPT: tuple(m.astype(BF16) for m in _dft_mats(SEQ)),
           SAMPLE: tuple(m.astype(BF16) for m in _dft_mats(DEC_SEQ))}
    cos_t, sin_t = _rope_tables()
    cache_k4 = cache_k.reshape(DEC_BATCH, n_odd, PAST_LEN, C_KV_HEADS * C_HEAD)
    cache_v4 = cache_v.reshape(DEC_BATCH, n_odd, PAST_LEN, C_KV_HEADS * C_HEAD)

    def pad_rows(w, slot):
        return jnp.pad(w, ((0, 0), (slot * LORA, (3 - slot) * LORA), (0, 0)))

    wup_pad = jnp.stack([pad_rows(even_w_up[:, 0], 0), pad_rows(even_w_up[:, 1], 1)], axis=1)
    aup_pad = jnp.stack([pad_rows(even_a_up[:, 0], 2), pad_rows(even_a_up[:, 1], 3)], axis=1)

    new_wkv, new_k, new_v = [], [], []
    hs = {st: _pre_norm(st, xs[st], norm_pre3, mods4, 0) for st in (PROMPT, SAMPLE)}
    for layer in range(DEPTH):
        i = layer // 2
        if layer % 2 == 0:
            p = {
                "mu": even_mu[i][None, :], "w0": even_w0[i], "a0": even_a0[i],
                "wup": wup_pad[i], "aup": aup_pad[i],
                "k_a": even_k_a[i][None, :], "r_k": even_r_k[i].reshape(1, A_WIDTH),
                "kkw_t": _chain_const_tiles(even_k_k[i]), "ka_t": _chain_const_tiles(even_k_a[i]),
                "gn_w": even_gn_w[i][None, :], "gn_b": even_gn_b[i][None, :],
            }
            s0s = state_wkv[:, i].reshape(DEC_BATCH, 2, A_HEADS // 2, 2, A_HEAD, A_HEAD)
            s0 = {PROMPT: jnp.zeros((BATCH // SEQ_PER_GROUP, 2, A_HEAD, A_HEAD, CHAINS), F32),
                  SAMPLE: s0s.transpose(3, 5, 4, 1, 2, 0).reshape(1, 2, A_HEAD, A_HEAD, CHAINS)}
            for st in (PROMPT, SAMPLE):
                proj = _in_proj(st, hs[st], even_w_in, i)
                ya, yb, s_fin = _even_mixer(st, proj, s0[st], p, gmat, cs_mat, dft[st])
                xs[st], hs[st] = _out_proj(st, ya, 0, yb, 0, even_w_out_b, i, xs[st], mods4, norm_post3, norm_pre3,
                                           layer)
                if st is PROMPT:
                    s_fin = s_fin.reshape(BATCH // SEQ_PER_GROUP, 2, A_HEAD, A_HEAD, 2, A_HEADS // 2, SEQ_PER_GROUP)
                    new_wkv.append(s_fin.transpose(0, 6, 4, 5, 1, 3, 2).reshape(BATCH, 2, A_HEADS, A_HEAD, A_HEAD))
        else:
            proj_p = _in_proj(PROMPT, hs[PROMPT], odd_w_in, i)
            proj_s = _in_proj(SAMPLE, hs[SAMPLE], odd_w_in, i)
            kv0 = C_HEADS * C_HEAD
            kvn = C_KV_HEADS * C_HEAD
            new_k.append(proj_p[:, kv0:kv0 + kvn].reshape(BATCH, SEQ, C_KV_HEADS, C_HEAD))
            new_v.append(proj_p[:, kv0 + kvn:kv0 + 2 * kvn].reshape(BATCH, SEQ, C_KV_HEADS, C_HEAD))
            att_p = _attn_context(proj_p, odd_sink[i])
            att_s = _attn_latent(proj_s, cache_k4, cache_v4, i, odd_sink[i], cos_t, sin_t)
            for st, att in ((PROMPT, att_p), (SAMPLE, att_s)):
                xs[st], hs[st] = _out_proj(st, att, 0, att, 1, odd_w_out_b, i, xs[st], mods4, norm_post3, norm_pre3,
                                           layer)

    y_prompt = xs[PROMPT].reshape(BATCH, SEQ, D_MODEL)
    y_sample = xs[SAMPLE].reshape(DEC_BATCH, DEC_SEQ, D_MODEL)
    return (y_prompt, y_sample, jnp.stack(new_wkv, axis=1), jnp.stack(new_k, axis=1), jnp.stack(new_v, axis=1))
```

```python
import collections
import functools

import numpy as np
import jax
import jax.numpy as jnp
from jax import lax
from jax.experimental import pallas as pl
from jax.experimental.pallas import tpu as pltpu

F32 = jnp.float32
BF16 = jnp.bfloat16

D_MODEL = 2048
BATCH = 32
SEQ = 256
DEPTH = 4
DEC_BATCH = 8
DEC_SEQ = 2048
PAST_LEN = 256
GRID_W = 64
RMS_EPS = 1e-6
A_WIDTH = 1024
A_HEAD = 64
A_HEADS = 16
LORA = 64
GN_EPS = 64e-5
DECAY_SCALE = float(np.exp(-0.5))
B_WIDTH = 1024
B_GROUPS = 4
B_GROUP_CH = 256
SHIFT_COLS = 3 * A_WIDTH + 4 * LORA
EVEN_IN = SHIFT_COLS + A_WIDTH + 2 * B_WIDTH
C_HEAD = 64
C_HEADS = 32
C_KV_HEADS = 8
C_GROUP = 4
WINDOW = 128
BLOCK = 128
ROPE_BASE = 10000.0
ROPE_PAIR = C_HEAD // 4
KK_NORM_FLOOR = 1e-12
ODD_IN = (C_HEADS + 2 * C_KV_HEADS) * C_HEAD + D_MODEL
NEG_INF = -1e30

N_COND = 16
SUBLANES = 8
LANES = 128
CHAINS = LANES
SEQ_PER_GROUP = CHAINS // A_HEADS
CW = 256
TM_IN = 1024
TN_IN = 1280
TM_ROW = 256
TM_OUT = 512
OUT_SUBTILES = 2
PRE_SUBTILES = 8
WKV_TT = 32
BIG_VMEM_LIMIT = 56 * 1024 * 1024
WKV_ROWS = 32
WKV_ACCS = 2
VMEM_LIMIT = 48 * 1024 * 1024

Stream = collections.namedtuple("Stream", "rows t_len n_seq cond0 cond_per_seq")
PROMPT = Stream(BATCH * SEQ, SEQ, BATCH, 0, 0)
SAMPLE = Stream(DEC_BATCH * DEC_SEQ, DEC_SEQ, DEC_BATCH, 1, 1)


def _cparams(sem):
    return pltpu.CompilerParams(dimension_semantics=sem, vmem_limit_bytes=VMEM_LIMIT)


def _cond_of_row(st, row0):
    return st.cond0 + st.cond_per_seq * (row0 // st.t_len)


def _sigmoid(x):
    return 1.0 / (1.0 + jnp.exp(-x))


def _silu(x):
    return x * _sigmoid(x)


def _split(a):
    hi = a.astype(BF16)
    lo = (a - hi.astype(F32)).astype(BF16)
    return hi, lo


def _dot(a, b):
    return jnp.dot(a, b, preferred_element_type=F32)


def _dot3_split(a_hl, b_hl):
    (ah, al), (bh, bl) = a_hl, b_hl
    return _dot(ah, bh) + (_dot(ah, bl) + _dot(al, bh))


def _dot3(a, b):
    return _dot3_split(_split(a), _split(b))


def _gsum(x, gmat):
    xh, xl = _split(x)
    return _dot(xh, gmat) + _dot(xl, gmat)


def _mod_kernel(c_ref, w_ref, b_ref, o_ref):
    o_ref[...] = _dot3(_silu(c_ref[...]), w_ref[...]) + b_ref[...]


def _modulation(conds, mod_w, mod_b):
    tn = 512
    n = 3 * D_MODEL
    return pl.pallas_call(
        _mod_kernel,
        out_shape=jax.ShapeDtypeStruct((DEPTH, N_COND, n), F32),
        grid=(DEPTH, n // tn),
        in_specs=[
            pl.BlockSpec((N_COND, D_MODEL), lambda l, j: (0, 0)),
            pl.BlockSpec((None, D_MODEL, tn), lambda l, j: (l, 0, j)),
            pl.BlockSpec((None, 1, tn), lambda l, j: (l, 0, j)),
        ],
        out_specs=pl.BlockSpec((None, N_COND, tn), lambda l, j: (l, 0, j)),
        compiler_params=_cparams(("parallel", "parallel")),
        name="modulation",
    )(conds, mod_w, mod_b.reshape(DEPTH, 1, n))


def _norm_modulate(x, g, sh, sc):
    ms = jnp.mean(x * x, axis=-1, keepdims=True)
    y = x * lax.rsqrt(ms + RMS_EPS) * g
    return (y * (1.0 + sc) + sh).astype(BF16)


def _pre_kernel(x_ref, g_ref, sh_ref, sc_ref, h_ref):
    sub = x_ref.shape[0] // PRE_SUBTILES
    for s in range(PRE_SUBTILES):
        rows = slice(s * sub, (s + 1) * sub)
        h_ref[rows, :] = _norm_modulate(x_ref[rows, :], g_ref[...], sh_ref[...], sc_ref[...])


def _pre_norm(st, x, norm_g, mods4, layer):
    tm = TM_ROW
    cond = lambda i: _cond_of_row(st, i * tm)
    return pl.pallas_call(
        _pre_kernel,
        out_shape=jax.ShapeDtypeStruct((st.rows, D_MODEL), BF16),
        grid=(st.rows // tm,),
        in_specs=[pl.BlockSpec((tm, D_MODEL), lambda i: (i, 0)),
                  pl.BlockSpec((None, 1, D_MODEL), lambda i: (layer, 0, 0)),
                  pl.BlockSpec((None, None, 1, D_MODEL), lambda i: (layer, cond(i), 0, 0)),
                  pl.BlockSpec((None, None, 1, D_MODEL), lambda i: (layer, cond(i), 0, 1))],
        out_specs=pl.BlockSpec((tm, D_MODEL), lambda i: (i, 0)),
        compiler_params=_cparams(("parallel",)),
        name="pre_norm",
    )(x, norm_g, mods4, mods4)


def _in_kernel(h_ref, w_ref, o_ref, wb_ref):
    @pl.when(pl.program_id(1) == 0)
    def _():
        wb_ref[...] = w_ref[...].astype(BF16)

    o_ref[...] = _dot(h_ref[...], wb_ref[...])


def _in_proj(st, h, w, li):
    n = w.shape[-1]
    return pl.pallas_call(
        _in_kernel,
        out_shape=jax.ShapeDtypeStruct((st.rows, n), F32),
        grid=(n // TN_IN, st.rows // TM_IN),
        in_specs=[pl.BlockSpec((TM_IN, D_MODEL), lambda j, i: (i, 0)),
                  pl.BlockSpec((None, D_MODEL, TN_IN), lambda j, i: (li, 0, j))],
        out_specs=pl.BlockSpec((TM_IN, TN_IN), lambda j, i: (i, j)),
        scratch_shapes=[pltpu.VMEM((D_MODEL, TN_IN), BF16)],
        compiler_params=pltpu.CompilerParams(dimension_semantics=("parallel", "arbitrary"),
                                             vmem_limit_bytes=BIG_VMEM_LIMIT),
        name="in_proj",
    )(h, w)


def _token_shift(cur, prev8, next8, mu, first, last):
    tm = cur.shape[0]
    rows = lax.broadcasted_iota(jnp.int32, cur.shape, 0)
    prow = jnp.where(first, 0.0, prev8[7:8, :])
    nrow = jnp.where(last, 0.0, next8[0:1, :])
    up = jnp.where(rows == 0, prow, pltpu.roll(cur, 1, axis=0))
    dn = jnp.where(rows == tm - 1, nrow, pltpu.roll(cur, tm - 1, axis=0))
    return cur + mu * (0.5 * (up + dn) - cur)


def _prep_kernel(r_ref, k_ref, v_ref, lo_ref, rp_ref, kp_ref, vp_ref, lp_ref, rn_ref, kn_ref, vn_ref, ln_ref,
                 mur_ref, muk_ref, muv_ref, mul_ref, w0_ref, wuph_ref, wupl_ref, a0_ref, auph_ref, aupl_ref,
                 ka_ref, rk_ref, gm_ref,
                 r_o, k_o, v_o, w_o, a_o, bon_o, *, tiles_per_seq):
    i = pl.program_id(0)
    first = lax.rem(i, tiles_per_seq) == 0
    last = lax.rem(i, tiles_per_seq) == tiles_per_seq - 1

    low = _token_shift(lo_ref[...], lp_ref[...], ln_ref[...], mul_ref[...], first, last)
    low_hl = _split(low)
    low_t_hl = _split(jnp.tanh(low))
    gm = gm_ref[...]

    for c in range(A_WIDTH // CW):
        cols = slice(c * CW, (c + 1) * CW)
        r = _token_shift(r_ref[:, cols], rp_ref[:, cols], rn_ref[:, cols], mur_ref[:, cols], first, last)
        k = _token_shift(k_ref[:, cols], kp_ref[:, cols], kn_ref[:, cols], muk_ref[:, cols], first, last)
        v = _token_shift(v_ref[:, cols], vp_ref[:, cols], vn_ref[:, cols], muv_ref[:, cols], first, last)
        ka = ka_ref[:, cols]
        kd_sum = None
        for d in range(2):
            w_raw = w0_ref[d:d + 1, cols] + _dot3_split(low_t_hl, (wuph_ref[d, :, cols], wupl_ref[d, :, cols]))
            a = _sigmoid(a0_ref[d:d + 1, cols] + _dot3_split(low_hl, (auph_ref[d, :, cols], aupl_ref[d, :, cols])))
            kd = k * (1.0 + (a - 1.0) * ka)
            w_o[d, :, cols] = jnp.exp(-DECAY_SCALE * _sigmoid(w_raw))
            a_o[d, :, cols] = a
            kd_sum = kd if kd_sum is None else kd_sum + kd
        r_o[:, cols] = r
        k_o[:, cols] = k
        v_o[:, cols] = v
        bon_o[:, cols] = _gsum(r * (0.5 * kd_sum) * rk_ref[:, cols], gm) * v


def _wkv_prep(st, proj, mu, w0, wup_pad, a0, aup_pad, k_a, r_k, gmat):
    tm = TM_ROW
    nb8 = st.rows // SUBLANES
    low_cb = 3 * A_WIDTH // CW
    assert 4 * LORA == CW and st.t_len % tm == 0

    prev_row = lambda i: jnp.maximum(i * (tm // SUBLANES) - 1, 0)
    next_row = lambda i: jnp.minimum((i + 1) * (tm // SUBLANES), nb8 - 1)

    def main(cb):
        return pl.BlockSpec((tm, A_WIDTH), lambda i: (i, cb))

    def prev(cb):
        return pl.BlockSpec((SUBLANES, A_WIDTH), lambda i: (prev_row(i), cb))

    def nxt(cb):
        return pl.BlockSpec((SUBLANES, A_WIDTH), lambda i: (next_row(i), cb))

    def fixed(spec_fn):
        return [spec_fn(0), spec_fn(1), spec_fn(2)]

    low_main = pl.BlockSpec((tm, CW), lambda i: (i, low_cb))
    low_prev = pl.BlockSpec((SUBLANES, CW), lambda i: (prev_row(i), low_cb))
    low_next = pl.BlockSpec((SUBLANES, CW), lambda i: (next_row(i), low_cb))

    def vec(cb):
        return pl.BlockSpec((1, A_WIDTH), lambda i: (0, cb))

    in_specs = (fixed(main) + [low_main] + fixed(prev) + [low_prev] + fixed(nxt) + [low_next]
                + [vec(0), vec(1), vec(2), pl.BlockSpec((1, CW), lambda i: (0, low_cb))]
                + [pl.BlockSpec((2, A_WIDTH), lambda i: (0, 0)),
                   pl.BlockSpec((2, CW, A_WIDTH), lambda i: (0, 0, 0)),
                   pl.BlockSpec((2, CW, A_WIDTH), lambda i: (0, 0, 0)),
                   pl.BlockSpec((2, A_WIDTH), lambda i: (0, 0)),
                   pl.BlockSpec((2, CW, A_WIDTH), lambda i: (0, 0, 0)),
                   pl.BlockSpec((2, CW, A_WIDTH), lambda i: (0, 0, 0)),
                   vec(0), vec(0),
                   pl.BlockSpec((CW, CW), lambda i: (0, 0))])
    tps = st.t_len // tm
    g_n = st.n_seq // SEQ_PER_GROUP

    def til(i):
        s = i // tps
        return (s // SEQ_PER_GROUP, lax.rem(i, tps), lax.rem(s, SEQ_PER_GROUP))

    one = pl.BlockSpec((tm, A_WIDTH), lambda i: (i, 0))
    til1 = pl.BlockSpec((None, tm, A_WIDTH), lambda i: til(i))
    til2 = pl.BlockSpec((2, None, tm, A_WIDTH), lambda i: (0,) + til(i))
    one_sds = jax.ShapeDtypeStruct((st.rows, A_WIDTH), F32)
    til1_sds = jax.ShapeDtypeStruct((g_n, st.t_len, SEQ_PER_GROUP * A_WIDTH), F32)
    til2_sds = jax.ShapeDtypeStruct((2, g_n, st.t_len, SEQ_PER_GROUP * A_WIDTH), F32)
    return pl.pallas_call(
        functools.partial(_prep_kernel, tiles_per_seq=tps),
        out_shape=[til1_sds, til1_sds, til1_sds, til2_sds, til2_sds, one_sds],
        grid=(st.rows // tm,),
        in_specs=in_specs,
        out_specs=[til1, til1, til1, til2, til2, one],
        compiler_params=_cparams(("parallel",)),
        name="wkv_prep",
    )(*([proj] * 12), mu, mu, mu, mu, w0, *_split(wup_pad), a0, *_split(aup_pad), k_a, r_k, gmat)


def _wkv_kernel(rf_ref, rb_ref, kf_ref, kb_ref, vf_ref, vb_ref, wf_ref, wb_ref, af_ref, ab_ref, kkw_ref, ka_ref,
                s0_ref, of_ref, ob_ref, sf_ref,
                s2_ref, r2_ref, kk2_ref, v2_ref, kka2_ref, kd2_ref, g_ref, o_ref, *, tt_steps):
    c = pl.program_id(1)
    nblk = WKV_ROWS // SUBLANES
    n_hv = A_HEADS // 2
    pairs = ((rf_ref, rb_ref), (kf_ref, kb_ref), (vf_ref, vb_ref), (wf_ref, wb_ref), (af_ref, ab_ref))

    @pl.when(c == 0)
    def _():
        s2_ref[...] = s0_ref[...]

    def to_chain_tiles(u, gammas):
        ub = tt_steps - 1 - u
        halves = []
        for xf_ref, xb_ref in pairs:
            x2 = jnp.concatenate([xf_ref[u, :, hv * LANES:(hv + 1) * LANES] for hv in range(n_hv)]
                                 + [xb_ref[ub, :, hv * LANES:(hv + 1) * LANES] for hv in range(n_hv)], axis=0)
            y = x2.T
            halves.append((y[:A_HEAD], y[A_HEAD:]))
        r_h, k_h, v_h, w_h, a_h = halves
        new_gammas = []
        for hp in range(2):
            kk = k_h[hp] * kkw_ref[hp]
            norm = jnp.sqrt(jnp.sum(kk * kk, axis=0, keepdims=True))
            kk = kk / jnp.maximum(norm, KK_NORM_FLOOR)
            kka = kk * a_h[hp]
            kd = k_h[hp] * (1.0 + (a_h[hp] - 1.0) * ka_ref[hp])
            gamma = gammas[hp] * w_h[hp]
            inv = 1.0 / gamma
            kk2_ref[hp, u] = gammas[hp] * kk
            kka2_ref[hp, u] = kka * inv
            kd2_ref[hp, u] = kd * inv
            r2_ref[hp, u] = r_h[hp] * gamma
            v2_ref[hp, u] = v_h[hp]
            new_gammas.append(gamma)
        return tuple(new_gammas)

    one = jnp.ones((A_HEAD, CHAINS), F32)
    gammas = lax.fori_loop(0, tt_steps, to_chain_tiles, (one, one), unroll=8)
    for hp in range(2):
        g_ref[hp] = gammas[hp]

    def bcast_row(ref, t, j):
        return ref[t, pl.ds(j, SUBLANES, stride=0), :]

    def time_step(hp, t):
        s_ref = s2_ref.at[hp]
        r_ref, kk_ref, v_ref, kka_ref, kd_ref = (ref.at[hp] for ref in (r2_ref, kk2_ref, v2_ref, kka2_ref, kd2_ref))
        for ib in range(A_HEAD // WKV_ROWS):
            i0 = ib * WKV_ROWS
            rows = [pl.ds(i0 + SUBLANES * b, SUBLANES) for b in range(nblk)]
            sa = [[None] * WKV_ACCS for _ in range(nblk)]
            for j in range(A_HEAD):
                kkj = bcast_row(kk_ref, t, j)
                for b in range(nblk):
                    p = s_ref[j, rows[b], :] * kkj
                    sa[b][j % WKV_ACCS] = p if sa[b][j % WKV_ACCS] is None else sa[b][j % WKV_ACCS] + p
            sa = [functools.reduce(lambda x, y: x + y, parts) for parts in sa]
            v8 = [v_ref[t, rows[b], :] for b in range(nblk)]
            out = [[None] * WKV_ACCS for _ in range(nblk)]
            for j in range(A_HEAD):
                kkaj = bcast_row(kka_ref, t, j)
                kdj = bcast_row(kd_ref, t, j)
                rj = bcast_row(r_ref, t, j)
                for b in range(nblk):
                    sn = s_ref[j, rows[b], :] - sa[b] * kkaj + v8[b] * kdj
                    s_ref[j, rows[b], :] = sn
                    q = sn * rj
                    out[b][j % WKV_ACCS] = q if out[b][j % WKV_ACCS] is None else out[b][j % WKV_ACCS] + q
            for b in range(nblk):
                o_ref[hp, rows[b], :] = functools.reduce(lambda x, y: x + y, out[b])

    def step(u, carry):
        for hp in range(2):
            time_step(hp, u)
        x2 = jnp.concatenate([o_ref[0], o_ref[1]], axis=0).T
        ub = tt_steps - 1 - u
        for hv in range(n_hv):
            of_ref[u, :, hv * LANES:(hv + 1) * LANES] = x2[hv * SUBLANES:(hv + 1) * SUBLANES, :]
            ob_ref[ub, :, hv * LANES:(hv + 1) * LANES] = x2[(n_hv + hv) * SUBLANES:(n_hv + hv + 1) * SUBLANES, :]
        return carry

    lax.fori_loop(0, tt_steps, step, 0)

    def rescale(j, carry):
        for hp in range(2):
            gj = g_ref[hp, pl.ds(j, SUBLANES, stride=0), :]
            for b in range(A_HEAD // SUBLANES):
                rows = pl.ds(b * SUBLANES, SUBLANES)
                s2_ref[hp, j, rows, :] = s2_ref[hp, j, rows, :] * gj
        return carry

    lax.fori_loop(0, A_HEAD, rescale, 0)

    @pl.when(c == pl.num_programs(1) - 1)
    def _():
        sf_ref[...] = s2_ref[...]


def _wkv(r, k, v, w, a, kkw_t, ka_t, s0):
    g_n, t_n = r.shape[0], r.shape[1]
    tt = WKV_TT
    nc = t_n // tt
    split = lambda a: a.reshape(a.shape[:-1] + (SEQ_PER_GROUP, A_WIDTH))
    blk = (tt, SEQ_PER_GROUP, A_WIDTH)

    fwd = pl.BlockSpec((None,) + blk, lambda g, c: (g, c, 0, 0))
    bwd = pl.BlockSpec((None,) + blk, lambda g, c: (g, nc - 1 - c, 0, 0))
    fwd_d = pl.BlockSpec((None, None) + blk, lambda g, c: (0, g, c, 0, 0))
    bwd_d = pl.BlockSpec((None, None) + blk, lambda g, c: (1, g, nc - 1 - c, 0, 0))
    state = pl.BlockSpec((None, 2, A_HEAD, A_HEAD, CHAINS), lambda g, c: (g, 0, 0, 0, 0),
                         pipeline_mode=pl.Buffered(1))
    tiles = pltpu.VMEM((2, tt, A_HEAD, CHAINS), F32)
    o_sds = jax.ShapeDtypeStruct((g_n, t_n, SEQ_PER_GROUP, A_WIDTH), F32)
    r, k, v, w, a = (split(x) for x in (r, k, v, w, a))
    const = pl.BlockSpec((2, A_HEAD, CHAINS), lambda g, c: (0, 0, 0))
    o_f, o_b, s_fin = pl.pallas_call(
        functools.partial(_wkv_kernel, tt_steps=tt),
        out_shape=[o_sds, o_sds, jax.ShapeDtypeStruct((g_n, 2, A_HEAD, A_HEAD, CHAINS), F32)],
        grid=(g_n, nc),
        in_specs=[fwd, bwd, fwd, bwd, fwd, bwd, fwd_d, bwd_d, fwd_d, bwd_d, const, const, state],
        out_specs=[fwd, bwd, state],
        scratch_shapes=([pltpu.VMEM((2, A_HEAD, A_HEAD, CHAINS), F32)] + [tiles] * 5
                        + [pltpu.VMEM((2, A_HEAD, CHAINS), F32)] * 2),
        compiler_params=pltpu.CompilerParams(dimension_semantics=("parallel", "arbitrary"),
                                             vmem_limit_bytes=BIG_VMEM_LIMIT),
        name="wkv",
    )(r, r, k, k, v, v, w, w, a, a, kkw_t, ka_t, s0)
    merge = lambda a: a.reshape(g_n, t_n, SEQ_PER_GROUP * A_WIDTH)
    return merge(o_f), merge(o_b), s_fin


def _ya_kernel(of_ref, ob_ref, bon_ref, g0_ref, g1_ref, g2_ref, g3_ref, gw_ref, gb_ref, gm_ref, o_ref):
    gm = gm_ref[...]
    for c, gate_ref in enumerate((g0_ref, g1_ref, g2_ref, g3_ref)):
        cols = slice(c * CW, (c + 1) * CW)
        o = of_ref[:, cols] + ob_ref[:, cols]
        mean = _gsum(o, gm) * (1.0 / A_HEAD)
        dev = o - mean
        var = _gsum(dev * dev, gm) * (1.0 / A_HEAD)
        y = dev * lax.rsqrt(var + GN_EPS) * gw_ref[:, cols] + gb_ref[:, cols]
        o_ref[:, cols] = ((y + bon_ref[:, cols]) * _silu(gate_ref[...])).astype(BF16)


def _ya(st, o_f, o_b, bonus, proj, gn_w, gn_b, gmat):
    tm = TM_ROW
    gate_cb = (SHIFT_COLS) // CW
    blk = pl.BlockSpec((tm, A_WIDTH), lambda i: (i, 0))
    vec = pl.BlockSpec((1, A_WIDTH), lambda i: (0, 0))
    tps = st.t_len // tm

    def til_idx(i):
        s = i // tps
        return (s // SEQ_PER_GROUP, lax.rem(i, tps), lax.rem(s, SEQ_PER_GROUP))

    til = pl.BlockSpec((None, tm, A_WIDTH), til_idx)
    gates = [pl.BlockSpec((tm, CW), lambda i, c=c: (i, gate_cb + c)) for c in range(A_WIDTH // CW)]
    return pl.pallas_call(
        _ya_kernel,
        out_shape=jax.ShapeDtypeStruct((st.rows, A_WIDTH), BF16),
        grid=(st.rows // tm,),
        in_specs=[til, til, blk] + gates + [vec, vec, pl.BlockSpec((CW, CW), lambda i: (0, 0))],
        out_specs=blk,
        compiler_params=_cparams(("parallel",)),
        name="wkv_post",
    )(o_f, o_b, bonus, proj, proj, proj, proj, gn_w, gn_b, gmat)


def _dft_ch_kernel(u_ref, cs_ref, zc_ref, zs_ref):
    z = _dot(u_ref[...].astype(BF16), cs_ref[...])
    zc_ref[...] = z[:, :B_GROUP_CH].astype(BF16)
    zs_ref[...] = z[:, B_GROUP_CH:].astype(BF16)


def _dft_channels(st, proj, cs_mat):
    tm = 2048
    u_cb = (SHIFT_COLS + A_WIDTH) // B_GROUP_CH
    blk = pl.BlockSpec((tm, B_GROUP_CH), lambda i, g: (i, g))
    sds = jax.ShapeDtypeStruct((st.rows, B_WIDTH), BF16)
    return pl.pallas_call(
        _dft_ch_kernel,
        out_shape=[sds, sds],
        grid=(st.rows // tm, B_GROUPS),
        in_specs=[pl.BlockSpec((tm, B_GROUP_CH), lambda i, g: (i, u_cb + g)),
                  pl.BlockSpec((B_GROUP_CH, 2 * B_GROUP_CH), lambda i, g: (0, 0))],
        out_specs=[blk, blk],
        compiler_params=_cparams(("parallel", "parallel")),
        name="dft_channels",
    )(proj, cs_mat)


def _dft_time_kernel(c_ref, s_ref, zc_ref, zs_ref, g0_ref, g1_ref, g2_ref, g3_ref, o_ref, *, scale):
    tm = o_ref.shape[0]
    rows = pl.ds(pl.multiple_of(pl.program_id(1) * tm, tm), tm)
    acc = _dot(c_ref[rows, :], zc_ref[...]) - _dot(s_ref[rows, :], zs_ref[...])
    for g, gate_ref in enumerate((g0_ref, g1_ref, g2_ref, g3_ref)):
        cols = slice(g * B_GROUP_CH, (g + 1) * B_GROUP_CH)
        o_ref[:, cols] = (acc[:, cols] * scale * _silu(gate_ref[...])).astype(BF16)


def _dft_time(st, zc, zs, proj, cmat, smat):
    t_len = st.t_len
    tm = min(t_len, 512)
    mt = t_len // tm
    whole = pl.BlockSpec((t_len, t_len), lambda b, m: (0, 0), pipeline_mode=pl.Buffered(1))
    gate_cb = (SHIFT_COLS + A_WIDTH + B_WIDTH) // B_GROUP_CH
    gate_specs = [pl.BlockSpec((tm, B_GROUP_CH), lambda b, m, g=g: (b * mt + m, gate_cb + g)) for g in range(B_GROUPS)]
    scale = 1.0 / float(np.sqrt(t_len * B_GROUP_CH))
    return pl.pallas_call(
        functools.partial(_dft_time_kernel, scale=scale),
        out_shape=jax.ShapeDtypeStruct((st.rows, B_WIDTH), BF16),
        grid=(st.n_seq, mt),
        in_specs=[whole, whole,
                  pl.BlockSpec((t_len, B_WIDTH), lambda b, m: (b, 0)),
                  pl.BlockSpec((t_len, B_WIDTH), lambda b, m: (b, 0))] + gate_specs,
        out_specs=pl.BlockSpec((tm, B_WIDTH), lambda b, m: (b * mt + m, 0)),
        compiler_params=_cparams(("parallel", "parallel")),
        name="dft_time",
    )(cmat, smat, zc, zs, proj, proj, proj, proj)


def _dft_mats(n):
    idx = jnp.arange(n, dtype=jnp.int32)
    prod = (idx[:, None] * idx[None, :]) % n
    ang = prod.astype(F32) * (2.0 * np.pi / n)
    return jnp.cos(ang), jnp.sin(ang)


def _out_kernel(a1_ref, a2_ref, w1_ref, w2_ref, x_ref, gate_ref, g_ref, *rest, emit_next):
    if emit_next:
        ng_ref, nsh_ref, nsc_ref, o_ref, h_ref = rest
    else:
        o_ref, = rest
    sub = x_ref.shape[0] // OUT_SUBTILES
    for s in range(OUT_SUBTILES):
        rows = slice(s * sub, (s + 1) * sub)
        y = _dot(a1_ref[rows, :], w1_ref[...]) + _dot(a2_ref[rows, :], w2_ref[...])
        ms = jnp.mean(y * y, axis=-1, keepdims=True)
        yn = y * lax.rsqrt(ms + RMS_EPS) * g_ref[...]
        x_new = x_ref[rows, :] + gate_ref[...] * yn
        if emit_next:
            h_ref[rows, :] = _norm_modulate(x_new, ng_ref[...], nsh_ref[...], nsc_ref[...])
        o_ref[rows, :] = x_new


def _out_proj(st, a1, a1_cb, a2, a2_cb, w_bf16, li, x, mods4, norm_post, norm_pre, layer):
    tm = TM_OUT
    half = D_MODEL // 2
    cond = lambda i: _cond_of_row(st, i * tm)
    emit_next = layer + 1 < DEPTH
    row = pl.BlockSpec((tm, D_MODEL), lambda i: (i, 0))
    in_specs = [pl.BlockSpec((tm, half), lambda i: (i, a1_cb)),
                pl.BlockSpec((tm, half), lambda i: (i, a2_cb)),
                pl.BlockSpec((None, half, D_MODEL), lambda i: (li, 0, 0)),
                pl.BlockSpec((None, half, D_MODEL), lambda i: (li, 1, 0)),
                row,
                pl.BlockSpec((None, None, 1, D_MODEL), lambda i: (layer, cond(i), 0, 2)),
                pl.BlockSpec((None, 1, D_MODEL), lambda i: (layer, 0, 0))]
    args = [a1, a2, w_bf16, w_bf16, x, mods4, norm_post]
    out_shape = [jax.ShapeDtypeStruct((st.rows, D_MODEL), F32)]
    out_specs = [row]
    if emit_next:
        in_specs += [pl.BlockSpec((None, 1, D_MODEL), lambda i: (layer + 1, 0, 0)),
                     pl.BlockSpec((None, None, 1, D_MODEL), lambda i: (layer + 1, cond(i), 0, 0)),
                     pl.BlockSpec((None, None, 1, D_MODEL), lambda i: (layer + 1, cond(i), 0, 1))]
        args += [norm_pre, mods4, mods4]
        out_shape.append(jax.ShapeDtypeStruct((st.rows, D_MODEL), BF16))
        out_specs.append(row)
    outs = pl.pallas_call(
        functools.partial(_out_kernel, emit_next=emit_next),
        out_shape=out_shape,
        grid=(st.rows // tm,),
        in_specs=in_specs,
        out_specs=out_specs,
        compiler_params=pltpu.CompilerParams(dimension_semantics=("parallel",), vmem_limit_bytes=BIG_VMEM_LIMIT),
        name="out_proj",
    )(*args)
    return (outs[0], outs[1]) if emit_next else (outs[0], None)


def _softmax_pv(scores, values, sink):
    m = sink
    for s in scores:
        m = jnp.maximum(m, jnp.max(s, axis=-1, keepdims=True))
    den = jnp.exp(sink - m)
    acc = None
    for s, v in zip(scores, values):
        p = jnp.exp(s - m)
        den = den + jnp.sum(p, axis=-1, keepdims=True)
        pv = _dot(p.astype(BF16), v)
        acc = pv if acc is None else acc + pv
    return acc / den


def _qk(q, k):
    return lax.dot_general(q, k, (((1,), (1,)), ((), ())), preferred_element_type=F32)


def _upper_half(rows):
    return lax.broadcasted_iota(jnp.int32, (rows, LANES), 1) >= C_HEAD


def _both_halves(x, hh):
    upper = _upper_half(x.shape[0])
    keep = upper if hh == 1 else jnp.logical_not(upper)
    return jnp.where(keep, x, pltpu.roll(x, C_HEAD, axis=1)).astype(BF16)


def _pair_heads(q_chunks, keys, values, sinks, masks):
    rows = q_chunks[0].shape[0]
    upper = _upper_half(rows)
    lower = jnp.logical_not(upper)
    outs = [None] * (2 * C_GROUP)
    for hh in range(2):
        kd = [_both_halves(k, hh) for k in keys]
        vd = [_both_halves(v, hh) for v in values]
        heads = [hh * C_GROUP + g for g in range(C_GROUP)]
        q4 = jnp.concatenate([jnp.where(upper if n % 2 else lower, q_chunks[n // 2], 0.0) for n in heads],
                             axis=0).astype(BF16)
        s4 = [_qk(q4, k) for k in kd]
        p4 = [[] for _ in kd]
        dens = []
        for g, n in enumerate(heads):
            sl = slice(g * rows, (g + 1) * rows)
            scores = [s[sl] if mask is None else jnp.where(mask, s[sl], NEG_INF) for s, mask in zip(s4, masks)]
            sink = sinks(n)
            m = sink
            for s in scores:
                m = jnp.maximum(m, jnp.max(s, axis=-1, keepdims=True))
            den = jnp.exp(sink - m)
            for t, s in enumerate(scores):
                p = jnp.exp(s - m)
                den = den + jnp.sum(p, axis=-1, keepdims=True)
                p4[t].append(p.astype(BF16))
            dens.append(den)
        pv4 = None
        for p_parts, v in zip(p4, vd):
            pv = _dot(jnp.concatenate(p_parts, axis=0), v)
            pv4 = pv if pv4 is None else pv4 + pv
        for g, n in enumerate(heads):
            outs[n] = pv4[g * rows:(g + 1) * rows] / dens[g]
    return jnp.concatenate([jnp.where(lower, outs[2 * m], outs[2 * m + 1]) for m in range(C_GROUP)], axis=-1)


def _attn_ctx_kernel(sink_ref, q_ref, k_ref, v_ref, gate_ref, o_ref):
    kp = pl.program_id(1)
    q = q_ref[...] * (C_HEAD ** -0.5)
    q_chunks = [q[:, m * LANES:(m + 1) * LANES] for m in range(C_GROUP)]
    o = _pair_heads(q_chunks, [k_ref[...]], [v_ref[...]], lambda n: sink_ref[kp * 2 * C_GROUP + n], [None])
    o_ref[...] = (o * _silu(gate_ref[...])).astype(BF16)


def _attn_context(proj, sink):
    qw = 2 * C_GROUP * C_HEAD
    k_cb = C_HEADS * C_HEAD // LANES
    v_cb = (C_HEADS + C_KV_HEADS) * C_HEAD // LANES
    gate_cb = (C_HEADS + 2 * C_KV_HEADS) * C_HEAD // qw
    return pl.pallas_call(
        _attn_ctx_kernel,
        out_shape=jax.ShapeDtypeStruct((PROMPT.rows, D_MODEL), BF16),
        grid=(BATCH, C_KV_HEADS // 2),
        in_specs=[pl.BlockSpec(memory_space=pltpu.SMEM),
                  pl.BlockSpec((SEQ, qw), lambda b, kp: (b, kp)),
                  pl.BlockSpec((SEQ, LANES), lambda b, kp: (b, k_cb + kp)),
                  pl.BlockSpec((SEQ, LANES), lambda b, kp: (b, v_cb + kp)),
                  pl.BlockSpec((SEQ, qw), lambda b, kp: (b, gate_cb + kp))],
        out_specs=pl.BlockSpec((SEQ, qw), lambda b, kp: (b, kp)),
        compiler_params=_cparams(("parallel", "parallel")),
        name="attn_context",
    )(sink, proj, proj, proj, proj)


def _rope(x, cos, sin_signed):
    lane = lax.broadcasted_iota(jnp.int32, x.shape, 1)
    first = (lane & (2 * ROPE_PAIR - 1)) < ROPE_PAIR
    partner = jnp.where(first, pltpu.roll(x, LANES - ROPE_PAIR, axis=1), pltpu.roll(x, ROPE_PAIR, axis=1))
    return x * cos + partner * sin_signed


def _attn_lat_kernel(sink_ref, q_ref, kp_ref, ko_ref, kn_ref, vp_ref, vo_ref, vn_ref, ck_ref, cv_ref,
                     cq_ref, sq_ref, cp_ref, sp_ref, cn_ref, sn_ref, gate_ref, o_ref):
    kpair = pl.program_id(1)
    qb = pl.program_id(2)
    cq = cq_ref[...]
    sq = sq_ref[...]
    q = q_ref[...] * (C_HEAD ** -0.5)
    qr = [_rope(q[:, n * LANES:(n + 1) * LANES], cq, sq) for n in range(4)]
    kband = jnp.concatenate([_rope(kp_ref[...], cp_ref[...], sp_ref[...]),
                             _rope(ko_ref[...], cq, sq),
                             _rope(kn_ref[...], cn_ref[...], sn_ref[...])], axis=0)
    vband = jnp.concatenate([vp_ref[...], vo_ref[...], vn_ref[...]], axis=0)
    qpos = qb * BLOCK + lax.broadcasted_iota(jnp.int32, (BLOCK, 3 * BLOCK), 0)
    kpos = (qb - 1) * BLOCK + lax.broadcasted_iota(jnp.int32, (BLOCK, 3 * BLOCK), 1)
    valid = (jnp.abs(qpos - kpos) <= WINDOW) & (kpos >= 0) & (kpos < DEC_SEQ)
    o = _pair_heads(qr, [kband, ck_ref[...]], [vband, cv_ref[...]],
                    lambda n: sink_ref[kpair * 2 * C_GROUP + n], [valid, None])
    o_ref[...] = (o * _silu(gate_ref[...])).astype(BF16)


def _attn_latent(proj, cache_k4, cache_v4, li, sink, cos_t, sin_t):
    qw = 2 * C_GROUP * C_HEAD
    k_cb = C_HEADS * C_HEAD // LANES
    v_cb = (C_HEADS + C_KV_HEADS) * C_HEAD // LANES
    gate_cb = (C_HEADS + 2 * C_KV_HEADS) * C_HEAD // qw
    nqb = DEC_SEQ // BLOCK

    def rows(delta):
        def idx(b, kp, qb):
            return b * nqb + jnp.clip(qb + delta, 0, nqb - 1)
        return idx

    def kv_spec(cb, delta):
        r = rows(delta)
        return pl.BlockSpec((BLOCK, LANES), lambda b, kp, qb: (r(b, kp, qb), cb + kp))

    def tab_spec(delta):
        return pl.BlockSpec((BLOCK, LANES), lambda b, kp, qb: (jnp.clip(qb + delta, 0, nqb - 1), 0))

    cache_spec = pl.BlockSpec((None, None, PAST_LEN, LANES), lambda b, kp, qb: (b, li, 0, kp))
    own = rows(0)
    return pl.pallas_call(
        _attn_lat_kernel,
        out_shape=jax.ShapeDtypeStruct((SAMPLE.rows, D_MODEL), BF16),
        grid=(DEC_BATCH, C_KV_HEADS // 2, nqb),
        in_specs=[pl.BlockSpec(memory_space=pltpu.SMEM),
                  pl.BlockSpec((BLOCK, qw), lambda b, kp, qb: (own(b, kp, qb), kp)),
                  kv_spec(k_cb, -1), kv_spec(k_cb, 0), kv_spec(k_cb, 1),
                  kv_spec(v_cb, -1), kv_spec(v_cb, 0), kv_spec(v_cb, 1),
                  cache_spec, cache_spec,
                  tab_spec(0), tab_spec(0), tab_spec(-1), tab_spec(-1), tab_spec(1), tab_spec(1),
                  pl.BlockSpec((BLOCK, qw), lambda b, kp, qb: (own(b, kp, qb), gate_cb + kp))],
        out_specs=pl.BlockSpec((BLOCK, qw), lambda b, kp, qb: (own(b, kp, qb), kp)),
        compiler_params=_cparams(("parallel", "parallel", "parallel")),
        name="attn_latent",
    )(sink, proj, proj, proj, proj, proj, proj, proj, cache_k4, cache_v4,
      cos_t, sin_t, cos_t, sin_t, cos_t, sin_t, proj)


def _rope_tables():
    t = jnp.arange(DEC_SEQ, dtype=jnp.int32)
    row = (t // GRID_W).astype(F32)
    col = (t % GRID_W).astype(F32)
    nf = ROPE_PAIR
    inv = 1.0 / (ROPE_BASE ** (jnp.arange(nf, dtype=F32) / nf))
    lane = np.arange(LANES)
    f_of_lane = lane % nf
    use_col = (lane % C_HEAD) >= C_HEAD // 2
    sign = np.where((lane % (2 * ROPE_PAIR)) < ROPE_PAIR, -1.0, 1.0).astype(np.float32)
    pos = jnp.where(jnp.asarray(use_col)[None, :], col[:, None], row[:, None])
    ang = pos * inv[jnp.asarray(f_of_lane)][None, :]
    return jnp.cos(ang), jnp.sin(ang) * jnp.asarray(sign)[None, :]


def _chain_const_tiles(vec):
    t = vec.reshape(A_HEADS // 2, 2, A_HEAD).transpose(1, 2, 0)
    t = jnp.broadcast_to(t[:, :, None, :, None], (2, A_HEAD, 2, A_HEADS // 2, SEQ_PER_GROUP))
    return t.reshape(2, A_HEAD, CHAINS)


def _even_mixer(st, proj, s0, p, gmat, cs_mat, dft):
    r, k, v, w2, a2, bonus = _wkv_prep(st, proj, p["mu"], p["w0"], p["wup"], p["a0"], p["aup"], p["k_a"], p["r_k"],
                                       gmat)
    o_f, o_b, s_fin = _wkv(r, k, v, w2, a2, p["kkw_t"], p["ka_t"], s0)
    ya = _ya(st, o_f, o_b, bonus, proj, p["gn_w"], p["gn_b"], gmat)
    zc, zs = _dft_channels(st, proj, cs_mat)
    yb = _dft_time(st, zc, zs, proj, dft[0], dft[1])
    return ya, yb, s_fin


def kernel(x_prompt, x_sample, c, state_wkv, cache_k, cache_v, c_ctx, mod_w, mod_b, norm_pre, norm_post,
           even_w_in, even_mu, even_w0, even_w_up, even_a0, even_a_up, even_k_k, even_k_a, even_r_k,
           even_gn_w, even_gn_b, even_w_out, odd_w_in, odd_sink, odd_w_out):
    n_odd = odd_w_in.shape[0]
    xs = {PROMPT: x_prompt.reshape(PROMPT.rows, D_MODEL), SAMPLE: x_sample.reshape(SAMPLE.rows, D_MODEL)}
    conds = jnp.concatenate([c_ctx[None, :], c, jnp.zeros((N_COND - 1 - DEC_BATCH, D_MODEL), F32)], axis=0)
    mods4 = _modulation(conds, mod_w, mod_b).reshape(DEPTH, N_COND, 1, 3 * D_MODEL)
    norm_pre3 = norm_pre.reshape(DEPTH, 1, D_MODEL)
    norm_post3 = norm_post.reshape(DEPTH, 1, D_MODEL)

    even_w_out_b = even_w_out.astype(BF16)
    odd_w_out_b = odd_w_out.astype(BF16)

    head_of_lane = np.arange(CW) // A_HEAD
    gmat = jnp.asarray((head_of_lane[:, None] == head_of_lane[None, :]).astype(np.float32)).astype(BF16)
    c_ch, s_ch = _dft_mats(B_GROUP_CH)
    cs_mat = jnp.concatenate([c_ch, s_ch], axis=1).astype(BF16)
    dft = {PROMPT: tuple(m.astype(BF16) for m in _dft_mats(SEQ)),
           SAMPLE: tuple(m.astype(BF16) for m in _dft_mats(DEC_SEQ))}
    cos_t, sin_t = _rope_tables()
    cache_k4 = cache_k.reshape(DEC_BATCH, n_odd, PAST_LEN, C_KV_HEADS * C_HEAD)
    cache_v4 = cache_v.reshape(DEC_BATCH, n_odd, PAST_LEN, C_KV_HEADS * C_HEAD)

    def pad_rows(w, slot):
        return jnp.pad(w, ((0, 0), (slot * LORA, (3 - slot) * LORA), (0, 0)))

    wup_pad = jnp.stack([pad_rows(even_w_up[:, 0], 0), pad_rows(even_w_up[:, 1], 1)], axis=1)
    aup_pad = jnp.stack([pad_rows(even_a_up[:, 0], 2), pad_rows(even_a_up[:, 1], 3)], axis=1)

    new_wkv, new_k, new_v = [], [], []
    hs = {st: _pre_norm(st, xs[st], norm_pre3, mods4, 0) for st in (PROMPT, SAMPLE)}
    for layer in range(DEPTH):
        i = layer // 2
        if layer % 2 == 0:
            p = {
                "mu": even_mu[i][None, :], "w0": even_w0[i], "a0": even_a0[i],
                "wup": wup_pad[i], "aup": aup_pad[i],
                "k_a": even_k_a[i][None, :], "r_k": even_r_k[i].reshape(1, A_WIDTH),
                "kkw_t": _chain_const_tiles(even_k_k[i]), "ka_t": _chain_const_tiles(even_k_a[i]),
                "gn_w": even_gn_w[i][None, :], "gn_b": even_gn_b[i][None, :],
            }
            s0s = state_wkv[:, i].reshape(DEC_BATCH, 2, A_HEADS // 2, 2, A_HEAD, A_HEAD)
            s0 = {PROMPT: jnp.zeros((BATCH // SEQ_PER_GROUP, 2, A_HEAD, A_HEAD, CHAINS), F32),
                  SAMPLE: s0s.transpose(3, 5, 4, 1, 2, 0).reshape(1, 2, A_HEAD, A_HEAD, CHAINS)}
            for st in (PROMPT, SAMPLE):
                proj = _in_proj(st, hs[st], even_w_in, i)
                ya, yb, s_fin = _even_mixer(st, proj, s0[st], p, gmat, cs_mat, dft[st])
                xs[st], hs[st] = _out_proj(st, ya, 0, yb, 0, even_w_out_b, i, xs[st], mods4, norm_post3, norm_pre3,
                                           layer)
                if st is PROMPT:
                    s_fin = s_fin.reshape(BATCH // SEQ_PER_GROUP, 2, A_HEAD, A_HEAD, 2, A_HEADS // 2, SEQ_PER_GROUP)
                    new_wkv.append(s_fin.transpose(0, 6, 4, 5, 1, 3, 2).reshape(BATCH, 2, A_HEADS, A_HEAD, A_HEAD))
        else:
            proj_p = _in_proj(PROMPT, hs[PROMPT], odd_w_in, i)
            proj_s = _in_proj(SAMPLE, hs[SAMPLE], odd_w_in, i)
            kv0 = C_HEADS * C_HEAD
            kvn = C_KV_HEADS * C_HEAD
            new_k.append(proj_p[:, kv0:kv0 + kvn].reshape(BATCH, SEQ, C_KV_HEADS, C_HEAD))
            new_v.append(proj_p[:, kv0 + kvn:kv0 + 2 * kvn].reshape(BATCH, SEQ, C_KV_HEADS, C_HEAD))
            att_p = _attn_context(proj_p, odd_sink[i])
            att_s = _attn_latent(proj_s, cache_k4, cache_v4, i, odd_sink[i], cos_t, sin_t)
            for st, att in ((PROMPT, att_p), (SAMPLE, att_s)):
                xs[st], hs[st] = _out_proj(st, att, 0, att, 1, odd_w_out_b, i, xs[st], mods4, norm_post3, norm_pre3,
                                           layer)

    y_prompt = xs[PROMPT].reshape(BATCH, SEQ, D_MODEL)
    y_sample = xs[SAMPLE].reshape(DEC_BATCH, DEC_SEQ, D_MODEL)
    return (y_prompt, y_sample, jnp.stack(new_wkv, axis=1), jnp.stack(new_k, axis=1), jnp.stack(new_v, axis=1))
```

```python
import collections
import functools

import numpy as np
import jax
import jax.numpy as jnp
from jax import lax
from jax.experimental import pallas as pl
from jax.experimental.pallas import tpu as pltpu

F32 = jnp.float32
BF16 = jnp.bfloat16

D_MODEL = 2048
BATCH = 32
SEQ = 256
DEPTH = 4
DEC_BATCH = 8
DEC_SEQ = 2048
PAST_LEN = 256
GRID_W = 64
RMS_EPS = 1e-6
A_WIDTH = 1024
A_HEAD = 64
A_HEADS = 16
LORA = 64
GN_EPS = 64e-5
DECAY_SCALE = float(np.exp(-0.5))
B_WIDTH = 1024
B_GROUPS = 4
B_GROUP_CH = 256
SHIFT_COLS = 3 * A_WIDTH + 4 * LORA
EVEN_IN = SHIFT_COLS + A_WIDTH + 2 * B_WIDTH
C_HEAD = 64
C_HEADS = 32
C_KV_HEADS = 8
C_GROUP = 4
WINDOW = 128
BLOCK = 128
ROPE_BASE = 10000.0
ROPE_PAIR = C_HEAD // 4
KK_NORM_FLOOR = 1e-12
ODD_IN = (C_HEADS + 2 * C_KV_HEADS) * C_HEAD + D_MODEL
NEG_INF = -1e30

N_COND = 16
SUBLANES = 8
LANES = 128
CHAINS = LANES
SEQ_PER_GROUP = CHAINS // A_HEADS
CW = 256
TM_IN = 1024
TN_IN = 1280
TM_ROW = 256
TM_OUT = 512
OUT_SUBTILES = 2
PRE_SUBTILES = 8
WKV_TT = 32
BIG_VMEM_LIMIT = 56 * 1024 * 1024
WKV_ROWS = 32
WKV_ACCS = 2
VMEM_LIMIT = 48 * 1024 * 1024

Stream = collections.namedtuple("Stream", "rows t_len n_seq cond0 cond_per_seq")
PROMPT = Stream(BATCH * SEQ, SEQ, BATCH, 0, 0)
SAMPLE = Stream(DEC_BATCH * DEC_SEQ, DEC_SEQ, DEC_BATCH, 1, 1)


def _cparams(sem):
    return pltpu.CompilerParams(dimension_semantics=sem, vmem_limit_bytes=VMEM_LIMIT)


def _cond_of_row(st, row0):
    return st.cond0 + st.cond_per_seq * (row0 // st.t_len)


def _sigmoid(x):
    return 1.0 / (1.0 + jnp.exp(-x))


def _silu(x):
    return x * _sigmoid(x)


def _split(a):
    hi = a.astype(BF16)
    lo = (a - hi.astype(F32)).astype(BF16)
    return hi, lo


def _dot(a, b):
    return jnp.dot(a, b, preferred_element_type=F32)


def _dot3_split(a_hl, b_hl):
    (ah, al), (bh, bl) = a_hl, b_hl
    return _dot(ah, bh) + (_dot(ah, bl) + _dot(al, bh))


def _dot3(a, b):
    return _dot3_split(_split(a), _split(b))


def _gsum(x, gmat):
    xh, xl = _split(x)
    return _dot(xh, gmat) + _dot(xl, gmat)


def _mod_kernel(c_ref, w_ref, b_ref, o_ref):
    o_ref[...] = _dot3(_silu(c_ref[...]), w_ref[...]) + b_ref[...]


def _modulation(conds, mod_w, mod_b):
    tn = 512
    n = 3 * D_MODEL
    return pl.pallas_call(
        _mod_kernel,
        out_shape=jax.ShapeDtypeStruct((DEPTH, N_COND, n), F32),
        grid=(DEPTH, n // tn),
        in_specs=[
            pl.BlockSpec((N_COND, D_MODEL), lambda l, j: (0, 0)),
            pl.BlockSpec((None, D_MODEL, tn), lambda l, j: (l, 0, j)),
            pl.BlockSpec((None, 1, tn), lambda l, j: (l, 0, j)),
        ],
        out_specs=pl.BlockSpec((None, N_COND, tn), lambda l, j: (l, 0, j)),
        compiler_params=_cparams(("parallel", "parallel")),
        name="modulation",
    )(conds, mod_w, mod_b.reshape(DEPTH, 1, n))


def _norm_modulate(x, g, sh, sc):
    ms = jnp.mean(x * x, axis=-1, keepdims=True)
    y = x * lax.rsqrt(ms + RMS_EPS) * g
    return (y * (1.0 + sc) + sh).astype(BF16)


def _pre_kernel(x_ref, g_ref, sh_ref, sc_ref, h_ref):
    sub = x_ref.shape[0] // PRE_SUBTILES
    for s in range(PRE_SUBTILES):
        rows = slice(s * sub, (s + 1) * sub)
        h_ref[rows, :] = _norm_modulate(x_ref[rows, :], g_ref[...], sh_ref[...], sc_ref[...])


def _pre_norm(st, x, norm_g, mods4, layer):
    tm = TM_ROW
    cond = lambda i: _cond_of_row(st, i * tm)
    return pl.pallas_call(
        _pre_kernel,
        out_shape=jax.ShapeDtypeStruct((st.rows, D_MODEL), BF16),
        grid=(st.rows // tm,),
        in_specs=[pl.BlockSpec((tm, D_MODEL), lambda i: (i, 0)),
                  pl.BlockSpec((None, 1, D_MODEL), lambda i: (layer, 0, 0)),
                  pl.BlockSpec((None, None, 1, D_MODEL), lambda i: (layer, cond(i), 0, 0)),
                  pl.BlockSpec((None, None, 1, D_MODEL), lambda i: (layer, cond(i), 0, 1))],
        out_specs=pl.BlockSpec((tm, D_MODEL), lambda i: (i, 0)),
        compiler_params=_cparams(("parallel",)),
        name="pre_norm",
    )(x, norm_g, mods4, mods4)


def _in_kernel(h_ref, w_ref, o_ref, wb_ref):
    @pl.when(pl.program_id(1) == 0)
    def _():
        wb_ref[...] = w_ref[...].astype(BF16)

    o_ref[...] = _dot(h_ref[...], wb_ref[...])


def _in_proj(st, h, w, li):
    n = w.shape[-1]
    return pl.pallas_call(
        _in_kernel,
        out_shape=jax.ShapeDtypeStruct((st.rows, n), F32),
        grid=(n // TN_IN, st.rows // TM_IN),
        in_specs=[pl.BlockSpec((TM_IN, D_MODEL), lambda j, i: (i, 0)),
                  pl.BlockSpec((None, D_MODEL, TN_IN), lambda j, i: (li, 0, j))],
        out_specs=pl.BlockSpec((TM_IN, TN_IN), lambda j, i: (i, j)),
        scratch_shapes=[pltpu.VMEM((D_MODEL, TN_IN), BF16)],
        compiler_params=pltpu.CompilerParams(dimension_semantics=("parallel", "arbitrary"),
                                             vmem_limit_bytes=BIG_VMEM_LIMIT),
        name="in_proj",
    )(h, w)


def _token_shift(cur, prev8, next8, mu, first, last):
    tm = cur.shape[0]
    rows = lax.broadcasted_iota(jnp.int32, cur.shape, 0)
    prow = jnp.where(first, 0.0, prev8[7:8, :])
    nrow = jnp.where(last, 0.0, next8[0:1, :])
    up = jnp.where(rows == 0, prow, pltpu.roll(cur, 1, axis=0))
    dn = jnp.where(rows == tm - 1, nrow, pltpu.roll(cur, tm - 1, axis=0))
    return cur + mu * (0.5 * (up + dn) - cur)


def _prep_kernel(r_ref, k_ref, v_ref, lo_ref, rp_ref, kp_ref, vp_ref, lp_ref, rn_ref, kn_ref, vn_ref, ln_ref,
                 mur_ref, muk_ref, muv_ref, mul_ref, w0_ref, wuph_ref, wupl_ref, a0_ref, auph_ref, aupl_ref,
                 ka_ref, rk_ref, gm_ref,
                 r_o, k_o, v_o, w_o, a_o, bon_o, *, tiles_per_seq):
    i = pl.program_id(0)
    first = lax.rem(i, tiles_per_seq) == 0
    last = lax.rem(i, tiles_per_seq) == tiles_per_seq - 1

    low = _token_shift(lo_ref[...], lp_ref[...], ln_ref[...], mul_ref[...], first, last)
    low_hl = _split(low)
    low_t_hl = _split(jnp.tanh(low))
    gm = gm_ref[...]

    for c in range(A_WIDTH // CW):
        cols = slice(c * CW, (c + 1) * CW)
        r = _token_shift(r_ref[:, cols], rp_ref[:, cols], rn_ref[:, cols], mur_ref[:, cols], first, last)
        k = _token_shift(k_ref[:, cols], kp_ref[:, cols], kn_ref[:, cols], muk_ref[:, cols], first, last)
        v = _token_shift(v_ref[:, cols], vp_ref[:, cols], vn_ref[:, cols], muv_ref[:, cols], first, last)
        ka = ka_ref[:, cols]
        kd_sum = None
        for d in range(2):
            w_raw = w0_ref[d:d + 1, cols] + _dot3_split(low_t_hl, (wuph_ref[d, :, cols], wupl_ref[d, :, cols]))
            a = _sigmoid(a0_ref[d:d + 1, cols] + _dot3_split(low_hl, (auph_ref[d, :, cols], aupl_ref[d, :, cols])))
            kd = k * (1.0 + (a - 1.0) * ka)
            w_o[d, :, cols] = jnp.exp(-DECAY_SCALE * _sigmoid(w_raw))
            a_o[d, :, cols] = a
            kd_sum = kd if kd_sum is None else kd_sum + kd
        r_o[:, cols] = r
        k_o[:, cols] = k
        v_o[:, cols] = v
        bon_o[:, cols] = _gsum(r * (0.5 * kd_sum) * rk_ref[:, cols], gm) * v


def _wkv_prep(st, proj, mu, w0, wup_pad, a0, aup_pad, k_a, r_k, gmat):
    tm = TM_ROW
    nb8 = st.rows // SUBLANES
    low_cb = 3 * A_WIDTH // CW
    assert 4 * LORA == CW and st.t_len % tm == 0

    prev_row = lambda i: jnp.maximum(i * (tm // SUBLANES) - 1, 0)
    next_row = lambda i: jnp.minimum((i + 1) * (tm // SUBLANES), nb8 - 1)

    def main(cb):
        return pl.BlockSpec((tm, A_WIDTH), lambda i: (i, cb))

    def prev(cb):
        return pl.BlockSpec((SUBLANES, A_WIDTH), lambda i: (prev_row(i), cb))

    def nxt(cb):
        return pl.BlockSpec((SUBLANES, A_WIDTH), lambda i: (next_row(i), cb))

    def fixed(spec_fn):
        return [spec_fn(0), spec_fn(1), spec_fn(2)]

    low_main = pl.BlockSpec((tm, CW), lambda i: (i, low_cb))
    low_prev = pl.BlockSpec((SUBLANES, CW), lambda i: (prev_row(i), low_cb))
    low_next = pl.BlockSpec((SUBLANES, CW), lambda i: (next_row(i), low_cb))

    def vec(cb):
        return pl.BlockSpec((1, A_WIDTH), lambda i: (0, cb))

    in_specs = (fixed(main) + [low_main] + fixed(prev) + [low_prev] + fixed(nxt) + [low_next]
                + [vec(0), vec(1), vec(2), pl.BlockSpec((1, CW), lambda i: (0, low_cb))]
                + [pl.BlockSpec((2, A_WIDTH), lambda i: (0, 0)),
                   pl.BlockSpec((2, CW, A_WIDTH), lambda i: (0, 0, 0)),
                   pl.BlockSpec((2, CW, A_WIDTH), lambda i: (0, 0, 0)),
                   pl.BlockSpec((2, A_WIDTH), lambda i: (0, 0)),
                   pl.BlockSpec((2, CW, A_WIDTH), lambda i: (0, 0, 0)),
                   pl.BlockSpec((2, CW, A_WIDTH), lambda i: (0, 0, 0)),
                   vec(0), vec(0),
                   pl.BlockSpec((CW, CW), lambda i: (0, 0))])
    tps = st.t_len // tm
    g_n = st.n_seq // SEQ_PER_GROUP

    def til(i):
        s = i // tps
        return (s // SEQ_PER_GROUP, lax.rem(i, tps), lax.rem(s, SEQ_PER_GROUP))

    one = pl.BlockSpec((tm, A_WIDTH), lambda i: (i, 0))
    til1 = pl.BlockSpec((None, tm, A_WIDTH), lambda i: til(i))
    til2 = pl.BlockSpec((2, None, tm, A_WIDTH), lambda i: (0,) + til(i))
    one_sds = jax.ShapeDtypeStruct((st.rows, A_WIDTH), F32)
    til1_sds = jax.ShapeDtypeStruct((g_n, st.t_len, SEQ_PER_GROUP * A_WIDTH), F32)
    til2_sds = jax.ShapeDtypeStruct((2, g_n, st.t_len, SEQ_PER_GROUP * A_WIDTH), F32)
    return pl.pallas_call(
        functools.partial(_prep_kernel, tiles_per_seq=tps),
        out_shape=[til1_sds, til1_sds, til1_sds, til2_sds, til2_sds, one_sds],
        grid=(st.rows // tm,),
        in_specs=in_specs,
        out_specs=[til1, til1, til1, til2, til2, one],
        compiler_params=_cparams(("parallel",)),
        name="wkv_prep",
    )(*([proj] * 12), mu, mu, mu, mu, w0, *_split(wup_pad), a0, *_split(aup_pad), k_a, r_k, gmat)


def _wkv_kernel(rf_ref, rb_ref, kf_ref, kb_ref, vf_ref, vb_ref, wf_ref, wb_ref, af_ref, ab_ref, kkw_ref, ka_ref,
                s0_ref, of_ref, ob_ref, sf_ref,
                s2_ref, r2_ref, kk2_ref, v2_ref, kka2_ref, kd2_ref, g_ref, o_ref, *, tt_steps):
    c = pl.program_id(1)
    nblk = WKV_ROWS // SUBLANES
    n_hv = A_HEADS // 2
    pairs = ((rf_ref, rb_ref), (kf_ref, kb_ref), (vf_ref, vb_ref), (wf_ref, wb_ref), (af_ref, ab_ref))

    @pl.when(c == 0)
    def _():
        s2_ref[...] = s0_ref[...]

    def to_chain_tiles(u, gammas):
        ub = tt_steps - 1 - u
        halves = []
        for xf_ref, xb_ref in pairs:
            x2 = jnp.concatenate([xf_ref[u, :, hv * LANES:(hv + 1) * LANES] for hv in range(n_hv)]
                                 + [xb_ref[ub, :, hv * LANES:(hv + 1) * LANES] for hv in range(n_hv)], axis=0)
            y = x2.T
            halves.append((y[:A_HEAD], y[A_HEAD:]))
        r_h, k_h, v_h, w_h, a_h = halves
        new_gammas = []
        for hp in range(2):
            kk = k_h[hp] * kkw_ref[hp]
            norm = jnp.sqrt(jnp.sum(kk * kk, axis=0, keepdims=True))
            kk = kk / jnp.maximum(norm, KK_NORM_FLOOR)
            kka = kk * a_h[hp]
            kd = k_h[hp] * (1.0 + (a_h[hp] - 1.0) * ka_ref[hp])
            gamma = gammas[hp] * w_h[hp]
            inv = 1.0 / gamma
            kk2_ref[hp, u] = gammas[hp] * kk
            kka2_ref[hp, u] = kka * inv
            kd2_ref[hp, u] = kd * inv
            r2_ref[hp, u] = r_h[hp] * gamma
            v2_ref[hp, u] = v_h[hp]
            new_gammas.append(gamma)
        return tuple(new_gammas)

    one = jnp.ones((A_HEAD, CHAINS), F32)
    gammas = lax.fori_loop(0, tt_steps, to_chain_tiles, (one, one), unroll=8)
    for hp in range(2):
        g_ref[hp] = gammas[hp]

    def bcast_row(ref, t, j):
        return ref[t, pl.ds(j, SUBLANES, stride=0), :]

    def time_step(hp, t):
        s_ref = s2_ref.at[hp]
        r_ref, kk_ref, v_ref, kka_ref, kd_ref = (ref.at[hp] for ref in (r2_ref, kk2_ref, v2_ref, kka2_ref, kd2_ref))
        for ib in range(A_HEAD // WKV_ROWS):
            i0 = ib * WKV_ROWS
            rows = [pl.ds(i0 + SUBLANES * b, SUBLANES) for b in range(nblk)]
            sa = [[None] * WKV_ACCS for _ in range(nblk)]
            for j in range(A_HEAD):
                kkj = bcast_row(kk_ref, t, j)
                for b in range(nblk):
                    p = s_ref[j, rows[b], :] * kkj
                    sa[b][j % WKV_ACCS] = p if sa[b][j % WKV_ACCS] is None else sa[b][j % WKV_ACCS] + p
            sa = [functools.reduce(lambda x, y: x + y, parts) for parts in sa]
            v8 = [v_ref[t, rows[b], :] for b in range(nblk)]
            out = [[None] * WKV_ACCS for _ in range(nblk)]
            for j in range(A_HEAD):
                kkaj = bcast_row(kka_ref, t, j)
                kdj = bcast_row(kd_ref, t, j)
                rj = bcast_row(r_ref, t, j)
                for b in range(nblk):
                    sn = s_ref[j, rows[b], :] - sa[b] * kkaj + v8[b] * kdj
                    s_ref[j, rows[b], :] = sn
                    q = sn * rj
                    out[b][j % WKV_ACCS] = q if out[b][j % WKV_ACCS] is None else out[b][j % WKV_ACCS] + q
            for b in range(nblk):
                o_ref[hp, rows[b], :] = functools.reduce(lambda x, y: x + y, out[b])

    def step(u, carry):
        for hp in range(2):
            time_step(hp, u)
        x2 = jnp.concatenate([o_ref[0], o_ref[1]], axis=0).T
        ub = tt_steps - 1 - u
        for hv in range(n_hv):
            of_ref[u, :, hv * LANES:(hv + 1) * LANES] = x2[hv * SUBLANES:(hv + 1) * SUBLANES, :]
            ob_ref[ub, :, hv * LANES:(hv + 1) * LANES] = x2[(n_hv + hv) * SUBLANES:(n_hv + hv + 1) * SUBLANES, :]
        return carry

    lax.fori_loop(0, tt_steps, step, 0)

    def rescale(j, carry):
        for hp in range(2):
            gj = g_ref[hp, pl.ds(j, SUBLANES, stride=0), :]
            for b in range(A_HEAD // SUBLANES):
                rows = pl.ds(b * SUBLANES, SUBLANES)
                s2_ref[hp, j, rows, :] = s2_ref[hp, j, rows, :] * gj
        return carry

    lax.fori_loop(0, A_HEAD, rescale, 0)

    @pl.when(c == pl.num_programs(1) - 1)
    def _():
        sf_ref[...] = s2_ref[...]


def _wkv(r, k, v, w, a, kkw_t, ka_t, s0):
    g_n, t_n = r.shape[0], r.shape[1]
    tt = WKV_TT
    nc = t_n // tt
    split = lambda a: a.reshape(a.shape[:-1] + (SEQ_PER_GROUP, A_WIDTH))
    blk = (tt, SEQ_PER_GROUP, A_WIDTH)

    fwd = pl.BlockSpec((None,) + blk, lambda g, c: (g, c, 0, 0))
    bwd = pl.BlockSpec((None,) + blk, lambda g, c: (g, nc - 1 - c, 0, 0))
    fwd_d = pl.BlockSpec((None, None) + blk, lambda g, c: (0, g, c, 0, 0))
    bwd_d = pl.BlockSpec((None, None) + blk, lambda g, c: (1, g, nc - 1 - c, 0, 0))
    state = pl.BlockSpec((None, 2, A_HEAD, A_HEAD, CHAINS), lambda g, c: (g, 0, 0, 0, 0),
                         pipeline_mode=pl.Buffered(1))
    tiles = pltpu.VMEM((2, tt, A_HEAD, CHAINS), F32)
    o_sds = jax.ShapeDtypeStruct((g_n, t_n, SEQ_PER_GROUP, A_WIDTH), F32)
    r, k, v, w, a = (split(x) for x in (r, k, v, w, a))
    const = pl.BlockSpec((2, A_HEAD, CHAINS), lambda g, c: (0, 0, 0))
    o_f, o_b, s_fin = pl.pallas_call(
        functools.partial(_wkv_kernel, tt_steps=tt),
        out_shape=[o_sds, o_sds, jax.ShapeDtypeStruct((g_n, 2, A_HEAD, A_HEAD, CHAINS), F32)],
        grid=(g_n, nc),
        in_specs=[fwd, bwd, fwd, bwd, fwd, bwd, fwd_d, bwd_d, fwd_d, bwd_d, const, const, state],
        out_specs=[fwd, bwd, state],
        scratch_shapes=([pltpu.VMEM((2, A_HEAD, A_HEAD, CHAINS), F32)] + [tiles] * 5
                        + [pltpu.VMEM((2, A_HEAD, CHAINS), F32)] * 2),
        compiler_params=pltpu.CompilerParams(dimension_semantics=("parallel", "arbitrary"),
                                             vmem_limit_bytes=BIG_VMEM_LIMIT),
        name="wkv",
    )(r, r, k, k, v, v, w, w, a, a, kkw_t, ka_t, s0)
    merge = lambda a: a.reshape(g_n, t_n, SEQ_PER_GROUP * A_WIDTH)
    return merge(o_f), merge(o_b), s_fin


def _ya_kernel(of_ref, ob_ref, bon_ref, g0_ref, g1_ref, g2_ref, g3_ref, gw_ref, gb_ref, gm_ref, o_ref):
    gm = gm_ref[...]
    for c, gate_ref in enumerate((g0_ref, g1_ref, g2_ref, g3_ref)):
        cols = slice(c * CW, (c + 1) * CW)
        o = of_ref[:, cols] + ob_ref[:, cols]
        mean = _gsum(o, gm) * (1.0 / A_HEAD)
        dev = o - mean
        var = _gsum(dev * dev, gm) * (1.0 / A_HEAD)
        y = dev * lax.rsqrt(var + GN_EPS) * gw_ref[:, cols] + gb_ref[:, cols]
        o_ref[:, cols] = ((y + bon_ref[:, cols]) * _silu(gate_ref[...])).astype(BF16)


def _ya(st, o_f, o_b, bonus, proj, gn_w, gn_b, gmat):
    tm = TM_ROW
    gate_cb = (SHIFT_COLS) // CW
    blk = pl.BlockSpec((tm, A_WIDTH), lambda i: (i, 0))
    vec = pl.BlockSpec((1, A_WIDTH), lambda i: (0, 0))
    tps = st.t_len // tm

    def til_idx(i):
        s = i // tps
        return (s // SEQ_PER_GROUP, lax.rem(i, tps), lax.rem(s, SEQ_PER_GROUP))

    til = pl.BlockSpec((None, tm, A_WIDTH), til_idx)
    gates = [pl.BlockSpec((tm, CW), lambda i, c=c: (i, gate_cb + c)) for c in range(A_WIDTH // CW)]
    return pl.pallas_call(
        _ya_kernel,
        out_shape=jax.ShapeDtypeStruct((st.rows, A_WIDTH), BF16),
        grid=(st.rows // tm,),
        in_specs=[til, til, blk] + gates + [vec, vec, pl.BlockSpec((CW, CW), lambda i: (0, 0))],
        out_specs=blk,
        compiler_params=_cparams(("parallel",)),
        name="wkv_post",
    )(o_f, o_b, bonus, proj, proj, proj, proj, gn_w, gn_b, gmat)


def _dft2_kernel(c_ref, s_ref, cs_ref, u0_ref, u1_ref, u2_ref, u3_ref, g0_ref, g1_ref, g2_ref, g3_ref, o_ref,
                 zc_ref, zs_ref, *, scale):
    tm = o_ref.shape[0]
    t_len = zc_ref.shape[0]

    @pl.when(pl.program_id(1) == 0)
    def _():
        for g, u_ref in enumerate((u0_ref, u1_ref, u2_ref, u3_ref)):
            cols = slice(g * B_GROUP_CH, (g + 1) * B_GROUP_CH)
            for r in range(t_len // tm):
                rws = slice(r * tm, (r + 1) * tm)
                z = _dot(u_ref[rws, :].astype(BF16), cs_ref[...])
                zc_ref[rws, cols] = z[:, :B_GROUP_CH].astype(BF16)
                zs_ref[rws, cols] = z[:, B_GROUP_CH:].astype(BF16)

    rows = pl.ds(pl.multiple_of(pl.program_id(1) * tm, tm), tm)
    acc = _dot(c_ref[rows, :], zc_ref[...]) - _dot(s_ref[rows, :], zs_ref[...])
    for g, gate_ref in enumerate((g0_ref, g1_ref, g2_ref, g3_ref)):
        cols = slice(g * B_GROUP_CH, (g + 1) * B_GROUP_CH)
        o_ref[:, cols] = (acc[:, cols] * scale * _silu(gate_ref[...])).astype(BF16)


def _dft2(st, proj, cs_mat, cmat, smat):
    t_len = st.t_len
    tm = min(t_len, 512)
    mt = t_len // tm
    whole = pl.BlockSpec((t_len, t_len), lambda b, m: (0, 0), pipeline_mode=pl.Buffered(1))
    gate_cb = (SHIFT_COLS + A_WIDTH + B_WIDTH) // B_GROUP_CH
    gate_specs = [pl.BlockSpec((tm, B_GROUP_CH), lambda b, m, g=g: (b * mt + m, gate_cb + g)) for g in range(B_GROUPS)]
    scale = 1.0 / float(np.sqrt(t_len * B_GROUP_CH))
    u_cb = (SHIFT_COLS + A_WIDTH) // B_GROUP_CH
    u_specs = [pl.BlockSpec((t_len, B_GROUP_CH), lambda b, m, g=g: (b, u_cb + g)) for g in range(B_GROUPS)]
    z_scratch = pltpu.VMEM((t_len, B_WIDTH), BF16)
    return pl.pallas_call(
        functools.partial(_dft2_kernel, scale=scale),
        out_shape=jax.ShapeDtypeStruct((st.rows, B_WIDTH), BF16),
        grid=(st.n_seq, mt),
        in_specs=[whole, whole, pl.BlockSpec((B_GROUP_CH, 2 * B_GROUP_CH), lambda b, m: (0, 0))] + u_specs + gate_specs,
        out_specs=pl.BlockSpec((tm, B_WIDTH), lambda b, m: (b * mt + m, 0)),
        scratch_shapes=[z_scratch, z_scratch],
        compiler_params=pltpu.CompilerParams(dimension_semantics=("parallel", "arbitrary"),
                                             vmem_limit_bytes=BIG_VMEM_LIMIT),
        name="dft2",
    )(cmat, smat, cs_mat, proj, proj, proj, proj, proj, proj, proj, proj)


def _dft_mats(n):
    idx = jnp.arange(n, dtype=jnp.int32)
    prod = (idx[:, None] * idx[None, :]) % n
    ang = prod.astype(F32) * (2.0 * np.pi / n)
    return jnp.cos(ang), jnp.sin(ang)


def _out_kernel(a1_ref, a2_ref, w1_ref, w2_ref, x_ref, gate_ref, g_ref, *rest, emit_next):
    if emit_next:
        ng_ref, nsh_ref, nsc_ref, o_ref, h_ref = rest
    else:
        o_ref, = rest
    sub = x_ref.shape[0] // OUT_SUBTILES
    for s in range(OUT_SUBTILES):
        rows = slice(s * sub, (s + 1) * sub)
        y = _dot(a1_ref[rows, :], w1_ref[...]) + _dot(a2_ref[rows, :], w2_ref[...])
        ms = jnp.mean(y * y, axis=-1, keepdims=True)
        yn = y * lax.rsqrt(ms + RMS_EPS) * g_ref[...]
        x_new = x_ref[rows, :] + gate_ref[...] * yn
        if emit_next:
            h_ref[rows, :] = _norm_modulate(x_new, ng_ref[...], nsh_ref[...], nsc_ref[...])
        o_ref[rows, :] = x_new


def _out_proj(st, a1, a1_cb, a2, a2_cb, w_bf16, li, x, mods4, norm_post, norm_pre, layer):
    tm = TM_OUT
    half = D_MODEL // 2
    cond = lambda i: _cond_of_row(st, i * tm)
    emit_next = layer + 1 < DEPTH
    row = pl.BlockSpec((tm, D_MODEL), lambda i: (i, 0))
    in_specs = [pl.BlockSpec((tm, half), lambda i: (i, a1_cb)),
                pl.BlockSpec((tm, half), lambda i: (i, a2_cb)),
                pl.BlockSpec((None, half, D_MODEL), lambda i: (li, 0, 0)),
                pl.BlockSpec((None, half, D_MODEL), lambda i: (li, 1, 0)),
                row,
                pl.BlockSpec((None, None, 1, D_MODEL), lambda i: (layer, cond(i), 0, 2)),
                pl.BlockSpec((None, 1, D_MODEL), lambda i: (layer, 0, 0))]
    args = [a1, a2, w_bf16, w_bf16, x, mods4, norm_post]
    out_shape = [jax.ShapeDtypeStruct((st.rows, D_MODEL), F32)]
    out_specs = [row]
    if emit_next:
        in_specs += [pl.BlockSpec((None, 1, D_MODEL), lambda i: (layer + 1, 0, 0)),
                     pl.BlockSpec((None, None, 1, D_MODEL), lambda i: (layer + 1, cond(i), 0, 0)),
                     pl.BlockSpec((None, None, 1, D_MODEL), lambda i: (layer + 1, cond(i), 0, 1))]
        args += [norm_pre, mods4, mods4]
        out_shape.append(jax.ShapeDtypeStruct((st.rows, D_MODEL), BF16))
        out_specs.append(row)
    outs = pl.pallas_call(
        functools.partial(_out_kernel, emit_next=emit_next),
        out_shape=out_shape,
        grid=(st.rows // tm,),
        in_specs=in_specs,
        out_specs=out_specs,
        compiler_params=pltpu.CompilerParams(dimension_semantics=("parallel",), vmem_limit_bytes=BIG_VMEM_LIMIT),
        name="out_proj",
    )(*args)
    return (outs[0], outs[1]) if emit_next else (outs[0], None)


def _softmax_pv(scores, values, sink):
    m = sink
    for s in scores:
        m = jnp.maximum(m, jnp.max(s, axis=-1, keepdims=True))
    den = jnp.exp(sink - m)
    acc = None
    for s, v in zip(scores, values):
        p = jnp.exp(s - m)
        den = den + jnp.sum(p, axis=-1, keepdims=True)
        pv = _dot(p.astype(BF16), v)
        acc = pv if acc is None else acc + pv
    return acc / den


def _qk(q, k):
    return lax.dot_general(q, k, (((1,), (1,)), ((), ())), preferred_element_type=F32)


def _upper_half(rows):
    return lax.broadcasted_iota(jnp.int32, (rows, LANES), 1) >= C_HEAD


def _both_halves(x, hh):
    upper = _upper_half(x.shape[0])
    keep = upper if hh == 1 else jnp.logical_not(upper)
    return jnp.where(keep, x, pltpu.roll(x, C_HEAD, axis=1)).astype(BF16)


def _pair_heads(q_chunks, keys, values, sinks, masks):
    rows = q_chunks[0].shape[0]
    upper = _upper_half(rows)
    lower = jnp.logical_not(upper)
    outs = [None] * (2 * C_GROUP)
    for hh in range(2):
        kd = [_both_halves(k, hh) for k in keys]
        vd = [_both_halves(v, hh) for v in values]
        heads = [hh * C_GROUP + g for g in range(C_GROUP)]
        q4 = jnp.concatenate([jnp.where(upper if n % 2 else lower, q_chunks[n // 2], 0.0) for n in heads],
                             axis=0).astype(BF16)
        s4 = [_qk(q4, k) for k in kd]
        p4 = [[] for _ in kd]
        dens = []
        for g, n in enumerate(heads):
            sl = slice(g * rows, (g + 1) * rows)
            scores = [s[sl] if mask is None else jnp.where(mask, s[sl], NEG_INF) for s, mask in zip(s4, masks)]
            sink = sinks(n)
            m = sink
            for s in scores:
                m = jnp.maximum(m, jnp.max(s, axis=-1, keepdims=True))
            den = jnp.exp(sink - m)
            for t, s in enumerate(scores):
                p = jnp.exp(s - m)
                den = den + jnp.sum(p, axis=-1, keepdims=True)
                p4[t].append(p.astype(BF16))
            dens.append(den)
        pv4 = None
        for p_parts, v in zip(p4, vd):
            pv = _dot(jnp.concatenate(p_parts, axis=0), v)
            pv4 = pv if pv4 is None else pv4 + pv
        for g, n in enumerate(heads):
            outs[n] = pv4[g * rows:(g + 1) * rows] / dens[g]
    return jnp.concatenate([jnp.where(lower, outs[2 * m], outs[2 * m + 1]) for m in range(C_GROUP)], axis=-1)


def _attn_ctx_kernel(sink_ref, q_ref, k_ref, v_ref, gate_ref, o_ref):
    kp = pl.program_id(1)
    q = q_ref[...] * (C_HEAD ** -0.5)
    q_chunks = [q[:, m * LANES:(m + 1) * LANES] for m in range(C_GROUP)]
    o = _pair_heads(q_chunks, [k_ref[...]], [v_ref[...]], lambda n: sink_ref[kp * 2 * C_GROUP + n], [None])
    o_ref[...] = (o * _silu(gate_ref[...])).astype(BF16)


def _attn_context(proj, sink):
    qw = 2 * C_GROUP * C_HEAD
    k_cb = C_HEADS * C_HEAD // LANES
    v_cb = (C_HEADS + C_KV_HEADS) * C_HEAD // LANES
    gate_cb = (C_HEADS + 2 * C_KV_HEADS) * C_HEAD // qw
    return pl.pallas_call(
        _attn_ctx_kernel,
        out_shape=jax.ShapeDtypeStruct((PROMPT.rows, D_MODEL), BF16),
        grid=(BATCH, C_KV_HEADS // 2),
        in_specs=[pl.BlockSpec(memory_space=pltpu.SMEM),
                  pl.BlockSpec((SEQ, qw), lambda b, kp: (b, kp)),
                  pl.BlockSpec((SEQ, LANES), lambda b, kp: (b, k_cb + kp)),
                  pl.BlockSpec((SEQ, LANES), lambda b, kp: (b, v_cb + kp)),
                  pl.BlockSpec((SEQ, qw), lambda b, kp: (b, gate_cb + kp))],
        out_specs=pl.BlockSpec((SEQ, qw), lambda b, kp: (b, kp)),
        compiler_params=_cparams(("parallel", "parallel")),
        name="attn_context",
    )(sink, proj, proj, proj, proj)


def _rope(x, cos, sin_signed):
    lane = lax.broadcasted_iota(jnp.int32, x.shape, 1)
    first = (lane & (2 * ROPE_PAIR - 1)) < ROPE_PAIR
    partner = jnp.where(first, pltpu.roll(x, LANES - ROPE_PAIR, axis=1), pltpu.roll(x, ROPE_PAIR, axis=1))
    return x * cos + partner * sin_signed


def _attn_lat_kernel(sink_ref, q_ref, kp_ref, ko_ref, kn_ref, vp_ref, vo_ref, vn_ref, ck_ref, cv_ref,
                     cq_ref, sq_ref, cp_ref, sp_ref, cn_ref, sn_ref, gate_ref, o_ref):
    kpair = pl.program_id(1)
    qb = pl.program_id(2)
    cq = cq_ref[...]
    sq = sq_ref[...]
    q = q_ref[...] * (C_HEAD ** -0.5)
    qr = [_rope(q[:, n * LANES:(n + 1) * LANES], cq, sq) for n in range(4)]
    kband = jnp.concatenate([_rope(kp_ref[...], cp_ref[...], sp_ref[...]),
                             _rope(ko_ref[...], cq, sq),
                             _rope(kn_ref[...], cn_ref[...], sn_ref[...])], axis=0)
    vband = jnp.concatenate([vp_ref[...], vo_ref[...], vn_ref[...]], axis=0)
    qpos = qb * BLOCK + lax.broadcasted_iota(jnp.int32, (BLOCK, 3 * BLOCK), 0)
    kpos = (qb - 1) * BLOCK + lax.broadcasted_iota(jnp.int32, (BLOCK, 3 * BLOCK), 1)
    valid = (jnp.abs(qpos - kpos) <= WINDOW) & (kpos >= 0) & (kpos < DEC_SEQ)
    o = _pair_heads(qr, [kband, ck_ref[...]], [vband, cv_ref[...]],
                    lambda n: sink_ref[kpair * 2 * C_GROUP + n], [valid, None])
    o_ref[...] = (o * _silu(gate_ref[...])).astype(BF16)


def _attn_latent(proj, cache_k4, cache_v4, li, sink, cos_t, sin_t):
    qw = 2 * C_GROUP * C_HEAD
    k_cb = C_HEADS * C_HEAD // LANES
    v_cb = (C_HEADS + C_KV_HEADS) * C_HEAD // LANES
    gate_cb = (C_HEADS + 2 * C_KV_HEADS) * C_HEAD // qw
    nqb = DEC_SEQ // BLOCK

    def rows(delta):
        def idx(b, kp, qb):
            return b * nqb + jnp.clip(qb + delta, 0, nqb - 1)
        return idx

    def kv_spec(cb, delta):
        r = rows(delta)
        return pl.BlockSpec((BLOCK, LANES), lambda b, kp, qb: (r(b, kp, qb), cb + kp))

    def tab_spec(delta):
        return pl.BlockSpec((BLOCK, LANES), lambda b, kp, qb: (jnp.clip(qb + delta, 0, nqb - 1), 0))

    cache_spec = pl.BlockSpec((None, None, PAST_LEN, LANES), lambda b, kp, qb: (b, li, 0, kp))
    own = rows(0)
    return pl.pallas_call(
        _attn_lat_kernel,
        out_shape=jax.ShapeDtypeStruct((SAMPLE.rows, D_MODEL), BF16),
        grid=(DEC_BATCH, C_KV_HEADS // 2, nqb),
        in_specs=[pl.BlockSpec(memory_space=pltpu.SMEM),
                  pl.BlockSpec((BLOCK, qw), lambda b, kp, qb: (own(b, kp, qb), kp)),
                  kv_spec(k_cb, -1), kv_spec(k_cb, 0), kv_spec(k_cb, 1),
                  kv_spec(v_cb, -1), kv_spec(v_cb, 0), kv_spec(v_cb, 1),
                  cache_spec, cache_spec,
                  tab_spec(0), tab_spec(0), tab_spec(-1), tab_spec(-1), tab_spec(1), tab_spec(1),
                  pl.BlockSpec((BLOCK, qw), lambda b, kp, qb: (own(b, kp, qb), gate_cb + kp))],
        out_specs=pl.BlockSpec((BLOCK, qw), lambda b, kp, qb: (own(b, kp, qb), kp)),
        compiler_params=_cparams(("parallel", "parallel", "parallel")),
        name="attn_latent",
    )(sink, proj, proj, proj, proj, proj, proj, proj, cache_k4, cache_v4,
      cos_t, sin_t, cos_t, sin_t, cos_t, sin_t, proj)


def _rope_tables():
    t = jnp.arange(DEC_SEQ, dtype=jnp.int32)
    row = (t // GRID_W).astype(F32)
    col = (t % GRID_W).astype(F32)
    nf = ROPE_PAIR
    inv = 1.0 / (ROPE_BASE ** (jnp.arange(nf, dtype=F32) / nf))
    lane = np.arange(LANES)
    f_of_lane = lane % nf
    use_col = (lane % C_HEAD) >= C_HEAD // 2
    sign = np.where((lane % (2 * ROPE_PAIR)) < ROPE_PAIR, -1.0, 1.0).astype(np.float32)
    pos = jnp.where(jnp.asarray(use_col)[None, :], col[:, None], row[:, None])
    ang = pos * inv[jnp.asarray(f_of_lane)][None, :]
    return jnp.cos(ang), jnp.sin(ang) * jnp.asarray(sign)[None, :]


def _chain_const_tiles(vec):
    t = vec.reshape(A_HEADS // 2, 2, A_HEAD).transpose(1, 2, 0)
    t = jnp.broadcast_to(t[:, :, None, :, None], (2, A_HEAD, 2, A_HEADS // 2, SEQ_PER_GROUP))
    return t.reshape(2, A_HEAD, CHAINS)


def _even_mixer(st, proj, s0, p, gmat, cs_mat, dft):
    r, k, v, w2, a2, bonus = _wkv_prep(st, proj, p["mu"], p["w0"], p["wup"], p["a0"], p["aup"], p["k_a"], p["r_k"],
                                       gmat)
    o_f, o_b, s_fin = _wkv(r, k, v, w2, a2, p["kkw_t"], p["ka_t"], s0)
    ya = _ya(st, o_f, o_b, bonus, proj, p["gn_w"], p["gn_b"], gmat)
    yb = _dft2(st, proj, cs_mat, dft[0], dft[1])
    return ya, yb, s_fin


def kernel(x_prompt, x_sample, c, state_wkv, cache_k, cache_v, c_ctx, mod_w, mod_b, norm_pre, norm_post,
           even_w_in, even_mu, even_w0, even_w_up, even_a0, even_a_up, even_k_k, even_k_a, even_r_k,
           even_gn_w, even_gn_b, even_w_out, odd_w_in, odd_sink, odd_w_out):
    n_odd = odd_w_in.shape[0]
    xs = {PROMPT: x_prompt.reshape(PROMPT.rows, D_MODEL), SAMPLE: x_sample.reshape(SAMPLE.rows, D_MODEL)}
    conds = jnp.concatenate([c_ctx[None, :], c, jnp.zeros((N_COND - 1 - DEC_BATCH, D_MODEL), F32)], axis=0)
    mods4 = _modulation(conds, mod_w, mod_b).reshape(DEPTH, N_COND, 1, 3 * D_MODEL)
    norm_pre3 = norm_pre.reshape(DEPTH, 1, D_MODEL)
    norm_post3 = norm_post.reshape(DEPTH, 1, D_MODEL)

    even_w_out_b = even_w_out.astype(BF16)
    odd_w_out_b = odd_w_out.astype(BF16)

    head_of_lane = np.arange(CW) // A_HEAD
    gmat = jnp.asarray((head_of_lane[:, None] == head_of_lane[None, :]).astype(np.float32)).astype(BF16)
    c_ch, s_ch = _dft_mats(B_GROUP_CH)
    cs_mat = jnp.concatenate([c_ch, s_ch], axis=1).astype(BF16)
    dft = {PROMPT: tuple(m.astype(BF16) for m in _dft_mats(SEQ)),
           SAMPLE: tuple(m.astype(BF16) for m in _dft_mats(DEC_SEQ))}
    cos_t, sin_t = _rope_tables()
    cache_k4 = cache_k.reshape(DEC_BATCH, n_odd, PAST_LEN, C_KV_HEADS * C_HEAD)
    cache_v4 = cache_v.reshape(DEC_BATCH, n_odd, PAST_LEN, C_KV_HEADS * C_HEAD)

    def pad_rows(w, slot):
        return jnp.pad(w, ((0, 0), (slot * LORA, (3 - slot) * LORA), (0, 0)))

    wup_pad = jnp.stack([pad_rows(even_w_up[:, 0], 0), pad_rows(even_w_up[:, 1], 1)], axis=1)
    aup_pad = jnp.stack([pad_rows(even_a_up[:, 0], 2), pad_rows(even_a_up[:, 1], 3)], axis=1)

    new_wkv, new_k, new_v = [], [], []
    hs = {st: _pre_norm(st, xs[st], norm_pre3, mods4, 0) for st in (PROMPT, SAMPLE)}
    for layer in range(DEPTH):
        i = layer // 2
        if layer % 2 == 0:
            p = {
                "mu": even_mu[i][None, :], "w0": even_w0[i], "a0": even_a0[i],
                "wup": wup_pad[i], "aup": aup_pad[i],
                "k_a": even_k_a[i][None, :], "r_k": even_r_k[i].reshape(1, A_WIDTH),
                "kkw_t": _chain_const_tiles(even_k_k[i]), "ka_t": _chain_const_tiles(even_k_a[i]),
                "gn_w": even_gn_w[i][None, :], "gn_b": even_gn_b[i][None, :],
            }
            s0s = state_wkv[:, i].reshape(DEC_BATCH, 2, A_HEADS // 2, 2, A_HEAD, A_HEAD)
            s0 = {PROMPT: jnp.zeros((BATCH // SEQ_PER_GROUP, 2, A_HEAD, A_HEAD, CHAINS), F32),
                  SAMPLE: s0s.transpose(3, 5, 4, 1, 2, 0).reshape(1, 2, A_HEAD, A_HEAD, CHAINS)}
            for st in (PROMPT, SAMPLE):
                proj = _in_proj(st, hs[st], even_w_in, i)
                ya, yb, s_fin = _even_mixer(st, proj, s0[st], p, gmat, cs_mat, dft[st])
                xs[st], hs[st] = _out_proj(st, ya, 0, yb, 0, even_w_out_b, i, xs[st], mods4, norm_post3, norm_pre3,
                                           layer)
                if st is PROMPT:
                    s_fin = s_fin.reshape(BATCH // SEQ_PER_GROUP, 2, A_HEAD, A_HEAD, 2, A_HEADS // 2, SEQ_PER_GROUP)
                    new_wkv.append(s_fin.transpose(0, 6, 4, 5, 1, 3, 2).reshape(BATCH, 2, A_HEADS, A_HEAD, A_HEAD))
        else:
            proj_p = _in_proj(PROMPT, hs[PROMPT], odd_w_in, i)
            proj_s = _in_proj(SAMPLE, hs[SAMPLE], odd_w_in, i)
            kv0 = C_HEADS * C_HEAD
            kvn = C_KV_HEADS * C_HEAD
            new_k.append(proj_p[:, kv0:kv0 + kvn].reshape(BATCH, SEQ, C_KV_HEADS, C_HEAD))
            new_v.append(proj_p[:, kv0 + kvn:kv0 + 2 * kvn].reshape(BATCH, SEQ, C_KV_HEADS, C_HEAD))
            att_p = _attn_context(proj_p, odd_sink[i])
            att_s = _attn_latent(proj_s, cache_k4, cache_v4, i, odd_sink[i], cos_t, sin_t)
            for st, att in ((PROMPT, att_p), (SAMPLE, att_s)):
                xs[st], hs[st] = _out_proj(st, att, 0, att, 1, odd_w_out_b, i, xs[st], mods4, norm_post3, norm_pre3,
                                           layer)

    y_prompt = xs[PROMPT].reshape(BATCH, SEQ, D_MODEL)
    y_sample = xs[SAMPLE].reshape(DEC_BATCH, DEC_SEQ, D_MODEL)
    return (y_prompt, y_sample, jnp.stack(new_wkv, axis=1), jnp.stack(new_k, axis=1), jnp.stack(new_v, axis=1))
```
